```python
import math
import jax, jax.numpy as jnp
from jax import lax
import numpy as np

D_MODEL = 1024
BATCH = 2
SEQ = 8192
DEPTH = 2
DEC_BATCH = 32
DEC_SEQ = 4
PAST_LEN = 16384
PAGE_SIZE = 128

N_A_LAYERS = DEPTH // 2
N_B_LAYERS = DEPTH - N_A_LAYERS
HEAD_DIM = 64
HEADS_PER_GROUP = 4
DIL_GROUPS = ((128, 1), (512, 4), (2048, 16))
N_GROUPS = len(DIL_GROUPS)
N_HEADS = HEADS_PER_GROUP * N_GROUPS
ATT_WIDTH = N_HEADS * HEAD_DIM
PAD_MULT = math.lcm(*[w for w, _ in DIL_GROUPS])
N_BUCKETS = 32
MAX_DISTANCE = 2048
CONV_A_WIDTH = 31
D_FF = 2816
CONV_F_WIDTH = 3
N_MOD = 6
EPS = 1e-6
LN_EPS = 1e-5
NEG_INF = -1e30

kernel_name = 'yoco_conformer_dilated_attn_step'


def _rmsnorm(x, g):
    x32 = x.astype(jnp.float32)
    y = x32 * lax.rsqrt(jnp.mean(x32 * x32, axis=-1, keepdims=True) + EPS) * g.astype(jnp.float32)
    return y.astype(x.dtype)


def _dwconv(full, w):
    return lax.conv_general_dilated(full, w[:, None, :].astype(full.dtype), (1,), 'VALID',
                                    dimension_numbers=('NWC', 'WIO', 'NWC'),
                                    feature_group_count=full.shape[-1])


def _rel_bucket(dist):
    max_exact = N_BUCKETS // 2
    d = jnp.maximum(dist, 1).astype(jnp.float32)
    large = max_exact + (jnp.log(d / max_exact) / math.log(MAX_DISTANCE / max_exact)
                         * (N_BUCKETS - max_exact)).astype(jnp.int32)
    large = jnp.minimum(large, N_BUCKETS - 1)
    return jnp.where(dist < max_exact, dist, large)


def _group_bias(rel_bias, g, dil, span):
    dist = jnp.arange(span + 1, dtype=jnp.int32) * dil
    hs = slice(g * HEADS_PER_GROUP, (g + 1) * HEADS_PER_GROUP)
    return rel_bias[_rel_bucket(dist)][:, hs].astype(jnp.float32)


def _prompt_dilated(q, k, v, bias_g, dil, span):
    B, S, H, Dh = q.shape
    L = S // dil
    nb = L // span

    def to_blocks(t):
        return t.reshape(B, L, dil, H, Dh).transpose(0, 2, 1, 3, 4).reshape(B * dil, nb, span, H, Dh)

    def with_prev(t):
        prev = jnp.concatenate([jnp.zeros_like(t[:, :1]), t[:, :-1]], axis=1)
        return jnp.concatenate([prev, t], axis=2)

    qb = to_blocks(q)
    kk = with_prev(to_blocks(k))
    vv = with_prev(to_blocks(v))
    s = jnp.einsum('nbqhd,nbkhd->nbhqk', qb, kk).astype(jnp.float32) * (HEAD_DIM ** -0.5)
    qi = jnp.arange(span)[:, None]
    ki = jnp.arange(2 * span)[None, :]
    m = qi + span - ki
    valid = ((m >= 0) & (m <= span))[None] & ((jnp.arange(nb)[:, None, None] > 0) | (ki >= span)[None])
    bias = jnp.transpose(bias_g[jnp.clip(m, 0, span)], (2, 0, 1))
    logits = jnp.where(valid[None, :, None], s + bias[None, None], NEG_INF)
    lse = jax.nn.logsumexp(logits, axis=-1)
    p = jnp.exp(logits - lse[..., None])
    o = jnp.einsum('nbhqk,nbkhd->nbqhd', p.astype(v.dtype), vv)
    o = o.reshape(B, dil, L, H, Dh).transpose(0, 2, 1, 3, 4).reshape(B, S, H, Dh)
    lse = lse.transpose(0, 1, 3, 2).reshape(B, dil, L, H).transpose(0, 2, 1, 3).reshape(B, S, H)
    return o, lse


def _sample_dilated(q, k_all, v_all, bias_g, dil, span):
    N, T, H, Dh = q.shape
    W = k_all.shape[1] - T
    idx = W + jnp.arange(T)[:, None] - jnp.arange(span + 1)[None, :] * dil
    valid = idx >= 0
    idx = jnp.maximum(idx, 0)
    kg = k_all[:, idx]
    vg = v_all[:, idx]
    s = jnp.einsum('nthd,ntmhd->nhtm', q, kg).astype(jnp.float32) * (HEAD_DIM ** -0.5)
    logits = jnp.where(valid[None, None], s + bias_g.T[None, :, None, :], NEG_INF)
    lse = jax.nn.logsumexp(logits, axis=-1)
    p = jnp.exp(logits - lse[..., None])
    o = jnp.einsum('nhtm,ntmhd->nthd', p.astype(v_all.dtype), vg)
    return o, lse.transpose(0, 2, 1)


def _conformer_conv(h, hist, w1, b1, dw, dwb, ln_g, ln_b, w2, b2):
    u = h @ w1 + b1
    a, g = jnp.split(u, 2, axis=-1)
    glu = a * jax.nn.sigmoid(g)
    full = jnp.concatenate([hist.astype(glu.dtype), glu], axis=1)
    y32 = (_dwconv(full, dw) + dwb).astype(jnp.float32)
    mu = jnp.mean(y32, axis=-1, keepdims=True)
    var = jnp.mean(jnp.square(y32 - mu), axis=-1, keepdims=True)
    y = (y32 - mu) * lax.rsqrt(var + LN_EPS) * ln_g.astype(jnp.float32) + ln_b.astype(jnp.float32)
    y = jax.nn.silu(y).astype(h.dtype)
    return y @ w2 + b2, full[:, -(CONV_A_WIDTH - 1):]


def _conv_ffn(h, hist, w_up, cw, cb, w_down):
    u = h @ w_up
    full = jnp.concatenate([hist.astype(u.dtype), u], axis=1)
    y = _dwconv(full, cw) + cb
    g, v = jnp.split(y, 2, axis=-1)
    return (jax.nn.silu(g) * v) @ w_down, full[:, -(CONV_F_WIDTH - 1):]


def _shared_kv(x, c_act, kv_bufs, buf_widths, prompt, p):
    N, T, _ = x.shape
    sh, sc = jnp.split((c_act @ p['w_mod_kv'] + p['b_mod_kv'])[:, None, :], 2, axis=-1)
    hkv = _rmsnorm(x, p['norm_kv']) * (1 + sc) + sh
    kv = (hkv @ p['w_kv']).reshape(N, T, 2, N_HEADS, HEAD_DIM)
    if prompt:
        s_pad = -(-T // PAD_MULT) * PAD_MULT
        kv_att = jnp.pad(kv, ((0, 0), (0, s_pad - T), (0, 0), (0, 0), (0, 0)))
    groups, bufs = [], []
    for g in range(N_GROUPS):
        hs = slice(g * HEADS_PER_GROUP, (g + 1) * HEADS_PER_GROUP)
        kv_g = kv[:, :, :, hs]
        W = buf_widths[g]
        if prompt:
            groups.append(kv_att[:, :, :, hs])
            if T >= W:
                bufs.append(kv_g[:, T - W:])
            else:
                bufs.append(jnp.pad(kv_g, ((0, 0), (W - T, 0), (0, 0), (0, 0), (0, 0))))
        else:
            full = jnp.concatenate([kv_bufs[g].astype(kv_g.dtype), kv_g], axis=1)
            groups.append(full)
            bufs.append(full[:, -W:])
    return groups, bufs


def _dilated_mixer(h, kv_groups, prompt, w_q, w_o, rel_bias):
    N, T, _ = h.shape
    q = (h @ w_q).reshape(N, T, N_HEADS, HEAD_DIM)
    if prompt:
        q = jnp.pad(q, ((0, 0), (0, kv_groups[0].shape[1] - T), (0, 0), (0, 0)))
    outs, lses = [], []
    for g, (window, dil) in enumerate(DIL_GROUPS):
        span = window // dil
        bias_g = _group_bias(rel_bias, g, dil, span)
        qg = q[:, :, g * HEADS_PER_GROUP:(g + 1) * HEADS_PER_GROUP]
        kg = kv_groups[g][:, :, 0]
        vg = kv_groups[g][:, :, 1]
        fn = _prompt_dilated if prompt else _sample_dilated
        o, lse = fn(qg, kg, vg, bias_g, dil, span)
        outs.append(o[:, :T])
        lses.append(lse[:, :T])
    o = jnp.stack(outs, axis=2)
    alpha = jax.nn.softmax(jnp.stack(lses, axis=2), axis=2)
    o = (o * alpha[..., None].astype(o.dtype)).reshape(N, T, ATT_WIDTH)
    return o @ w_o


def _trunk(x, c, hist_a, hist_f, kv_bufs, buf_widths, prompt, p):
    c_act = jax.nn.silu(c)
    kv_groups, new_bufs = None, None
    new_a, new_f = [], []
    for l in range(DEPTH):
        if l == N_A_LAYERS:
            kv_groups, new_bufs = _shared_kv(x, c_act, kv_bufs, buf_widths, prompt, p)
        mod = c_act @ p['w_mod'][l] + p['b_mod'][l]
        sh_m, sc_m, g_m, sh_f, sc_f, g_f = jnp.split(mod[:, None, :], N_MOD, axis=-1)
        h = _rmsnorm(x, p['norm_mix'][l]) * (1 + sc_m) + sh_m
        if l < N_A_LAYERS:
            out, nh = _conformer_conv(h, hist_a[l], p['a_w1'][l], p['a_b1'][l], p['a_dw'][l], p['a_dwb'][l],
                                      p['a_ln_g'][l], p['a_ln_b'][l], p['a_w2'][l], p['a_b2'][l])
            new_a.append(nh)
        else:
            lb = l - N_A_LAYERS
            out = _dilated_mixer(h, kv_groups, prompt, p['w_q'][lb], p['w_o'][lb], p['rel_bias'])
        x = x + g_m * out
        h = _rmsnorm(x, p['norm_ffn'][l]) * (1 + sc_f) + sh_f
        out, nh = _conv_ffn(h, hist_f[l], p['f_wup'][l], p['f_cw'][l], p['f_cb'][l], p['f_wdown'][l])
        new_f.append(nh)
        x = x + g_f * out
    return _rmsnorm(x, p['norm_f']), new_bufs, jnp.stack(new_a), jnp.stack(new_f)


def setup_inputs(seed: int = 0) -> dict:
    key = jax.random.key(seed)
    keys = iter(jax.random.split(key, 64))
    f32 = jnp.float32

    def nrm(shape, scale):
        return jax.random.normal(next(keys), shape, f32) * scale

    D = D_MODEL
    buf_w = [min(w, PAST_LEN) for w, _ in DIL_GROUPS]
    return {
        'x_prompt': nrm((BATCH, SEQ, D), 1.0),
        'x_sample': nrm((DEC_BATCH, DEC_SEQ, D), 1.0),
        'cache_kv_w128': nrm((DEC_BATCH, buf_w[0], 2, HEADS_PER_GROUP, HEAD_DIM), 1.0),
        'cache_kv_w512': nrm((DEC_BATCH, buf_w[1], 2, HEADS_PER_GROUP, HEAD_DIM), 1.0),
        'cache_kv_w2048': nrm((DEC_BATCH, buf_w[2], 2, HEADS_PER_GROUP, HEAD_DIM), 1.0),
        'state_conv_a': nrm((N_A_LAYERS, DEC_BATCH, CONV_A_WIDTH - 1, D), 0.5),
        'state_conv_ffn': nrm((DEPTH, DEC_BATCH, CONV_F_WIDTH - 1, 2 * D_FF), 1.0),
        'c_prompt': nrm((BATCH, D), 1.0),
        'c_sample': nrm((DEC_BATCH, D), 1.0),
        'w_mod': nrm((DEPTH, D, N_MOD * D), 0.5 * D ** -0.5),
        'b_mod': nrm((DEPTH, N_MOD * D), 0.02),
        'norm_mix': 1.0 + nrm((DEPTH, D), 0.05),
        'norm_ffn': 1.0 + nrm((DEPTH, D), 0.05),
        'a_w1': nrm((N_A_LAYERS, D, 2 * D), D ** -0.5),
        'a_b1': nrm((N_A_LAYERS, 2 * D), 0.02),
        'a_dw': nrm((N_A_LAYERS, CONV_A_WIDTH, D), CONV_A_WIDTH ** -0.5),
        'a_dwb': nrm((N_A_LAYERS, D), 0.02),
        'a_ln_g': 1.0 + nrm((N_A_LAYERS, D), 0.05),
        'a_ln_b': nrm((N_A_LAYERS, D), 0.02),
        'a_w2': nrm((N_A_LAYERS, D, D), D ** -0.5),
        'a_b2': nrm((N_A_LAYERS, D), 0.02),
        'w_mod_kv': nrm((D, 2 * D), 0.5 * D ** -0.5),
        'b_mod_kv': nrm((2 * D,), 0.02),
        'norm_kv': 1.0 + nrm((D,), 0.05),
        'w_kv': nrm((D, 2 * ATT_WIDTH), D ** -0.5),
        'w_q': nrm((N_B_LAYERS, D, ATT_WIDTH), D ** -0.5),
        'w_o': nrm((N_B_LAYERS, ATT_WIDTH, D), ATT_WIDTH ** -0.5),
        'rel_bias': nrm((N_BUCKETS, N_HEADS), 0.5),
        'f_wup': nrm((DEPTH, D, 2 * D_FF), D ** -0.5),
        'f_cw': nrm((DEPTH, CONV_F_WIDTH, 2 * D_FF), CONV_F_WIDTH ** -0.5),
        'f_cb': nrm((DEPTH, 2 * D_FF), 0.02),
        'f_wdown': nrm((DEPTH, D_FF, D), D_FF ** -0.5),
        'norm_f': 1.0 + nrm((D,), 0.05),
    }


def reference(x_prompt, x_sample, cache_kv_w128, cache_kv_w512, cache_kv_w2048, state_conv_a,
              state_conv_ffn, c_prompt, c_sample, w_mod, b_mod, norm_mix, norm_ffn, a_w1, a_b1, a_dw,
              a_dwb, a_ln_g, a_ln_b, a_w2, a_b2, w_mod_kv, b_mod_kv, norm_kv, w_kv, w_q, w_o, rel_bias,
              f_wup, f_cw, f_cb, f_wdown, norm_f):
    p = dict(w_mod=w_mod, b_mod=b_mod, norm_mix=norm_mix, norm_ffn=norm_ffn, a_w1=a_w1, a_b1=a_b1,
             a_dw=a_dw, a_dwb=a_dwb, a_ln_g=a_ln_g, a_ln_b=a_ln_b, a_w2=a_w2, a_b2=a_b2,
             w_mod_kv=w_mod_kv, b_mod_kv=b_mod_kv, norm_kv=norm_kv, w_kv=w_kv, w_q=w_q, w_o=w_o,
             rel_bias=rel_bias, f_wup=f_wup, f_cw=f_cw, f_cb=f_cb, f_wdown=f_wdown, norm_f=norm_f)
    kv_bufs = (cache_kv_w128, cache_kv_w512, cache_kv_w2048)
    buf_widths = tuple(b.shape[1] for b in kv_bufs)
    B = x_prompt.shape[0]
    hist_a0 = jnp.zeros((N_A_LAYERS, B, CONV_A_WIDTH - 1, D_MODEL), x_prompt.dtype)
    hist_f0 = jnp.zeros((DEPTH, B, CONV_F_WIDTH - 1, 2 * D_FF), x_prompt.dtype)
    y_prompt, kv_p, conv_a_p, conv_f_p = _trunk(x_prompt, c_prompt, hist_a0, hist_f0, None,
                                                buf_widths, True, p)
    y_sample, kv_s, conv_a_s, conv_f_s = _trunk(x_sample, c_sample, state_conv_a, state_conv_ffn,
                                                kv_bufs, buf_widths, False, p)
    return (y_prompt, y_sample, kv_p[0], kv_p[1], kv_p[2], conv_a_p, conv_f_p,
            kv_s[0], kv_s[1], kv_s[2], conv_a_s, conv_f_s)
```

```python
import functools
import math

import jax
import jax.numpy as jnp
from jax import lax
from jax.experimental import pallas as pl
from jax.experimental.pallas import tpu as pltpu

D_MODEL = 1024
DEPTH = 2
HEAD_DIM = 64
HEADS_PER_GROUP = 4
DIL_GROUPS = ((128, 1), (512, 4), (2048, 16))
N_GROUPS = len(DIL_GROUPS)
N_HEADS = HEADS_PER_GROUP * N_GROUPS
ATT_WIDTH = N_HEADS * HEAD_DIM
GROUP_WIDTH = HEADS_PER_GROUP * HEAD_DIM
SPAN = 128
N_BUCKETS = 32
MAX_DISTANCE = 2048
CONV_A_WIDTH = 31
D_FF = 2816
CONV_F_WIDTH = 3
N_MOD = 6
EPS = 1e-6
LN_EPS = 1e-5
NEG_INF = -1e30

BF = jnp.bfloat16
F32 = jnp.float32

HEAD_SHIFT = HEAD_DIM.bit_length() - 1
SUBLANES = 8
SUBLANE_SHIFT = SUBLANES.bit_length() - 1
LANES = 128
BF16_ROWS = 16
VMEM_LIMIT = 56 * 1024 * 1024

TM_PROMPT = 512
FF_CHUNK = 1408
SAMPLE_QROWS = HEADS_PER_GROUP * SUBLANES
SAMPLE_KEYS = SPAN + SUBLANES


def _round_up(a, b):
    return -(-a // b) * b


def _params(sem):
    return pltpu.CompilerParams(dimension_semantics=sem, vmem_limit_bytes=VMEM_LIMIT)


def _const_spec(shape):
    nd = len(shape)
    return pl.BlockSpec(shape, lambda *_: (0,) * nd, pipeline_mode=pl.Buffered(1))


def _dot(a, b):
    return jnp.dot(a, b, preferred_element_type=F32)


def _dot_nt(a, b):
    return lax.dot_general(a, b, (((1,), (1,)), ((), ())), preferred_element_type=F32)


def _sigmoid(v):
    return 1.0 / (1.0 + jnp.exp(-v))


def _mod_rows(mod_ref, idx, r0, rows, per_row):
    if per_row:
        return mod_ref[0, idx, pl.ds(r0, rows), :]
    return mod_ref[0, idx]


def _rms_mod_rows(x_ref, tm, targets, per_row):
    rc = BF16_ROWS

    def body(i, c):
        r0 = pl.multiple_of(i * rc, rc)
        x = x_ref[0, pl.ds(r0, rc), :]
        xn = x * lax.rsqrt(jnp.mean(x * x, axis=-1, keepdims=True) + EPS)
        for gain_ref, mod_ref, i_sh, i_sc, dst_ref in targets:
            sh = _mod_rows(mod_ref, i_sh, r0, rc, per_row)
            sc = _mod_rows(mod_ref, i_sc, r0, rc, per_row)
            dst_ref[pl.ds(r0, rc), :] = ((xn * gain_ref[...]) * (1.0 + sc) + sh).astype(BF)
        return c

    lax.fori_loop(0, tm // rc, body, 0)


def _shift_conv(load, weights, offsets, rows):
    out = None
    for s in range(SUBLANES):
        taps = [k for k in range(len(offsets)) if offsets[k] % SUBLANES == s]
        if not taps:
            continue
        n = rows + (SUBLANES if s else 0)
        g = None
        for k in taps:
            term = weights[k] * load(offsets[k] - s, n)
            g = term if g is None else g + term
        if s:
            g = g[s:s + rows]
        out = g if out is None else out + g
    return out


def _row_loader(ref, lead, r0, c0, cc, need_rows):
    if need_rows * cc <= 16 * SUBLANES * LANES:
        ext = ref[lead + (pl.ds(r0, need_rows), slice(c0, c0 + cc))]
        return lambda a, n: ext[a:a + n]
    return lambda a, n: ref[lead + (pl.ds(r0 + a, n), slice(c0, c0 + cc))]


def _mod_kernel(c_ref, w_ref, b_ref, o_ref):
    c = c_ref[...]
    a = (c * _sigmoid(c)).astype(BF)
    o_ref[0] = _dot(a, w_ref[0].astype(BF)) + b_ref[0]


def _mod_call(c_all, w, b, tn):
    nl, d, n = w.shape
    mp = c_all.shape[0]
    return pl.pallas_call(
        _mod_kernel,
        out_shape=jax.ShapeDtypeStruct((nl, mp, n), F32),
        grid=(nl, n // tn),
        in_specs=[
            pl.BlockSpec((mp, d), lambda l, j: (0, 0)),
            pl.BlockSpec((1, d, tn), lambda l, j: (l, 0, j)),
            pl.BlockSpec((1, 1, tn), lambda l, j: (l, 0, j)),
        ],
        out_specs=pl.BlockSpec((1, mp, tn), lambda l, j: (l, 0, j)),
        compiler_params=_params(("arbitrary", "arbitrary")),
        name="adaln_mod",
    )(c_all, w, b)


def _conformer_kernel(x_ref, mod_ref, hist_ref, nrm_ref, w1_ref, b1_ref, dw_ref, dwb_ref, lng_ref, lnb_ref,
                      w2_ref, b2_ref, o_ref, st_ref, h_ref, u_ref, full_ref, y_ref, *, tm, rd, per_row):
    d = D_MODEL
    hh = (CONV_A_WIDTH - 1) * rd
    hp = _round_up(hh, SUBLANES)
    t = pl.program_id(1)

    @pl.when(t == 0)
    def _():
        full_ref[0:hp, :] = jnp.zeros((hp, d), F32)
        full_ref[hp - hh:hp, :] = hist_ref[0]

    _rms_mod_rows(x_ref, tm, [(nrm_ref, mod_ref, 0, 1, h_ref)], per_row)
    u_ref[...] = _dot(h_ref[...], w1_ref[...])

    rc, cc = 16, 512

    def glu_body(i, c):
        r0 = pl.multiple_of(i * rc, rc)
        for c0 in range(0, d, cc):
            a = u_ref[pl.ds(r0, rc), c0:c0 + cc] + b1_ref[:, c0:c0 + cc]
            g = u_ref[pl.ds(r0, rc), d + c0:d + c0 + cc] + b1_ref[:, d + c0:d + c0 + cc]
            full_ref[pl.ds(hp + r0, rc), c0:c0 + cc] = a * _sigmoid(g)
        return c

    lax.fori_loop(0, tm // rc, glu_body, 0)

    rcv, ccv = 32, LANES
    offsets = [hp - hh + k * rd for k in range(CONV_A_WIDTH)]

    def conv_body(i, c):
        r0 = pl.multiple_of(i * rcv, rcv)
        for c0 in range(0, d, ccv):
            load = _row_loader(full_ref, (), r0, c0, ccv, hp + rcv)
            w = [dw_ref[k:k + 1, c0:c0 + ccv] for k in range(CONV_A_WIDTH)]
            y_ref[pl.ds(r0, rcv), c0:c0 + ccv] = _shift_conv(load, w, offsets, rcv) + dwb_ref[:, c0:c0 + ccv]
        return c

    lax.fori_loop(0, tm // rcv, conv_body, 0)

    def ln_body(i, c):
        r0 = pl.multiple_of(i * BF16_ROWS, BF16_ROWS)
        y = y_ref[pl.ds(r0, BF16_ROWS), :]
        mu = jnp.mean(y, axis=-1, keepdims=True)
        dv = y - mu
        var = jnp.mean(dv * dv, axis=-1, keepdims=True)
        yn = dv * lax.rsqrt(var + LN_EPS) * lng_ref[...] + lnb_ref[...]
        h_ref[pl.ds(r0, BF16_ROWS), :] = (yn * _sigmoid(yn)).astype(BF)
        return c

    lax.fori_loop(0, tm // BF16_ROWS, ln_body, 0)

    out = _dot(h_ref[...], w2_ref[...]) + b2_ref[...]
    o_ref[0] = x_ref[0] + mod_ref[0, 2] * out

    new_hist = full_ref[hp + tm - hh:hp + tm, :]
    st_ref[0] = new_hist
    full_ref[hp - hh:hp, :] = new_hist


def _conformer_call(x, mods, hist, nrm, w1, b1, dw, dwb, lng, lnb, w2, b2, *, tm, rd):
    ns, s, d = x.shape
    r = mods.shape[2]
    hh = hist.shape[1]
    hp = _round_up(hh, SUBLANES)
    per_row = r > 1
    assert s % tm == 0 and (not per_row or (r == tm and s == tm))
    kern = functools.partial(_conformer_kernel, tm=tm, rd=rd, per_row=per_row)
    return pl.pallas_call(
        kern,
        out_shape=(jax.ShapeDtypeStruct((ns, s, d), F32), jax.ShapeDtypeStruct((ns, hh, d), F32)),
        grid=(ns, s // tm),
        in_specs=[
            pl.BlockSpec((1, tm, d), lambda n, t: (n, t, 0)),
            pl.BlockSpec((1, N_MOD, r, d), lambda n, t: (n, 0, 0, 0)),
            pl.BlockSpec((1, hh, d), lambda n, t: (n, 0, 0)),
            _const_spec((1, d)),
            _const_spec((d, 2 * d)),
            _const_spec((1, 2 * d)),
            _const_spec((CONV_A_WIDTH, d)),
            _const_spec((1, d)),
            _const_spec((1, d)),
            _const_spec((1, d)),
            _const_spec((d, d)),
            _const_spec((1, d)),
        ],
        out_specs=(
            pl.BlockSpec((1, tm, d), lambda n, t: (n, t, 0)),
            pl.BlockSpec((1, hh, d), lambda n, t: (n, 0, 0)),
        ),
        scratch_shapes=[
            pltpu.VMEM((tm, d), BF),
            pltpu.VMEM((tm, 2 * d), F32),
            pltpu.VMEM((hp + tm, d), F32),
            pltpu.VMEM((tm, d), F32),
        ],
        compiler_params=_params(("arbitrary", "arbitrary")),
        name="conformer_mixer",
    )(x, mods, hist, nrm, w1, b1, dw, dwb, lng, lnb, w2, b2)


def _ffn_kernel(x_ref, mod_ref, hg_ref, hv_ref, nrm_ref, wug_ref, wuv_ref, cwg_ref, cwv_ref, cbg_ref, cbv_ref,
                wd_ref, nf_ref, o_ref, st_ref, h_ref, ubuf_ref, carry_ref, act_ref,
                *, tm, rd, per_row, nj, fc, final_norm):
    hh = (CONV_F_WIDTH - 1) * rd
    hp = _round_up(hh, SUBLANES)
    t = pl.program_id(1)
    j = pl.program_id(2)

    @pl.when(j == 0)
    def _():
        _rms_mod_rows(x_ref, tm, [(nrm_ref, mod_ref, 3, 4, h_ref)], per_row)

    halves = ((hg_ref, wug_ref, cwg_ref, cbg_ref), (hv_ref, wuv_ref, cwv_ref, cbv_ref))
    for half, (hist_ref, wu_ref, _, _) in enumerate(halves):
        @pl.when(t == 0)
        def _():
            ubuf_ref[half, 0:hp, :] = jnp.zeros((hp, fc), F32)
            ubuf_ref[half, hp - hh:hp, :] = hist_ref[0]

        @pl.when(t > 0)
        def _():
            ubuf_ref[half, 0:hp, :] = carry_ref[half, j]

        ubuf_ref[half, hp:hp + tm, :] = _dot(h_ref[...], wu_ref[...])
        st_ref[0, half * nj + j] = ubuf_ref[half, hp + tm - hh:hp + tm, :]
        carry_ref[half, j] = ubuf_ref[half, tm:tm + hp, :]

    rc, cc = 32, LANES
    offsets = [hp - hh + k * rd for k in range(CONV_F_WIDTH)]

    def act_body(i, c):
        r0 = pl.multiple_of(i * rc, rc)
        for c0 in range(0, fc, cc):
            ys = []
            for half, (_, _, cw_ref, cb_ref) in enumerate(halves):
                load = _row_loader(ubuf_ref, (half,), r0, c0, cc, hp + rc)
                w = [cw_ref[k:k + 1, c0:c0 + cc] for k in range(CONV_F_WIDTH)]
                ys.append(_shift_conv(load, w, offsets, rc) + cb_ref[:, c0:c0 + cc])
            yg, yv = ys
            act_ref[pl.ds(r0, rc), c0:c0 + cc] = (yg * _sigmoid(yg) * yv).astype(BF)
        return c

    lax.fori_loop(0, tm // rc, act_body, 0)

    part = _dot(act_ref[...], wd_ref[...])

    @pl.when(j == 0)
    def _():
        o_ref[0] = part

    @pl.when(j > 0)
    def _():
        o_ref[0] += part

    @pl.when(j == nj - 1)
    def _():
        rows = BF16_ROWS

        def fin_body(i, c):
            r0 = pl.multiple_of(i * rows, rows)
            gate = _mod_rows(mod_ref, 5, r0, rows, per_row)
            xo = x_ref[0, pl.ds(r0, rows), :] + gate * o_ref[0, pl.ds(r0, rows), :]
            if final_norm:
                xo = xo * lax.rsqrt(jnp.mean(xo * xo, axis=-1, keepdims=True) + EPS) * nf_ref[...]
            o_ref[0, pl.ds(r0, rows), :] = xo
            return c

        lax.fori_loop(0, tm // rows, fin_body, 0)


def _ffn_call(x, mods, hist, nrm, wup, cw, cb, wdown, nf, *, tm, rd, final_norm):
    ns, s, d = x.shape
    r = mods.shape[2]
    hh = hist.shape[1]
    hp = _round_up(hh, SUBLANES)
    f = wdown.shape[0]
    fc = FF_CHUNK
    nj = f // fc
    per_row = r > 1
    assert s % tm == 0 and f % fc == 0 and (not per_row or (r == tm and s == tm))
    kern = functools.partial(_ffn_kernel, tm=tm, rd=rd, per_row=per_row, nj=nj, fc=fc, final_norm=final_norm)
    y, st = pl.pallas_call(
        kern,
        out_shape=(jax.ShapeDtypeStruct((ns, s, d), F32), jax.ShapeDtypeStruct((ns, 2 * nj, hh, fc), F32)),
        grid=(ns, s // tm, nj),
        in_specs=[
            pl.BlockSpec((1, tm, d), lambda n, t, j: (n, t, 0)),
            pl.BlockSpec((1, N_MOD, r, d), lambda n, t, j: (n, 0, 0, 0)),
            pl.BlockSpec((1, hh, fc), lambda n, t, j: (n, 0, j)),
            pl.BlockSpec((1, hh, fc), lambda n, t, j: (n, 0, nj + j)),
            _const_spec((1, d)),
            pl.BlockSpec((d, fc), lambda n, t, j: (0, j)),
            pl.BlockSpec((d, fc), lambda n, t, j: (0, nj + j)),
            pl.BlockSpec((CONV_F_WIDTH, fc), lambda n, t, j: (0, j)),
            pl.BlockSpec((CONV_F_WIDTH, fc), lambda n, t, j: (0, nj + j)),
            pl.BlockSpec((1, fc), lambda n, t, j: (0, j)),
            pl.BlockSpec((1, fc), lambda n, t, j: (0, nj + j)),
            pl.BlockSpec((fc, d), lambda n, t, j: (j, 0)),
            _const_spec((1, d)),
        ],
        out_specs=(
            pl.BlockSpec((1, tm, d), lambda n, t, j: (n, t, 0)),
            pl.BlockSpec((1, 2 * nj, hh, fc), lambda n, t, j: (n, 0, 0, 0)),
        ),
        scratch_shapes=[
            pltpu.VMEM((tm, d), BF),
            pltpu.VMEM((2, hp + tm, fc), F32),
            pltpu.VMEM((2, nj, hp, fc), F32),
            pltpu.VMEM((tm, fc), BF),
        ],
        compiler_params=_params(("arbitrary", "arbitrary", "arbitrary")),
        name="conv_ffn",
    )(x, mods, hist, hist, nrm, wup, wup, cw, cw, cb, cb, wdown, nf)
    return y, jnp.swapaxes(st, 1, 2).reshape(ns, hh, 2 * f)


def _qkv_kernel(x_ref, mod_ref, modkv_ref, nq_ref, nkv_ref, wq_ref, wkv_ref, q_ref, kv_ref, hq_ref, hkv_ref,
                *, tm, per_row):
    _rms_mod_rows(x_ref, tm, [(nq_ref, mod_ref, 0, 1, hq_ref), (nkv_ref, modkv_ref, 0, 1, hkv_ref)], per_row)
    q_ref[0] = _dot(hq_ref[...], wq_ref[...])
    kv_ref[0] = _dot(hkv_ref[...], wkv_ref[...])


def _qkv_call(x, mods, modkv, nq, nkv, wq, wkv, *, tm):
    ns, s, d = x.shape
    r = mods.shape[2]
    per_row = r > 1
    kern = functools.partial(_qkv_kernel, tm=tm, per_row=per_row)
    return pl.pallas_call(
        kern,
        out_shape=(jax.ShapeDtypeStruct((ns, s, ATT_WIDTH), F32), jax.ShapeDtypeStruct((ns, s, 2 * ATT_WIDTH), F32)),
        grid=(ns, s // tm),
        in_specs=[
            pl.BlockSpec((1, tm, d), lambda n, t: (n, t, 0)),
            pl.BlockSpec((1, N_MOD, r, d), lambda n, t: (n, 0, 0, 0)),
            pl.BlockSpec((1, 2, r, d), lambda n, t: (n, 0, 0, 0)),
            _const_spec((1, d)),
            _const_spec((1, d)),
            _const_spec((d, ATT_WIDTH)),
            _const_spec((d, 2 * ATT_WIDTH)),
        ],
        out_specs=(
            pl.BlockSpec((1, tm, ATT_WIDTH), lambda n, t: (n, t, 0)),
            pl.BlockSpec((1, tm, 2 * ATT_WIDTH), lambda n, t: (n, t, 0)),
        ),
        scratch_shapes=[pltpu.VMEM((tm, d), BF), pltpu.VMEM((tm, d), BF)],
        compiler_params=_params(("arbitrary", "arbitrary")),
        name="qkv_proj",
    )(x, mods, modkv, nq, nkv, wq, wkv)


def _rel_bucket(dist):
    max_exact = N_BUCKETS // 2
    dd = jnp.maximum(dist, 1).astype(F32)
    large = max_exact + (jnp.log(dd / max_exact) / math.log(MAX_DISTANCE / max_exact)
                         * (N_BUCKETS - max_exact)).astype(jnp.int32)
    large = jnp.minimum(large, N_BUCKETS - 1)
    return jnp.where(dist < max_exact, dist, large)


def _prompt_bucket_table(dil):
    qi = jnp.arange(SPAN, dtype=jnp.int32)[:, None]
    ki = jnp.arange(2 * SPAN, dtype=jnp.int32)[None, :]
    m = qi + SPAN - ki
    valid = (m >= 0) & (m <= SPAN)
    return jnp.where(valid, _rel_bucket(jnp.clip(m, 0, SPAN) * dil), -1).astype(jnp.int32)


def _sample_bucket_table(g, dil, t_new):
    row = jnp.arange(SAMPLE_QROWS, dtype=jnp.int32)[:, None]
    t = (row % SUBLANES) % t_new
    col = jnp.arange(SAMPLE_KEYS, dtype=jnp.int32)[None, :]
    tn = col - SPAN
    if g == 0:
        m_old = SPAN + t - col
        ok_old = m_old <= SPAN
        m_new = t - tn
        ok_new = (tn >= 0) & (tn < t_new) & (m_new >= 0)
    else:
        m_old = SPAN - col + 0 * t
        ok_old = m_old >= 1
        m_new = 0 * tn
        ok_new = (tn == t)
    m = jnp.where(col < SPAN, m_old, m_new)
    ok = jnp.where(col < SPAN, ok_old, ok_new)
    return jnp.where(ok, _rel_bucket(jnp.clip(m, 0, SPAN) * dil), -1).astype(jnp.int32)


def _bias_from_buckets(bkt, relb_ref, head):
    acc = jnp.full(bkt.shape, NEG_INF, F32)
    for b in range(N_BUCKETS):
        acc = jnp.where(bkt == b, relb_ref[b, head], acc)
    return acc


def _pattn_kernel(relb_ref, bkt_ref, q_ref, kp_ref, kc_ref, vp_ref, vc_ref, o_ref, l_ref, bias_ref, *, g):
    b = pl.program_id(0)
    r = pl.program_id(1)
    lb = pl.program_id(2)
    hpg = HEADS_PER_GROUP

    @pl.when((b == 0) & (r == 0) & (lb == 0))
    def _():
        bkt = bkt_ref[...]
        for h in range(hpg):
            bias_ref[h] = _bias_from_buckets(bkt, relb_ref, g * hpg + h)

    lanehead = lax.broadcasted_iota(jnp.int32, (SPAN, GROUP_WIDTH), 1) >> HEAD_SHIFT
    col = lax.broadcasted_iota(jnp.int32, (SPAN, 2 * SPAN), 1)
    keep = col >= jnp.where(lb > 0, 0, SPAN)

    q = q_ref[0] * (HEAD_DIM ** -0.5)
    lhs = jnp.concatenate([jnp.where(lanehead == h, q, 0.0).astype(BF) for h in range(hpg)], axis=0)
    kcat = jnp.concatenate([kp_ref[0], kc_ref[0]], axis=0).astype(BF)
    s = _dot_nt(lhs, kcat)

    ps, ms, sums = [], [], []
    for h in range(hpg):
        logit = jnp.where(keep, s[h * SPAN:(h + 1) * SPAN] + bias_ref[h], NEG_INF)
        m = jnp.max(logit, axis=-1, keepdims=True)
        e = jnp.exp(logit - m)
        ps.append(e.astype(BF))
        ms.append(m)
        sums.append(jnp.sum(e, axis=-1, keepdims=True))
    vcat = jnp.concatenate([vp_ref[0], vc_ref[0]], axis=0).astype(BF)
    pv = _dot(jnp.concatenate(ps, axis=0), vcat)

    o = jnp.zeros((SPAN, GROUP_WIDTH), F32)
    l = jnp.zeros((SPAN, GROUP_WIDTH), F32)
    for h in range(hpg):
        o = jnp.where(lanehead == h, pv[h * SPAN:(h + 1) * SPAN] * (1.0 / sums[h]), o)
        l = jnp.where(lanehead == h, ms[h] + jnp.log(sums[h]), l)
    o_ref[0] = o
    l_ref[0] = l


def _pattn_call(q, kv, rel_bias, g, dil):
    bsz, s, _ = q.shape
    assert s % (dil * SPAN) == 0
    ll = s // dil
    nb = ll // SPAN
    gw = GROUP_WIDTH
    qv = q.reshape(bsz, ll, dil * ATT_WIDTH)
    kvv = kv.reshape(bsz, ll, dil * 2 * ATT_WIDTH)
    nq = ATT_WIDTH // gw
    nkv = 2 * ATT_WIDTH // gw
    blk = (1, SPAN, gw)
    out = jax.ShapeDtypeStruct((bsz, ll, dil * gw), F32)
    o, l = pl.pallas_call(
        functools.partial(_pattn_kernel, g=g),
        out_shape=(out, out),
        grid=(bsz, dil, nb),
        in_specs=[
            pl.BlockSpec(memory_space=pltpu.SMEM),
            _const_spec((SPAN, 2 * SPAN)),
            pl.BlockSpec(blk, lambda b, r, i: (b, i, r * nq + g)),
            pl.BlockSpec(blk, lambda b, r, i: (b, jnp.maximum(i - 1, 0), r * nkv + g)),
            pl.BlockSpec(blk, lambda b, r, i: (b, i, r * nkv + g)),
            pl.BlockSpec(blk, lambda b, r, i: (b, jnp.maximum(i - 1, 0), r * nkv + N_GROUPS + g)),
            pl.BlockSpec(blk, lambda b, r, i: (b, i, r * nkv + N_GROUPS + g)),
        ],
        out_specs=(
            pl.BlockSpec(blk, lambda b, r, i: (b, i, r)),
            pl.BlockSpec(blk, lambda b, r, i: (b, i, r)),
        ),
        scratch_shapes=[pltpu.VMEM((HEADS_PER_GROUP, SPAN, 2 * SPAN), F32)],
        compiler_params=_params(("arbitrary", "arbitrary", "arbitrary")),
        name=f"prompt_attn_g{g}",
    )(rel_bias, _prompt_bucket_table(dil), qv, kvv, kvv, kvv, kvv)
    return o.reshape(bsz, s, gw), l.reshape(bsz, s, gw)


def _sattn_kernel(relb_ref, bkt_ref, q_ref, kvn_ref, c0_ref, c1_ref, c2_ref,
                  o0_ref, o1_ref, o2_ref, l0_ref, l1_ref, l2_ref, bias_ref, *, t_new):
    hpg = HEADS_PER_GROUP
    gw = GROUP_WIDTH
    rows = SAMPLE_QROWS

    @pl.when(pl.program_id(0) == 0)
    def _():
        for g in range(N_GROUPS):
            for h in range(hpg):
                sl = slice(h * SUBLANES, (h + 1) * SUBLANES)
                bias_ref[g, sl, :] = _bias_from_buckets(bkt_ref[g, sl, :], relb_ref, g * hpg + h)

    lanehead = lax.broadcasted_iota(jnp.int32, (rows, gw), 1) >> HEAD_SHIFT
    rowhead = lax.broadcasted_iota(jnp.int32, (rows, gw), 0) >> SUBLANE_SHIFT
    own = lanehead == rowhead
    rowt = lax.broadcasted_iota(jnp.int32, (rows, 1), 0) & (t_new - 1)

    caches = (c0_ref, c1_ref, c2_ref)
    outs = ((o0_ref, l0_ref), (o1_ref, l1_ref), (o2_ref, l2_ref))
    for g in range(N_GROUPS):
        qm = jnp.where(own, q_ref[0, :, g * gw:(g + 1) * gw] * (HEAD_DIM ** -0.5), 0.0).astype(BF)
        knew = kvn_ref[0, :, g * gw:(g + 1) * gw]
        vnew = kvn_ref[0, :, ATT_WIDTH + g * gw:ATT_WIDTH + (g + 1) * gw]
        cref = caches[g]
        osel = None
        lsel = None
        classes = (0,) if g == 0 else tuple(range(t_new))
        for t in classes:
            base = t * 2 * gw
            kext = jnp.concatenate([cref[0, :, base:base + gw], knew], axis=0).astype(BF)
            vext = jnp.concatenate([cref[0, :, base + gw:base + 2 * gw], vnew], axis=0).astype(BF)
            logit = _dot_nt(qm, kext) + bias_ref[g]
            m = jnp.max(logit, axis=-1, keepdims=True)
            e = jnp.exp(logit - m)
            ssum = jnp.sum(e, axis=-1, keepdims=True)
            on = _dot(e.astype(BF), vext) * (1.0 / ssum)
            lse = m + jnp.log(ssum)
            if osel is None:
                osel, lsel = on, lse
            else:
                osel = jnp.where(rowt == t, on, osel)
                lsel = jnp.where(rowt == t, lse, lsel)
        om = jnp.where(own, osel, 0.0)
        lm = jnp.where(own, lsel, 0.0)
        o_ref, l_ref = outs[g]
        o_acc = om[0:SUBLANES]
        l_acc = lm[0:SUBLANES]
        for h in range(1, hpg):
            o_acc = o_acc + om[h * SUBLANES:(h + 1) * SUBLANES]
            l_acc = l_acc + lm[h * SUBLANES:(h + 1) * SUBLANES]
        o_ref[0] = o_acc
        l_ref[0] = l_acc


def _sattn_call(q, kvn, caches, rel_bias):
    n, t_new, _ = q.shape
    gw = GROUP_WIDTH
    assert t_new <= SUBLANES and t_new <= min(d for _, d in DIL_GROUPS[1:]) and t_new & (t_new - 1) == 0
    views = []
    for g, (window, dil) in enumerate(DIL_GROUPS):
        assert caches[g].shape[1] == window and window == SPAN * dil
        views.append(caches[g].reshape(n, SPAN, dil * 2 * gw))
    pad = SUBLANES - t_new
    qp = jnp.tile(jnp.pad(q, ((0, 0), (0, pad), (0, 0))), (1, HEADS_PER_GROUP, 1))
    kvp = jnp.pad(kvn, ((0, 0), (0, pad), (0, 0)))
    bkt = jnp.stack([_sample_bucket_table(g, dil, t_new) for g, (_, dil) in enumerate(DIL_GROUPS)])
    out = jax.ShapeDtypeStruct((n, SUBLANES, gw), F32)
    oblk = pl.BlockSpec((1, SUBLANES, gw), lambda i: (i, 0, 0))
    cw = [min(dil, t_new) * 2 * gw for _, dil in DIL_GROUPS]
    res = pl.pallas_call(
        functools.partial(_sattn_kernel, t_new=t_new),
        out_shape=(out,) * 6,
        grid=(n,),
        in_specs=[
            pl.BlockSpec(memory_space=pltpu.SMEM),
            _const_spec((N_GROUPS, SAMPLE_QROWS, SAMPLE_KEYS)),
            pl.BlockSpec((1, SAMPLE_QROWS, ATT_WIDTH), lambda i: (i, 0, 0)),
            pl.BlockSpec((1, SUBLANES, 2 * ATT_WIDTH), lambda i: (i, 0, 0)),
            pl.BlockSpec((1, SPAN, cw[0]), lambda i: (i, 0, 0)),
            pl.BlockSpec((1, SPAN, cw[1]), lambda i: (i, 0, 0)),
            pl.BlockSpec((1, SPAN, cw[2]), lambda i: (i, 0, 0)),
        ],
        out_specs=(oblk,) * 6,
        scratch_shapes=[pltpu.VMEM((N_GROUPS, SAMPLE_QROWS, SAMPLE_KEYS), F32)],
        compiler_params=_params(("arbitrary",)),
        name="sample_attn",
    )(rel_bias, bkt, qp, kvp, *views)
    return [a[:, :t_new] for a in res[:3]], [a[:, :t_new] for a in res[3:]]


def _oproj_kernel(x_ref, mod_ref, o0_ref, o1_ref, o2_ref, l0_ref, l1_ref, l2_ref, wo_ref, out_ref, att_ref,
                  *, tm, per_row):
    gw = GROUP_WIDTH
    rc = BF16_ROWS
    o_refs = (o0_ref, o1_ref, o2_ref)
    l_refs = (l0_ref, l1_ref, l2_ref)

    def body(i, c):
        r0 = pl.multiple_of(i * rc, rc)
        ls = [ref[0, pl.ds(r0, rc), :] for ref in l_refs]
        mx = jnp.maximum(jnp.maximum(ls[0], ls[1]), ls[2])
        es = [jnp.exp(v - mx) for v in ls]
        inv = 1.0 / (es[0] + es[1] + es[2])
        for g in range(N_GROUPS):
            att_ref[pl.ds(r0, rc), g * gw:(g + 1) * gw] = (o_refs[g][0, pl.ds(r0, rc), :] * (es[g] * inv)).astype(BF)
        return c

    lax.fori_loop(0, tm // rc, body, 0)
    out_ref[0] = x_ref[0] + mod_ref[0, 2] * _dot(att_ref[...], wo_ref[...])


def _oproj_call(x, mods, os_, ls_, wo, *, tm):
    ns, s, d = x.shape
    r = mods.shape[2]
    gw = GROUP_WIDTH
    kern = functools.partial(_oproj_kernel, tm=tm, per_row=r > 1)
    gspec = pl.BlockSpec((1, tm, gw), lambda n, t: (n, t, 0))
    return pl.pallas_call(
        kern,
        out_shape=jax.ShapeDtypeStruct((ns, s, d), F32),
        grid=(ns, s // tm),
        in_specs=[
            pl.BlockSpec((1, tm, d), lambda n, t: (n, t, 0)),
            pl.BlockSpec((1, N_MOD, r, d), lambda n, t: (n, 0, 0, 0)),
        ] + [gspec] * 6 + [_const_spec((ATT_WIDTH, d))],
        out_specs=pl.BlockSpec((1, tm, d), lambda n, t: (n, t, 0)),
        scratch_shapes=[pltpu.VMEM((tm, ATT_WIDTH), BF)],
        compiler_params=_params(("arbitrary", "arbitrary")),
        name="attn_out_proj",
    )(x, mods, *os_, *ls_, wo)


def _trunk(x, mods, modkv, hist_a, hist_f, w, attn_fn, *, tm, rd):
    row = lambda v: v.reshape(1, -1)
    x, st_a = _conformer_call(x, mods[0], hist_a, row(w["norm_mix"][0]), w["a_w1"], row(w["a_b1"][0]),
                              w["a_dw"][0], row(w["a_dwb"][0]), row(w["a_ln_g"][0]), row(w["a_ln_b"][0]),
                              w["a_w2"], row(w["a_b2"][0]), tm=tm, rd=rd)
    x, st_f0 = _ffn_call(x, mods[0], hist_f[0], row(w["norm_ffn"][0]), w["f_wup"][0], w["f_cw"][0],
                         row(w["f_cb"][0]), w["f_wdown"][0], row(w["norm_f"]), tm=tm, rd=rd, final_norm=False)
    q, kv = _qkv_call(x, mods[1], modkv, row(w["norm_mix"][1]), row(w["norm_kv"]), w["w_q"], w["w_kv"], tm=tm)
    os_, ls_ = attn_fn(q, kv)
    x = _oproj_call(x, mods[1], os_, ls_, w["w_o"], tm=tm)
    y, st_f1 = _ffn_call(x, mods[1], hist_f[1], row(w["norm_ffn"][1]), w["f_wup"][1], w["f_cw"][1],
                         row(w["f_cb"][1]), w["f_wdown"][1], row(w["norm_f"]), tm=tm, rd=rd, final_norm=True)
    return y, kv, st_a, [st_f0, st_f1]


def kernel(x_prompt, x_sample, cache_kv_w128, cache_kv_w512, cache_kv_w2048, state_conv_a, state_conv_ffn, c_prompt, c_sample, w_mod, b_mod, norm_mix, norm_ffn, a_w1, a_b1, a_dw, a_dwb, a_ln_g, a_ln_b, a_w2, a_b2, w_mod_kv, b_mod_kv, norm_kv, w_kv, w_q, w_o, rel_bias, f_wup, f_cw, f_cb, f_wdown, norm_f):
    d = D_MODEL
    bsz, seq, _ = x_prompt.shape
    nseq, t_new, _ = x_sample.shape
    caches = (cache_kv_w128, cache_kv_w512, cache_kv_w2048)
    hpg = HEADS_PER_GROUP

    w = dict(norm_mix=norm_mix, norm_ffn=norm_ffn, a_w1=a_w1[0].astype(BF), a_b1=a_b1, a_dw=a_dw, a_dwb=a_dwb,
             a_ln_g=a_ln_g, a_ln_b=a_ln_b, a_w2=a_w2[0].astype(BF), a_b2=a_b2, norm_kv=norm_kv,
             w_kv=w_kv.astype(BF), w_q=w_q[0].astype(BF), w_o=w_o[0].astype(BF), f_wup=f_wup.astype(BF),
             f_cw=f_cw, f_cb=f_cb, f_wdown=f_wdown.astype(BF), norm_f=norm_f)

    n_c = bsz + nseq
    mp = _round_up(n_c, SUBLANES)
    c_all = jnp.pad(jnp.concatenate([c_prompt, c_sample], axis=0), ((0, mp - n_c), (0, 0)))
    mod = _mod_call(c_all, w_mod, b_mod.reshape(DEPTH, 1, N_MOD * d), tn=1536)
    modkv = _mod_call(c_all, w_mod_kv[None], b_mod_kv.reshape(1, 1, 2 * d), tn=1024)[0]

    mods_p = [mod[l, :bsz].reshape(bsz, N_MOD, 1, d) for l in range(DEPTH)]
    modkv_p = modkv[:bsz].reshape(bsz, 2, 1, d)
    hist_a_p = jnp.zeros((bsz, CONV_A_WIDTH - 1, d), F32)
    hist_f_p = [jnp.zeros((bsz, CONV_F_WIDTH - 1, 2 * D_FF), F32)] * DEPTH

    def prompt_attn(q, kv):
        res = [_pattn_call(q, kv, rel_bias, g, dil) for g, (_, dil) in enumerate(DIL_GROUPS)]
        return [r[0] for r in res], [r[1] for r in res]

    y_p, kv_p, st_a_p, st_f_p = _trunk(x_prompt, mods_p, modkv_p, hist_a_p, hist_f_p, w, prompt_attn,
                                       tm=TM_PROMPT, rd=1)
    kv_p = kv_p.reshape(bsz, seq, 2, N_HEADS, HEAD_DIM)
    kv_bufs_p = []
    for g in range(N_GROUPS):
        wg = caches[g].shape[1]
        assert seq >= wg
        kv_bufs_p.append(kv_p[:, seq - wg:, :, g * hpg:(g + 1) * hpg])
    conv_a_p = st_a_p[None]
    conv_f_p = jnp.stack(st_f_p)

    rows = t_new * nseq
    tmaj = lambda v: jnp.swapaxes(v, 0, 1).reshape(1, -1, v.shape[-1])
    smaj = lambda v, c: jnp.swapaxes(v.reshape(-1, nseq, c), 0, 1)

    def per_row(m, k):
        return jnp.tile(jnp.swapaxes(m.reshape(nseq, k, d), 0, 1), (1, t_new, 1))[None]

    mods_s = [per_row(mod[l, bsz:n_c], N_MOD) for l in range(DEPTH)]
    modkv_s = per_row(modkv[bsz:n_c], 2)
    hist_a_s = tmaj(state_conv_a[0])
    hist_f_s = [tmaj(state_conv_ffn[l]) for l in range(DEPTH)]

    def sample_attn(q, kv):
        os_, ls_ = _sattn_call(smaj(q, ATT_WIDTH), smaj(kv, 2 * ATT_WIDTH), caches, rel_bias)
        return [tmaj(a) for a in os_], [tmaj(a) for a in ls_]

    y_s, kv_s, st_a_s, st_f_s = _trunk(tmaj(x_sample), mods_s, modkv_s, hist_a_s, hist_f_s, w, sample_attn,
                                       tm=rows, rd=nseq)
    y_s = smaj(y_s, d)
    kv_s = smaj(kv_s, 2 * ATT_WIDTH).reshape(nseq, t_new, 2, N_HEADS, HEAD_DIM)
    kv_bufs_s = [jnp.concatenate([caches[g][:, t_new:], kv_s[:, :, :, g * hpg:(g + 1) * hpg]], axis=1)
                 for g in range(N_GROUPS)]
    conv_a_s = smaj(st_a_s, d)[None]
    conv_f_s = jnp.stack([smaj(s, 2 * D_FF) for s in st_f_s])

    return (y_p, y_s, kv_bufs_p[0], kv_bufs_p[1], kv_bufs_p[2], conv_a_p, conv_f_p,
            kv_bufs_s[0], kv_bufs_s[1], kv_bufs_s[2], conv_a_s, conv_f_s)
```

```python
import functools
import math

import jax
import jax.numpy as jnp
from jax import lax
from jax.experimental import pallas as pl
from jax.experimental.pallas import tpu as pltpu

D_MODEL = 1024
DEPTH = 2
HEAD_DIM = 64
HEADS_PER_GROUP = 4
DIL_GROUPS = ((128, 1), (512, 4), (2048, 16))
N_GROUPS = len(DIL_GROUPS)
N_HEADS = HEADS_PER_GROUP * N_GROUPS
ATT_WIDTH = N_HEADS * HEAD_DIM
GROUP_WIDTH = HEADS_PER_GROUP * HEAD_DIM
SPAN = 128
N_BUCKETS = 32
MAX_DISTANCE = 2048
CONV_A_WIDTH = 31
D_FF = 2816
CONV_F_WIDTH = 3
N_MOD = 6
EPS = 1e-6
LN_EPS = 1e-5
NEG_INF = -1e30

BF = jnp.bfloat16
F32 = jnp.float32

HEAD_SHIFT = HEAD_DIM.bit_length() - 1
SUBLANES = 8
SUBLANE_SHIFT = SUBLANES.bit_length() - 1
LANES = 128
BF16_ROWS = 16
ROW_UNROLL = 4
VMEM_LIMIT = 56 * 1024 * 1024

TM_PROMPT = 512
FF_CHUNK = 1408
SAMPLE_QROWS = HEADS_PER_GROUP * SUBLANES


def _round_up(a, b):
    return -(-a // b) * b


def _params(sem):
    return pltpu.CompilerParams(dimension_semantics=sem, vmem_limit_bytes=VMEM_LIMIT)


def _const_spec(shape):
    nd = len(shape)
    return pl.BlockSpec(shape, lambda *_: (0,) * nd, pipeline_mode=pl.Buffered(1))


def _dot(a, b):
    return jnp.dot(a, b, preferred_element_type=F32)


def _dot_nt(a, b):
    return lax.dot_general(a, b, (((1,), (1,)), ((), ())), preferred_element_type=F32)


def _sigmoid(v):
    return 1.0 / (1.0 + jnp.exp(-v))


def _mod_rows(mod_ref, idx, r0, rows, per_row):
    if per_row:
        return mod_ref[0, idx, pl.ds(r0, rows), :]
    return mod_ref[0, idx]


def _rms_mod_rows(x_ref, tm, targets, per_row):
    rc = BF16_ROWS

    def body(i, c):
        r0 = pl.multiple_of(i * rc, rc)
        x = x_ref[0, pl.ds(r0, rc), :]
        xn = x * lax.rsqrt(jnp.mean(x * x, axis=-1, keepdims=True) + EPS)
        for gain_ref, mod_ref, i_sh, i_sc, dst_ref in targets:
            sh = _mod_rows(mod_ref, i_sh, r0, rc, per_row)
            sc = _mod_rows(mod_ref, i_sc, r0, rc, per_row)
            dst_ref[pl.ds(r0, rc), :] = ((xn * gain_ref[...]) * (1.0 + sc) + sh).astype(BF)
        return c

    lax.fori_loop(0, tm // rc, body, 0, unroll=ROW_UNROLL)


def _shift_conv(load, weights, offsets, rows):
    out = None
    for s in range(SUBLANES):
        taps = [k for k in range(len(offsets)) if offsets[k] % SUBLANES == s]
        if not taps:
            continue
        n = rows + (SUBLANES if s else 0)
        g = None
        for k in taps:
            term = weights[k] * load(offsets[k] - s, n)
            g = term if g is None else g + term
        if s:
            g = g[s:s + rows]
        out = g if out is None else out + g
    return out


def _row_loader(ref, lead, r0, c0, cc, need_rows):
    if need_rows * cc <= 16 * SUBLANES * LANES:
        ext = ref[lead + (pl.ds(r0, need_rows), slice(c0, c0 + cc))]
        return lambda a, n: ext[a:a + n]
    return lambda a, n: ref[lead + (pl.ds(r0 + a, n), slice(c0, c0 + cc))]


def _mod_kernel(c_ref, w_ref, b_ref, o_ref):
    c = c_ref[...]
    a = (c * _sigmoid(c)).astype(BF)
    o_ref[0] = _dot(a, w_ref[0].astype(BF)) + b_ref[0]


def _mod_call(c_all, w, b, tn):
    nl, d, n = w.shape
    mp = c_all.shape[0]
    return pl.pallas_call(
        _mod_kernel,
        out_shape=jax.ShapeDtypeStruct((nl, mp, n), F32),
        grid=(nl, n // tn),
        in_specs=[
            pl.BlockSpec((mp, d), lambda l, j: (0, 0)),
            pl.BlockSpec((1, d, tn), lambda l, j: (l, 0, j)),
            pl.BlockSpec((1, 1, tn), lambda l, j: (l, 0, j)),
        ],
        out_specs=pl.BlockSpec((1, mp, tn), lambda l, j: (l, 0, j)),
        compiler_params=_params(("arbitrary", "arbitrary")),
        name="adaln_mod",
    )(c_all, w, b)


def _conformer_kernel(x_ref, mod_ref, hist_ref, nrm_ref, w1_ref, b1_ref, dw_ref, dwb_ref, lng_ref, lnb_ref,
                      w2_ref, b2_ref, o_ref, st_ref, h_ref, u_ref, full_ref, y_ref, *, tm, rd, per_row):
    d = D_MODEL
    hh = (CONV_A_WIDTH - 1) * rd
    hp = _round_up(hh, SUBLANES)
    t = pl.program_id(1)

    @pl.when(t == 0)
    def _():
        full_ref[0:hp, :] = jnp.zeros((hp, d), F32)
        full_ref[hp - hh:hp, :] = hist_ref[0]

    _rms_mod_rows(x_ref, tm, [(nrm_ref, mod_ref, 0, 1, h_ref)], per_row)
    u_ref[...] = _dot(h_ref[...], w1_ref[...])

    rc, cc = 16, 512

    def glu_body(i, c):
        r0 = pl.multiple_of(i * rc, rc)
        for c0 in range(0, d, cc):
            a = u_ref[pl.ds(r0, rc), c0:c0 + cc] + b1_ref[:, c0:c0 + cc]
            g = u_ref[pl.ds(r0, rc), d + c0:d + c0 + cc] + b1_ref[:, d + c0:d + c0 + cc]
            full_ref[pl.ds(hp + r0, rc), c0:c0 + cc] = a * _sigmoid(g)
        return c

    lax.fori_loop(0, tm // rc, glu_body, 0)

    rcv, ccv = 32, LANES
    offsets = [hp - hh + k * rd for k in range(CONV_A_WIDTH)]

    def conv_body(i, c):
        r0 = pl.multiple_of(i * rcv, rcv)
        for c0 in range(0, d, ccv):
            load = _row_loader(full_ref, (), r0, c0, ccv, hp + rcv)
            w = [dw_ref[k:k + 1, c0:c0 + ccv] for k in range(CONV_A_WIDTH)]
            y_ref[pl.ds(r0, rcv), c0:c0 + ccv] = _shift_conv(load, w, offsets, rcv) + dwb_ref[:, c0:c0 + ccv]
        return c

    lax.fori_loop(0, tm // rcv, conv_body, 0)

    def ln_body(i, c):
        r0 = pl.multiple_of(i * BF16_ROWS, BF16_ROWS)
        y = y_ref[pl.ds(r0, BF16_ROWS), :]
        mu = jnp.mean(y, axis=-1, keepdims=True)
        dv = y - mu
        var = jnp.mean(dv * dv, axis=-1, keepdims=True)
        yn = dv * lax.rsqrt(var + LN_EPS) * lng_ref[...] + lnb_ref[...]
        h_ref[pl.ds(r0, BF16_ROWS), :] = (yn * _sigmoid(yn)).astype(BF)
        return c

    lax.fori_loop(0, tm // BF16_ROWS, ln_body, 0, unroll=ROW_UNROLL)

    out = _dot(h_ref[...], w2_ref[...]) + b2_ref[...]
    o_ref[0] = x_ref[0] + mod_ref[0, 2] * out

    new_hist = full_ref[hp + tm - hh:hp + tm, :]
    st_ref[0] = new_hist
    full_ref[hp - hh:hp, :] = new_hist


def _conformer_call(x, mods, hist, nrm, w1, b1, dw, dwb, lng, lnb, w2, b2, *, tm, rd):
    ns, s, d = x.shape
    r = mods.shape[2]
    hh = hist.shape[1]
    hp = _round_up(hh, SUBLANES)
    per_row = r > 1
    assert s % tm == 0 and (not per_row or (r == tm and s == tm))
    kern = functools.partial(_conformer_kernel, tm=tm, rd=rd, per_row=per_row)
    return pl.pallas_call(
        kern,
        out_shape=(jax.ShapeDtypeStruct((ns, s, d), F32), jax.ShapeDtypeStruct((ns, hh, d), F32)),
        grid=(ns, s // tm),
        in_specs=[
            pl.BlockSpec((1, tm, d), lambda n, t: (n, t, 0)),
            pl.BlockSpec((1, N_MOD, r, d), lambda n, t: (n, 0, 0, 0)),
            pl.BlockSpec((1, hh, d), lambda n, t: (n, 0, 0)),
            _const_spec((1, d)),
            _const_spec((d, 2 * d)),
            _const_spec((1, 2 * d)),
            _const_spec((CONV_A_WIDTH, d)),
            _const_spec((1, d)),
            _const_spec((1, d)),
            _const_spec((1, d)),
            _const_spec((d, d)),
            _const_spec((1, d)),
        ],
        out_specs=(
            pl.BlockSpec((1, tm, d), lambda n, t: (n, t, 0)),
            pl.BlockSpec((1, hh, d), lambda n, t: (n, 0, 0)),
        ),
        scratch_shapes=[
            pltpu.VMEM((tm, d), BF),
            pltpu.VMEM((tm, 2 * d), F32),
            pltpu.VMEM((hp + tm, d), F32),
            pltpu.VMEM((tm, d), F32),
        ],
        compiler_params=_params(("arbitrary", "arbitrary")),
        name="conformer_mixer",
    )(x, mods, hist, nrm, w1, b1, dw, dwb, lng, lnb, w2, b2)


def _ffn_kernel(x_ref, mod_ref, hg_ref, hv_ref, nrm_ref, wug_ref, wuv_ref, cwg_ref, cwv_ref, cbg_ref, cbv_ref,
                wd_ref, nf_ref, o_ref, st_ref, h_ref, ubuf_ref, carry_ref, act_ref,
                *, tm, rd, per_row, nj, fc, final_norm):
    hh = (CONV_F_WIDTH - 1) * rd
    hp = _round_up(hh, SUBLANES)
    t = pl.program_id(1)
    j = pl.program_id(2)

    @pl.when(j == 0)
    def _():
        _rms_mod_rows(x_ref, tm, [(nrm_ref, mod_ref, 3, 4, h_ref)], per_row)

    halves = ((hg_ref, wug_ref, cwg_ref, cbg_ref), (hv_ref, wuv_ref, cwv_ref, cbv_ref))
    for half, (hist_ref, wu_ref, _, _) in enumerate(halves):
        @pl.when(t == 0)
        def _():
            ubuf_ref[half, 0:hp, :] = jnp.zeros((hp, fc), F32)
            ubuf_ref[half, hp - hh:hp, :] = hist_ref[0]

        @pl.when(t > 0)
        def _():
            ubuf_ref[half, 0:hp, :] = carry_ref[half, j]

        ubuf_ref[half, hp:hp + tm, :] = _dot(h_ref[...], wu_ref[...])
        st_ref[0, half * nj + j] = ubuf_ref[half, hp + tm - hh:hp + tm, :]
        carry_ref[half, j] = ubuf_ref[half, tm:tm + hp, :]

    rc, cc = 32, LANES
    offsets = [hp - hh + k * rd for k in range(CONV_F_WIDTH)]

    def act_body(i, c):
        r0 = pl.multiple_of(i * rc, rc)
        for c0 in range(0, fc, cc):
            ys = []
            for half, (_, _, cw_ref, cb_ref) in enumerate(halves):
                load = _row_loader(ubuf_ref, (half,), r0, c0, cc, hp + rc)
                w = [cw_ref[k:k + 1, c0:c0 + cc] for k in range(CONV_F_WIDTH)]
                ys.append(_shift_conv(load, w, offsets, rc) + cb_ref[:, c0:c0 + cc])
            yg, yv = ys
            act_ref[pl.ds(r0, rc), c0:c0 + cc] = (yg * _sigmoid(yg) * yv).astype(BF)
        return c

    lax.fori_loop(0, tm // rc, act_body, 0)

    part = _dot(act_ref[...], wd_ref[...])

    @pl.when(j == 0)
    def _():
        o_ref[0] = part

    @pl.when(j > 0)
    def _():
        o_ref[0] += part

    @pl.when(j == nj - 1)
    def _():
        rows = BF16_ROWS

        def fin_body(i, c):
            r0 = pl.multiple_of(i * rows, rows)
            gate = _mod_rows(mod_ref, 5, r0, rows, per_row)
            xo = x_ref[0, pl.ds(r0, rows), :] + gate * o_ref[0, pl.ds(r0, rows), :]
            if final_norm:
                xo = xo * lax.rsqrt(jnp.mean(xo * xo, axis=-1, keepdims=True) + EPS) * nf_ref[...]
            o_ref[0, pl.ds(r0, rows), :] = xo
            return c

        lax.fori_loop(0, tm // rows, fin_body, 0, unroll=ROW_UNROLL)


def _ffn_call(x, mods, hist, nrm, wup, cw, cb, wdown, nf, *, tm, rd, final_norm):
    ns, s, d = x.shape
    r = mods.shape[2]
    hh = hist.shape[1]
    hp = _round_up(hh, SUBLANES)
    f = wdown.shape[0]
    fc = FF_CHUNK
    nj = f // fc
    per_row = r > 1
    assert s % tm == 0 and f % fc == 0 and (not per_row or (r == tm and s == tm))
    kern = functools.partial(_ffn_kernel, tm=tm, rd=rd, per_row=per_row, nj=nj, fc=fc, final_norm=final_norm)
    y, st = pl.pallas_call(
        kern,
        out_shape=(jax.ShapeDtypeStruct((ns, s, d), F32), jax.ShapeDtypeStruct((ns, 2 * nj, hh, fc), F32)),
        grid=(ns, s // tm, nj),
        in_specs=[
            pl.BlockSpec((1, tm, d), lambda n, t, j: (n, t, 0)),
            pl.BlockSpec((1, N_MOD, r, d), lambda n, t, j: (n, 0, 0, 0)),
            pl.BlockSpec((1, hh, fc), lambda n, t, j: (n, 0, j)),
            pl.BlockSpec((1, hh, fc), lambda n, t, j: (n, 0, nj + j)),
            _const_spec((1, d)),
            pl.BlockSpec((d, fc), lambda n, t, j: (0, j)),
            pl.BlockSpec((d, fc), lambda n, t, j: (0, nj + j)),
            pl.BlockSpec((CONV_F_WIDTH, fc), lambda n, t, j: (0, j)),
            pl.BlockSpec((CONV_F_WIDTH, fc), lambda n, t, j: (0, nj + j)),
            pl.BlockSpec((1, fc), lambda n, t, j: (0, j)),
            pl.BlockSpec((1, fc), lambda n, t, j: (0, nj + j)),
            pl.BlockSpec((fc, d), lambda n, t, j: (j, 0)),
            _const_spec((1, d)),
        ],
        out_specs=(
            pl.BlockSpec((1, tm, d), lambda n, t, j: (n, t, 0)),
            pl.BlockSpec((1, 2 * nj, hh, fc), lambda n, t, j: (n, 0, 0, 0)),
        ),
        scratch_shapes=[
            pltpu.VMEM((tm, d), BF),
            pltpu.VMEM((2, hp + tm, fc), F32),
            pltpu.VMEM((2, nj, hp, fc), F32),
            pltpu.VMEM((tm, fc), BF),
        ],
        compiler_params=_params(("arbitrary", "arbitrary", "arbitrary")),
        name="conv_ffn",
    )(x, mods, hist, hist, nrm, wup, wup, cw, cw, cb, cb, wdown, nf)
    return y, jnp.swapaxes(st, 1, 2).reshape(ns, hh, 2 * f)


def _qkv_kernel(x_ref, mod_ref, modkv_ref, nq_ref, nkv_ref, wq_ref, wkv_ref, q_ref, kv_ref, hq_ref, hkv_ref,
                *, tm, per_row):
    _rms_mod_rows(x_ref, tm, [(nq_ref, mod_ref, 0, 1, hq_ref), (nkv_ref, modkv_ref, 0, 1, hkv_ref)], per_row)
    q_ref[0] = _dot(hq_ref[...], wq_ref[...])
    kv_ref[0] = _dot(hkv_ref[...], wkv_ref[...])


def _qkv_call(x, mods, modkv, nq, nkv, wq, wkv, *, tm):
    ns, s, d = x.shape
    r = mods.shape[2]
    per_row = r > 1
    kern = functools.partial(_qkv_kernel, tm=tm, per_row=per_row)
    return pl.pallas_call(
        kern,
        out_shape=(jax.ShapeDtypeStruct((ns, s, ATT_WIDTH), F32), jax.ShapeDtypeStruct((ns, s, 2 * ATT_WIDTH), F32)),
        grid=(ns, s // tm),
        in_specs=[
            pl.BlockSpec((1, tm, d), lambda n, t: (n, t, 0)),
            pl.BlockSpec((1, N_MOD, r, d), lambda n, t: (n, 0, 0, 0)),
            pl.BlockSpec((1, 2, r, d), lambda n, t: (n, 0, 0, 0)),
            _const_spec((1, d)),
            _const_spec((1, d)),
            _const_spec((d, ATT_WIDTH)),
            _const_spec((d, 2 * ATT_WIDTH)),
        ],
        out_specs=(
            pl.BlockSpec((1, tm, ATT_WIDTH), lambda n, t: (n, t, 0)),
            pl.BlockSpec((1, tm, 2 * ATT_WIDTH), lambda n, t: (n, t, 0)),
        ),
        scratch_shapes=[pltpu.VMEM((tm, d), BF), pltpu.VMEM((tm, d), BF)],
        compiler_params=_params(("arbitrary", "arbitrary")),
        name="qkv_proj",
    )(x, mods, modkv, nq, nkv, wq, wkv)


def _rel_bucket(dist):
    max_exact = N_BUCKETS // 2
    dd = jnp.maximum(dist, 1).astype(F32)
    large = max_exact + (jnp.log(dd / max_exact) / math.log(MAX_DISTANCE / max_exact)
                         * (N_BUCKETS - max_exact)).astype(jnp.int32)
    large = jnp.minimum(large, N_BUCKETS - 1)
    return jnp.where(dist < max_exact, dist, large)


def _prompt_bucket_table(dil):
    qi = jnp.arange(SPAN, dtype=jnp.int32)[:, None]
    ki = jnp.arange(2 * SPAN, dtype=jnp.int32)[None, :]
    m = qi + SPAN - ki
    valid = (m >= 0) & (m <= SPAN)
    return jnp.where(valid, _rel_bucket(jnp.clip(m, 0, SPAN) * dil), -1).astype(jnp.int32)


def _sample_bucket_tables(dil, width, t_new):
    row = jnp.arange(SAMPLE_QROWS, dtype=jnp.int32)[:, None]
    t = (row % SUBLANES) % t_new
    tn = jnp.arange(SUBLANES, dtype=jnp.int32)[None, :]

    def table(dist, ok):
        ok = ok & (dist >= 0) & (dist % dil == 0) & (dist // dil <= SPAN)
        return jnp.where(ok, _rel_bucket(jnp.clip(dist, 0, SPAN * dil)), -1).astype(jnp.int32)

    old = table(width + t - jnp.arange(width, dtype=jnp.int32)[None, :], True)
    new = table(t - tn, tn < t_new)
    return old, new


def _bias_from_buckets(bkt, relb_ref, head):
    acc = jnp.full(bkt.shape, NEG_INF, F32)
    for b in range(N_BUCKETS):
        acc = jnp.where(bkt == b, relb_ref[b, head], acc)
    return acc


def _pattn_kernel(relb_ref, bkt_ref, q0_ref, q1_ref, kp0_ref, kp1_ref, kc0_ref, kc1_ref, vp0_ref, vp1_ref,
                  vc0_ref, vc1_ref, o0_ref, o1_ref, l0_ref, l1_ref, bias_ref, *, g, dil):
    b = pl.program_id(0)
    i = pl.program_id(1)
    hpg = HEADS_PER_GROUP

    @pl.when((b == 0) & (i == 0))
    def _():
        bkt = bkt_ref[...]
        for h in range(hpg):
            bias_ref[h] = _bias_from_buckets(bkt, relb_ref, g * hpg + h)

    lanehead = lax.broadcasted_iota(jnp.int32, (SPAN, GROUP_WIDTH), 1) >> HEAD_SHIFT
    col = lax.broadcasted_iota(jnp.int32, (SPAN, 2 * SPAN), 1)
    keep = col >= jnp.where(i > 0, 0, SPAN)

    def body(r, c):
        rows = pl.ds(r, SPAN, stride=dil)
        both = lambda r0, r1: jnp.concatenate([r0[0, rows, :], r1[0, rows, :]], axis=1)
        q = both(q0_ref, q1_ref) * (HEAD_DIM ** -0.5)
        lhs = jnp.concatenate([jnp.where(lanehead == h, q, 0.0).astype(BF) for h in range(hpg)], axis=0)
        kcat = jnp.concatenate([both(kp0_ref, kp1_ref), both(kc0_ref, kc1_ref)], axis=0).astype(BF)
        s = _dot_nt(lhs, kcat)

        ps, ms, sums = [], [], []
        for h in range(hpg):
            logit = jnp.where(keep, s[h * SPAN:(h + 1) * SPAN] + bias_ref[h], NEG_INF)
            m = jnp.max(logit, axis=-1, keepdims=True)
            e = jnp.exp(logit - m)
            ps.append(e.astype(BF))
            ms.append(m)
            sums.append(jnp.sum(e, axis=-1, keepdims=True))
        vcat = jnp.concatenate([both(vp0_ref, vp1_ref), both(vc0_ref, vc1_ref)], axis=0).astype(BF)
        pv = _dot(jnp.concatenate(ps, axis=0), vcat)

        o = jnp.zeros((SPAN, GROUP_WIDTH), F32)
        l = jnp.zeros((SPAN, GROUP_WIDTH), F32)
        for h in range(hpg):
            o = jnp.where(lanehead == h, pv[h * SPAN:(h + 1) * SPAN] * (1.0 / sums[h]), o)
            l = jnp.where(lanehead == h, ms[h] + jnp.log(sums[h]), l)
        o0_ref[0, rows, :] = o[:, :LANES]
        o1_ref[0, rows, :] = o[:, LANES:]
        l0_ref[0, rows, :] = l[:, :LANES]
        l1_ref[0, rows, :] = l[:, LANES:]
        return c

    lax.fori_loop(0, dil, body, 0)


def _pattn_call(q, kv, rel_bias, g, dil):
    bsz, s, _ = q.shape
    rows = dil * SPAN
    assert s % rows == 0 and GROUP_WIDTH == 2 * LANES
    blk = (1, rows, LANES)
    kcol = 2 * g
    vcol = ATT_WIDTH // LANES + 2 * g
    prev = lambda i: jnp.maximum(i - 1, 0)
    out = jax.ShapeDtypeStruct((bsz, s, LANES), F32)
    ospec = pl.BlockSpec(blk, lambda b, i: (b, i, 0))
    res = pl.pallas_call(
        functools.partial(_pattn_kernel, g=g, dil=dil),
        out_shape=(out,) * 4,
        grid=(bsz, s // rows),
        in_specs=[
            pl.BlockSpec(memory_space=pltpu.SMEM),
            _const_spec((SPAN, 2 * SPAN)),
            pl.BlockSpec(blk, lambda b, i: (b, i, kcol)),
            pl.BlockSpec(blk, lambda b, i: (b, i, kcol + 1)),
            pl.BlockSpec(blk, lambda b, i: (b, prev(i), kcol)),
            pl.BlockSpec(blk, lambda b, i: (b, prev(i), kcol + 1)),
            pl.BlockSpec(blk, lambda b, i: (b, i, kcol)),
            pl.BlockSpec(blk, lambda b, i: (b, i, kcol + 1)),
            pl.BlockSpec(blk, lambda b, i: (b, prev(i), vcol)),
            pl.BlockSpec(blk, lambda b, i: (b, prev(i), vcol + 1)),
            pl.BlockSpec(blk, lambda b, i: (b, i, vcol)),
            pl.BlockSpec(blk, lambda b, i: (b, i, vcol + 1)),
        ],
        out_specs=(ospec,) * 4,
        scratch_shapes=[pltpu.VMEM((HEADS_PER_GROUP, SPAN, 2 * SPAN), F32)],
        compiler_params=_params(("arbitrary", "arbitrary")),
        name=f"prompt_attn_g{g}",
    )(rel_bias, _prompt_bucket_table(dil), q, q, kv, kv, kv, kv, kv, kv, kv, kv)
    return list(res[:2]), list(res[2:])


def _sattn_kernel(relb_ref, bo0_ref, bo1_ref, bo2_ref, bn_ref, q_ref, kvn_ref, kvt_ref, c0_ref, c1_ref, c2_ref,
                  o0_ref, o1_ref, o2_ref, l0_ref, l1_ref, l2_ref, n0_ref, n1_ref, n2_ref,
                  bias0_ref, bias1_ref, bias2_ref, biasn_ref, tail_ref, *, t_new):
    hpg = HEADS_PER_GROUP
    gw = GROUP_WIDTH
    rows = SAMPLE_QROWS
    bo_refs = (bo0_ref, bo1_ref, bo2_ref)
    bias_refs = (bias0_ref, bias1_ref, bias2_ref)
    caches = (c0_ref, c1_ref, c2_ref)
    outs = ((o0_ref, l0_ref, n0_ref), (o1_ref, l1_ref, n1_ref), (o2_ref, l2_ref, n2_ref))

    @pl.when(pl.program_id(0) == 0)
    def _():
        tail_ref[...] = jnp.zeros(tail_ref.shape, F32)
        for g in range(N_GROUPS):
            for h in range(hpg):
                sl = slice(h * SUBLANES, (h + 1) * SUBLANES)
                bias_refs[g][sl, :] = _bias_from_buckets(bo_refs[g][sl, :], relb_ref, g * hpg + h)
                biasn_ref[g, sl, :] = _bias_from_buckets(bn_ref[g, sl, :], relb_ref, g * hpg + h)

    lanehead = lax.broadcasted_iota(jnp.int32, (rows, gw), 1) >> HEAD_SHIFT
    rowhead = lax.broadcasted_iota(jnp.int32, (rows, gw), 0) >> SUBLANE_SHIFT
    own = lanehead == rowhead
    tail_lane = lax.broadcasted_iota(jnp.int32, (gw, LANES), 1) >= LANES - t_new

    for g in range(N_GROUPS):
        cref = caches[g]
        o_ref, l_ref, n_ref = outs[g]
        width = cref.shape[-1]
        qm = jnp.where(own, q_ref[0, :, g * gw:(g + 1) * gw] * (HEAD_DIM ** -0.5), 0.0).astype(BF)
        knew = kvn_ref[0, :, g * gw:(g + 1) * gw].astype(BF)
        vnew = kvn_ref[0, :, ATT_WIDTH + g * gw:ATT_WIDTH + (g + 1) * gw].astype(BF)
        lo = _dot(qm, cref[0, 0].astype(BF)) + bias_refs[g][...]
        ln = _dot_nt(qm, knew) + biasn_ref[g]
        m = jnp.maximum(jnp.max(lo, axis=-1, keepdims=True), jnp.max(ln, axis=-1, keepdims=True))
        eo = jnp.exp(lo - m)
        en = jnp.exp(ln - m)
        ssum = jnp.sum(eo, axis=-1, keepdims=True) + jnp.sum(en, axis=-1, keepdims=True)
        pv = _dot_nt(eo.astype(BF), cref[0, 1].astype(BF)) + _dot(en.astype(BF), vnew)
        om = jnp.where(own, pv * (1.0 / ssum), 0.0)
        lm = jnp.where(own, m + jnp.log(ssum), 0.0)
        o_acc = om[0:SUBLANES]
        l_acc = lm[0:SUBLANES]
        for h in range(1, hpg):
            o_acc = o_acc + om[h * SUBLANES:(h + 1) * SUBLANES]
            l_acc = l_acc + lm[h * SUBLANES:(h + 1) * SUBLANES]
        o_ref[0] = o_acc
        l_ref[0] = l_acc

        for kv in range(2):
            rolled = pltpu.roll(cref[0, kv], width - t_new, 1)
            tail_ref[:, 0:SUBLANES] = kvt_ref[0, kv * ATT_WIDTH + g * gw:kv * ATT_WIDTH + (g + 1) * gw, :]
            tail = pltpu.roll(tail_ref[...], LANES - t_new, 1)
            if width > LANES:
                n_ref[0, kv, :, 0:width - LANES] = rolled[:, 0:width - LANES]
            n_ref[0, kv, :, width - LANES:width] = jnp.where(tail_lane, tail, rolled[:, width - LANES:width])


def _sattn_call(q, kvn, caches, rel_bias):
    n, t_new, _ = q.shape
    gw = GROUP_WIDTH
    hpg = HEADS_PER_GROUP
    assert t_new <= SUBLANES
    views, bkt_old, bkt_new = [], [], []
    for g, (window, dil) in enumerate(DIL_GROUPS):
        width = caches[g].shape[1]
        assert width == window and width % LANES == 0
        views.append(jnp.transpose(caches[g], (0, 2, 3, 4, 1)).reshape(n, 2, gw, width))
        old, new = _sample_bucket_tables(dil, width, t_new)
        bkt_old.append(old)
        bkt_new.append(new)
    pad = SUBLANES - t_new
    qp = jnp.tile(jnp.pad(q, ((0, 0), (0, pad), (0, 0))), (1, hpg, 1))
    kvp = jnp.pad(kvn, ((0, 0), (0, pad), (0, 0)))
    kvt = jnp.swapaxes(kvp, 1, 2)
    out = jax.ShapeDtypeStruct((n, SUBLANES, gw), F32)
    oblk = pl.BlockSpec((1, SUBLANES, gw), lambda i: (i, 0, 0))
    cspecs = [pl.BlockSpec((1, 2, gw, v.shape[-1]), lambda i: (i, 0, 0, 0)) for v in views]
    res = pl.pallas_call(
        functools.partial(_sattn_kernel, t_new=t_new),
        out_shape=(out,) * 6 + tuple(jax.ShapeDtypeStruct(v.shape, F32) for v in views),
        grid=(n,),
        in_specs=[pl.BlockSpec(memory_space=pltpu.SMEM)]
        + [_const_spec(b.shape) for b in bkt_old]
        + [
            _const_spec((N_GROUPS, SAMPLE_QROWS, SUBLANES)),
            pl.BlockSpec((1, SAMPLE_QROWS, ATT_WIDTH), lambda i: (i, 0, 0)),
            pl.BlockSpec((1, SUBLANES, 2 * ATT_WIDTH), lambda i: (i, 0, 0)),
            pl.BlockSpec((1, 2 * ATT_WIDTH, SUBLANES), lambda i: (i, 0, 0)),
        ]
        + cspecs,
        out_specs=(oblk,) * 6 + tuple(cspecs),
        scratch_shapes=[pltpu.VMEM(b.shape, F32) for b in bkt_old]
        + [pltpu.VMEM((N_GROUPS, SAMPLE_QROWS, SUBLANES), F32), pltpu.VMEM((gw, LANES), F32)],
        compiler_params=_params(("arbitrary",)),
        name="sample_attn",
    )(rel_bias, *bkt_old, jnp.stack(bkt_new), qp, kvp, kvt, *views)
    new_caches = [jnp.transpose(c.reshape(n, 2, hpg, HEAD_DIM, c.shape[-1]), (0, 4, 1, 2, 3)) for c in res[6:]]
    return [a[:, :t_new] for a in res[:3]], [a[:, :t_new] for a in res[3:6]], new_caches


def _oproj_kernel(x_ref, mod_ref, *refs, tm):
    nblk = ATT_WIDTH // LANES
    halves = nblk // N_GROUPS
    o_refs, l_refs = refs[:nblk], refs[nblk:2 * nblk]
    wo_ref, out_ref, att_ref = refs[2 * nblk:]
    rc = BF16_ROWS

    def body(i, c):
        r0 = pl.multiple_of(i * rc, rc)
        for hf in range(halves):
            blks = [g * halves + hf for g in range(N_GROUPS)]
            ls = [l_refs[k][0, pl.ds(r0, rc), :] for k in blks]
            mx = functools.reduce(jnp.maximum, ls)
            es = [jnp.exp(v - mx) for v in ls]
            inv = 1.0 / functools.reduce(lambda a, b: a + b, es)
            for k, e in zip(blks, es):
                att_ref[pl.ds(r0, rc), k * LANES:(k + 1) * LANES] = (o_refs[k][0, pl.ds(r0, rc), :] * (e * inv)).astype(BF)
        return c

    lax.fori_loop(0, tm // rc, body, 0, unroll=2)
    out_ref[0] = x_ref[0] + mod_ref[0, 2] * _dot(att_ref[...], wo_ref[...])


def _oproj_call(x, mods, os_, ls_, wo, *, tm):
    ns, s, d = x.shape
    r = mods.shape[2]
    kern = functools.partial(_oproj_kernel, tm=tm)
    gspec = pl.BlockSpec((1, tm, LANES), lambda n, t: (n, t, 0))
    return pl.pallas_call(
        kern,
        out_shape=jax.ShapeDtypeStruct((ns, s, d), F32),
        grid=(ns, s // tm),
        in_specs=[
            pl.BlockSpec((1, tm, d), lambda n, t: (n, t, 0)),
            pl.BlockSpec((1, N_MOD, r, d), lambda n, t: (n, 0, 0, 0)),
        ] + [gspec] * (2 * ATT_WIDTH // LANES) + [_const_spec((ATT_WIDTH, d))],
        out_specs=pl.BlockSpec((1, tm, d), lambda n, t: (n, t, 0)),
        scratch_shapes=[pltpu.VMEM((tm, ATT_WIDTH), BF)],
        compiler_params=_params(("arbitrary", "arbitrary")),
        name="attn_out_proj",
    )(x, mods, *os_, *ls_, wo)


def _trunk(x, mods, modkv, hist_a, hist_f, w, attn_fn, *, tm, rd):
    row = lambda v: v.reshape(1, -1)
    x, st_a = _conformer_call(x, mods[0], hist_a, row(w["norm_mix"][0]), w["a_w1"], row(w["a_b1"][0]),
                              w["a_dw"][0], row(w["a_dwb"][0]), row(w["a_ln_g"][0]), row(w["a_ln_b"][0]),
                              w["a_w2"], row(w["a_b2"][0]), tm=tm, rd=rd)
    x, st_f0 = _ffn_call(x, mods[0], hist_f[0], row(w["norm_ffn"][0]), w["f_wup"][0], w["f_cw"][0],
                         row(w["f_cb"][0]), w["f_wdown"][0], row(w["norm_f"]), tm=tm, rd=rd, final_norm=False)
    q, kv = _qkv_call(x, mods[1], modkv, row(w["norm_mix"][1]), row(w["norm_kv"]), w["w_q"], w["w_kv"], tm=tm)
    os_, ls_, attn_extra = attn_fn(q, kv)
    x = _oproj_call(x, mods[1], os_, ls_, w["w_o"], tm=tm)
    y, st_f1 = _ffn_call(x, mods[1], hist_f[1], row(w["norm_ffn"][1]), w["f_wup"][1], w["f_cw"][1],
                         row(w["f_cb"][1]), w["f_wdown"][1], row(w["norm_f"]), tm=tm, rd=rd, final_norm=True)
    return y, kv, st_a, [st_f0, st_f1], attn_extra


def kernel(x_prompt, x_sample, cache_kv_w128, cache_kv_w512, cache_kv_w2048, state_conv_a, state_conv_ffn, c_prompt, c_sample, w_mod, b_mod, norm_mix, norm_ffn, a_w1, a_b1, a_dw, a_dwb, a_ln_g, a_ln_b, a_w2, a_b2, w_mod_kv, b_mod_kv, norm_kv, w_kv, w_q, w_o, rel_bias, f_wup, f_cw, f_cb, f_wdown, norm_f):
    d = D_MODEL
    bsz, seq, _ = x_prompt.shape
    nseq, t_new, _ = x_sample.shape
    caches = (cache_kv_w128, cache_kv_w512, cache_kv_w2048)
    hpg = HEADS_PER_GROUP

    w = dict(norm_mix=norm_mix, norm_ffn=norm_ffn, a_w1=a_w1[0].astype(BF), a_b1=a_b1, a_dw=a_dw, a_dwb=a_dwb,
             a_ln_g=a_ln_g, a_ln_b=a_ln_b, a_w2=a_w2[0].astype(BF), a_b2=a_b2, norm_kv=norm_kv,
             w_kv=w_kv.astype(BF), w_q=w_q[0].astype(BF), w_o=w_o[0].astype(BF), f_wup=f_wup.astype(BF),
             f_cw=f_cw, f_cb=f_cb, f_wdown=f_wdown.astype(BF), norm_f=norm_f)

    n_c = bsz + nseq
    mp = _round_up(n_c, SUBLANES)
    c_all = jnp.pad(jnp.concatenate([c_prompt, c_sample], axis=0), ((0, mp - n_c), (0, 0)))
    mod = _mod_call(c_all, w_mod, b_mod.reshape(DEPTH, 1, N_MOD * d), tn=1536)
    modkv = _mod_call(c_all, w_mod_kv[None], b_mod_kv.reshape(1, 1, 2 * d), tn=1024)[0]

    mods_p = [mod[l, :bsz].reshape(bsz, N_MOD, 1, d) for l in range(DEPTH)]
    modkv_p = modkv[:bsz].reshape(bsz, 2, 1, d)
    hist_a_p = jnp.zeros((bsz, CONV_A_WIDTH - 1, d), F32)
    hist_f_p = [jnp.zeros((bsz, CONV_F_WIDTH - 1, 2 * D_FF), F32)] * DEPTH

    def prompt_attn(q, kv):
        res = [_pattn_call(q, kv, rel_bias, g, dil) for g, (_, dil) in enumerate(DIL_GROUPS)]
        return [a for r in res for a in r[0]], [a for r in res for a in r[1]], None

    y_p, kv_p, st_a_p, st_f_p, _ = _trunk(x_prompt, mods_p, modkv_p, hist_a_p, hist_f_p, w, prompt_attn,
                                          tm=TM_PROMPT, rd=1)
    wmax = max(c.shape[1] for c in caches)
    assert seq >= wmax
    kv_tail = kv_p[:, seq - wmax:].reshape(bsz, wmax, 2, N_HEADS, HEAD_DIM)
    kv_bufs_p = [kv_tail[:, wmax - c.shape[1]:, :, g * hpg:(g + 1) * hpg] for g, c in enumerate(caches)]
    conv_a_p = st_a_p[None]
    conv_f_p = jnp.stack(st_f_p)

    rows = t_new * nseq
    tmaj = lambda v: jnp.swapaxes(v, 0, 1).reshape(1, -1, v.shape[-1])
    smaj = lambda v, c: jnp.swapaxes(v.reshape(-1, nseq, c), 0, 1)

    def per_row(m, k):
        return jnp.tile(jnp.swapaxes(m.reshape(nseq, k, d), 0, 1), (1, t_new, 1))[None]

    mods_s = [per_row(mod[l, bsz:n_c], N_MOD) for l in range(DEPTH)]
    modkv_s = per_row(modkv[bsz:n_c], 2)
    hist_a_s = tmaj(state_conv_a[0])
    hist_f_s = [tmaj(state_conv_ffn[l]) for l in range(DEPTH)]

    def sample_attn(q, kv):
        os_, ls_, new_caches = _sattn_call(smaj(q, ATT_WIDTH), smaj(kv, 2 * ATT_WIDTH), caches, rel_bias)
        halves = lambda vs: [tmaj(a[..., c0:c0 + LANES]) for a in vs for c0 in range(0, GROUP_WIDTH, LANES)]
        return halves(os_), halves(ls_), new_caches

    y_s, _, st_a_s, st_f_s, kv_bufs_s = _trunk(tmaj(x_sample), mods_s, modkv_s, hist_a_s, hist_f_s, w,
                                               sample_attn, tm=rows, rd=nseq)
    y_s = smaj(y_s, d)
    conv_a_s = smaj(st_a_s, d)[None]
    conv_f_s = jnp.stack([smaj(s, 2 * D_FF) for s in st_f_s])

    return (y_p, y_s, kv_bufs_p[0], kv_bufs_p[1], kv_bufs_p[2], conv_a_p, conv_f_p,
            kv_bufs_s[0], kv_bufs_s[1], kv_bufs_s[2], conv_a_s, conv_f_s)
```

```python
import functools
import math

import jax
import jax.numpy as jnp
from jax import lax
from jax.experimental import pallas as pl
from jax.experimental.pallas import tpu as pltpu

D_MODEL = 1024
DEPTH = 2
HEAD_DIM = 64
HEADS_PER_GROUP = 4
DIL_GROUPS = ((128, 1), (512, 4), (2048, 16))
N_GROUPS = len(DIL_GROUPS)
N_HEADS = HEADS_PER_GROUP * N_GROUPS
ATT_WIDTH = N_HEADS * HEAD_DIM
GROUP_WIDTH = HEADS_PER_GROUP * HEAD_DIM
SPAN = 128
N_BUCKETS = 32
MAX_DISTANCE = 2048
CONV_A_WIDTH = 31
D_FF = 2816
CONV_F_WIDTH = 3
N_MOD = 6
EPS = 1e-6
LN_EPS = 1e-5
NEG_INF = -1e30

BF = jnp.bfloat16
F32 = jnp.float32

HEAD_SHIFT = HEAD_DIM.bit_length() - 1
SUBLANES = 8
SUBLANE_SHIFT = SUBLANES.bit_length() - 1
LANES = 128
BF16_ROWS = 16
ROW_UNROLL = 4
VMEM_LIMIT = 56 * 1024 * 1024

TM_PROMPT = 512
FF_SUB = 256
SAMPLE_QROWS = HEADS_PER_GROUP * SUBLANES


def _round_up(a, b):
    return -(-a // b) * b


def _params(sem):
    return pltpu.CompilerParams(dimension_semantics=sem, vmem_limit_bytes=VMEM_LIMIT)


def _const_spec(shape):
    nd = len(shape)
    return pl.BlockSpec(shape, lambda *_: (0,) * nd, pipeline_mode=pl.Buffered(1))


def _dot(a, b):
    return jnp.dot(a, b, preferred_element_type=F32)


def _dot_nt(a, b):
    return lax.dot_general(a, b, (((1,), (1,)), ((), ())), preferred_element_type=F32)


def _sigmoid(v):
    return 1.0 / (1.0 + jnp.exp(-v))


def _mod_rows(mod_ref, idx, r0, rows, per_row):
    if per_row:
        return mod_ref[0, idx, pl.ds(r0, rows), :]
    return mod_ref[0, idx]


def _rms_mod_rows(x_ref, tm, targets, per_row):
    rc = BF16_ROWS

    def body(i, c):
        r0 = pl.multiple_of(i * rc, rc)
        x = x_ref[0, pl.ds(r0, rc), :]
        xn = x * lax.rsqrt(jnp.mean(x * x, axis=-1, keepdims=True) + EPS)
        for gain_ref, mod_ref, i_sh, i_sc, dst_ref in targets:
            sh = _mod_rows(mod_ref, i_sh, r0, rc, per_row)
            sc = _mod_rows(mod_ref, i_sc, r0, rc, per_row)
            dst_ref[pl.ds(r0, rc), :] = ((xn * gain_ref[...]) * (1.0 + sc) + sh).astype(BF)
        return c

    lax.fori_loop(0, tm // rc, body, 0, unroll=ROW_UNROLL)


def _shift_conv(load, weights, offsets, rows):
    out = None
    for s in range(SUBLANES):
        taps = [k for k in range(len(offsets)) if offsets[k] % SUBLANES == s]
        if not taps:
            continue
        n = rows + (SUBLANES if s else 0)
        g = None
        for k in taps:
            term = weights[k] * load(offsets[k] - s, n)
            g = term if g is None else g + term
        if s:
            g = g[s:s + rows]
        out = g if out is None else out + g
    return out


def _row_loader(ref, lead, r0, c0, cc, need_rows):
    if need_rows * cc <= 16 * SUBLANES * LANES:
        ext = ref[lead + (pl.ds(r0, need_rows), slice(c0, c0 + cc))]
        return lambda a, n: ext[a:a + n]
    return lambda a, n: ref[lead + (pl.ds(r0 + a, n), slice(c0, c0 + cc))]


def _mod_kernel(c_ref, w_ref, b_ref, o_ref):
    c = c_ref[...]
    a = (c * _sigmoid(c)).astype(BF)
    o_ref[0] = _dot(a, w_ref[0].astype(BF)) + b_ref[0]


def _mod_call(c_all, w, b, tn):
    nl, d, n = w.shape
    mp = c_all.shape[0]
    return pl.pallas_call(
        _mod_kernel,
        out_shape=jax.ShapeDtypeStruct((nl, mp, n), F32),
        grid=(nl, n // tn),
        in_specs=[
            pl.BlockSpec((mp, d), lambda l, j: (0, 0)),
            pl.BlockSpec((1, d, tn), lambda l, j: (l, 0, j)),
            pl.BlockSpec((1, 1, tn), lambda l, j: (l, 0, j)),
        ],
        out_specs=pl.BlockSpec((1, mp, tn), lambda l, j: (l, 0, j)),
        compiler_params=_params(("arbitrary", "arbitrary")),
        name="adaln_mod",
    )(c_all, w, b)


def _conformer_kernel(x_ref, mod_ref, hist_ref, nrm_ref, w1_ref, b1_ref, dw_ref, dwb_ref, lng_ref, lnb_ref,
                      w2_ref, b2_ref, o_ref, st_ref, h_ref, u_ref, full_ref, y_ref, *, tm, rd, per_row):
    d = D_MODEL
    hh = (CONV_A_WIDTH - 1) * rd
    hp = _round_up(hh, SUBLANES)
    t = pl.program_id(1)

    @pl.when(t == 0)
    def _():
        full_ref[0:hp, :] = jnp.zeros((hp, d), F32)
        full_ref[hp - hh:hp, :] = hist_ref[0]

    _rms_mod_rows(x_ref, tm, [(nrm_ref, mod_ref, 0, 1, h_ref)], per_row)
    u_ref[...] = _dot(h_ref[...], w1_ref[...])

    rc, cc = 16, 512

    def glu_body(i, c):
        r0 = pl.multiple_of(i * rc, rc)
        for c0 in range(0, d, cc):
            a = u_ref[pl.ds(r0, rc), c0:c0 + cc] + b1_ref[:, c0:c0 + cc]
            g = u_ref[pl.ds(r0, rc), d + c0:d + c0 + cc] + b1_ref[:, d + c0:d + c0 + cc]
            full_ref[pl.ds(hp + r0, rc), c0:c0 + cc] = a * _sigmoid(g)
        return c

    lax.fori_loop(0, tm // rc, glu_body, 0)

    rcv, ccv = 32, LANES
    offsets = [hp - hh + k * rd for k in range(CONV_A_WIDTH)]

    def conv_body(i, c):
        r0 = pl.multiple_of(i * rcv, rcv)
        for c0 in range(0, d, ccv):
            load = _row_loader(full_ref, (), r0, c0, ccv, hp + rcv)
            w = [dw_ref[k:k + 1, c0:c0 + ccv] for k in range(CONV_A_WIDTH)]
            y_ref[pl.ds(r0, rcv), c0:c0 + ccv] = _shift_conv(load, w, offsets, rcv) + dwb_ref[:, c0:c0 + ccv]
        return c

    lax.fori_loop(0, tm // rcv, conv_body, 0)

    def ln_body(i, c):
        r0 = pl.multiple_of(i * BF16_ROWS, BF16_ROWS)
        y = y_ref[pl.ds(r0, BF16_ROWS), :]
        mu = jnp.mean(y, axis=-1, keepdims=True)
        dv = y - mu
        var = jnp.mean(dv * dv, axis=-1, keepdims=True)
        yn = dv * lax.rsqrt(var + LN_EPS) * lng_ref[...] + lnb_ref[...]
        h_ref[pl.ds(r0, BF16_ROWS), :] = (yn * _sigmoid(yn)).astype(BF)
        return c

    lax.fori_loop(0, tm // BF16_ROWS, ln_body, 0, unroll=ROW_UNROLL)

    out = _dot(h_ref[...], w2_ref[...]) + b2_ref[...]
    o_ref[0] = x_ref[0] + mod_ref[0, 2] * out

    new_hist = full_ref[hp + tm - hh:hp + tm, :]
    st_ref[0] = new_hist
    full_ref[hp - hh:hp, :] = new_hist


def _conformer_call(x, mods, hist, nrm, w1, b1, dw, dwb, lng, lnb, w2, b2, *, tm, rd):
    ns, s, d = x.shape
    r = mods.shape[2]
    hh = hist.shape[1]
    hp = _round_up(hh, SUBLANES)
    per_row = r > 1
    assert s % tm == 0 and (not per_row or (r == tm and s == tm))
    kern = functools.partial(_conformer_kernel, tm=tm, rd=rd, per_row=per_row)
    return pl.pallas_call(
        kern,
        out_shape=(jax.ShapeDtypeStruct((ns, s, d), F32), jax.ShapeDtypeStruct((ns, hh, d), F32)),
        grid=(ns, s // tm),
        in_specs=[
            pl.BlockSpec((1, tm, d), lambda n, t: (n, t, 0)),
            pl.BlockSpec((1, N_MOD, r, d), lambda n, t: (n, 0, 0, 0)),
            pl.BlockSpec((1, hh, d), lambda n, t: (n, 0, 0)),
            _const_spec((1, d)),
            _const_spec((d, 2 * d)),
            _const_spec((1, 2 * d)),
            _const_spec((CONV_A_WIDTH, d)),
            _const_spec((1, d)),
            _const_spec((1, d)),
            _const_spec((1, d)),
            _const_spec((d, d)),
            _const_spec((1, d)),
        ],
        out_specs=(
            pl.BlockSpec((1, tm, d), lambda n, t: (n, t, 0)),
            pl.BlockSpec((1, hh, d), lambda n, t: (n, 0, 0)),
        ),
        scratch_shapes=[
            pltpu.VMEM((tm, d), BF),
            pltpu.VMEM((tm, 2 * d), F32),
            pltpu.VMEM((hp + tm, d), F32),
            pltpu.VMEM((tm, d), F32),
        ],
        compiler_params=_params(("arbitrary", "arbitrary")),
        name="conformer_mixer",
    )(x, mods, hist, nrm, w1, b1, dw, dwb, lng, lnb, w2, b2)


def _ffn_kernel(x_ref, mod_ref, hist_ref, nrm_ref, wu_ref, cw_ref, cb_ref, wd_ref, nf_ref, o_ref, st_ref,
                h_ref, ubuf_ref, carry_ref, act_ref, *, tm, rd, per_row, final_norm):
    f = D_FF
    hh = (CONV_F_WIDTH - 1) * rd
    hp = _round_up(hh, SUBLANES)
    t = pl.program_id(1)

    _rms_mod_rows(x_ref, tm, [(nrm_ref, mod_ref, 3, 4, h_ref)], per_row)

    @pl.when(t == 0)
    def _():
        carry_ref[...] = jnp.zeros((hp, 2 * f), F32)
        carry_ref[hp - hh:hp, :] = hist_ref[0]

    ubuf_ref[0:hp, :] = carry_ref[...]

    rc = 32
    offsets = [hp - hh + k * rd for k in range(CONV_F_WIDTH)]
    for c0 in range(0, f, FF_SUB):
        for half in range(2):
            col = half * f + c0
            ubuf_ref[hp:hp + tm, col:col + FF_SUB] = _dot(h_ref[...], wu_ref[:, col:col + FF_SUB])
        for r0 in range(0, tm, rc):
            for cc0 in range(c0, c0 + FF_SUB, LANES):
                ys = []
                for half in range(2):
                    col = half * f + cc0
                    ext = ubuf_ref[r0:r0 + hp + rc, col:col + LANES]
                    w = [cw_ref[k:k + 1, col:col + LANES] for k in range(CONV_F_WIDTH)]
                    ys.append(_shift_conv(lambda a, n, ext=ext: ext[a:a + n], w, offsets, rc)
                              + cb_ref[:, col:col + LANES])
                yg, yv = ys
                act_ref[r0:r0 + rc, cc0:cc0 + LANES] = (yg * _sigmoid(yg) * yv).astype(BF)

    st_ref[0] = ubuf_ref[hp + tm - hh:hp + tm, :]
    carry_ref[...] = ubuf_ref[tm:tm + hp, :]

    o_ref[0] = _dot(act_ref[...], wd_ref[...])

    rows = BF16_ROWS

    def fin_body(i, c):
        r0 = pl.multiple_of(i * rows, rows)
        gate = _mod_rows(mod_ref, 5, r0, rows, per_row)
        xo = x_ref[0, pl.ds(r0, rows), :] + gate * o_ref[0, pl.ds(r0, rows), :]
        if final_norm:
            xo = xo * lax.rsqrt(jnp.mean(xo * xo, axis=-1, keepdims=True) + EPS) * nf_ref[...]
        o_ref[0, pl.ds(r0, rows), :] = xo
        return c

    lax.fori_loop(0, tm // rows, fin_body, 0, unroll=ROW_UNROLL)


def _ffn_call(x, mods, hist, nrm, wup, cw, cb, wdown, nf, *, tm, rd, final_norm):
    ns, s, d = x.shape
    r = mods.shape[2]
    hh = hist.shape[1]
    hp = _round_up(hh, SUBLANES)
    f = wdown.shape[0]
    per_row = r > 1
    assert f == D_FF and f % FF_SUB == 0 and s % tm == 0 and (not per_row or (r == tm and s == tm))
    kern = functools.partial(_ffn_kernel, tm=tm, rd=rd, per_row=per_row, final_norm=final_norm)
    return pl.pallas_call(
        kern,
        out_shape=(jax.ShapeDtypeStruct((ns, s, d), F32), jax.ShapeDtypeStruct((ns, hh, 2 * f), F32)),
        grid=(ns, s // tm),
        in_specs=[
            pl.BlockSpec((1, tm, d), lambda n, t: (n, t, 0)),
            pl.BlockSpec((1, N_MOD, r, d), lambda n, t: (n, 0, 0, 0)),
            pl.BlockSpec((1, hh, 2 * f), lambda n, t: (n, 0, 0)),
            _const_spec((1, d)),
            _const_spec((d, 2 * f)),
            _const_spec((CONV_F_WIDTH, 2 * f)),
            _const_spec((1, 2 * f)),
            _const_spec((f, d)),
            _const_spec((1, d)),
        ],
        out_specs=(
            pl.BlockSpec((1, tm, d), lambda n, t: (n, t, 0)),
            pl.BlockSpec((1, hh, 2 * f), lambda n, t: (n, 0, 0)),
        ),
        scratch_shapes=[
            pltpu.VMEM((tm, d), BF),
            pltpu.VMEM((hp + tm, 2 * f), F32),
            pltpu.VMEM((hp, 2 * f), F32),
            pltpu.VMEM((tm, f), BF),
        ],
        compiler_params=_params(("arbitrary", "arbitrary")),
        name="conv_ffn",
    )(x, mods, hist, nrm, wup, cw, cb, wdown, nf)


def _qkv_kernel(x_ref, mod_ref, modkv_ref, nq_ref, nkv_ref, wq_ref, wkv_ref, q_ref, kv_ref, hq_ref, hkv_ref,
                *, tm, per_row):
    _rms_mod_rows(x_ref, tm, [(nq_ref, mod_ref, 0, 1, hq_ref), (nkv_ref, modkv_ref, 0, 1, hkv_ref)], per_row)
    q_ref[0] = _dot(hq_ref[...], wq_ref[...])
    kv_ref[0] = _dot(hkv_ref[...], wkv_ref[...])


def _qkv_call(x, mods, modkv, nq, nkv, wq, wkv, *, tm):
    ns, s, d = x.shape
    r = mods.shape[2]
    per_row = r > 1
    kern = functools.partial(_qkv_kernel, tm=tm, per_row=per_row)
    return pl.pallas_call(
        kern,
        out_shape=(jax.ShapeDtypeStruct((ns, s, ATT_WIDTH), F32), jax.ShapeDtypeStruct((ns, s, 2 * ATT_WIDTH), F32)),
        grid=(ns, s // tm),
        in_specs=[
            pl.BlockSpec((1, tm, d), lambda n, t: (n, t, 0)),
            pl.BlockSpec((1, N_MOD, r, d), lambda n, t: (n, 0, 0, 0)),
            pl.BlockSpec((1, 2, r, d), lambda n, t: (n, 0, 0, 0)),
            _const_spec((1, d)),
            _const_spec((1, d)),
            _const_spec((d, ATT_WIDTH)),
            _const_spec((d, 2 * ATT_WIDTH)),
        ],
        out_specs=(
            pl.BlockSpec((1, tm, ATT_WIDTH), lambda n, t: (n, t, 0)),
            pl.BlockSpec((1, tm, 2 * ATT_WIDTH), lambda n, t: (n, t, 0)),
        ),
        scratch_shapes=[pltpu.VMEM((tm, d), BF), pltpu.VMEM((tm, d), BF)],
        compiler_params=_params(("arbitrary", "arbitrary")),
        name="qkv_proj",
    )(x, mods, modkv, nq, nkv, wq, wkv)


def _rel_bucket(dist):
    max_exact = N_BUCKETS // 2
    dd = jnp.maximum(dist, 1).astype(F32)
    large = max_exact + (jnp.log(dd / max_exact) / math.log(MAX_DISTANCE / max_exact)
                         * (N_BUCKETS - max_exact)).astype(jnp.int32)
    large = jnp.minimum(large, N_BUCKETS - 1)
    return jnp.where(dist < max_exact, dist, large)


def _prompt_bucket_table(dil):
    qi = jnp.arange(SPAN, dtype=jnp.int32)[:, None]
    ki = jnp.arange(2 * SPAN, dtype=jnp.int32)[None, :]
    m = qi + SPAN - ki
    valid = (m >= 0) & (m <= SPAN)
    return jnp.where(valid, _rel_bucket(jnp.clip(m, 0, SPAN) * dil), -1).astype(jnp.int32)


def _sample_bucket_tables(dil, width, t_new):
    row = jnp.arange(SAMPLE_QROWS, dtype=jnp.int32)[:, None]
    t = (row % SUBLANES) % t_new
    tn = jnp.arange(SUBLANES, dtype=jnp.int32)[None, :]

    def table(dist, ok):
        ok = ok & (dist >= 0) & (dist % dil == 0) & (dist // dil <= SPAN)
        return jnp.where(ok, _rel_bucket(jnp.clip(dist, 0, SPAN * dil)), -1).astype(jnp.int32)

    old = table(width + t - jnp.arange(width, dtype=jnp.int32)[None, :], True)
    new = table(t - tn, tn < t_new)
    return old, new


def _bias_from_buckets(bkt, relb_ref, head):
    acc = jnp.full(bkt.shape, NEG_INF, F32)
    for b in range(N_BUCKETS):
        acc = jnp.where(bkt == b, relb_ref[b, head], acc)
    return acc


def _pattn_kernel(relb_ref, bkt_ref, q0_ref, q1_ref, kp0_ref, kp1_ref, kc0_ref, kc1_ref, vp0_ref, vp1_ref,
                  vc0_ref, vc1_ref, o0_ref, o1_ref, l0_ref, l1_ref, bias_ref, *, g, dil):
    b = pl.program_id(0)
    i = pl.program_id(1)
    hpg = HEADS_PER_GROUP

    @pl.when((b == 0) & (i == 0))
    def _():
        bkt = bkt_ref[...]
        for h in range(hpg):
            bias_ref[h] = _bias_from_buckets(bkt, relb_ref, g * hpg + h)

    lanehead = lax.broadcasted_iota(jnp.int32, (SPAN, GROUP_WIDTH), 1) >> HEAD_SHIFT
    col = lax.broadcasted_iota(jnp.int32, (SPAN, 2 * SPAN), 1)
    keep = col >= jnp.where(i > 0, 0, SPAN)

    def body(r, c):
        rows = pl.ds(r, SPAN, stride=dil)
        both = lambda r0, r1: jnp.concatenate([r0[0, rows, :], r1[0, rows, :]], axis=1)
        q = both(q0_ref, q1_ref) * (HEAD_DIM ** -0.5)
        lhs = jnp.concatenate([jnp.where(lanehead == h, q, 0.0).astype(BF) for h in range(hpg)], axis=0)
        kcat = jnp.concatenate([both(kp0_ref, kp1_ref), both(kc0_ref, kc1_ref)], axis=0).astype(BF)
        s = _dot_nt(lhs, kcat)

        ps, ms, sums = [], [], []
        for h in range(hpg):
            logit = jnp.where(keep, s[h * SPAN:(h + 1) * SPAN] + bias_ref[h], NEG_INF)
            m = jnp.max(logit, axis=-1, keepdims=True)
            e = jnp.exp(logit - m)
            ps.append(e.astype(BF))
            ms.append(m)
            sums.append(jnp.sum(e, axis=-1, keepdims=True))
        vcat = jnp.concatenate([both(vp0_ref, vp1_ref), both(vc0_ref, vc1_ref)], axis=0).astype(BF)
        pv = _dot(jnp.concatenate(ps, axis=0), vcat)

        o = jnp.zeros((SPAN, GROUP_WIDTH), F32)
        l = jnp.zeros((SPAN, GROUP_WIDTH), F32)
        for h in range(hpg):
            o = jnp.where(lanehead == h, pv[h * SPAN:(h + 1) * SPAN] * (1.0 / sums[h]), o)
            l = jnp.where(lanehead == h, ms[h] + jnp.log(sums[h]), l)
        o0_ref[0, rows, :] = o[:, :LANES]
        o1_ref[0, rows, :] = o[:, LANES:]
        l0_ref[0, rows, :] = l[:, :LANES]
        l1_ref[0, rows, :] = l[:, LANES:]
        return c

    lax.fori_loop(0, dil, body, 0)


def _pattn_call(q, kv, rel_bias, g, dil):
    bsz, s, _ = q.shape
    rows = dil * SPAN
    assert s % rows == 0 and GROUP_WIDTH == 2 * LANES
    blk = (1, rows, LANES)
    kcol = 2 * g
    vcol = ATT_WIDTH // LANES + 2 * g
    prev = lambda i: jnp.maximum(i - 1, 0)
    out = jax.ShapeDtypeStruct((bsz, s, LANES), F32)
    ospec = pl.BlockSpec(blk, lambda b, i: (b, i, 0))
    res = pl.pallas_call(
        functools.partial(_pattn_kernel, g=g, dil=dil),
        out_shape=(out,) * 4,
        grid=(bsz, s // rows),
        in_specs=[
            pl.BlockSpec(memory_space=pltpu.SMEM),
            _const_spec((SPAN, 2 * SPAN)),
            pl.BlockSpec(blk, lambda b, i: (b, i, kcol)),
            pl.BlockSpec(blk, lambda b, i: (b, i, kcol + 1)),
            pl.BlockSpec(blk, lambda b, i: (b, prev(i), kcol)),
            pl.BlockSpec(blk, lambda b, i: (b, prev(i), kcol + 1)),
            pl.BlockSpec(blk, lambda b, i: (b, i, kcol)),
            pl.BlockSpec(blk, lambda b, i: (b, i, kcol + 1)),
            pl.BlockSpec(blk, lambda b, i: (b, prev(i), vcol)),
            pl.BlockSpec(blk, lambda b, i: (b, prev(i), vcol + 1)),
            pl.BlockSpec(blk, lambda b, i: (b, i, vcol)),
            pl.BlockSpec(blk, lambda b, i: (b, i, vcol + 1)),
        ],
        out_specs=(ospec,) * 4,
        scratch_shapes=[pltpu.VMEM((HEADS_PER_GROUP, SPAN, 2 * SPAN), F32)],
        compiler_params=_params(("arbitrary", "arbitrary")),
        name=f"prompt_attn_g{g}",
    )(rel_bias, _prompt_bucket_table(dil), q, q, kv, kv, kv, kv, kv, kv, kv, kv)
    return list(res[:2]), list(res[2:])


def _sattn_kernel(relb_ref, bo0_ref, bo1_ref, bo2_ref, bn_ref, q_ref, kvn_ref, kvt_ref, c0_ref, c1_ref, c2_ref,
                  o0_ref, o1_ref, o2_ref, l0_ref, l1_ref, l2_ref, n0_ref, n1_ref, n2_ref,
                  bias0_ref, bias1_ref, bias2_ref, biasn_ref, tail_ref, *, t_new):
    hpg = HEADS_PER_GROUP
    gw = GROUP_WIDTH
    rows = SAMPLE_QROWS
    bo_refs = (bo0_ref, bo1_ref, bo2_ref)
    bias_refs = (bias0_ref, bias1_ref, bias2_ref)
    caches = (c0_ref, c1_ref, c2_ref)
    outs = ((o0_ref, l0_ref, n0_ref), (o1_ref, l1_ref, n1_ref), (o2_ref, l2_ref, n2_ref))

    @pl.when(pl.program_id(0) == 0)
    def _():
        tail_ref[...] = jnp.zeros(tail_ref.shape, F32)
        for g in range(N_GROUPS):
            for h in range(hpg):
                sl = slice(h * SUBLANES, (h + 1) * SUBLANES)
                bias_refs[g][sl, :] = _bias_from_buckets(bo_refs[g][sl, :], relb_ref, g * hpg + h)
                biasn_ref[g, sl, :] = _bias_from_buckets(bn_ref[g, sl, :], relb_ref, g * hpg + h)

    lanehead = lax.broadcasted_iota(jnp.int32, (rows, gw), 1) >> HEAD_SHIFT
    rowhead = lax.broadcasted_iota(jnp.int32, (rows, gw), 0) >> SUBLANE_SHIFT
    own = lanehead == rowhead
    tail_lane = lax.broadcasted_iota(jnp.int32, (gw, LANES), 1) >= LANES - t_new

    for g in range(N_GROUPS):
        cref = caches[g]
        o_ref, l_ref, n_ref = outs[g]
        width = cref.shape[-1]
        qm = jnp.where(own, q_ref[0, :, g * gw:(g + 1) * gw] * (HEAD_DIM ** -0.5), 0.0).astype(BF)
        knew = kvn_ref[0, :, g * gw:(g + 1) * gw].astype(BF)
        vnew = kvn_ref[0, :, ATT_WIDTH + g * gw:ATT_WIDTH + (g + 1) * gw].astype(BF)
        lo = _dot(qm, cref[0, 0].astype(BF)) + bias_refs[g][...]
        ln = _dot_nt(qm, knew) + biasn_ref[g]
        m = jnp.maximum(jnp.max(lo, axis=-1, keepdims=True), jnp.max(ln, axis=-1, keepdims=True))
        eo = jnp.exp(lo - m)
        en = jnp.exp(ln - m)
        ssum = jnp.sum(eo, axis=-1, keepdims=True) + jnp.sum(en, axis=-1, keepdims=True)
        pv = _dot_nt(eo.astype(BF), cref[0, 1].astype(BF)) + _dot(en.astype(BF), vnew)
        om = jnp.where(own, pv * (1.0 / ssum), 0.0)
        lm = jnp.where(own, m + jnp.log(ssum), 0.0)
        o_acc = om[0:SUBLANES]
        l_acc = lm[0:SUBLANES]
        for h in range(1, hpg):
            o_acc = o_acc + om[h * SUBLANES:(h + 1) * SUBLANES]
            l_acc = l_acc + lm[h * SUBLANES:(h + 1) * SUBLANES]
        o_ref[0] = o_acc
        l_ref[0] = l_acc

        for kv in range(2):
            rolled = pltpu.roll(cref[0, kv], width - t_new, 1)
            tail_ref[:, 0:SUBLANES] = kvt_ref[0, kv * ATT_WIDTH + g * gw:kv * ATT_WIDTH + (g + 1) * gw, :]
            tail = pltpu.roll(tail_ref[...], LANES - t_new, 1)
            if width > LANES:
                n_ref[0, kv, :, 0:width - LANES] = rolled[:, 0:width - LANES]
            n_ref[0, kv, :, width - LANES:width] = jnp.where(tail_lane, tail, rolled[:, width - LANES:width])


def _sattn_call(q, kvn, caches, rel_bias):
    n, t_new, _ = q.shape
    gw = GROUP_WIDTH
    hpg = HEADS_PER_GROUP
    assert t_new <= SUBLANES
    views, bkt_old, bkt_new = [], [], []
    for g, (window, dil) in enumerate(DIL_GROUPS):
        width = caches[g].shape[1]
        assert width == window and width % LANES == 0
        views.append(jnp.transpose(caches[g], (0, 2, 3, 4, 1)).reshape(n, 2, gw, width))
        old, new = _sample_bucket_tables(dil, width, t_new)
        bkt_old.append(old)
        bkt_new.append(new)
    pad = SUBLANES - t_new
    qp = jnp.tile(jnp.pad(q, ((0, 0), (0, pad), (0, 0))), (1, hpg, 1))
    kvp = jnp.pad(kvn, ((0, 0), (0, pad), (0, 0)))
    kvt = jnp.swapaxes(kvp, 1, 2)
    out = jax.ShapeDtypeStruct((n, SUBLANES, gw), F32)
    oblk = pl.BlockSpec((1, SUBLANES, gw), lambda i: (i, 0, 0))
    cspecs = [pl.BlockSpec((1, 2, gw, v.shape[-1]), lambda i: (i, 0, 0, 0)) for v in views]
    res = pl.pallas_call(
        functools.partial(_sattn_kernel, t_new=t_new),
        out_shape=(out,) * 6 + tuple(jax.ShapeDtypeStruct(v.shape, F32) for v in views),
        grid=(n,),
        in_specs=[pl.BlockSpec(memory_space=pltpu.SMEM)]
        + [_const_spec(b.shape) for b in bkt_old]
        + [
            _const_spec((N_GROUPS, SAMPLE_QROWS, SUBLANES)),
            pl.BlockSpec((1, SAMPLE_QROWS, ATT_WIDTH), lambda i: (i, 0, 0)),
            pl.BlockSpec((1, SUBLANES, 2 * ATT_WIDTH), lambda i: (i, 0, 0)),
            pl.BlockSpec((1, 2 * ATT_WIDTH, SUBLANES), lambda i: (i, 0, 0)),
        ]
        + cspecs,
        out_specs=(oblk,) * 6 + tuple(cspecs),
        scratch_shapes=[pltpu.VMEM(b.shape, F32) for b in bkt_old]
        + [pltpu.VMEM((N_GROUPS, SAMPLE_QROWS, SUBLANES), F32), pltpu.VMEM((gw, LANES), F32)],
        compiler_params=_params(("arbitrary",)),
        name="sample_attn",
    )(rel_bias, *bkt_old, jnp.stack(bkt_new), qp, kvp, kvt, *views)
    new_caches = [jnp.transpose(c.reshape(n, 2, hpg, HEAD_DIM, c.shape[-1]), (0, 4, 1, 2, 3)) for c in res[6:]]
    return [a[:, :t_new] for a in res[:3]], [a[:, :t_new] for a in res[3:6]], new_caches


def _oproj_kernel(x_ref, mod_ref, *refs, tm):
    nblk = ATT_WIDTH // LANES
    halves = nblk // N_GROUPS
    o_refs, l_refs = refs[:nblk], refs[nblk:2 * nblk]
    wo_ref, out_ref, att_ref = refs[2 * nblk:]
    rc = BF16_ROWS

    def body(i, c):
        r0 = pl.multiple_of(i * rc, rc)
        for hf in range(halves):
            blks = [g * halves + hf for g in range(N_GROUPS)]
            ls = [l_refs[k][0, pl.ds(r0, rc), :] for k in blks]
            mx = functools.reduce(jnp.maximum, ls)
            es = [jnp.exp(v - mx) for v in ls]
            inv = 1.0 / functools.reduce(lambda a, b: a + b, es)
            for k, e in zip(blks, es):
                att_ref[pl.ds(r0, rc), k * LANES:(k + 1) * LANES] = (o_refs[k][0, pl.ds(r0, rc), :] * (e * inv)).astype(BF)
        return c

    lax.fori_loop(0, tm // rc, body, 0, unroll=2)
    out_ref[0] = x_ref[0] + mod_ref[0, 2] * _dot(att_ref[...], wo_ref[...])


def _oproj_call(x, mods, os_, ls_, wo, *, tm):
    ns, s, d = x.shape
    r = mods.shape[2]
    kern = functools.partial(_oproj_kernel, tm=tm)
    gspec = pl.BlockSpec((1, tm, LANES), lambda n, t: (n, t, 0))
    return pl.pallas_call(
        kern,
        out_shape=jax.ShapeDtypeStruct((ns, s, d), F32),
        grid=(ns, s // tm),
        in_specs=[
            pl.BlockSpec((1, tm, d), lambda n, t: (n, t, 0)),
            pl.BlockSpec((1, N_MOD, r, d), lambda n, t: (n, 0, 0, 0)),
        ] + [gspec] * (2 * ATT_WIDTH // LANES) + [_const_spec((ATT_WIDTH, d))],
        out_specs=pl.BlockSpec((1, tm, d), lambda n, t: (n, t, 0)),
        scratch_shapes=[pltpu.VMEM((tm, ATT_WIDTH), BF)],
        compiler_params=_params(("arbitrary", "arbitrary")),
        name="attn_out_proj",
    )(x, mods, *os_, *ls_, wo)


def _trunk(x, mods, modkv, hist_a, hist_f, w, attn_fn, *, tm, rd):
    row = lambda v: v.reshape(1, -1)
    x, st_a = _conformer_call(x, mods[0], hist_a, row(w["norm_mix"][0]), w["a_w1"], row(w["a_b1"][0]),
                              w["a_dw"][0], row(w["a_dwb"][0]), row(w["a_ln_g"][0]), row(w["a_ln_b"][0]),
                              w["a_w2"], row(w["a_b2"][0]), tm=tm, rd=rd)
    x, st_f0 = _ffn_call(x, mods[0], hist_f[0], row(w["norm_ffn"][0]), w["f_wup"][0], w["f_cw"][0],
                         row(w["f_cb"][0]), w["f_wdown"][0], row(w["norm_f"]), tm=tm, rd=rd, final_norm=False)
    q, kv = _qkv_call(x, mods[1], modkv, row(w["norm_mix"][1]), row(w["norm_kv"]), w["w_q"], w["w_kv"], tm=tm)
    os_, ls_, attn_extra = attn_fn(q, kv)
    x = _oproj_call(x, mods[1], os_, ls_, w["w_o"], tm=tm)
    y, st_f1 = _ffn_call(x, mods[1], hist_f[1], row(w["norm_ffn"][1]), w["f_wup"][1], w["f_cw"][1],
                         row(w["f_cb"][1]), w["f_wdown"][1], row(w["norm_f"]), tm=tm, rd=rd, final_norm=True)
    return y, kv, st_a, [st_f0, st_f1], attn_extra


def kernel(x_prompt, x_sample, cache_kv_w128, cache_kv_w512, cache_kv_w2048, state_conv_a, state_conv_ffn, c_prompt, c_sample, w_mod, b_mod, norm_mix, norm_ffn, a_w1, a_b1, a_dw, a_dwb, a_ln_g, a_ln_b, a_w2, a_b2, w_mod_kv, b_mod_kv, norm_kv, w_kv, w_q, w_o, rel_bias, f_wup, f_cw, f_cb, f_wdown, norm_f):
    d = D_MODEL
    bsz, seq, _ = x_prompt.shape
    nseq, t_new, _ = x_sample.shape
    caches = (cache_kv_w128, cache_kv_w512, cache_kv_w2048)
    hpg = HEADS_PER_GROUP

    w = dict(norm_mix=norm_mix, norm_ffn=norm_ffn, a_w1=a_w1[0].astype(BF), a_b1=a_b1, a_dw=a_dw, a_dwb=a_dwb,
             a_ln_g=a_ln_g, a_ln_b=a_ln_b, a_w2=a_w2[0].astype(BF), a_b2=a_b2, norm_kv=norm_kv,
             w_kv=w_kv.astype(BF), w_q=w_q[0].astype(BF), w_o=w_o[0].astype(BF), f_wup=f_wup.astype(BF),
             f_cw=f_cw, f_cb=f_cb, f_wdown=f_wdown.astype(BF), norm_f=norm_f)

    n_c = bsz + nseq
    mp = _round_up(n_c, SUBLANES)
    c_all = jnp.pad(jnp.concatenate([c_prompt, c_sample], axis=0), ((0, mp - n_c), (0, 0)))
    mod = _mod_call(c_all, w_mod, b_mod.reshape(DEPTH, 1, N_MOD * d), tn=1536)
    modkv = _mod_call(c_all, w_mod_kv[None], b_mod_kv.reshape(1, 1, 2 * d), tn=1024)[0]

    mods_p = [mod[l, :bsz].reshape(bsz, N_MOD, 1, d) for l in range(DEPTH)]
    modkv_p = modkv[:bsz].reshape(bsz, 2, 1, d)
    hist_a_p = jnp.zeros((bsz, CONV_A_WIDTH - 1, d), F32)
    hist_f_p = [jnp.zeros((bsz, CONV_F_WIDTH - 1, 2 * D_FF), F32)] * DEPTH

    def prompt_attn(q, kv):
        res = [_pattn_call(q, kv, rel_bias, g, dil) for g, (_, dil) in enumerate(DIL_GROUPS)]
        return [a for r in res for a in r[0]], [a for r in res for a in r[1]], None

    y_p, kv_p, st_a_p, st_f_p, _ = _trunk(x_prompt, mods_p, modkv_p, hist_a_p, hist_f_p, w, prompt_attn,
                                          tm=TM_PROMPT, rd=1)
    wmax = max(c.shape[1] for c in caches)
    assert seq >= wmax
    kv_tail = kv_p[:, seq - wmax:].reshape(bsz, wmax, 2, N_HEADS, HEAD_DIM)
    kv_bufs_p = [kv_tail[:, wmax - c.shape[1]:, :, g * hpg:(g + 1) * hpg] for g, c in enumerate(caches)]
    conv_a_p = st_a_p[None]
    conv_f_p = jnp.stack(st_f_p)

    rows = t_new * nseq
    tmaj = lambda v: jnp.swapaxes(v, 0, 1).reshape(1, -1, v.shape[-1])
    smaj = lambda v, c: jnp.swapaxes(v.reshape(-1, nseq, c), 0, 1)

    def per_row(m, k):
        return jnp.tile(jnp.swapaxes(m.reshape(nseq, k, d), 0, 1), (1, t_new, 1))[None]

    mods_s = [per_row(mod[l, bsz:n_c], N_MOD) for l in range(DEPTH)]
    modkv_s = per_row(modkv[bsz:n_c], 2)
    hist_a_s = tmaj(state_conv_a[0])
    hist_f_s = [tmaj(state_conv_ffn[l]) for l in range(DEPTH)]

    def sample_attn(q, kv):
        os_, ls_, new_caches = _sattn_call(smaj(q, ATT_WIDTH), smaj(kv, 2 * ATT_WIDTH), caches, rel_bias)
        halves = lambda vs: [tmaj(a[..., c0:c0 + LANES]) for a in vs for c0 in range(0, GROUP_WIDTH, LANES)]
        return halves(os_), halves(ls_), new_caches

    y_s, _, st_a_s, st_f_s, kv_bufs_s = _trunk(tmaj(x_sample), mods_s, modkv_s, hist_a_s, hist_f_s, w,
                                               sample_attn, tm=rows, rd=nseq)
    y_s = smaj(y_s, d)
    conv_a_s = smaj(st_a_s, d)[None]
    conv_f_s = jnp.stack([smaj(s, 2 * D_FF) for s in st_f_s])

    return (y_p, y_s, kv_bufs_p[0], kv_bufs_p[1], kv_bufs_p[2], conv_a_p, conv_f_p,
            kv_bufs_s[0], kv_bufs_s[1], kv_bufs_s[2], conv_a_s, conv_f_s)
```

```python
import functools
import math

import jax
import jax.numpy as jnp
from jax import lax
from jax.experimental import pallas as pl
from jax.experimental.pallas import tpu as pltpu

D_MODEL = 1024
DEPTH = 2
HEAD_DIM = 64
HEADS_PER_GROUP = 4
DIL_GROUPS = ((128, 1), (512, 4), (2048, 16))
N_GROUPS = len(DIL_GROUPS)
N_HEADS = HEADS_PER_GROUP * N_GROUPS
ATT_WIDTH = N_HEADS * HEAD_DIM
GROUP_WIDTH = HEADS_PER_GROUP * HEAD_DIM
SPAN = 128
N_BUCKETS = 32
MAX_DISTANCE = 2048
CONV_A_WIDTH = 31
D_FF = 2816
CONV_F_WIDTH = 3
N_MOD = 6
EPS = 1e-6
LN_EPS = 1e-5
NEG_INF = -1e30

BF = jnp.bfloat16
F32 = jnp.float32

HEAD_SHIFT = HEAD_DIM.bit_length() - 1
SUBLANES = 8
SUBLANE_SHIFT = SUBLANES.bit_length() - 1
LANES = 128
BF16_ROWS = 16
ROW_UNROLL = 4
VMEM_LIMIT = 56 * 1024 * 1024

TM_PROMPT = 512
CONF_ROWS = 128
FF_SUB = 256
SAMPLE_QROWS = HEADS_PER_GROUP * SUBLANES


def _round_up(a, b):
    return -(-a // b) * b


def _params(sem):
    return pltpu.CompilerParams(dimension_semantics=sem, vmem_limit_bytes=VMEM_LIMIT)


def _const_spec(shape):
    nd = len(shape)
    return pl.BlockSpec(shape, lambda *_: (0,) * nd, pipeline_mode=pl.Buffered(1))


def _dot(a, b):
    return jnp.dot(a, b, preferred_element_type=F32)


def _dot_nt(a, b):
    return lax.dot_general(a, b, (((1,), (1,)), ((), ())), preferred_element_type=F32)


def _sigmoid(v):
    return 1.0 / (1.0 + jnp.exp(-v))


def _mod_rows(mod_ref, idx, r0, rows, per_row):
    if per_row:
        return mod_ref[0, idx, pl.ds(r0, rows), :]
    return mod_ref[0, idx]


def _rms_mod_rows(x_ref, tm, targets, per_row):
    rc = BF16_ROWS

    def body(i, c):
        r0 = pl.multiple_of(i * rc, rc)
        x = x_ref[0, pl.ds(r0, rc), :]
        xn = x * lax.rsqrt(jnp.mean(x * x, axis=-1, keepdims=True) + EPS)
        for gain_ref, mod_ref, i_sh, i_sc, dst_ref in targets:
            sh = _mod_rows(mod_ref, i_sh, r0, rc, per_row)
            sc = _mod_rows(mod_ref, i_sc, r0, rc, per_row)
            dst_ref[pl.ds(r0, rc), :] = ((xn * gain_ref[...]) * (1.0 + sc) + sh).astype(BF)
        return c

    lax.fori_loop(0, tm // rc, body, 0, unroll=ROW_UNROLL)


def _shift_conv(load, weights, offsets, rows):
    out = None
    for s in range(SUBLANES):
        taps = [k for k in range(len(offsets)) if offsets[k] % SUBLANES == s]
        if not taps:
            continue
        n = rows + (SUBLANES if s else 0)
        g = None
        for k in taps:
            term = weights[k] * load(offsets[k] - s, n)
            g = term if g is None else g + term
        if s:
            g = g[s:s + rows]
        out = g if out is None else out + g
    return out


def _row_loader(ref, lead, r0, c0, cc, need_rows):
    if need_rows * cc <= 16 * SUBLANES * LANES:
        ext = ref[lead + (pl.ds(r0, need_rows), slice(c0, c0 + cc))]
        return lambda a, n: ext[a:a + n]
    return lambda a, n: ref[lead + (pl.ds(r0 + a, n), slice(c0, c0 + cc))]


def _mod_kernel(c_ref, w_ref, b_ref, o_ref):
    c = c_ref[...]
    a = (c * _sigmoid(c)).astype(BF)
    o_ref[0] = _dot(a, w_ref[0].astype(BF)) + b_ref[0]


def _mod_call(c_all, w, b, tn):
    nl, d, n = w.shape
    mp = c_all.shape[0]
    return pl.pallas_call(
        _mod_kernel,
        out_shape=jax.ShapeDtypeStruct((nl, mp, n), F32),
        grid=(nl, n // tn),
        in_specs=[
            pl.BlockSpec((mp, d), lambda l, j: (0, 0)),
            pl.BlockSpec((1, d, tn), lambda l, j: (l, 0, j)),
            pl.BlockSpec((1, 1, tn), lambda l, j: (l, 0, j)),
        ],
        out_specs=pl.BlockSpec((1, mp, tn), lambda l, j: (l, 0, j)),
        compiler_params=_params(("arbitrary", "arbitrary")),
        name="adaln_mod",
    )(c_all, w, b)


def _conformer_kernel(x_ref, mod_ref, hist_ref, nrm_ref, w1_ref, b1_ref, dw_ref, dwb_ref, lng_ref, lnb_ref,
                      w2_ref, b2_ref, o_ref, st_ref, h_ref, h2_ref, u_ref, full_ref, y_ref, *, tm, rd, per_row):
    d = D_MODEL
    hh = (CONV_A_WIDTH - 1) * rd
    hp = _round_up(hh, SUBLANES)
    t = pl.program_id(1)

    @pl.when(t == 0)
    def _():
        full_ref[0:hp, :] = jnp.zeros((hp, d), F32)
        full_ref[hp - hh:hp, :] = hist_ref[0]

    _rms_mod_rows(x_ref, tm, [(nrm_ref, mod_ref, 0, 1, h_ref)], per_row)

    rb = min(tm, CONF_ROWS)
    rg, cg = 16, 512
    rcv, ccv = 32, LANES
    offsets = [hp - hh + k * rd for k in range(CONV_A_WIDTH)]
    for rs in range(0, tm, rb):
        u_ref[rs:rs + rb, :] = _dot(h_ref[rs:rs + rb, :], w1_ref[...])

        for r0 in range(rs, rs + rb, rg):
            for c0 in range(0, d, cg):
                a = u_ref[r0:r0 + rg, c0:c0 + cg] + b1_ref[:, c0:c0 + cg]
                g = u_ref[r0:r0 + rg, d + c0:d + c0 + cg] + b1_ref[:, d + c0:d + c0 + cg]
                full_ref[hp + r0:hp + r0 + rg, c0:c0 + cg] = a * _sigmoid(g)

        for r0 in range(rs, rs + rb, rcv):
            for c0 in range(0, d, ccv):
                load = _row_loader(full_ref, (), r0, c0, ccv, hp + rcv)
                w = [dw_ref[k:k + 1, c0:c0 + ccv] for k in range(CONV_A_WIDTH)]
                y_ref[r0:r0 + rcv, c0:c0 + ccv] = _shift_conv(load, w, offsets, rcv) + dwb_ref[:, c0:c0 + ccv]

        for r0 in range(rs, rs + rb, BF16_ROWS):
            y = y_ref[r0:r0 + BF16_ROWS, :]
            mu = jnp.mean(y, axis=-1, keepdims=True)
            dv = y - mu
            var = jnp.mean(dv * dv, axis=-1, keepdims=True)
            yn = dv * lax.rsqrt(var + LN_EPS) * lng_ref[...] + lnb_ref[...]
            h2_ref[r0:r0 + BF16_ROWS, :] = (yn * _sigmoid(yn)).astype(BF)

        out = _dot(h2_ref[rs:rs + rb, :], w2_ref[...]) + b2_ref[...]
        gate = mod_ref[0, 2, rs:rs + rb, :] if per_row else mod_ref[0, 2]
        o_ref[0, rs:rs + rb, :] = x_ref[0, rs:rs + rb, :] + gate * out

    new_hist = full_ref[hp + tm - hh:hp + tm, :]
    st_ref[0] = new_hist
    full_ref[hp - hh:hp, :] = new_hist


def _conformer_call(x, mods, hist, nrm, w1, b1, dw, dwb, lng, lnb, w2, b2, *, tm, rd):
    ns, s, d = x.shape
    r = mods.shape[2]
    hh = hist.shape[1]
    hp = _round_up(hh, SUBLANES)
    per_row = r > 1
    assert s % tm == 0 and (not per_row or (r == tm and s == tm))
    kern = functools.partial(_conformer_kernel, tm=tm, rd=rd, per_row=per_row)
    return pl.pallas_call(
        kern,
        out_shape=(jax.ShapeDtypeStruct((ns, s, d), F32), jax.ShapeDtypeStruct((ns, hh, d), F32)),
        grid=(ns, s // tm),
        in_specs=[
            pl.BlockSpec((1, tm, d), lambda n, t: (n, t, 0)),
            pl.BlockSpec((1, N_MOD, r, d), lambda n, t: (n, 0, 0, 0)),
            pl.BlockSpec((1, hh, d), lambda n, t: (n, 0, 0)),
            _const_spec((1, d)),
            _const_spec((d, 2 * d)),
            _const_spec((1, 2 * d)),
            _const_spec((CONV_A_WIDTH, d)),
            _const_spec((1, d)),
            _const_spec((1, d)),
            _const_spec((1, d)),
            _const_spec((d, d)),
            _const_spec((1, d)),
        ],
        out_specs=(
            pl.BlockSpec((1, tm, d), lambda n, t: (n, t, 0)),
            pl.BlockSpec((1, hh, d), lambda n, t: (n, 0, 0)),
        ),
        scratch_shapes=[
            pltpu.VMEM((tm, d), BF),
            pltpu.VMEM((tm, d), BF),
            pltpu.VMEM((tm, 2 * d), F32),
            pltpu.VMEM((hp + tm, d), F32),
            pltpu.VMEM((tm, d), F32),
        ],
        compiler_params=_params(("arbitrary", "arbitrary")),
        name="conformer_mixer",
    )(x, mods, hist, nrm, w1, b1, dw, dwb, lng, lnb, w2, b2)


def _ffn_kernel(x_ref, mod_ref, hist_ref, nrm_ref, wu_ref, cw_ref, cb_ref, wd_ref, nf_ref, o_ref, st_ref,
                h_ref, ubuf_ref, carry_ref, act_ref, part_ref, *, tm, rd, per_row, final_norm):
    f = D_FF
    hh = (CONV_F_WIDTH - 1) * rd
    hp = _round_up(hh, SUBLANES)
    t = pl.program_id(1)

    _rms_mod_rows(x_ref, tm, [(nrm_ref, mod_ref, 3, 4, h_ref)], per_row)

    @pl.when(t == 0)
    def _():
        carry_ref[...] = jnp.zeros((hp, 2 * f), F32)
        carry_ref[hp - hh:hp, :] = hist_ref[0]

    ubuf_ref[0:hp, :] = carry_ref[...]

    rc = 32
    offsets = [hp - hh + k * rd for k in range(CONV_F_WIDTH)]
    for c0 in range(0, f, FF_SUB):
        for half in range(2):
            col = half * f + c0
            ubuf_ref[hp:hp + tm, col:col + FF_SUB] = _dot(h_ref[...], wu_ref[:, col:col + FF_SUB])
        for r0 in range(0, tm, rc):
            for cc0 in range(c0, c0 + FF_SUB, LANES):
                ys = []
                for half in range(2):
                    col = half * f + cc0
                    ext = ubuf_ref[r0:r0 + hp + rc, col:col + LANES]
                    w = [cw_ref[k:k + 1, col:col + LANES] for k in range(CONV_F_WIDTH)]
                    ys.append(_shift_conv(lambda a, n, ext=ext: ext[a:a + n], w, offsets, rc)
                              + cb_ref[:, col:col + LANES])
                yg, yv = ys
                act_ref[r0:r0 + rc, cc0:cc0 + LANES] = (yg * _sigmoid(yg) * yv).astype(BF)

    st_ref[0] = ubuf_ref[hp + tm - hh:hp + tm, :]
    carry_ref[...] = ubuf_ref[tm:tm + hp, :]

    part_ref[...] = _dot(act_ref[...], wd_ref[...])

    rows = BF16_ROWS

    def fin_body(i, c):
        r0 = pl.multiple_of(i * rows, rows)
        gate = _mod_rows(mod_ref, 5, r0, rows, per_row)
        xo = x_ref[0, pl.ds(r0, rows), :] + gate * part_ref[pl.ds(r0, rows), :]
        if final_norm:
            xo = xo * lax.rsqrt(jnp.mean(xo * xo, axis=-1, keepdims=True) + EPS) * nf_ref[...]
        o_ref[0, pl.ds(r0, rows), :] = xo
        return c

    lax.fori_loop(0, tm // rows, fin_body, 0, unroll=ROW_UNROLL)


def _ffn_call(x, mods, hist, nrm, wup, cw, cb, wdown, nf, *, tm, rd, final_norm):
    ns, s, d = x.shape
    r = mods.shape[2]
    hh = hist.shape[1]
    hp = _round_up(hh, SUBLANES)
    f = wdown.shape[0]
    per_row = r > 1
    assert f == D_FF and f % FF_SUB == 0 and s % tm == 0 and (not per_row or (r == tm and s == tm))
    kern = functools.partial(_ffn_kernel, tm=tm, rd=rd, per_row=per_row, final_norm=final_norm)
    return pl.pallas_call(
        kern,
        out_shape=(jax.ShapeDtypeStruct((ns, s, d), F32), jax.ShapeDtypeStruct((ns, hh, 2 * f), F32)),
        grid=(ns, s // tm),
        in_specs=[
            pl.BlockSpec((1, tm, d), lambda n, t: (n, t, 0)),
            pl.BlockSpec((1, N_MOD, r, d), lambda n, t: (n, 0, 0, 0)),
            pl.BlockSpec((1, hh, 2 * f), lambda n, t: (n, 0, 0)),
            _const_spec((1, d)),
            _const_spec((d, 2 * f)),
            _const_spec((CONV_F_WIDTH, 2 * f)),
            _const_spec((1, 2 * f)),
            _const_spec((f, d)),
            _const_spec((1, d)),
        ],
        out_specs=(
            pl.BlockSpec((1, tm, d), lambda n, t: (n, t, 0)),
            pl.BlockSpec((1, hh, 2 * f), lambda n, t: (n, 0, 0)),
        ),
        scratch_shapes=[
            pltpu.VMEM((tm, d), BF),
            pltpu.VMEM((hp + tm, 2 * f), F32),
            pltpu.VMEM((hp, 2 * f), F32),
            pltpu.VMEM((tm, f), BF),
            pltpu.VMEM((tm, d), F32),
        ],
        compiler_params=_params(("arbitrary", "arbitrary")),
        name="conv_ffn",
    )(x, mods, hist, nrm, wup, cw, cb, wdown, nf)


def _qkv_kernel(x_ref, mod_ref, modkv_ref, nq_ref, nkv_ref, wq_ref, wkv_ref, q_ref, kv_ref, hq_ref, hkv_ref,
                *, tm, per_row):
    _rms_mod_rows(x_ref, tm, [(nq_ref, mod_ref, 0, 1, hq_ref), (nkv_ref, modkv_ref, 0, 1, hkv_ref)], per_row)
    q_ref[0] = _dot(hq_ref[...], wq_ref[...])
    kv_ref[0] = _dot(hkv_ref[...], wkv_ref[...])


def _qkv_call(x, mods, modkv, nq, nkv, wq, wkv, *, tm):
    ns, s, d = x.shape
    r = mods.shape[2]
    per_row = r > 1
    kern = functools.partial(_qkv_kernel, tm=tm, per_row=per_row)
    return pl.pallas_call(
        kern,
        out_shape=(jax.ShapeDtypeStruct((ns, s, ATT_WIDTH), F32), jax.ShapeDtypeStruct((ns, s, 2 * ATT_WIDTH), F32)),
        grid=(ns, s // tm),
        in_specs=[
            pl.BlockSpec((1, tm, d), lambda n, t: (n, t, 0)),
            pl.BlockSpec((1, N_MOD, r, d), lambda n, t: (n, 0, 0, 0)),
            pl.BlockSpec((1, 2, r, d), lambda n, t: (n, 0, 0, 0)),
            _const_spec((1, d)),
            _const_spec((1, d)),
            _const_spec((d, ATT_WIDTH)),
            _const_spec((d, 2 * ATT_WIDTH)),
        ],
        out_specs=(
            pl.BlockSpec((1, tm, ATT_WIDTH), lambda n, t: (n, t, 0)),
            pl.BlockSpec((1, tm, 2 * ATT_WIDTH), lambda n, t: (n, t, 0)),
        ),
        scratch_shapes=[pltpu.VMEM((tm, d), BF), pltpu.VMEM((tm, d), BF)],
        compiler_params=_params(("arbitrary", "arbitrary")),
        name="qkv_proj",
    )(x, mods, modkv, nq, nkv, wq, wkv)


def _rel_bucket(dist):
    max_exact = N_BUCKETS // 2
    dd = jnp.maximum(dist, 1).astype(F32)
    large = max_exact + (jnp.log(dd / max_exact) / math.log(MAX_DISTANCE / max_exact)
                         * (N_BUCKETS - max_exact)).astype(jnp.int32)
    large = jnp.minimum(large, N_BUCKETS - 1)
    return jnp.where(dist < max_exact, dist, large)


def _prompt_bucket_table(dil):
    qi = jnp.arange(SPAN, dtype=jnp.int32)[:, None]
    ki = jnp.arange(2 * SPAN, dtype=jnp.int32)[None, :]
    m = qi + SPAN - ki
    valid = (m >= 0) & (m <= SPAN)
    return jnp.where(valid, _rel_bucket(jnp.clip(m, 0, SPAN) * dil), -1).astype(jnp.int32)


def _sample_bucket_tables(dil, width, t_new):
    row = jnp.arange(SAMPLE_QROWS, dtype=jnp.int32)[:, None]
    t = (row % SUBLANES) % t_new
    tn = jnp.arange(SUBLANES, dtype=jnp.int32)[None, :]

    def table(dist, ok):
        ok = ok & (dist >= 0) & (dist % dil == 0) & (dist // dil <= SPAN)
        return jnp.where(ok, _rel_bucket(jnp.clip(dist, 0, SPAN * dil)), -1).astype(jnp.int32)

    old = table(width + t - jnp.arange(width, dtype=jnp.int32)[None, :], True)
    new = table(t - tn, tn < t_new)
    return old, new


def _bias_from_buckets(bkt, relb_ref, head):
    acc = jnp.full(bkt.shape, NEG_INF, F32)
    for b in range(N_BUCKETS):
        acc = jnp.where(bkt == b, relb_ref[b, head], acc)
    return acc


def _pattn_kernel(relb_ref, bkt_ref, q0_ref, q1_ref, kp0_ref, kp1_ref, kc0_ref, kc1_ref, vp0_ref, vp1_ref,
                  vc0_ref, vc1_ref, o0_ref, o1_ref, l0_ref, l1_ref, bias_ref, *, g, dil):
    b = pl.program_id(0)
    i = pl.program_id(1)
    hpg = HEADS_PER_GROUP

    @pl.when((b == 0) & (i == 0))
    def _():
        bkt = bkt_ref[...]
        for h in range(hpg):
            bias_ref[h] = _bias_from_buckets(bkt, relb_ref, g * hpg + h)

    lanehead = lax.broadcasted_iota(jnp.int32, (SPAN, GROUP_WIDTH), 1) >> HEAD_SHIFT
    col = lax.broadcasted_iota(jnp.int32, (SPAN, 2 * SPAN), 1)
    keep = col >= jnp.where(i > 0, 0, SPAN)

    def body(r, c):
        rows = pl.ds(r, SPAN, stride=dil)
        both = lambda r0, r1: jnp.concatenate([r0[0, rows, :], r1[0, rows, :]], axis=1)
        q = both(q0_ref, q1_ref) * (HEAD_DIM ** -0.5)
        lhs = jnp.concatenate([jnp.where(lanehead == h, q, 0.0).astype(BF) for h in range(hpg)], axis=0)
        kcat = jnp.concatenate([both(kp0_ref, kp1_ref), both(kc0_ref, kc1_ref)], axis=0).astype(BF)
        s = _dot_nt(lhs, kcat)

        ps, ms, sums = [], [], []
        for h in range(hpg):
            logit = jnp.where(keep, s[h * SPAN:(h + 1) * SPAN] + bias_ref[h], NEG_INF)
            m = jnp.max(logit, axis=-1, keepdims=True)
            e = jnp.exp(logit - m)
            ps.append(e.astype(BF))
            ms.append(m)
            sums.append(jnp.sum(e, axis=-1, keepdims=True))
        vcat = jnp.concatenate([both(vp0_ref, vp1_ref), both(vc0_ref, vc1_ref)], axis=0).astype(BF)
        pv = _dot(jnp.concatenate(ps, axis=0), vcat)

        o = jnp.zeros((SPAN, GROUP_WIDTH), F32)
        l = jnp.zeros((SPAN, GROUP_WIDTH), F32)
        for h in range(hpg):
            o = jnp.where(lanehead == h, pv[h * SPAN:(h + 1) * SPAN] * (1.0 / sums[h]), o)
            l = jnp.where(lanehead == h, ms[h] + jnp.log(sums[h]), l)
        o0_ref[0, rows, :] = o[:, :LANES]
        o1_ref[0, rows, :] = o[:, LANES:]
        l0_ref[0, rows, :] = l[:, :LANES]
        l1_ref[0, rows, :] = l[:, LANES:]
        return c

    lax.fori_loop(0, dil, body, 0)


def _pattn_call(q, kv, rel_bias, g, dil):
    bsz, s, _ = q.shape
    rows = dil * SPAN
    assert s % rows == 0 and GROUP_WIDTH == 2 * LANES
    blk = (1, rows, LANES)
    kcol = 2 * g
    vcol = ATT_WIDTH // LANES + 2 * g
    prev = lambda i: jnp.maximum(i - 1, 0)
    out = jax.ShapeDtypeStruct((bsz, s, LANES), F32)
    ospec = pl.BlockSpec(blk, lambda b, i: (b, i, 0))
    res = pl.pallas_call(
        functools.partial(_pattn_kernel, g=g, dil=dil),
        out_shape=(out,) * 4,
        grid=(bsz, s // rows),
        in_specs=[
            pl.BlockSpec(memory_space=pltpu.SMEM),
            _const_spec((SPAN, 2 * SPAN)),
            pl.BlockSpec(blk, lambda b, i: (b, i, kcol)),
            pl.BlockSpec(blk, lambda b, i: (b, i, kcol + 1)),
            pl.BlockSpec(blk, lambda b, i: (b, prev(i), kcol)),
            pl.BlockSpec(blk, lambda b, i: (b, prev(i), kcol + 1)),
            pl.BlockSpec(blk, lambda b, i: (b, i, kcol)),
            pl.BlockSpec(blk, lambda b, i: (b, i, kcol + 1)),
            pl.BlockSpec(blk, lambda b, i: (b, prev(i), vcol)),
            pl.BlockSpec(blk, lambda b, i: (b, prev(i), vcol + 1)),
            pl.BlockSpec(blk, lambda b, i: (b, i, vcol)),
            pl.BlockSpec(blk, lambda b, i: (b, i, vcol + 1)),
        ],
        out_specs=(ospec,) * 4,
        scratch_shapes=[pltpu.VMEM((HEADS_PER_GROUP, SPAN, 2 * SPAN), F32)],
        compiler_params=_params(("arbitrary", "arbitrary")),
        name=f"prompt_attn_g{g}",
    )(rel_bias, _prompt_bucket_table(dil), q, q, kv, kv, kv, kv, kv, kv, kv, kv)
    return list(res[:2]), list(res[2:])


def _sattn_kernel(relb_ref, bo0_ref, bo1_ref, bo2_ref, bn_ref, q_ref, kvn_ref, kvt_ref, c0_ref, c1_ref, c2_ref,
                  o0_ref, o1_ref, o2_ref, l0_ref, l1_ref, l2_ref, n0_ref, n1_ref, n2_ref,
                  bias0_ref, bias1_ref, bias2_ref, biasn_ref, tail_ref, *, t_new):
    hpg = HEADS_PER_GROUP
    gw = GROUP_WIDTH
    rows = SAMPLE_QROWS
    bo_refs = (bo0_ref, bo1_ref, bo2_ref)
    bias_refs = (bias0_ref, bias1_ref, bias2_ref)
    caches = (c0_ref, c1_ref, c2_ref)
    outs = ((o0_ref, l0_ref, n0_ref), (o1_ref, l1_ref, n1_ref), (o2_ref, l2_ref, n2_ref))

    @pl.when(pl.program_id(0) == 0)
    def _():
        tail_ref[...] = jnp.zeros(tail_ref.shape, F32)
        for g in range(N_GROUPS):
            for h in range(hpg):
                sl = slice(h * SUBLANES, (h + 1) * SUBLANES)
                bias_refs[g][sl, :] = _bias_from_buckets(bo_refs[g][sl, :], relb_ref, g * hpg + h)
                biasn_ref[g, sl, :] = _bias_from_buckets(bn_ref[g, sl, :], relb_ref, g * hpg + h)

    lanehead = lax.broadcasted_iota(jnp.int32, (rows, gw), 1) >> HEAD_SHIFT
    rowhead = lax.broadcasted_iota(jnp.int32, (rows, gw), 0) >> SUBLANE_SHIFT
    own = lanehead == rowhead
    tail_lane = lax.broadcasted_iota(jnp.int32, (gw, LANES), 1) >= LANES - t_new

    for g in range(N_GROUPS):
        cref = caches[g]
        o_ref, l_ref, n_ref = outs[g]
        width = cref.shape[-1]
        qm = jnp.where(own, q_ref[0, :, g * gw:(g + 1) * gw] * (HEAD_DIM ** -0.5), 0.0).astype(BF)
        knew = kvn_ref[0, :, g * gw:(g + 1) * gw].astype(BF)
        vnew = kvn_ref[0, :, ATT_WIDTH + g * gw:ATT_WIDTH + (g + 1) * gw].astype(BF)
        lo = _dot(qm, cref[0, 0].astype(BF)) + bias_refs[g][...]
        ln = _dot_nt(qm, knew) + biasn_ref[g]
        m = jnp.maximum(jnp.max(lo, axis=-1, keepdims=True), jnp.max(ln, axis=-1, keepdims=True))
        eo = jnp.exp(lo - m)
        en = jnp.exp(ln - m)
        ssum = jnp.sum(eo, axis=-1, keepdims=True) + jnp.sum(en, axis=-1, keepdims=True)
        pv = _dot_nt(eo.astype(BF), cref[0, 1].astype(BF)) + _dot(en.astype(BF), vnew)
        om = jnp.where(own, pv * (1.0 / ssum), 0.0)
        lm = jnp.where(own, m + jnp.log(ssum), 0.0)
        o_acc = om[0:SUBLANES]
        l_acc = lm[0:SUBLANES]
        for h in range(1, hpg):
            o_acc = o_acc + om[h * SUBLANES:(h + 1) * SUBLANES]
            l_acc = l_acc + lm[h * SUBLANES:(h + 1) * SUBLANES]
        o_ref[0] = o_acc
        l_ref[0] = l_acc

        for kv in range(2):
            rolled = pltpu.roll(cref[0, kv], width - t_new, 1)
            tail_ref[:, 0:SUBLANES] = kvt_ref[0, kv * ATT_WIDTH + g * gw:kv * ATT_WIDTH + (g + 1) * gw, :]
            tail = pltpu.roll(tail_ref[...], LANES - t_new, 1)
            if width > LANES:
                n_ref[0, kv, :, 0:width - LANES] = rolled[:, 0:width - LANES]
            n_ref[0, kv, :, width - LANES:width] = jnp.where(tail_lane, tail, rolled[:, width - LANES:width])


def _sattn_call(q, kvn, caches, rel_bias):
    n, t_new, _ = q.shape
    gw = GROUP_WIDTH
    hpg = HEADS_PER_GROUP
    assert t_new <= SUBLANES
    views, bkt_old, bkt_new = [], [], []
    for g, (window, dil) in enumerate(DIL_GROUPS):
        width = caches[g].shape[1]
        assert width == window and width % LANES == 0
        views.append(jnp.transpose(caches[g], (0, 2, 3, 4, 1)).reshape(n, 2, gw, width))
        old, new = _sample_bucket_tables(dil, width, t_new)
        bkt_old.append(old)
        bkt_new.append(new)
    pad = SUBLANES - t_new
    qp = jnp.tile(jnp.pad(q, ((0, 0), (0, pad), (0, 0))), (1, hpg, 1))
    kvp = jnp.pad(kvn, ((0, 0), (0, pad), (0, 0)))
    kvt = jnp.swapaxes(kvp, 1, 2)
    out = jax.ShapeDtypeStruct((n, SUBLANES, gw), F32)
    oblk = pl.BlockSpec((1, SUBLANES, gw), lambda i: (i, 0, 0))
    cspecs = [pl.BlockSpec((1, 2, gw, v.shape[-1]), lambda i: (i, 0, 0, 0)) for v in views]
    res = pl.pallas_call(
        functools.partial(_sattn_kernel, t_new=t_new),
        out_shape=(out,) * 6 + tuple(jax.ShapeDtypeStruct(v.shape, F32) for v in views),
        grid=(n,),
        in_specs=[pl.BlockSpec(memory_space=pltpu.SMEM)]
        + [_const_spec(b.shape) for b in bkt_old]
        + [
            _const_spec((N_GROUPS, SAMPLE_QROWS, SUBLANES)),
            pl.BlockSpec((1, SAMPLE_QROWS, ATT_WIDTH), lambda i: (i, 0, 0)),
            pl.BlockSpec((1, SUBLANES, 2 * ATT_WIDTH), lambda i: (i, 0, 0)),
            pl.BlockSpec((1, 2 * ATT_WIDTH, SUBLANES), lambda i: (i, 0, 0)),
        ]
        + cspecs,
        out_specs=(oblk,) * 6 + tuple(cspecs),
        scratch_shapes=[pltpu.VMEM(b.shape, F32) for b in bkt_old]
        + [pltpu.VMEM((N_GROUPS, SAMPLE_QROWS, SUBLANES), F32), pltpu.VMEM((gw, LANES), F32)],
        compiler_params=_params(("arbitrary",)),
        name="sample_attn",
    )(rel_bias, *bkt_old, jnp.stack(bkt_new), qp, kvp, kvt, *views)
    new_caches = [jnp.transpose(c.reshape(n, 2, hpg, HEAD_DIM, c.shape[-1]), (0, 4, 1, 2, 3)) for c in res[6:]]
    return [a[:, :t_new] for a in res[:3]], [a[:, :t_new] for a in res[3:6]], new_caches


def _oproj_kernel(x_ref, mod_ref, *refs, tm):
    nblk = ATT_WIDTH // LANES
    halves = nblk // N_GROUPS
    o_refs, l_refs = refs[:nblk], refs[nblk:2 * nblk]
    wo_ref, out_ref, att_ref = refs[2 * nblk:]
    rc = BF16_ROWS

    def body(i, c):
        r0 = pl.multiple_of(i * rc, rc)
        for hf in range(halves):
            blks = [g * halves + hf for g in range(N_GROUPS)]
            ls = [l_refs[k][0, pl.ds(r0, rc), :] for k in blks]
            mx = functools.reduce(jnp.maximum, ls)
            es = [jnp.exp(v - mx) for v in ls]
            inv = 1.0 / functools.reduce(lambda a, b: a + b, es)
            for k, e in zip(blks, es):
                att_ref[pl.ds(r0, rc), k * LANES:(k + 1) * LANES] = (o_refs[k][0, pl.ds(r0, rc), :] * (e * inv)).astype(BF)
        return c

    lax.fori_loop(0, tm // rc, body, 0, unroll=2)
    out_ref[0] = x_ref[0] + mod_ref[0, 2] * _dot(att_ref[...], wo_ref[...])


def _oproj_call(x, mods, os_, ls_, wo, *, tm):
    ns, s, d = x.shape
    r = mods.shape[2]
    kern = functools.partial(_oproj_kernel, tm=tm)
    gspec = pl.BlockSpec((1, tm, LANES), lambda n, t: (n, t, 0))
    return pl.pallas_call(
        kern,
        out_shape=jax.ShapeDtypeStruct((ns, s, d), F32),
        grid=(ns, s // tm),
        in_specs=[
            pl.BlockSpec((1, tm, d), lambda n, t: (n, t, 0)),
            pl.BlockSpec((1, N_MOD, r, d), lambda n, t: (n, 0, 0, 0)),
        ] + [gspec] * (2 * ATT_WIDTH // LANES) + [_const_spec((ATT_WIDTH, d))],
        out_specs=pl.BlockSpec((1, tm, d), lambda n, t: (n, t, 0)),
        scratch_shapes=[pltpu.VMEM((tm, ATT_WIDTH), BF)],
        compiler_params=_params(("arbitrary", "arbitrary")),
        name="attn_out_proj",
    )(x, mods, *os_, *ls_, wo)


def _trunk(x, mods, modkv, hist_a, hist_f, w, attn_fn, *, tm, rd):
    row = lambda v: v.reshape(1, -1)
    x, st_a = _conformer_call(x, mods[0], hist_a, row(w["norm_mix"][0]), w["a_w1"], row(w["a_b1"][0]),
                              w["a_dw"][0], row(w["a_dwb"][0]), row(w["a_ln_g"][0]), row(w["a_ln_b"][0]),
                              w["a_w2"], row(w["a_b2"][0]), tm=tm, rd=rd)
    x, st_f0 = _ffn_call(x, mods[0], hist_f[0], row(w["norm_ffn"][0]), w["f_wup"][0], w["f_cw"][0],
                         row(w["f_cb"][0]), w["f_wdown"][0], row(w["norm_f"]), tm=tm, rd=rd, final_norm=False)
    q, kv = _qkv_call(x, mods[1], modkv, row(w["norm_mix"][1]), row(w["norm_kv"]), w["w_q"], w["w_kv"], tm=tm)
    os_, ls_, attn_extra = attn_fn(q, kv)
    x = _oproj_call(x, mods[1], os_, ls_, w["w_o"], tm=tm)
    y, st_f1 = _ffn_call(x, mods[1], hist_f[1], row(w["norm_ffn"][1]), w["f_wup"][1], w["f_cw"][1],
                         row(w["f_cb"][1]), w["f_wdown"][1], row(w["norm_f"]), tm=tm, rd=rd, final_norm=True)
    return y, kv, st_a, [st_f0, st_f1], attn_extra


def kernel(x_prompt, x_sample, cache_kv_w128, cache_kv_w512, cache_kv_w2048, state_conv_a, state_conv_ffn, c_prompt, c_sample, w_mod, b_mod, norm_mix, norm_ffn, a_w1, a_b1, a_dw, a_dwb, a_ln_g, a_ln_b, a_w2, a_b2, w_mod_kv, b_mod_kv, norm_kv, w_kv, w_q, w_o, rel_bias, f_wup, f_cw, f_cb, f_wdown, norm_f):
    d = D_MODEL
    bsz, seq, _ = x_prompt.shape
    nseq, t_new, _ = x_sample.shape
    caches = (cache_kv_w128, cache_kv_w512, cache_kv_w2048)
    hpg = HEADS_PER_GROUP

    w = dict(norm_mix=norm_mix, norm_ffn=norm_ffn, a_w1=a_w1[0].astype(BF), a_b1=a_b1, a_dw=a_dw, a_dwb=a_dwb,
             a_ln_g=a_ln_g, a_ln_b=a_ln_b, a_w2=a_w2[0].astype(BF), a_b2=a_b2, norm_kv=norm_kv,
             w_kv=w_kv.astype(BF), w_q=w_q[0].astype(BF), w_o=w_o[0].astype(BF), f_wup=f_wup.astype(BF),
             f_cw=f_cw, f_cb=f_cb, f_wdown=f_wdown.astype(BF), norm_f=norm_f)

    n_c = bsz + nseq
    mp = _round_up(n_c, SUBLANES)
    c_all = jnp.pad(jnp.concatenate([c_prompt, c_sample], axis=0), ((0, mp - n_c), (0, 0)))
    mod = _mod_call(c_all, w_mod, b_mod.reshape(DEPTH, 1, N_MOD * d), tn=1536)
    modkv = _mod_call(c_all, w_mod_kv[None], b_mod_kv.reshape(1, 1, 2 * d), tn=1024)[0]

    mods_p = [mod[l, :bsz].reshape(bsz, N_MOD, 1, d) for l in range(DEPTH)]
    modkv_p = modkv[:bsz].reshape(bsz, 2, 1, d)
    hist_a_p = jnp.zeros((bsz, CONV_A_WIDTH - 1, d), F32)
    hist_f_p = [jnp.zeros((bsz, CONV_F_WIDTH - 1, 2 * D_FF), F32)] * DEPTH

    def prompt_attn(q, kv):
        res = [_pattn_call(q, kv, rel_bias, g, dil) for g, (_, dil) in enumerate(DIL_GROUPS)]
        return [a for r in res for a in r[0]], [a for r in res for a in r[1]], None

    y_p, kv_p, st_a_p, st_f_p, _ = _trunk(x_prompt, mods_p, modkv_p, hist_a_p, hist_f_p, w, prompt_attn,
                                          tm=TM_PROMPT, rd=1)
    wmax = max(c.shape[1] for c in caches)
    assert seq >= wmax
    kv_tail = kv_p[:, seq - wmax:].reshape(bsz, wmax, 2, N_HEADS, HEAD_DIM)
    kv_bufs_p = [kv_tail[:, wmax - c.shape[1]:, :, g * hpg:(g + 1) * hpg] for g, c in enumerate(caches)]
    conv_a_p = st_a_p[None]
    conv_f_p = jnp.stack(st_f_p)

    rows = t_new * nseq
    tmaj = lambda v: jnp.swapaxes(v, 0, 1).reshape(1, -1, v.shape[-1])
    smaj = lambda v, c: jnp.swapaxes(v.reshape(-1, nseq, c), 0, 1)

    def per_row(m, k):
        return jnp.tile(jnp.swapaxes(m.reshape(nseq, k, d), 0, 1), (1, t_new, 1))[None]

    mods_s = [per_row(mod[l, bsz:n_c], N_MOD) for l in range(DEPTH)]
    modkv_s = per_row(modkv[bsz:n_c], 2)
    hist_a_s = tmaj(state_conv_a[0])
    hist_f_s = [tmaj(state_conv_ffn[l]) for l in range(DEPTH)]

    def sample_attn(q, kv):
        os_, ls_, new_caches = _sattn_call(smaj(q, ATT_WIDTH), smaj(kv, 2 * ATT_WIDTH), caches, rel_bias)
        halves = lambda vs: [tmaj(a[..., c0:c0 + LANES]) for a in vs for c0 in range(0, GROUP_WIDTH, LANES)]
        return halves(os_), halves(ls_), new_caches

    y_s, _, st_a_s, st_f_s, kv_bufs_s = _trunk(tmaj(x_sample), mods_s, modkv_s, hist_a_s, hist_f_s, w,
                                               sample_attn, tm=rows, rd=nseq)
    y_s = smaj(y_s, d)
    conv_a_s = smaj(st_a_s, d)[None]
    conv_f_s = jnp.stack([smaj(s, 2 * D_FF) for s in st_f_s])

    return (y_p, y_s, kv_bufs_p[0], kv_bufs_p[1], kv_bufs_p[2], conv_a_p, conv_f_p,
            kv_bufs_s[0], kv_bufs_s[1], kv_bufs_s[2], conv_a_s, conv_f_s)
```

```python
import functools
import math

import jax
import jax.numpy as jnp
from jax import lax
from jax.experimental import pallas as pl
from jax.experimental.pallas import tpu as pltpu

D_MODEL = 1024
DEPTH = 2
HEAD_DIM = 64
HEADS_PER_GROUP = 4
DIL_GROUPS = ((128, 1), (512, 4), (2048, 16))
N_GROUPS = len(DIL_GROUPS)
N_HEADS = HEADS_PER_GROUP * N_GROUPS
ATT_WIDTH = N_HEADS * HEAD_DIM
GROUP_WIDTH = HEADS_PER_GROUP * HEAD_DIM
SPAN = 128
N_BUCKETS = 32
MAX_DISTANCE = 2048
CONV_A_WIDTH = 31
D_FF = 2816
CONV_F_WIDTH = 3
N_MOD = 6
EPS = 1e-6
LN_EPS = 1e-5
NEG_INF = -1e30

BF = jnp.bfloat16
F32 = jnp.float32

HEAD_SHIFT = HEAD_DIM.bit_length() - 1
SUBLANES = 8
SUBLANE_SHIFT = SUBLANES.bit_length() - 1
LANES = 128
BF16_ROWS = 16
ROW_UNROLL = 4
VMEM_LIMIT = 56 * 1024 * 1024

TM_PROMPT = 512
CONF_ROWS = 128
FF_SUB = 256
SAMPLE_QROWS = HEADS_PER_GROUP * SUBLANES


def _round_up(a, b):
    return -(-a // b) * b


def _params(sem):
    return pltpu.CompilerParams(dimension_semantics=sem, vmem_limit_bytes=VMEM_LIMIT)


def _const_spec(shape):
    nd = len(shape)
    return pl.BlockSpec(shape, lambda *_: (0,) * nd, pipeline_mode=pl.Buffered(1))


def _dot(a, b):
    return jnp.dot(a, b, preferred_element_type=F32)


def _dot_nt(a, b):
    return lax.dot_general(a, b, (((1,), (1,)), ((), ())), preferred_element_type=F32)


def _sigmoid(v):
    return 1.0 / (1.0 + jnp.exp(-v))


def _mod_rows(mod_ref, idx, r0, rows, per_row):
    if per_row:
        return mod_ref[0, idx, pl.ds(r0, rows), :]
    return mod_ref[0, idx]


def _rms_mod_rows(x_ref, tm, targets, per_row):
    rc = BF16_ROWS

    def body(i, c):
        r0 = pl.multiple_of(i * rc, rc)
        x = x_ref[0, pl.ds(r0, rc), :]
        xn = x * lax.rsqrt(jnp.mean(x * x, axis=-1, keepdims=True) + EPS)
        for gain_ref, mod_ref, i_sh, i_sc, dst_ref in targets:
            sh = _mod_rows(mod_ref, i_sh, r0, rc, per_row)
            sc = _mod_rows(mod_ref, i_sc, r0, rc, per_row)
            dst_ref[pl.ds(r0, rc), :] = ((xn * gain_ref[...]) * (1.0 + sc) + sh).astype(BF)
        return c

    lax.fori_loop(0, tm // rc, body, 0, unroll=ROW_UNROLL)


def _tap_conv(src_ref, blk, w_ref, offsets, r0, rows):
    cols = slice(blk * LANES, (blk + 1) * LANES)
    acc = None
    for k, off in enumerate(offsets):
        term = w_ref[k:k + 1, cols] * src_ref[blk, r0 + off:r0 + off + rows, :]
        acc = term if acc is None else acc + term
    return acc


def _mod_kernel(c_ref, w_ref, b_ref, o_ref):
    c = c_ref[...]
    a = (c * _sigmoid(c)).astype(BF)
    o_ref[0] = _dot(a, w_ref[0].astype(BF)) + b_ref[0]


def _mod_call(c_all, w, b, tn):
    nl, d, n = w.shape
    mp = c_all.shape[0]
    return pl.pallas_call(
        _mod_kernel,
        out_shape=jax.ShapeDtypeStruct((nl, mp, n), F32),
        grid=(nl, n // tn),
        in_specs=[
            pl.BlockSpec((mp, d), lambda l, j: (0, 0)),
            pl.BlockSpec((1, d, tn), lambda l, j: (l, 0, j)),
            pl.BlockSpec((1, 1, tn), lambda l, j: (l, 0, j)),
        ],
        out_specs=pl.BlockSpec((1, mp, tn), lambda l, j: (l, 0, j)),
        compiler_params=_params(("arbitrary", "arbitrary")),
        name="adaln_mod",
    )(c_all, w, b)


def _conformer_kernel(x_ref, mod_ref, hist_ref, nrm_ref, w1_ref, b1_ref, dw_ref, dwb_ref, lng_ref, lnb_ref,
                      w2_ref, b2_ref, o_ref, st_ref, h_ref, h2_ref, u_ref, full_ref, y_ref, *, tm, rd, per_row):
    d = D_MODEL
    hh = (CONV_A_WIDTH - 1) * rd
    hp = _round_up(hh, SUBLANES)
    t = pl.program_id(1)

    nblk = d // LANES
    lanes = lambda blk: slice(blk * LANES, (blk + 1) * LANES)

    @pl.when(t == 0)
    def _():
        full_ref[:, 0:hp, :] = jnp.zeros((nblk, hp, LANES), F32)
        for blk in range(nblk):
            full_ref[blk, hp - hh:hp, :] = hist_ref[0, :, lanes(blk)]

    _rms_mod_rows(x_ref, tm, [(nrm_ref, mod_ref, 0, 1, h_ref)], per_row)

    rb = min(tm, CONF_ROWS)
    rg, cg = 16, 512
    rcv = 64
    offsets = [hp - hh + k * rd for k in range(CONV_A_WIDTH)]
    for rs in range(0, tm, rb):
        u_ref[rs:rs + rb, :] = _dot(h_ref[rs:rs + rb, :], w1_ref[...])

        for r0 in range(rs, rs + rb, rg):
            for c0 in range(0, d, cg):
                a = u_ref[r0:r0 + rg, c0:c0 + cg] + b1_ref[:, c0:c0 + cg]
                g = u_ref[r0:r0 + rg, d + c0:d + c0 + cg] + b1_ref[:, d + c0:d + c0 + cg]
                glu = a * _sigmoid(g)
                for j in range(cg // LANES):
                    full_ref[c0 // LANES + j, hp + r0:hp + r0 + rg, :] = glu[:, lanes(j)]

        for r0 in range(rs, rs + rb, rcv):
            for blk in range(nblk):
                y_ref[r0:r0 + rcv, lanes(blk)] = (_tap_conv(full_ref, blk, dw_ref, offsets, r0, rcv)
                                                  + dwb_ref[:, lanes(blk)])

        for r0 in range(rs, rs + rb, BF16_ROWS):
            y = y_ref[r0:r0 + BF16_ROWS, :]
            mu = jnp.mean(y, axis=-1, keepdims=True)
            dv = y - mu
            var = jnp.mean(dv * dv, axis=-1, keepdims=True)
            yn = dv * lax.rsqrt(var + LN_EPS) * lng_ref[...] + lnb_ref[...]
            h2_ref[r0:r0 + BF16_ROWS, :] = (yn * _sigmoid(yn)).astype(BF)

        out = _dot(h2_ref[rs:rs + rb, :], w2_ref[...]) + b2_ref[...]
        gate = mod_ref[0, 2, rs:rs + rb, :] if per_row else mod_ref[0, 2]
        o_ref[0, rs:rs + rb, :] = x_ref[0, rs:rs + rb, :] + gate * out

    for blk in range(nblk):
        new_hist = full_ref[blk, hp + tm - hh:hp + tm, :]
        st_ref[0, :, lanes(blk)] = new_hist
        full_ref[blk, hp - hh:hp, :] = new_hist


def _conformer_call(x, mods, hist, nrm, w1, b1, dw, dwb, lng, lnb, w2, b2, *, tm, rd):
    ns, s, d = x.shape
    r = mods.shape[2]
    hh = hist.shape[1]
    hp = _round_up(hh, SUBLANES)
    per_row = r > 1
    assert s % tm == 0 and (not per_row or (r == tm and s == tm))
    kern = functools.partial(_conformer_kernel, tm=tm, rd=rd, per_row=per_row)
    return pl.pallas_call(
        kern,
        out_shape=(jax.ShapeDtypeStruct((ns, s, d), F32), jax.ShapeDtypeStruct((ns, hh, d), F32)),
        grid=(ns, s // tm),
        in_specs=[
            pl.BlockSpec((1, tm, d), lambda n, t: (n, t, 0)),
            pl.BlockSpec((1, N_MOD, r, d), lambda n, t: (n, 0, 0, 0)),
            pl.BlockSpec((1, hh, d), lambda n, t: (n, 0, 0)),
            _const_spec((1, d)),
            _const_spec((d, 2 * d)),
            _const_spec((1, 2 * d)),
            _const_spec((CONV_A_WIDTH, d)),
            _const_spec((1, d)),
            _const_spec((1, d)),
            _const_spec((1, d)),
            _const_spec((d, d)),
            _const_spec((1, d)),
        ],
        out_specs=(
            pl.BlockSpec((1, tm, d), lambda n, t: (n, t, 0)),
            pl.BlockSpec((1, hh, d), lambda n, t: (n, 0, 0)),
        ),
        scratch_shapes=[
            pltpu.VMEM((tm, d), BF),
            pltpu.VMEM((tm, d), BF),
            pltpu.VMEM((tm, 2 * d), F32),
            pltpu.VMEM((d // LANES, hp + tm, LANES), F32),
            pltpu.VMEM((tm, d), F32),
        ],
        compiler_params=_params(("arbitrary", "arbitrary")),
        name="conformer_mixer",
    )(x, mods, hist, nrm, w1, b1, dw, dwb, lng, lnb, w2, b2)


def _ffn_kernel(x_ref, mod_ref, hist_ref, nrm_ref, wu_ref, cw_ref, cb_ref, wd_ref, nf_ref, o_ref, st_ref,
                h_ref, ubuf_ref, carry_ref, act_ref, part_ref, *, tm, rd, per_row, final_norm):
    f = D_FF
    hh = (CONV_F_WIDTH - 1) * rd
    hp = _round_up(hh, SUBLANES)
    t = pl.program_id(1)

    _rms_mod_rows(x_ref, tm, [(nrm_ref, mod_ref, 3, 4, h_ref)], per_row)

    nblk = 2 * f // LANES
    lanes = lambda blk: slice(blk * LANES, (blk + 1) * LANES)

    @pl.when(t == 0)
    def _():
        carry_ref[...] = jnp.zeros((nblk, hp, LANES), F32)
        for blk in range(nblk):
            carry_ref[blk, hp - hh:hp, :] = hist_ref[0, :, lanes(blk)]

    ubuf_ref[:, 0:hp, :] = carry_ref[...]

    rc = 32
    offsets = [hp - hh + k * rd for k in range(CONV_F_WIDTH)]
    for c0 in range(0, f, FF_SUB):
        for half in range(2):
            col = half * f + c0
            u = _dot(h_ref[...], wu_ref[:, col:col + FF_SUB])
            for j in range(FF_SUB // LANES):
                ubuf_ref[col // LANES + j, hp:hp + tm, :] = u[:, lanes(j)]
        for r0 in range(0, tm, rc):
            for cc0 in range(c0, c0 + FF_SUB, LANES):
                ys = []
                for half in range(2):
                    blk = (half * f + cc0) // LANES
                    ys.append(_tap_conv(ubuf_ref, blk, cw_ref, offsets, r0, rc) + cb_ref[:, lanes(blk)])
                yg, yv = ys
                act_ref[r0:r0 + rc, cc0:cc0 + LANES] = (yg * _sigmoid(yg) * yv).astype(BF)

    for blk in range(nblk):
        st_ref[0, :, lanes(blk)] = ubuf_ref[blk, hp + tm - hh:hp + tm, :]
    carry_ref[...] = ubuf_ref[:, tm:tm + hp, :]

    xo = x_ref[0] + mod_ref[0, 5] * _dot(act_ref[...], wd_ref[...])
    if not final_norm:
        o_ref[0] = xo
        return
    part_ref[...] = xo
    rows = BF16_ROWS

    def norm_body(i, c):
        r0 = pl.multiple_of(i * rows, rows)
        v = part_ref[pl.ds(r0, rows), :]
        o_ref[0, pl.ds(r0, rows), :] = v * lax.rsqrt(jnp.mean(v * v, axis=-1, keepdims=True) + EPS) * nf_ref[...]
        return c

    lax.fori_loop(0, tm // rows, norm_body, 0, unroll=ROW_UNROLL)


def _ffn_call(x, mods, hist, nrm, wup, cw, cb, wdown, nf, *, tm, rd, final_norm):
    ns, s, d = x.shape
    r = mods.shape[2]
    hh = hist.shape[1]
    hp = _round_up(hh, SUBLANES)
    f = wdown.shape[0]
    per_row = r > 1
    assert f == D_FF and f % FF_SUB == 0 and s % tm == 0 and (not per_row or (r == tm and s == tm))
    kern = functools.partial(_ffn_kernel, tm=tm, rd=rd, per_row=per_row, final_norm=final_norm)
    return pl.pallas_call(
        kern,
        out_shape=(jax.ShapeDtypeStruct((ns, s, d), F32), jax.ShapeDtypeStruct((ns, hh, 2 * f), F32)),
        grid=(ns, s // tm),
        in_specs=[
            pl.BlockSpec((1, tm, d), lambda n, t: (n, t, 0)),
            pl.BlockSpec((1, N_MOD, r, d), lambda n, t: (n, 0, 0, 0)),
            pl.BlockSpec((1, hh, 2 * f), lambda n, t: (n, 0, 0)),
            _const_spec((1, d)),
            _const_spec((d, 2 * f)),
            _const_spec((CONV_F_WIDTH, 2 * f)),
            _const_spec((1, 2 * f)),
            _const_spec((f, d)),
            _const_spec((1, d)),
        ],
        out_specs=(
            pl.BlockSpec((1, tm, d), lambda n, t: (n, t, 0)),
            pl.BlockSpec((1, hh, 2 * f), lambda n, t: (n, 0, 0)),
        ),
        scratch_shapes=[
            pltpu.VMEM((tm, d), BF),
            pltpu.VMEM((2 * f // LANES, hp + tm, LANES), F32),
            pltpu.VMEM((2 * f // LANES, hp, LANES), F32),
            pltpu.VMEM((tm, f), BF),
            pltpu.VMEM((tm, d), F32),
        ],
        compiler_params=_params(("arbitrary", "arbitrary")),
        name="conv_ffn",
    )(x, mods, hist, nrm, wup, cw, cb, wdown, nf)


def _qkv_kernel(x_ref, mod_ref, modkv_ref, nq_ref, nkv_ref, wq_ref, wkv_ref, q_ref, kv_ref, hq_ref, hkv_ref,
                *, tm, per_row):
    _rms_mod_rows(x_ref, tm, [(nq_ref, mod_ref, 0, 1, hq_ref), (nkv_ref, modkv_ref, 0, 1, hkv_ref)], per_row)
    q_ref[0] = _dot(hq_ref[...], wq_ref[...])
    kv_ref[0] = _dot(hkv_ref[...], wkv_ref[...])


def _qkv_call(x, mods, modkv, nq, nkv, wq, wkv, *, tm):
    ns, s, d = x.shape
    r = mods.shape[2]
    per_row = r > 1
    kern = functools.partial(_qkv_kernel, tm=tm, per_row=per_row)
    return pl.pallas_call(
        kern,
        out_shape=(jax.ShapeDtypeStruct((ns, s, ATT_WIDTH), F32), jax.ShapeDtypeStruct((ns, s, 2 * ATT_WIDTH), F32)),
        grid=(ns, s // tm),
        in_specs=[
            pl.BlockSpec((1, tm, d), lambda n, t: (n, t, 0)),
            pl.BlockSpec((1, N_MOD, r, d), lambda n, t: (n, 0, 0, 0)),
            pl.BlockSpec((1, 2, r, d), lambda n, t: (n, 0, 0, 0)),
            _const_spec((1, d)),
            _const_spec((1, d)),
            _const_spec((d, ATT_WIDTH)),
            _const_spec((d, 2 * ATT_WIDTH)),
        ],
        out_specs=(
            pl.BlockSpec((1, tm, ATT_WIDTH), lambda n, t: (n, t, 0)),
            pl.BlockSpec((1, tm, 2 * ATT_WIDTH), lambda n, t: (n, t, 0)),
        ),
        scratch_shapes=[pltpu.VMEM((tm, d), BF), pltpu.VMEM((tm, d), BF)],
        compiler_params=_params(("arbitrary", "arbitrary")),
        name="qkv_proj",
    )(x, mods, modkv, nq, nkv, wq, wkv)


def _rel_bucket(dist):
    max_exact = N_BUCKETS // 2
    dd = jnp.maximum(dist, 1).astype(F32)
    large = max_exact + (jnp.log(dd / max_exact) / math.log(MAX_DISTANCE / max_exact)
                         * (N_BUCKETS - max_exact)).astype(jnp.int32)
    large = jnp.minimum(large, N_BUCKETS - 1)
    return jnp.where(dist < max_exact, dist, large)


def _prompt_bucket_table(dil):
    qi = jnp.arange(SPAN, dtype=jnp.int32)[:, None]
    ki = jnp.arange(2 * SPAN, dtype=jnp.int32)[None, :]
    m = qi + SPAN - ki
    valid = (m >= 0) & (m <= SPAN)
    return jnp.where(valid, _rel_bucket(jnp.clip(m, 0, SPAN) * dil), -1).astype(jnp.int32)


def _sample_bucket_tables(dil, width, t_new):
    row = jnp.arange(SAMPLE_QROWS, dtype=jnp.int32)[:, None]
    t = (row % SUBLANES) % t_new
    tn = jnp.arange(SUBLANES, dtype=jnp.int32)[None, :]

    def table(dist, ok):
        ok = ok & (dist >= 0) & (dist % dil == 0) & (dist // dil <= SPAN)
        return jnp.where(ok, _rel_bucket(jnp.clip(dist, 0, SPAN * dil)), -1).astype(jnp.int32)

    old = table(width + t - jnp.arange(width, dtype=jnp.int32)[None, :], True)
    new = table(t - tn, tn < t_new)
    return old, new


def _bias_from_buckets(bkt, relb_ref, head):
    acc = jnp.full(bkt.shape, NEG_INF, F32)
    for b in range(N_BUCKETS):
        acc = jnp.where(bkt == b, relb_ref[b, head], acc)
    return acc


def _pattn_kernel(relb_ref, bkt_ref, q0_ref, q1_ref, kp0_ref, kp1_ref, kc0_ref, kc1_ref, vp0_ref, vp1_ref,
                  vc0_ref, vc1_ref, o0_ref, o1_ref, l0_ref, l1_ref, bias_ref, *, g, dil):
    b = pl.program_id(0)
    i = pl.program_id(1)
    hpg = HEADS_PER_GROUP

    @pl.when((b == 0) & (i == 0))
    def _():
        bkt = bkt_ref[...]
        for h in range(hpg):
            bias_ref[h] = _bias_from_buckets(bkt, relb_ref, g * hpg + h)

    lanehead = lax.broadcasted_iota(jnp.int32, (SPAN, GROUP_WIDTH), 1) >> HEAD_SHIFT
    col = lax.broadcasted_iota(jnp.int32, (SPAN, 2 * SPAN), 1)
    keep = col >= jnp.where(i > 0, 0, SPAN)

    def body(r, c):
        rows = pl.ds(r, SPAN, stride=dil)
        both = lambda r0, r1: jnp.concatenate([r0[0, rows, :], r1[0, rows, :]], axis=1)
        q = both(q0_ref, q1_ref) * (HEAD_DIM ** -0.5)
        lhs = jnp.concatenate([jnp.where(lanehead == h, q, 0.0).astype(BF) for h in range(hpg)], axis=0)
        kcat = jnp.concatenate([both(kp0_ref, kp1_ref), both(kc0_ref, kc1_ref)], axis=0).astype(BF)
        s = _dot_nt(lhs, kcat)

        ps, ms, sums = [], [], []
        for h in range(hpg):
            logit = jnp.where(keep, s[h * SPAN:(h + 1) * SPAN] + bias_ref[h], NEG_INF)
            m = jnp.max(logit, axis=-1, keepdims=True)
            e = jnp.exp(logit - m)
            ps.append(e.astype(BF))
            ms.append(m)
            sums.append(jnp.sum(e, axis=-1, keepdims=True))
        vcat = jnp.concatenate([both(vp0_ref, vp1_ref), both(vc0_ref, vc1_ref)], axis=0).astype(BF)
        pv = _dot(jnp.concatenate(ps, axis=0), vcat)

        o = jnp.zeros((SPAN, GROUP_WIDTH), F32)
        l = jnp.zeros((SPAN, GROUP_WIDTH), F32)
        for h in range(hpg):
            o = jnp.where(lanehead == h, pv[h * SPAN:(h + 1) * SPAN] * (1.0 / sums[h]), o)
            l = jnp.where(lanehead == h, ms[h] + jnp.log(sums[h]), l)
        o0_ref[0, rows, :] = o[:, :LANES]
        o1_ref[0, rows, :] = o[:, LANES:]
        l0_ref[0, rows, :] = l[:, :LANES]
        l1_ref[0, rows, :] = l[:, LANES:]
        return c

    lax.fori_loop(0, dil, body, 0)


def _pattn_call(q, kv, rel_bias, g, dil):
    bsz, s, _ = q.shape
    rows = dil * SPAN
    assert s % rows == 0 and GROUP_WIDTH == 2 * LANES
    blk = (1, rows, LANES)
    kcol = 2 * g
    vcol = ATT_WIDTH // LANES + 2 * g
    prev = lambda i: jnp.maximum(i - 1, 0)
    out = jax.ShapeDtypeStruct((bsz, s, LANES), F32)
    ospec = pl.BlockSpec(blk, lambda b, i: (b, i, 0))
    res = pl.pallas_call(
        functools.partial(_pattn_kernel, g=g, dil=dil),
        out_shape=(out,) * 4,
        grid=(bsz, s // rows),
        in_specs=[
            pl.BlockSpec(memory_space=pltpu.SMEM),
            _const_spec((SPAN, 2 * SPAN)),
            pl.BlockSpec(blk, lambda b, i: (b, i, kcol)),
            pl.BlockSpec(blk, lambda b, i: (b, i, kcol + 1)),
            pl.BlockSpec(blk, lambda b, i: (b, prev(i), kcol)),
            pl.BlockSpec(blk, lambda b, i: (b, prev(i), kcol + 1)),
            pl.BlockSpec(blk, lambda b, i: (b, i, kcol)),
            pl.BlockSpec(blk, lambda b, i: (b, i, kcol + 1)),
            pl.BlockSpec(blk, lambda b, i: (b, prev(i), vcol)),
            pl.BlockSpec(blk, lambda b, i: (b, prev(i), vcol + 1)),
            pl.BlockSpec(blk, lambda b, i: (b, i, vcol)),
            pl.BlockSpec(blk, lambda b, i: (b, i, vcol + 1)),
        ],
        out_specs=(ospec,) * 4,
        scratch_shapes=[pltpu.VMEM((HEADS_PER_GROUP, SPAN, 2 * SPAN), F32)],
        compiler_params=_params(("arbitrary", "arbitrary")),
        name=f"prompt_attn_g{g}",
    )(rel_bias, _prompt_bucket_table(dil), q, q, kv, kv, kv, kv, kv, kv, kv, kv)
    return list(res[:2]), list(res[2:])


def _sattn_kernel(relb_ref, bo0_ref, bo1_ref, bo2_ref, bn_ref, q_ref, kvn_ref, kvt_ref, c0_ref, c1_ref, c2_ref,
                  o0_ref, o1_ref, o2_ref, l0_ref, l1_ref, l2_ref, n0_ref, n1_ref, n2_ref,
                  bias0_ref, bias1_ref, bias2_ref, biasn_ref, tail_ref, *, t_new):
    hpg = HEADS_PER_GROUP
    gw = GROUP_WIDTH
    rows = SAMPLE_QROWS
    bo_refs = (bo0_ref, bo1_ref, bo2_ref)
    bias_refs = (bias0_ref, bias1_ref, bias2_ref)
    caches = (c0_ref, c1_ref, c2_ref)
    outs = ((o0_ref, l0_ref, n0_ref), (o1_ref, l1_ref, n1_ref), (o2_ref, l2_ref, n2_ref))

    @pl.when(pl.program_id(0) == 0)
    def _():
        tail_ref[...] = jnp.zeros(tail_ref.shape, F32)
        for g in range(N_GROUPS):
            for h in range(hpg):
                sl = slice(h * SUBLANES, (h + 1) * SUBLANES)
                bias_refs[g][sl, :] = _bias_from_buckets(bo_refs[g][sl, :], relb_ref, g * hpg + h)
                biasn_ref[g, sl, :] = _bias_from_buckets(bn_ref[g, sl, :], relb_ref, g * hpg + h)

    lanehead = lax.broadcasted_iota(jnp.int32, (rows, gw), 1) >> HEAD_SHIFT
    rowhead = lax.broadcasted_iota(jnp.int32, (rows, gw), 0) >> SUBLANE_SHIFT
    own = lanehead == rowhead
    tail_lane = lax.broadcasted_iota(jnp.int32, (gw, LANES), 1) >= LANES - t_new

    for g in range(N_GROUPS):
        cref = caches[g]
        o_ref, l_ref, n_ref = outs[g]
        width = cref.shape[-1]
        qm = jnp.where(own, q_ref[0, :, g * gw:(g + 1) * gw] * (HEAD_DIM ** -0.5), 0.0).astype(BF)
        knew = kvn_ref[0, :, g * gw:(g + 1) * gw].astype(BF)
        vnew = kvn_ref[0, :, ATT_WIDTH + g * gw:ATT_WIDTH + (g + 1) * gw].astype(BF)
        lo = _dot(qm, cref[0, 0].astype(BF)) + bias_refs[g][...]
        ln = _dot_nt(qm, knew) + biasn_ref[g]
        m = jnp.maximum(jnp.max(lo, axis=-1, keepdims=True), jnp.max(ln, axis=-1, keepdims=True))
        eo = jnp.exp(lo - m)
        en = jnp.exp(ln - m)
        ssum = jnp.sum(eo, axis=-1, keepdims=True) + jnp.sum(en, axis=-1, keepdims=True)
        pv = _dot_nt(eo.astype(BF), cref[0, 1].astype(BF)) + _dot(en.astype(BF), vnew)
        om = jnp.where(own, pv * (1.0 / ssum), 0.0)
        lm = jnp.where(own, m + jnp.log(ssum), 0.0)
        o_acc = om[0:SUBLANES]
        l_acc = lm[0:SUBLANES]
        for h in range(1, hpg):
            o_acc = o_acc + om[h * SUBLANES:(h + 1) * SUBLANES]
            l_acc = l_acc + lm[h * SUBLANES:(h + 1) * SUBLANES]
        o_ref[0] = o_acc
        l_ref[0] = l_acc

        for kv in range(2):
            rolled = pltpu.roll(cref[0, kv], width - t_new, 1)
            tail_ref[:, 0:SUBLANES] = kvt_ref[0, kv * ATT_WIDTH + g * gw:kv * ATT_WIDTH + (g + 1) * gw, :]
            tail = pltpu.roll(tail_ref[...], LANES - t_new, 1)
            if width > LANES:
                n_ref[0, kv, :, 0:width - LANES] = rolled[:, 0:width - LANES]
            n_ref[0, kv, :, width - LANES:width] = jnp.where(tail_lane, tail, rolled[:, width - LANES:width])


def _sattn_call(q, kvn, caches, rel_bias):
    n, t_new, _ = q.shape
    gw = GROUP_WIDTH
    hpg = HEADS_PER_GROUP
    assert t_new <= SUBLANES
    views, bkt_old, bkt_new = [], [], []
    for g, (window, dil) in enumerate(DIL_GROUPS):
        width = caches[g].shape[1]
        assert width == window and width % LANES == 0
        views.append(jnp.transpose(caches[g], (0, 2, 3, 4, 1)).reshape(n, 2, gw, width))
        old, new = _sample_bucket_tables(dil, width, t_new)
        bkt_old.append(old)
        bkt_new.append(new)
    pad = SUBLANES - t_new
    qp = jnp.tile(jnp.pad(q, ((0, 0), (0, pad), (0, 0))), (1, hpg, 1))
    kvp = jnp.pad(kvn, ((0, 0), (0, pad), (0, 0)))
    kvt = jnp.swapaxes(kvp, 1, 2)
    out = jax.ShapeDtypeStruct((n, SUBLANES, gw), F32)
    oblk = pl.BlockSpec((1, SUBLANES, gw), lambda i: (i, 0, 0))
    cspecs = [pl.BlockSpec((1, 2, gw, v.shape[-1]), lambda i: (i, 0, 0, 0)) for v in views]
    res = pl.pallas_call(
        functools.partial(_sattn_kernel, t_new=t_new),
        out_shape=(out,) * 6 + tuple(jax.ShapeDtypeStruct(v.shape, F32) for v in views),
        grid=(n,),
        in_specs=[pl.BlockSpec(memory_space=pltpu.SMEM)]
        + [_const_spec(b.shape) for b in bkt_old]
        + [
            _const_spec((N_GROUPS, SAMPLE_QROWS, SUBLANES)),
            pl.BlockSpec((1, SAMPLE_QROWS, ATT_WIDTH), lambda i: (i, 0, 0)),
            pl.BlockSpec((1, SUBLANES, 2 * ATT_WIDTH), lambda i: (i, 0, 0)),
            pl.BlockSpec((1, 2 * ATT_WIDTH, SUBLANES), lambda i: (i, 0, 0)),
        ]
        + cspecs,
        out_specs=(oblk,) * 6 + tuple(cspecs),
        scratch_shapes=[pltpu.VMEM(b.shape, F32) for b in bkt_old]
        + [pltpu.VMEM((N_GROUPS, SAMPLE_QROWS, SUBLANES), F32), pltpu.VMEM((gw, LANES), F32)],
        compiler_params=_params(("arbitrary",)),
        name="sample_attn",
    )(rel_bias, *bkt_old, jnp.stack(bkt_new), qp, kvp, kvt, *views)
    new_caches = [jnp.transpose(c.reshape(n, 2, hpg, HEAD_DIM, c.shape[-1]), (0, 4, 1, 2, 3)) for c in res[6:]]
    return [a[:, :t_new] for a in res[:3]], [a[:, :t_new] for a in res[3:6]], new_caches


def _oproj_kernel(x_ref, mod_ref, *refs, tm):
    nblk = ATT_WIDTH // LANES
    halves = nblk // N_GROUPS
    o_refs, l_refs = refs[:nblk], refs[nblk:2 * nblk]
    wo_ref, out_ref, att_ref = refs[2 * nblk:]
    rc = BF16_ROWS

    def body(i, c):
        r0 = pl.multiple_of(i * rc, rc)
        for hf in range(halves):
            blks = [g * halves + hf for g in range(N_GROUPS)]
            ls = [l_refs[k][0, pl.ds(r0, rc), :] for k in blks]
            mx = functools.reduce(jnp.maximum, ls)
            es = [jnp.exp(v - mx) for v in ls]
            inv = 1.0 / functools.reduce(lambda a, b: a + b, es)
            for k, e in zip(blks, es):
                att_ref[pl.ds(r0, rc), k * LANES:(k + 1) * LANES] = (o_refs[k][0, pl.ds(r0, rc), :] * (e * inv)).astype(BF)
        return c

    lax.fori_loop(0, tm // rc, body, 0, unroll=2)
    out_ref[0] = x_ref[0] + mod_ref[0, 2] * _dot(att_ref[...], wo_ref[...])


def _oproj_call(x, mods, os_, ls_, wo, *, tm):
    ns, s, d = x.shape
    r = mods.shape[2]
    kern = functools.partial(_oproj_kernel, tm=tm)
    gspec = pl.BlockSpec((1, tm, LANES), lambda n, t: (n, t, 0))
    return pl.pallas_call(
        kern,
        out_shape=jax.ShapeDtypeStruct((ns, s, d), F32),
        grid=(ns, s // tm),
        in_specs=[
            pl.BlockSpec((1, tm, d), lambda n, t: (n, t, 0)),
            pl.BlockSpec((1, N_MOD, r, d), lambda n, t: (n, 0, 0, 0)),
        ] + [gspec] * (2 * ATT_WIDTH // LANES) + [_const_spec((ATT_WIDTH, d))],
        out_specs=pl.BlockSpec((1, tm, d), lambda n, t: (n, t, 0)),
        scratch_shapes=[pltpu.VMEM((tm, ATT_WIDTH), BF)],
        compiler_params=_params(("arbitrary", "arbitrary")),
        name="attn_out_proj",
    )(x, mods, *os_, *ls_, wo)


def _trunk(x, mods, modkv, hist_a, hist_f, w, attn_fn, *, tm, rd):
    row = lambda v: v.reshape(1, -1)
    x, st_a = _conformer_call(x, mods[0], hist_a, row(w["norm_mix"][0]), w["a_w1"], row(w["a_b1"][0]),
                              w["a_dw"][0], row(w["a_dwb"][0]), row(w["a_ln_g"][0]), row(w["a_ln_b"][0]),
                              w["a_w2"], row(w["a_b2"][0]), tm=tm, rd=rd)
    x, st_f0 = _ffn_call(x, mods[0], hist_f[0], row(w["norm_ffn"][0]), w["f_wup"][0], w["f_cw"][0],
                         row(w["f_cb"][0]), w["f_wdown"][0], row(w["norm_f"]), tm=tm, rd=rd, final_norm=False)
    q, kv = _qkv_call(x, mods[1], modkv, row(w["norm_mix"][1]), row(w["norm_kv"]), w["w_q"], w["w_kv"], tm=tm)
    os_, ls_, attn_extra = attn_fn(q, kv)
    x = _oproj_call(x, mods[1], os_, ls_, w["w_o"], tm=tm)
    y, st_f1 = _ffn_call(x, mods[1], hist_f[1], row(w["norm_ffn"][1]), w["f_wup"][1], w["f_cw"][1],
                         row(w["f_cb"][1]), w["f_wdown"][1], row(w["norm_f"]), tm=tm, rd=rd, final_norm=True)
    return y, kv, st_a, [st_f0, st_f1], attn_extra


def kernel(x_prompt, x_sample, cache_kv_w128, cache_kv_w512, cache_kv_w2048, state_conv_a, state_conv_ffn, c_prompt, c_sample, w_mod, b_mod, norm_mix, norm_ffn, a_w1, a_b1, a_dw, a_dwb, a_ln_g, a_ln_b, a_w2, a_b2, w_mod_kv, b_mod_kv, norm_kv, w_kv, w_q, w_o, rel_bias, f_wup, f_cw, f_cb, f_wdown, norm_f):
    d = D_MODEL
    bsz, seq, _ = x_prompt.shape
    nseq, t_new, _ = x_sample.shape
    caches = (cache_kv_w128, cache_kv_w512, cache_kv_w2048)
    hpg = HEADS_PER_GROUP

    w = dict(norm_mix=norm_mix, norm_ffn=norm_ffn, a_w1=a_w1[0].astype(BF), a_b1=a_b1, a_dw=a_dw, a_dwb=a_dwb,
             a_ln_g=a_ln_g, a_ln_b=a_ln_b, a_w2=a_w2[0].astype(BF), a_b2=a_b2, norm_kv=norm_kv,
             w_kv=w_kv.astype(BF), w_q=w_q[0].astype(BF), w_o=w_o[0].astype(BF),
             f_wup=[f_wup[l].astype(BF) for l in range(DEPTH)], f_cw=f_cw, f_cb=f_cb,
             f_wdown=[f_wdown[l].astype(BF) for l in range(DEPTH)], norm_f=norm_f)

    n_c = bsz + nseq
    mp = _round_up(n_c, SUBLANES)
    c_all = jnp.pad(jnp.concatenate([c_prompt, c_sample], axis=0), ((0, mp - n_c), (0, 0)))
    mod = _mod_call(c_all, w_mod, b_mod.reshape(DEPTH, 1, N_MOD * d), tn=1536)
    modkv = _mod_call(c_all, w_mod_kv[None], b_mod_kv.reshape(1, 1, 2 * d), tn=1024)[0]

    mods_p = [mod[l, :bsz].reshape(bsz, N_MOD, 1, d) for l in range(DEPTH)]
    modkv_p = modkv[:bsz].reshape(bsz, 2, 1, d)
    hist_a_p = jnp.zeros((bsz, CONV_A_WIDTH - 1, d), F32)
    hist_f_p = [jnp.zeros((bsz, CONV_F_WIDTH - 1, 2 * D_FF), F32)] * DEPTH

    def prompt_attn(q, kv):
        res = [_pattn_call(q, kv, rel_bias, g, dil) for g, (_, dil) in enumerate(DIL_GROUPS)]
        return [a for r in res for a in r[0]], [a for r in res for a in r[1]], None

    y_p, kv_p, st_a_p, st_f_p, _ = _trunk(x_prompt, mods_p, modkv_p, hist_a_p, hist_f_p, w, prompt_attn,
                                          tm=TM_PROMPT, rd=1)
    wmax = max(c.shape[1] for c in caches)
    assert seq >= wmax
    kv_tail = kv_p[:, seq - wmax:].reshape(bsz, wmax, 2, N_HEADS, HEAD_DIM)
    kv_bufs_p = [kv_tail[:, wmax - c.shape[1]:, :, g * hpg:(g + 1) * hpg] for g, c in enumerate(caches)]
    conv_a_p = st_a_p[None]
    conv_f_p = jnp.stack(st_f_p)

    rows = t_new * nseq
    tmaj = lambda v: jnp.swapaxes(v, 0, 1).reshape(1, -1, v.shape[-1])
    smaj = lambda v, c: jnp.swapaxes(v.reshape(-1, nseq, c), 0, 1)

    def per_row(m, k):
        return jnp.tile(jnp.swapaxes(m.reshape(nseq, k, d), 0, 1), (1, t_new, 1))[None]

    mods_s = [per_row(mod[l, bsz:n_c], N_MOD) for l in range(DEPTH)]
    modkv_s = per_row(modkv[bsz:n_c], 2)
    hist_a_s = tmaj(state_conv_a[0])
    hist_f_s = [tmaj(state_conv_ffn[l]) for l in range(DEPTH)]

    def sample_attn(q, kv):
        os_, ls_, new_caches = _sattn_call(smaj(q, ATT_WIDTH), smaj(kv, 2 * ATT_WIDTH), caches, rel_bias)
        halves = lambda vs: [tmaj(a[..., c0:c0 + LANES]) for a in vs for c0 in range(0, GROUP_WIDTH, LANES)]
        return halves(os_), halves(ls_), new_caches

    y_s, _, st_a_s, st_f_s, kv_bufs_s = _trunk(tmaj(x_sample), mods_s, modkv_s, hist_a_s, hist_f_s, w,
                                               sample_attn, tm=rows, rd=nseq)
    y_s = smaj(y_s, d)
    conv_a_s = smaj(st_a_s, d)[None]
    conv_f_s = jnp.stack([smaj(s, 2 * D_FF) for s in st_f_s])

    return (y_p, y_s, kv_bufs_p[0], kv_bufs_p[1], kv_bufs_p[2], conv_a_p, conv_f_p,
            kv_bufs_s[0], kv_bufs_s[1], kv_bufs_s[2], conv_a_s, conv_f_s)
```

```python
import functools
import math

import jax
import jax.numpy as jnp
from jax import lax
from jax.experimental import pallas as pl
from jax.experimental.pallas import tpu as pltpu

D_MODEL = 1024
DEPTH = 2
HEAD_DIM = 64
HEADS_PER_GROUP = 4
DIL_GROUPS = ((128, 1), (512, 4), (2048, 16))
N_GROUPS = len(DIL_GROUPS)
N_HEADS = HEADS_PER_GROUP * N_GROUPS
ATT_WIDTH = N_HEADS * HEAD_DIM
GROUP_WIDTH = HEADS_PER_GROUP * HEAD_DIM
SPAN = 128
N_BUCKETS = 32
MAX_DISTANCE = 2048
CONV_A_WIDTH = 31
D_FF = 2816
CONV_F_WIDTH = 3
N_MOD = 6
EPS = 1e-6
LN_EPS = 1e-5
NEG_INF = -1e30

BF = jnp.bfloat16
F32 = jnp.float32

HEAD_SHIFT = HEAD_DIM.bit_length() - 1
SUBLANES = 8
SUBLANE_SHIFT = SUBLANES.bit_length() - 1
LANES = 128
BF16_ROWS = 16
ROW_UNROLL = 4
VMEM_LIMIT = 56 * 1024 * 1024

TM_PROMPT = 512
CONF_ROWS = 128
FF_SUB = 256
SAMPLE_QROWS = HEADS_PER_GROUP * SUBLANES


def _round_up(a, b):
    return -(-a // b) * b


def _params(sem):
    return pltpu.CompilerParams(dimension_semantics=sem, vmem_limit_bytes=VMEM_LIMIT)


def _const_spec(shape):
    nd = len(shape)
    return pl.BlockSpec(shape, lambda *_: (0,) * nd, pipeline_mode=pl.Buffered(1))


def _dot(a, b):
    return jnp.dot(a, b, preferred_element_type=F32)


def _dot_nt(a, b):
    return lax.dot_general(a, b, (((1,), (1,)), ((), ())), preferred_element_type=F32)


def _sigmoid(v):
    return 1.0 / (1.0 + jnp.exp(-v))


def _mod_rows(mod_ref, idx, r0, rows, per_row):
    if per_row:
        return mod_ref[0, idx, pl.ds(r0, rows), :]
    return mod_ref[0, idx]


def _rms_mod_rows(x_ref, rows, targets, per_row, *, slot=None, straight=False):
    rc = BF16_ROWS
    lo, hi = rows

    def chunk(r0):
        x = x_ref[0, pl.ds(r0, rc), :]
        xn = x * lax.rsqrt(jnp.mean(x * x, axis=-1, keepdims=True) + EPS)
        for gain_ref, mod_ref, i_sh, i_sc, dst_ref in targets:
            sh = _mod_rows(mod_ref, i_sh, r0, rc, per_row)
            sc = _mod_rows(mod_ref, i_sc, r0, rc, per_row)
            idx = (pl.ds(r0, rc), slice(None))
            dst_ref[idx if slot is None else (slot,) + idx] = ((xn * gain_ref[...]) * (1.0 + sc) + sh).astype(BF)

    if straight:
        for r0 in range(lo, hi, rc):
            chunk(r0)
        return

    def body(i, c):
        chunk(pl.multiple_of(lo + i * rc, rc))
        return c

    lax.fori_loop(0, (hi - lo) // rc, body, 0, unroll=ROW_UNROLL)


def _tap_conv(src_ref, blk, w_ref, offsets, r0, rows):
    cols = slice(blk * LANES, (blk + 1) * LANES)
    acc = None
    for k, off in enumerate(offsets):
        term = w_ref[k:k + 1, cols] * src_ref[blk, r0 + off:r0 + off + rows, :]
        acc = term if acc is None else acc + term
    return acc


def _mod_kernel(c_ref, w_ref, b_ref, o_ref):
    c = c_ref[...]
    a = (c * _sigmoid(c)).astype(BF)
    o_ref[0] = _dot(a, w_ref[0].astype(BF)) + b_ref[0]


def _mod_call(c_all, w, b, tn):
    nl, d, n = w.shape
    mp = c_all.shape[0]
    return pl.pallas_call(
        _mod_kernel,
        out_shape=jax.ShapeDtypeStruct((nl, mp, n), F32),
        grid=(nl, n // tn),
        in_specs=[
            pl.BlockSpec((mp, d), lambda l, j: (0, 0)),
            pl.BlockSpec((1, d, tn), lambda l, j: (l, 0, j)),
            pl.BlockSpec((1, 1, tn), lambda l, j: (l, 0, j)),
        ],
        out_specs=pl.BlockSpec((1, mp, tn), lambda l, j: (l, 0, j)),
        compiler_params=_params(("arbitrary", "arbitrary")),
        name="adaln_mod",
    )(c_all, w, b)


def _conformer_kernel(x_ref, mod_ref, hist_ref, nrm_ref, w1_ref, b1_ref, dw_ref, dwb_ref, lng_ref, lnb_ref,
                      w2_ref, b2_ref, o_ref, st_ref, h_ref, h2_ref, u_ref, full_ref, y_ref, *, tm, rd, per_row):
    d = D_MODEL
    hh = (CONV_A_WIDTH - 1) * rd
    hp = _round_up(hh, SUBLANES)
    t = pl.program_id(1)

    nblk = d // LANES
    lanes = lambda blk: slice(blk * LANES, (blk + 1) * LANES)

    @pl.when(t == 0)
    def _():
        full_ref[:, 0:hp, :] = jnp.zeros((nblk, hp, LANES), F32)
        for blk in range(nblk):
            full_ref[blk, hp - hh:hp, :] = hist_ref[0, :, lanes(blk)]

    rb = min(tm, CONF_ROWS)
    rg, cg = 16, 512
    rcv = 64
    offsets = [hp - hh + k * rd for k in range(CONV_A_WIDTH)]
    for rs in range(0, tm, rb):
        _rms_mod_rows(x_ref, (rs, rs + rb), [(nrm_ref, mod_ref, 0, 1, h_ref)], per_row, straight=True)
        u_ref[rs:rs + rb, :] = _dot(h_ref[rs:rs + rb, :], w1_ref[...])

        for r0 in range(rs, rs + rb, rg):
            for c0 in range(0, d, cg):
                a = u_ref[r0:r0 + rg, c0:c0 + cg] + b1_ref[:, c0:c0 + cg]
                g = u_ref[r0:r0 + rg, d + c0:d + c0 + cg] + b1_ref[:, d + c0:d + c0 + cg]
                glu = a * _sigmoid(g)
                for j in range(cg // LANES):
                    full_ref[c0 // LANES + j, hp + r0:hp + r0 + rg, :] = glu[:, lanes(j)]

        for r0 in range(rs, rs + rb, rcv):
            for blk in range(nblk):
                y_ref[r0:r0 + rcv, lanes(blk)] = (_tap_conv(full_ref, blk, dw_ref, offsets, r0, rcv)
                                                  + dwb_ref[:, lanes(blk)])

        for r0 in range(rs, rs + rb, BF16_ROWS):
            y = y_ref[r0:r0 + BF16_ROWS, :]
            mu = jnp.mean(y, axis=-1, keepdims=True)
            dv = y - mu
            var = jnp.mean(dv * dv, axis=-1, keepdims=True)
            yn = dv * lax.rsqrt(var + LN_EPS) * lng_ref[...] + lnb_ref[...]
            h2_ref[r0:r0 + BF16_ROWS, :] = (yn * _sigmoid(yn)).astype(BF)

        out = _dot(h2_ref[rs:rs + rb, :], w2_ref[...]) + b2_ref[...]
        gate = mod_ref[0, 2, rs:rs + rb, :] if per_row else mod_ref[0, 2]
        o_ref[0, rs:rs + rb, :] = x_ref[0, rs:rs + rb, :] + gate * out

    for blk in range(nblk):
        new_hist = full_ref[blk, hp + tm - hh:hp + tm, :]
        st_ref[0, :, lanes(blk)] = new_hist
        full_ref[blk, hp - hh:hp, :] = new_hist


def _conformer_call(x, mods, hist, nrm, w1, b1, dw, dwb, lng, lnb, w2, b2, *, tm, rd):
    ns, s, d = x.shape
    r = mods.shape[2]
    hh = hist.shape[1]
    hp = _round_up(hh, SUBLANES)
    per_row = r > 1
    assert s % tm == 0 and (not per_row or (r == tm and s == tm))
    kern = functools.partial(_conformer_kernel, tm=tm, rd=rd, per_row=per_row)
    return pl.pallas_call(
        kern,
        out_shape=(jax.ShapeDtypeStruct((ns, s, d), F32), jax.ShapeDtypeStruct((ns, hh, d), F32)),
        grid=(ns, s // tm),
        in_specs=[
            pl.BlockSpec((1, tm, d), lambda n, t: (n, t, 0)),
            pl.BlockSpec((1, N_MOD, r, d), lambda n, t: (n, 0, 0, 0)),
            pl.BlockSpec((1, hh, d), lambda n, t: (n, 0, 0)),
            _const_spec((1, d)),
            _const_spec((d, 2 * d)),
            _const_spec((1, 2 * d)),
            _const_spec((CONV_A_WIDTH, d)),
            _const_spec((1, d)),
            _const_spec((1, d)),
            _const_spec((1, d)),
            _const_spec((d, d)),
            _const_spec((1, d)),
        ],
        out_specs=(
            pl.BlockSpec((1, tm, d), lambda n, t: (n, t, 0)),
            pl.BlockSpec((1, hh, d), lambda n, t: (n, 0, 0)),
        ),
        scratch_shapes=[
            pltpu.VMEM((tm, d), BF),
            pltpu.VMEM((tm, d), BF),
            pltpu.VMEM((tm, 2 * d), F32),
            pltpu.VMEM((d // LANES, hp + tm, LANES), F32),
            pltpu.VMEM((tm, d), F32),
        ],
        compiler_params=_params(("arbitrary", "arbitrary")),
        name="conformer_mixer",
    )(x, mods, hist, nrm, w1, b1, dw, dwb, lng, lnb, w2, b2)


def _ffn_kernel(x_ref, xnext_ref, mod_ref, hist_ref, nrm_ref, wu_ref, cw_ref, cb_ref, wd_ref, nf_ref, o_ref, st_ref,
                h_ref, ubuf_ref, carry_ref, act_ref, part_ref, *, tm, rd, per_row, final_norm):
    f = D_FF
    hh = (CONV_F_WIDTH - 1) * rd
    hp = _round_up(hh, SUBLANES)
    t = pl.program_id(1)

    cur = lax.rem(t, 2)
    norm_targets = [(nrm_ref, mod_ref, 3, 4, h_ref)]

    @pl.when(t == 0)
    def _():
        _rms_mod_rows(x_ref, (0, tm), norm_targets, per_row, slot=0)

    nblk = 2 * f // LANES
    lanes = lambda blk: slice(blk * LANES, (blk + 1) * LANES)

    @pl.when(t == 0)
    def _():
        carry_ref[...] = jnp.zeros((nblk, hp, LANES), F32)
        for blk in range(nblk):
            carry_ref[blk, hp - hh:hp, :] = hist_ref[0, :, lanes(blk)]

    ubuf_ref[:, 0:hp, :] = carry_ref[...]

    rc = 32
    offsets = [hp - hh + k * rd for k in range(CONV_F_WIDTH)]
    for c0 in range(0, f, FF_SUB):
        for half in range(2):
            col = half * f + c0
            u = _dot(h_ref[cur], wu_ref[:, col:col + FF_SUB])
            for j in range(FF_SUB // LANES):
                ubuf_ref[col // LANES + j, hp:hp + tm, :] = u[:, lanes(j)]
        for r0 in range(0, tm, rc):
            for cc0 in range(c0, c0 + FF_SUB, LANES):
                ys = []
                for half in range(2):
                    blk = (half * f + cc0) // LANES
                    ys.append(_tap_conv(ubuf_ref, blk, cw_ref, offsets, r0, rc) + cb_ref[:, lanes(blk)])
                yg, yv = ys
                act_ref[r0:r0 + rc, cc0:cc0 + LANES] = (yg * _sigmoid(yg) * yv).astype(BF)

    for blk in range(nblk):
        st_ref[0, :, lanes(blk)] = ubuf_ref[blk, hp + tm - hh:hp + tm, :]
    carry_ref[...] = ubuf_ref[:, tm:tm + hp, :]

    _rms_mod_rows(xnext_ref, (0, tm), norm_targets, per_row, slot=1 - cur, straight=True)
    xo = x_ref[0] + mod_ref[0, 5] * _dot(act_ref[...], wd_ref[...])
    if not final_norm:
        o_ref[0] = xo
        return
    part_ref[...] = xo
    rows = BF16_ROWS

    def norm_body(i, c):
        r0 = pl.multiple_of(i * rows, rows)
        v = part_ref[pl.ds(r0, rows), :]
        o_ref[0, pl.ds(r0, rows), :] = v * lax.rsqrt(jnp.mean(v * v, axis=-1, keepdims=True) + EPS) * nf_ref[...]
        return c

    lax.fori_loop(0, tm // rows, norm_body, 0, unroll=ROW_UNROLL)


def _ffn_call(x, mods, hist, nrm, wup, cw, cb, wdown, nf, *, tm, rd, final_norm):
    ns, s, d = x.shape
    r = mods.shape[2]
    hh = hist.shape[1]
    hp = _round_up(hh, SUBLANES)
    f = wdown.shape[0]
    per_row = r > 1
    nt = s // tm
    assert f == D_FF and f % FF_SUB == 0 and s % tm == 0 and (not per_row or (r == tm and s == tm))
    kern = functools.partial(_ffn_kernel, tm=tm, rd=rd, per_row=per_row, final_norm=final_norm)
    return pl.pallas_call(
        kern,
        out_shape=(jax.ShapeDtypeStruct((ns, s, d), F32), jax.ShapeDtypeStruct((ns, hh, 2 * f), F32)),
        grid=(ns, nt),
        in_specs=[
            pl.BlockSpec((1, tm, d), lambda n, t: (n, t, 0)),
            pl.BlockSpec((1, tm, d), lambda n, t: (n, jnp.minimum(t + 1, nt - 1), 0)),
            pl.BlockSpec((1, N_MOD, r, d), lambda n, t: (n, 0, 0, 0)),
            pl.BlockSpec((1, hh, 2 * f), lambda n, t: (n, 0, 0)),
            _const_spec((1, d)),
            _const_spec((d, 2 * f)),
            _const_spec((CONV_F_WIDTH, 2 * f)),
            _const_spec((1, 2 * f)),
            _const_spec((f, d)),
            _const_spec((1, d)),
        ],
        out_specs=(
            pl.BlockSpec((1, tm, d), lambda n, t: (n, t, 0)),
            pl.BlockSpec((1, hh, 2 * f), lambda n, t: (n, 0, 0)),
        ),
        scratch_shapes=[
            pltpu.VMEM((2, tm, d), BF),
            pltpu.VMEM((2 * f // LANES, hp + tm, LANES), F32),
            pltpu.VMEM((2 * f // LANES, hp, LANES), F32),
            pltpu.VMEM((tm, f), BF),
            pltpu.VMEM((tm, d), F32),
        ],
        compiler_params=_params(("arbitrary", "arbitrary")),
        name="conv_ffn",
    )(x, x, mods, hist, nrm, wup, cw, cb, wdown, nf)


def _qkv_kernel(x_ref, xnext_ref, mod_ref, modkv_ref, nq_ref, nkv_ref, wq_ref, wkv_ref, q_ref, kv_ref,
                hq_ref, hkv_ref, *, tm, per_row):
    t = pl.program_id(1)
    cur = lax.rem(t, 2)
    norm_targets = [(nq_ref, mod_ref, 0, 1, hq_ref), (nkv_ref, modkv_ref, 0, 1, hkv_ref)]

    @pl.when(t == 0)
    def _():
        _rms_mod_rows(x_ref, (0, tm), norm_targets, per_row, slot=0)

    q_ref[0] = _dot(hq_ref[cur], wq_ref[...])
    _rms_mod_rows(xnext_ref, (0, tm), norm_targets, per_row, slot=1 - cur, straight=True)
    kv_ref[0] = _dot(hkv_ref[cur], wkv_ref[...])


def _qkv_call(x, mods, modkv, nq, nkv, wq, wkv, *, tm):
    ns, s, d = x.shape
    r = mods.shape[2]
    per_row = r > 1
    nt = s // tm
    kern = functools.partial(_qkv_kernel, tm=tm, per_row=per_row)
    return pl.pallas_call(
        kern,
        out_shape=(jax.ShapeDtypeStruct((ns, s, ATT_WIDTH), F32), jax.ShapeDtypeStruct((ns, s, 2 * ATT_WIDTH), F32)),
        grid=(ns, nt),
        in_specs=[
            pl.BlockSpec((1, tm, d), lambda n, t: (n, t, 0)),
            pl.BlockSpec((1, tm, d), lambda n, t: (n, jnp.minimum(t + 1, nt - 1), 0)),
            pl.BlockSpec((1, N_MOD, r, d), lambda n, t: (n, 0, 0, 0)),
            pl.BlockSpec((1, 2, r, d), lambda n, t: (n, 0, 0, 0)),
            _const_spec((1, d)),
            _const_spec((1, d)),
            _const_spec((d, ATT_WIDTH)),
            _const_spec((d, 2 * ATT_WIDTH)),
        ],
        out_specs=(
            pl.BlockSpec((1, tm, ATT_WIDTH), lambda n, t: (n, t, 0)),
            pl.BlockSpec((1, tm, 2 * ATT_WIDTH), lambda n, t: (n, t, 0)),
        ),
        scratch_shapes=[pltpu.VMEM((2, tm, d), BF), pltpu.VMEM((2, tm, d), BF)],
        compiler_params=_params(("arbitrary", "arbitrary")),
        name="qkv_proj",
    )(x, x, mods, modkv, nq, nkv, wq, wkv)


def _rel_bucket(dist):
    max_exact = N_BUCKETS // 2
    dd = jnp.maximum(dist, 1).astype(F32)
    large = max_exact + (jnp.log(dd / max_exact) / math.log(MAX_DISTANCE / max_exact)
                         * (N_BUCKETS - max_exact)).astype(jnp.int32)
    large = jnp.minimum(large, N_BUCKETS - 1)
    return jnp.where(dist < max_exact, dist, large)


def _prompt_bucket_table(dil):
    qi = jnp.arange(SPAN, dtype=jnp.int32)[:, None]
    ki = jnp.arange(2 * SPAN, dtype=jnp.int32)[None, :]
    m = qi + SPAN - ki
    valid = (m >= 0) & (m <= SPAN)
    return jnp.where(valid, _rel_bucket(jnp.clip(m, 0, SPAN) * dil), -1).astype(jnp.int32)


def _sample_bucket_tables(dil, width, t_new):
    row = jnp.arange(SAMPLE_QROWS, dtype=jnp.int32)[:, None]
    t = (row % SUBLANES) % t_new
    tn = jnp.arange(SUBLANES, dtype=jnp.int32)[None, :]

    def table(dist, ok):
        ok = ok & (dist >= 0) & (dist % dil == 0) & (dist // dil <= SPAN)
        return jnp.where(ok, _rel_bucket(jnp.clip(dist, 0, SPAN * dil)), -1).astype(jnp.int32)

    old = table(width + t - jnp.arange(width, dtype=jnp.int32)[None, :], True)
    new = table(t - tn, tn < t_new)
    return old, new


def _bias_from_buckets(bkt, relb_ref, head):
    acc = jnp.full(bkt.shape, NEG_INF, F32)
    for b in range(N_BUCKETS):
        acc = jnp.where(bkt == b, relb_ref[b, head], acc)
    return acc


def _pattn_kernel(relb_ref, bkt_ref, q0_ref, q1_ref, kp0_ref, kp1_ref, kc0_ref, kc1_ref, vp0_ref, vp1_ref,
                  vc0_ref, vc1_ref, o0_ref, o1_ref, l0_ref, l1_ref, bias_ref, *, g, dil):
    b = pl.program_id(0)
    i = pl.program_id(1)
    hpg = HEADS_PER_GROUP

    @pl.when((b == 0) & (i == 0))
    def _():
        bkt = bkt_ref[...]
        for h in range(hpg):
            bias_ref[h] = _bias_from_buckets(bkt, relb_ref, g * hpg + h)

    lanehead = lax.broadcasted_iota(jnp.int32, (SPAN, GROUP_WIDTH), 1) >> HEAD_SHIFT
    col = lax.broadcasted_iota(jnp.int32, (SPAN, 2 * SPAN), 1)
    keep = col >= jnp.where(i > 0, 0, SPAN)

    def body(r, c):
        rows = pl.ds(r, SPAN, stride=dil)
        both = lambda r0, r1: jnp.concatenate([r0[0, rows, :], r1[0, rows, :]], axis=1)
        q = both(q0_ref, q1_ref) * (HEAD_DIM ** -0.5)
        lhs = jnp.concatenate([jnp.where(lanehead == h, q, 0.0).astype(BF) for h in range(hpg)], axis=0)
        kcat = jnp.concatenate([both(kp0_ref, kp1_ref), both(kc0_ref, kc1_ref)], axis=0).astype(BF)
        s = _dot_nt(lhs, kcat)

        ps, ms, sums = [], [], []
        for h in range(hpg):
            logit = jnp.where(keep, s[h * SPAN:(h + 1) * SPAN] + bias_ref[h], NEG_INF)
            m = jnp.max(logit, axis=-1, keepdims=True)
            e = jnp.exp(logit - m)
            ps.append(e.astype(BF))
            ms.append(m)
            sums.append(jnp.sum(e, axis=-1, keepdims=True))
        vcat = jnp.concatenate([both(vp0_ref, vp1_ref), both(vc0_ref, vc1_ref)], axis=0).astype(BF)
        pv = _dot(jnp.concatenate(ps, axis=0), vcat)

        o = jnp.zeros((SPAN, GROUP_WIDTH), F32)
        l = jnp.zeros((SPAN, GROUP_WIDTH), F32)
        for h in range(hpg):
            o = jnp.where(lanehead == h, pv[h * SPAN:(h + 1) * SPAN] * (1.0 / sums[h]), o)
            l = jnp.where(lanehead == h, ms[h] + jnp.log(sums[h]), l)
        o0_ref[0, rows, :] = o[:, :LANES]
        o1_ref[0, rows, :] = o[:, LANES:]
        l0_ref[0, rows, :] = l[:, :LANES]
        l1_ref[0, rows, :] = l[:, LANES:]
        return c

    lax.fori_loop(0, dil, body, 0)


def _pattn_call(q, kv, rel_bias, g, dil):
    bsz, s, _ = q.shape
    rows = dil * SPAN
    assert s % rows == 0 and GROUP_WIDTH == 2 * LANES
    blk = (1, rows, LANES)
    kcol = 2 * g
    vcol = ATT_WIDTH // LANES + 2 * g
    prev = lambda i: jnp.maximum(i - 1, 0)
    out = jax.ShapeDtypeStruct((bsz, s, LANES), F32)
    ospec = pl.BlockSpec(blk, lambda b, i: (b, i, 0))
    res = pl.pallas_call(
        functools.partial(_pattn_kernel, g=g, dil=dil),
        out_shape=(out,) * 4,
        grid=(bsz, s // rows),
        in_specs=[
            pl.BlockSpec(memory_space=pltpu.SMEM),
            _const_spec((SPAN, 2 * SPAN)),
            pl.BlockSpec(blk, lambda b, i: (b, i, kcol)),
            pl.BlockSpec(blk, lambda b, i: (b, i, kcol + 1)),
            pl.BlockSpec(blk, lambda b, i: (b, prev(i), kcol)),
            pl.BlockSpec(blk, lambda b, i: (b, prev(i), kcol + 1)),
            pl.BlockSpec(blk, lambda b, i: (b, i, kcol)),
            pl.BlockSpec(blk, lambda b, i: (b, i, kcol + 1)),
            pl.BlockSpec(blk, lambda b, i: (b, prev(i), vcol)),
            pl.BlockSpec(blk, lambda b, i: (b, prev(i), vcol + 1)),
            pl.BlockSpec(blk, lambda b, i: (b, i, vcol)),
            pl.BlockSpec(blk, lambda b, i: (b, i, vcol + 1)),
        ],
        out_specs=(ospec,) * 4,
        scratch_shapes=[pltpu.VMEM((HEADS_PER_GROUP, SPAN, 2 * SPAN), F32)],
        compiler_params=_params(("arbitrary", "arbitrary")),
        name=f"prompt_attn_g{g}",
    )(rel_bias, _prompt_bucket_table(dil), q, q, kv, kv, kv, kv, kv, kv, kv, kv)
    return list(res[:2]), list(res[2:])


def _sattn_kernel(relb_ref, bo0_ref, bo1_ref, bo2_ref, bn_ref, q_ref, kvn_ref, kvt_ref, c0_ref, c1_ref, c2_ref,
                  o0_ref, o1_ref, o2_ref, l0_ref, l1_ref, l2_ref, n0_ref, n1_ref, n2_ref,
                  bias0_ref, bias1_ref, bias2_ref, biasn_ref, tail_ref, *, t_new):
    hpg = HEADS_PER_GROUP
    gw = GROUP_WIDTH
    rows = SAMPLE_QROWS
    bo_refs = (bo0_ref, bo1_ref, bo2_ref)
    bias_refs = (bias0_ref, bias1_ref, bias2_ref)
    caches = (c0_ref, c1_ref, c2_ref)
    outs = ((o0_ref, l0_ref, n0_ref), (o1_ref, l1_ref, n1_ref), (o2_ref, l2_ref, n2_ref))

    @pl.when(pl.program_id(0) == 0)
    def _():
        tail_ref[...] = jnp.zeros(tail_ref.shape, F32)
        for g in range(N_GROUPS):
            for h in range(hpg):
                sl = slice(h * SUBLANES, (h + 1) * SUBLANES)
                bias_refs[g][sl, :] = _bias_from_buckets(bo_refs[g][sl, :], relb_ref, g * hpg + h)
                biasn_ref[g, sl, :] = _bias_from_buckets(bn_ref[g, sl, :], relb_ref, g * hpg + h)

    lanehead = lax.broadcasted_iota(jnp.int32, (rows, gw), 1) >> HEAD_SHIFT
    rowhead = lax.broadcasted_iota(jnp.int32, (rows, gw), 0) >> SUBLANE_SHIFT
    own = lanehead == rowhead
    tail_lane = lax.broadcasted_iota(jnp.int32, (gw, LANES), 1) >= LANES - t_new

    for g in range(N_GROUPS):
        cref = caches[g]
        o_ref, l_ref, n_ref = outs[g]
        width = cref.shape[-1]
        qm = jnp.where(own, q_ref[0, :, g * gw:(g + 1) * gw] * (HEAD_DIM ** -0.5), 0.0).astype(BF)
        knew = kvn_ref[0, :, g * gw:(g + 1) * gw].astype(BF)
        vnew = kvn_ref[0, :, ATT_WIDTH + g * gw:ATT_WIDTH + (g + 1) * gw].astype(BF)
        lo = _dot(qm, cref[0, 0].astype(BF)) + bias_refs[g][...]
        ln = _dot_nt(qm, knew) + biasn_ref[g]
        m = jnp.maximum(jnp.max(lo, axis=-1, keepdims=True), jnp.max(ln, axis=-1, keepdims=True))
        eo = jnp.exp(lo - m)
        en = jnp.exp(ln - m)
        ssum = jnp.sum(eo, axis=-1, keepdims=True) + jnp.sum(en, axis=-1, keepdims=True)
        pv = _dot_nt(eo.astype(BF), cref[0, 1].astype(BF)) + _dot(en.astype(BF), vnew)
        om = jnp.where(own, pv * (1.0 / ssum), 0.0)
        lm = jnp.where(own, m + jnp.log(ssum), 0.0)
        o_acc = om[0:SUBLANES]
        l_acc = lm[0:SUBLANES]
        for h in range(1, hpg):
            o_acc = o_acc + om[h * SUBLANES:(h + 1) * SUBLANES]
            l_acc = l_acc + lm[h * SUBLANES:(h + 1) * SUBLANES]
        o_ref[0] = o_acc
        l_ref[0] = l_acc

        for kv in range(2):
            rolled = pltpu.roll(cref[0, kv], width - t_new, 1)
            tail_ref[:, 0:SUBLANES] = kvt_ref[0, kv * ATT_WIDTH + g * gw:kv * ATT_WIDTH + (g + 1) * gw, :]
            tail = pltpu.roll(tail_ref[...], LANES - t_new, 1)
            if width > LANES:
                n_ref[0, kv, :, 0:width - LANES] = rolled[:, 0:width - LANES]
            n_ref[0, kv, :, width - LANES:width] = jnp.where(tail_lane, tail, rolled[:, width - LANES:width])


def _sattn_call(q, kvn, caches, rel_bias):
    n, t_new, _ = q.shape
    gw = GROUP_WIDTH
    hpg = HEADS_PER_GROUP
    assert t_new <= SUBLANES
    views, bkt_old, bkt_new = [], [], []
    for g, (window, dil) in enumerate(DIL_GROUPS):
        width = caches[g].shape[1]
        assert width == window and width % LANES == 0
        views.append(jnp.transpose(caches[g], (0, 2, 3, 4, 1)).reshape(n, 2, gw, width))
        old, new = _sample_bucket_tables(dil, width, t_new)
        bkt_old.append(old)
        bkt_new.append(new)
    pad = SUBLANES - t_new
    qp = jnp.tile(jnp.pad(q, ((0, 0), (0, pad), (0, 0))), (1, hpg, 1))
    kvp = jnp.pad(kvn, ((0, 0), (0, pad), (0, 0)))
    kvt = jnp.swapaxes(kvp, 1, 2)
    out = jax.ShapeDtypeStruct((n, SUBLANES, gw), F32)
    oblk = pl.BlockSpec((1, SUBLANES, gw), lambda i: (i, 0, 0))
    cspecs = [pl.BlockSpec((1, 2, gw, v.shape[-1]), lambda i: (i, 0, 0, 0)) for v in views]
    res = pl.pallas_call(
        functools.partial(_sattn_kernel, t_new=t_new),
        out_shape=(out,) * 6 + tuple(jax.ShapeDtypeStruct(v.shape, F32) for v in views),
        grid=(n,),
        in_specs=[pl.BlockSpec(memory_space=pltpu.SMEM)]
        + [_const_spec(b.shape) for b in bkt_old]
        + [
            _const_spec((N_GROUPS, SAMPLE_QROWS, SUBLANES)),
            pl.BlockSpec((1, SAMPLE_QROWS, ATT_WIDTH), lambda i: (i, 0, 0)),
            pl.BlockSpec((1, SUBLANES, 2 * ATT_WIDTH), lambda i: (i, 0, 0)),
            pl.BlockSpec((1, 2 * ATT_WIDTH, SUBLANES), lambda i: (i, 0, 0)),
        ]
        + cspecs,
        out_specs=(oblk,) * 6 + tuple(cspecs),
        scratch_shapes=[pltpu.VMEM(b.shape, F32) for b in bkt_old]
        + [pltpu.VMEM((N_GROUPS, SAMPLE_QROWS, SUBLANES), F32), pltpu.VMEM((gw, LANES), F32)],
        compiler_params=_params(("arbitrary",)),
        name="sample_attn",
    )(rel_bias, *bkt_old, jnp.stack(bkt_new), qp, kvp, kvt, *views)
    new_caches = [jnp.transpose(c.reshape(n, 2, hpg, HEAD_DIM, c.shape[-1]), (0, 4, 1, 2, 3)) for c in res[6:]]
    return [a[:, :t_new] for a in res[:3]], [a[:, :t_new] for a in res[3:6]], new_caches


def _oproj_kernel(x_ref, mod_ref, *refs, tm):
    nblk = ATT_WIDTH // LANES
    halves = nblk // N_GROUPS
    o_refs, l_refs = refs[:nblk], refs[nblk:2 * nblk]
    wo_ref, out_ref, att_ref = refs[2 * nblk:]
    rc = BF16_ROWS

    def body(i, c):
        r0 = pl.multiple_of(i * rc, rc)
        for hf in range(halves):
            blks = [g * halves + hf for g in range(N_GROUPS)]
            ls = [l_refs[k][0, pl.ds(r0, rc), :] for k in blks]
            mx = functools.reduce(jnp.maximum, ls)
            es = [jnp.exp(v - mx) for v in ls]
            inv = 1.0 / functools.reduce(lambda a, b: a + b, es)
            for k, e in zip(blks, es):
                att_ref[pl.ds(r0, rc), k * LANES:(k + 1) * LANES] = (o_refs[k][0, pl.ds(r0, rc), :] * (e * inv)).astype(BF)
        return c

    lax.fori_loop(0, tm // rc, body, 0, unroll=2)
    out_ref[0] = x_ref[0] + mod_ref[0, 2] * _dot(att_ref[...], wo_ref[...])


def _oproj_call(x, mods, os_, ls_, wo, *, tm):
    ns, s, d = x.shape
    r = mods.shape[2]
    kern = functools.partial(_oproj_kernel, tm=tm)
    gspec = pl.BlockSpec((1, tm, LANES), lambda n, t: (n, t, 0))
    return pl.pallas_call(
        kern,
        out_shape=jax.ShapeDtypeStruct((ns, s, d), F32),
        grid=(ns, s // tm),
        in_specs=[
            pl.BlockSpec((1, tm, d), lambda n, t: (n, t, 0)),
            pl.BlockSpec((1, N_MOD, r, d), lambda n, t: (n, 0, 0, 0)),
        ] + [gspec] * (2 * ATT_WIDTH // LANES) + [_const_spec((ATT_WIDTH, d))],
        out_specs=pl.BlockSpec((1, tm, d), lambda n, t: (n, t, 0)),
        scratch_shapes=[pltpu.VMEM((tm, ATT_WIDTH), BF)],
        compiler_params=_params(("arbitrary", "arbitrary")),
        name="attn_out_proj",
    )(x, mods, *os_, *ls_, wo)


def _trunk(x, mods, modkv, hist_a, hist_f, w, attn_fn, *, tm, rd):
    row = lambda v: v.reshape(1, -1)
    x, st_a = _conformer_call(x, mods[0], hist_a, row(w["norm_mix"][0]), w["a_w1"], row(w["a_b1"][0]),
                              w["a_dw"][0], row(w["a_dwb"][0]), row(w["a_ln_g"][0]), row(w["a_ln_b"][0]),
                              w["a_w2"], row(w["a_b2"][0]), tm=tm, rd=rd)
    x, st_f0 = _ffn_call(x, mods[0], hist_f[0], row(w["norm_ffn"][0]), w["f_wup"][0], w["f_cw"][0],
                         row(w["f_cb"][0]), w["f_wdown"][0], row(w["norm_f"]), tm=tm, rd=rd, final_norm=False)
    q, kv = _qkv_call(x, mods[1], modkv, row(w["norm_mix"][1]), row(w["norm_kv"]), w["w_q"], w["w_kv"], tm=tm)
    os_, ls_, attn_extra = attn_fn(q, kv)
    x = _oproj_call(x, mods[1], os_, ls_, w["w_o"], tm=tm)
    y, st_f1 = _ffn_call(x, mods[1], hist_f[1], row(w["norm_ffn"][1]), w["f_wup"][1], w["f_cw"][1],
                         row(w["f_cb"][1]), w["f_wdown"][1], row(w["norm_f"]), tm=tm, rd=rd, final_norm=True)
    return y, kv, st_a, [st_f0, st_f1], attn_extra


def kernel(x_prompt, x_sample, cache_kv_w128, cache_kv_w512, cache_kv_w2048, state_conv_a, state_conv_ffn, c_prompt, c_sample, w_mod, b_mod, norm_mix, norm_ffn, a_w1, a_b1, a_dw, a_dwb, a_ln_g, a_ln_b, a_w2, a_b2, w_mod_kv, b_mod_kv, norm_kv, w_kv, w_q, w_o, rel_bias, f_wup, f_cw, f_cb, f_wdown, norm_f):
    d = D_MODEL
    bsz, seq, _ = x_prompt.shape
    nseq, t_new, _ = x_sample.shape
    caches = (cache_kv_w128, cache_kv_w512, cache_kv_w2048)
    hpg = HEADS_PER_GROUP

    w = dict(norm_mix=norm_mix, norm_ffn=norm_ffn, a_w1=a_w1[0].astype(BF), a_b1=a_b1, a_dw=a_dw, a_dwb=a_dwb,
             a_ln_g=a_ln_g, a_ln_b=a_ln_b, a_w2=a_w2[0].astype(BF), a_b2=a_b2, norm_kv=norm_kv,
             w_kv=w_kv.astype(BF), w_q=w_q[0].astype(BF), w_o=w_o[0].astype(BF),
             f_wup=[f_wup[l].astype(BF) for l in range(DEPTH)], f_cw=f_cw, f_cb=f_cb,
             f_wdown=[f_wdown[l].astype(BF) for l in range(DEPTH)], norm_f=norm_f)

    n_c = bsz + nseq
    mp = _round_up(n_c, SUBLANES)
    c_all = jnp.pad(jnp.concatenate([c_prompt, c_sample], axis=0), ((0, mp - n_c), (0, 0)))
    mod = _mod_call(c_all, w_mod, b_mod.reshape(DEPTH, 1, N_MOD * d), tn=1536)
    modkv = _mod_call(c_all, w_mod_kv[None], b_mod_kv.reshape(1, 1, 2 * d), tn=1024)[0]

    mods_p = [mod[l, :bsz].reshape(bsz, N_MOD, 1, d) for l in range(DEPTH)]
    modkv_p = modkv[:bsz].reshape(bsz, 2, 1, d)
    hist_a_p = jnp.zeros((bsz, CONV_A_WIDTH - 1, d), F32)
    hist_f_p = [jnp.zeros((bsz, CONV_F_WIDTH - 1, 2 * D_FF), F32)] * DEPTH

    def prompt_attn(q, kv):
        res = [_pattn_call(q, kv, rel_bias, g, dil) for g, (_, dil) in enumerate(DIL_GROUPS)]
        return [a for r in res for a in r[0]], [a for r in res for a in r[1]], None

    y_p, kv_p, st_a_p, st_f_p, _ = _trunk(x_prompt, mods_p, modkv_p, hist_a_p, hist_f_p, w, prompt_attn,
                                          tm=TM_PROMPT, rd=1)
    wmax = max(c.shape[1] for c in caches)
    assert seq >= wmax
    kv_tail = kv_p[:, seq - wmax:].reshape(bsz, wmax, 2, N_HEADS, HEAD_DIM)
    kv_bufs_p = [kv_tail[:, wmax - c.shape[1]:, :, g * hpg:(g + 1) * hpg] for g, c in enumerate(caches)]
    conv_a_p = st_a_p[None]
    conv_f_p = jnp.stack(st_f_p)

    rows = t_new * nseq
    tmaj = lambda v: jnp.swapaxes(v, 0, 1).reshape(1, -1, v.shape[-1])
    smaj = lambda v, c: jnp.swapaxes(v.reshape(-1, nseq, c), 0, 1)

    def per_row(m, k):
        return jnp.tile(jnp.swapaxes(m.reshape(nseq, k, d), 0, 1), (1, t_new, 1))[None]

    mods_s = [per_row(mod[l, bsz:n_c], N_MOD) for l in range(DEPTH)]
    modkv_s = per_row(modkv[bsz:n_c], 2)
    hist_a_s = tmaj(state_conv_a[0])
    hist_f_s = [tmaj(state_conv_ffn[l]) for l in range(DEPTH)]

    def sample_attn(q, kv):
        os_, ls_, new_caches = _sattn_call(smaj(q, ATT_WIDTH), smaj(kv, 2 * ATT_WIDTH), caches, rel_bias)
        halves = lambda vs: [tmaj(a[..., c0:c0 + LANES]) for a in vs for c0 in range(0, GROUP_WIDTH, LANES)]
        return halves(os_), halves(ls_), new_caches

    y_s, _, st_a_s, st_f_s, kv_bufs_s = _trunk(tmaj(x_sample), mods_s, modkv_s, hist_a_s, hist_f_s, w,
                                               sample_attn, tm=rows, rd=nseq)
    y_s = smaj(y_s, d)
    conv_a_s = smaj(st_a_s, d)[None]
    conv_f_s = jnp.stack([smaj(s, 2 * D_FF) for s in st_f_s])

    return (y_p, y_s, kv_bufs_p[0], kv_bufs_p[1], kv_bufs_p[2], conv_a_p, conv_f_p,
            kv_bufs_s[0], kv_bufs_s[1], kv_bufs_s[2], conv_a_s, conv_f_s)
```

```python
import functools
import math

import jax
import jax.numpy as jnp
from jax import lax
from jax.experimental import pallas as pl
from jax.experimental.pallas import tpu as pltpu

D_MODEL = 1024
DEPTH = 2
HEAD_DIM = 64
HEADS_PER_GROUP = 4
DIL_GROUPS = ((128, 1), (512, 4), (2048, 16))
N_GROUPS = len(DIL_GROUPS)
N_HEADS = HEADS_PER_GROUP * N_GROUPS
ATT_WIDTH = N_HEADS * HEAD_DIM
GROUP_WIDTH = HEADS_PER_GROUP * HEAD_DIM
SPAN = 128
N_BUCKETS = 32
MAX_DISTANCE = 2048
CONV_A_WIDTH = 31
D_FF = 2816
CONV_F_WIDTH = 3
N_MOD = 6
EPS = 1e-6
LN_EPS = 1e-5
NEG_INF = -1e30

BF = jnp.bfloat16
F32 = jnp.float32

HEAD_SHIFT = HEAD_DIM.bit_length() - 1
SUBLANES = 8
SUBLANE_SHIFT = SUBLANES.bit_length() - 1
LANES = 128
BF16_ROWS = 16
ROW_UNROLL = 4
VMEM_LIMIT = 56 * 1024 * 1024

TM_PROMPT = 512
PATTN_ROWS = 512
CONF_ROWS = 128
FF_SUB = 256
SAMPLE_QROWS = HEADS_PER_GROUP * SUBLANES


def _round_up(a, b):
    return -(-a // b) * b


def _params(sem):
    return pltpu.CompilerParams(dimension_semantics=sem, vmem_limit_bytes=VMEM_LIMIT)


def _const_spec(shape):
    nd = len(shape)
    return pl.BlockSpec(shape, lambda *_: (0,) * nd, pipeline_mode=pl.Buffered(1))


def _dot(a, b):
    return jnp.dot(a, b, preferred_element_type=F32)


def _dot_nt(a, b):
    return lax.dot_general(a, b, (((1,), (1,)), ((), ())), preferred_element_type=F32)


def _sigmoid(v):
    return 1.0 / (1.0 + jnp.exp(-v))


def _mod_rows(mod_ref, idx, r0, rows, per_row):
    if per_row:
        return mod_ref[0, idx, pl.ds(r0, rows), :]
    return mod_ref[0, idx]


def _rms_mod_rows(x_ref, rows, targets, per_row, *, slot=None, straight=False):
    rc = BF16_ROWS
    lo, hi = rows

    def chunk(r0):
        x = x_ref[0, pl.ds(r0, rc), :]
        xn = x * lax.rsqrt(jnp.mean(x * x, axis=-1, keepdims=True) + EPS)
        for gain_ref, mod_ref, i_sh, i_sc, dst_ref in targets:
            sh = _mod_rows(mod_ref, i_sh, r0, rc, per_row)
            sc = _mod_rows(mod_ref, i_sc, r0, rc, per_row)
            idx = (pl.ds(r0, rc), slice(None))
            dst_ref[idx if slot is None else (slot,) + idx] = ((xn * gain_ref[...]) * (1.0 + sc) + sh).astype(BF)

    if straight:
        for r0 in range(lo, hi, rc):
            chunk(r0)
        return

    def body(i, c):
        chunk(pl.multiple_of(lo + i * rc, rc))
        return c

    lax.fori_loop(0, (hi - lo) // rc, body, 0, unroll=ROW_UNROLL)


def _tap_conv(src_ref, blk, w_ref, offsets, r0, rows):
    cols = slice(blk * LANES, (blk + 1) * LANES)
    acc = None
    for k, off in enumerate(offsets):
        term = w_ref[k:k + 1, cols] * src_ref[blk, r0 + off:r0 + off + rows, :]
        acc = term if acc is None else acc + term
    return acc


def _mod_kernel(c_ref, w_ref, b_ref, o_ref):
    c = c_ref[...]
    a = (c * _sigmoid(c)).astype(BF)
    o_ref[0] = _dot(a, w_ref[0].astype(BF)) + b_ref[0]


def _mod_call(c_all, w, b, tn):
    nl, d, n = w.shape
    mp = c_all.shape[0]
    return pl.pallas_call(
        _mod_kernel,
        out_shape=jax.ShapeDtypeStruct((nl, mp, n), F32),
        grid=(nl, n // tn),
        in_specs=[
            pl.BlockSpec((mp, d), lambda l, j: (0, 0)),
            pl.BlockSpec((1, d, tn), lambda l, j: (l, 0, j)),
            pl.BlockSpec((1, 1, tn), lambda l, j: (l, 0, j)),
        ],
        out_specs=pl.BlockSpec((1, mp, tn), lambda l, j: (l, 0, j)),
        compiler_params=_params(("arbitrary", "arbitrary")),
        name="adaln_mod",
    )(c_all, w, b)


def _conformer_kernel(x_ref, mod_ref, hist_ref, nrm_ref, w1_ref, b1_ref, dw_ref, dwb_ref, lng_ref, lnb_ref,
                      w2_ref, b2_ref, o_ref, st_ref, h_ref, h2_ref, u_ref, full_ref, y_ref, *, tm, rd, per_row):
    d = D_MODEL
    hh = (CONV_A_WIDTH - 1) * rd
    hp = _round_up(hh, SUBLANES)
    t = pl.program_id(1)

    nblk = d // LANES
    lanes = lambda blk: slice(blk * LANES, (blk + 1) * LANES)

    @pl.when(t == 0)
    def _():
        full_ref[:, 0:hp, :] = jnp.zeros((nblk, hp, LANES), F32)
        for blk in range(nblk):
            full_ref[blk, hp - hh:hp, :] = hist_ref[0, :, lanes(blk)]

    rb = min(tm, CONF_ROWS)
    rg, cg = 16, 512
    rcv = 64
    offsets = [hp - hh + k * rd for k in range(CONV_A_WIDTH)]
    for rs in range(0, tm, rb):
        _rms_mod_rows(x_ref, (rs, rs + rb), [(nrm_ref, mod_ref, 0, 1, h_ref)], per_row, straight=True)
        u_ref[rs:rs + rb, :] = _dot(h_ref[rs:rs + rb, :], w1_ref[...])

        for r0 in range(rs, rs + rb, rg):
            for c0 in range(0, d, cg):
                a = u_ref[r0:r0 + rg, c0:c0 + cg] + b1_ref[:, c0:c0 + cg]
                g = u_ref[r0:r0 + rg, d + c0:d + c0 + cg] + b1_ref[:, d + c0:d + c0 + cg]
                glu = a * _sigmoid(g)
                for j in range(cg // LANES):
                    full_ref[c0 // LANES + j, hp + r0:hp + r0 + rg, :] = glu[:, lanes(j)]

        for r0 in range(rs, rs + rb, rcv):
            for blk in range(nblk):
                y_ref[r0:r0 + rcv, lanes(blk)] = (_tap_conv(full_ref, blk, dw_ref, offsets, r0, rcv)
                                                  + dwb_ref[:, lanes(blk)])

        for r0 in range(rs, rs + rb, BF16_ROWS):
            y = y_ref[r0:r0 + BF16_ROWS, :]
            mu = jnp.mean(y, axis=-1, keepdims=True)
            dv = y - mu
            var = jnp.mean(dv * dv, axis=-1, keepdims=True)
            yn = dv * lax.rsqrt(var + LN_EPS) * lng_ref[...] + lnb_ref[...]
            h2_ref[r0:r0 + BF16_ROWS, :] = (yn * _sigmoid(yn)).astype(BF)

        out = _dot(h2_ref[rs:rs + rb, :], w2_ref[...]) + b2_ref[...]
        gate = mod_ref[0, 2, rs:rs + rb, :] if per_row else mod_ref[0, 2]
        o_ref[0, rs:rs + rb, :] = x_ref[0, rs:rs + rb, :] + gate * out

    for blk in range(nblk):
        new_hist = full_ref[blk, hp + tm - hh:hp + tm, :]
        st_ref[0, :, lanes(blk)] = new_hist
        full_ref[blk, hp - hh:hp, :] = new_hist


def _conformer_call(x, mods, hist, nrm, w1, b1, dw, dwb, lng, lnb, w2, b2, *, tm, rd):
    ns, s, d = x.shape
    r = mods.shape[2]
    hh = hist.shape[1]
    hp = _round_up(hh, SUBLANES)
    per_row = r > 1
    assert s % tm == 0 and (not per_row or (r == tm and s == tm))
    kern = functools.partial(_conformer_kernel, tm=tm, rd=rd, per_row=per_row)
    return pl.pallas_call(
        kern,
        out_shape=(jax.ShapeDtypeStruct((ns, s, d), F32), jax.ShapeDtypeStruct((ns, hh, d), F32)),
        grid=(ns, s // tm),
        in_specs=[
            pl.BlockSpec((1, tm, d), lambda n, t: (n, t, 0)),
            pl.BlockSpec((1, N_MOD, r, d), lambda n, t: (n, 0, 0, 0)),
            pl.BlockSpec((1, hh, d), lambda n, t: (n, 0, 0)),
            _const_spec((1, d)),
            _const_spec((d, 2 * d)),
            _const_spec((1, 2 * d)),
            _const_spec((CONV_A_WIDTH, d)),
            _const_spec((1, d)),
            _const_spec((1, d)),
            _const_spec((1, d)),
            _const_spec((d, d)),
            _const_spec((1, d)),
        ],
        out_specs=(
            pl.BlockSpec((1, tm, d), lambda n, t: (n, t, 0)),
            pl.BlockSpec((1, hh, d), lambda n, t: (n, 0, 0)),
        ),
        scratch_shapes=[
            pltpu.VMEM((tm, d), BF),
            pltpu.VMEM((tm, d), BF),
            pltpu.VMEM((tm, 2 * d), F32),
            pltpu.VMEM((d // LANES, hp + tm, LANES), F32),
            pltpu.VMEM((tm, d), F32),
        ],
        compiler_params=_params(("arbitrary", "arbitrary")),
        name="conformer_mixer",
    )(x, mods, hist, nrm, w1, b1, dw, dwb, lng, lnb, w2, b2)


def _ffn_kernel(x_ref, xnext_ref, mod_ref, hist_ref, nrm_ref, wu_ref, cw_ref, cb_ref, wd_ref, nf_ref, o_ref, st_ref,
                h_ref, ubuf_ref, carry_ref, act_ref, part_ref, *, tm, rd, per_row, final_norm):
    f = D_FF
    hh = (CONV_F_WIDTH - 1) * rd
    hp = _round_up(hh, SUBLANES)
    t = pl.program_id(1)

    cur = lax.rem(t, 2)
    norm_targets = [(nrm_ref, mod_ref, 3, 4, h_ref)]

    @pl.when(t == 0)
    def _():
        _rms_mod_rows(x_ref, (0, tm), norm_targets, per_row, slot=0)

    nblk = 2 * f // LANES
    lanes = lambda blk: slice(blk * LANES, (blk + 1) * LANES)

    @pl.when(t == 0)
    def _():
        carry_ref[...] = jnp.zeros((nblk, hp, LANES), F32)
        for blk in range(nblk):
            carry_ref[blk, hp - hh:hp, :] = hist_ref[0, :, lanes(blk)]

    ubuf_ref[:, 0:hp, :] = carry_ref[...]

    rc = 32
    offsets = [hp - hh + k * rd for k in range(CONV_F_WIDTH)]
    for c0 in range(0, f, FF_SUB):
        for half in range(2):
            col = half * f + c0
            u = _dot(h_ref[cur], wu_ref[:, col:col + FF_SUB])
            for j in range(FF_SUB // LANES):
                ubuf_ref[col // LANES + j, hp:hp + tm, :] = u[:, lanes(j)]
        for r0 in range(0, tm, rc):
            for cc0 in range(c0, c0 + FF_SUB, LANES):
                ys = []
                for half in range(2):
                    blk = (half * f + cc0) // LANES
                    ys.append(_tap_conv(ubuf_ref, blk, cw_ref, offsets, r0, rc) + cb_ref[:, lanes(blk)])
                yg, yv = ys
                act_ref[r0:r0 + rc, cc0:cc0 + LANES] = (yg * _sigmoid(yg) * yv).astype(BF)

    for blk in range(nblk):
        st_ref[0, :, lanes(blk)] = ubuf_ref[blk, hp + tm - hh:hp + tm, :]
    carry_ref[...] = ubuf_ref[:, tm:tm + hp, :]

    _rms_mod_rows(xnext_ref, (0, tm), norm_targets, per_row, slot=1 - cur, straight=True)
    xo = x_ref[0] + mod_ref[0, 5] * _dot(act_ref[...], wd_ref[...])
    if not final_norm:
        o_ref[0] = xo
        return
    part_ref[...] = xo
    rows = BF16_ROWS

    def norm_body(i, c):
        r0 = pl.multiple_of(i * rows, rows)
        v = part_ref[pl.ds(r0, rows), :]
        o_ref[0, pl.ds(r0, rows), :] = v * lax.rsqrt(jnp.mean(v * v, axis=-1, keepdims=True) + EPS) * nf_ref[...]
        return c

    lax.fori_loop(0, tm // rows, norm_body, 0, unroll=ROW_UNROLL)


def _ffn_call(x, mods, hist, nrm, wup, cw, cb, wdown, nf, *, tm, rd, final_norm):
    ns, s, d = x.shape
    r = mods.shape[2]
    hh = hist.shape[1]
    hp = _round_up(hh, SUBLANES)
    f = wdown.shape[0]
    per_row = r > 1
    nt = s // tm
    assert f == D_FF and f % FF_SUB == 0 and s % tm == 0 and (not per_row or (r == tm and s == tm))
    kern = functools.partial(_ffn_kernel, tm=tm, rd=rd, per_row=per_row, final_norm=final_norm)
    return pl.pallas_call(
        kern,
        out_shape=(jax.ShapeDtypeStruct((ns, s, d), F32), jax.ShapeDtypeStruct((ns, hh, 2 * f), F32)),
        grid=(ns, nt),
        in_specs=[
            pl.BlockSpec((1, tm, d), lambda n, t: (n, t, 0)),
            pl.BlockSpec((1, tm, d), lambda n, t: (n, jnp.minimum(t + 1, nt - 1), 0)),
            pl.BlockSpec((1, N_MOD, r, d), lambda n, t: (n, 0, 0, 0)),
            pl.BlockSpec((1, hh, 2 * f), lambda n, t: (n, 0, 0)),
            _const_spec((1, d)),
            _const_spec((d, 2 * f)),
            _const_spec((CONV_F_WIDTH, 2 * f)),
            _const_spec((1, 2 * f)),
            _const_spec((f, d)),
            _const_spec((1, d)),
        ],
        out_specs=(
            pl.BlockSpec((1, tm, d), lambda n, t: (n, t, 0)),
            pl.BlockSpec((1, hh, 2 * f), lambda n, t: (n, 0, 0)),
        ),
        scratch_shapes=[
            pltpu.VMEM((2, tm, d), BF),
            pltpu.VMEM((2 * f // LANES, hp + tm, LANES), F32),
            pltpu.VMEM((2 * f // LANES, hp, LANES), F32),
            pltpu.VMEM((tm, f), BF),
            pltpu.VMEM((tm, d), F32),
        ],
        compiler_params=_params(("arbitrary", "arbitrary")),
        name="conv_ffn",
    )(x, x, mods, hist, nrm, wup, cw, cb, wdown, nf)


def _qkv_kernel(x_ref, xnext_ref, mod_ref, modkv_ref, nq_ref, nkv_ref, wq_ref, wkv_ref, q_ref, kv_ref,
                hq_ref, hkv_ref, *, tm, per_row):
    t = pl.program_id(1)
    cur = lax.rem(t, 2)
    norm_targets = [(nq_ref, mod_ref, 0, 1, hq_ref), (nkv_ref, modkv_ref, 0, 1, hkv_ref)]

    @pl.when(t == 0)
    def _():
        _rms_mod_rows(x_ref, (0, tm), norm_targets, per_row, slot=0)

    q_ref[0] = _dot(hq_ref[cur], wq_ref[...])
    _rms_mod_rows(xnext_ref, (0, tm), norm_targets, per_row, slot=1 - cur, straight=True)
    kv_ref[0] = _dot(hkv_ref[cur], wkv_ref[...])


def _qkv_call(x, mods, modkv, nq, nkv, wq, wkv, *, tm):
    ns, s, d = x.shape
    r = mods.shape[2]
    per_row = r > 1
    nt = s // tm
    kern = functools.partial(_qkv_kernel, tm=tm, per_row=per_row)
    return pl.pallas_call(
        kern,
        out_shape=(jax.ShapeDtypeStruct((ns, s, ATT_WIDTH), F32), jax.ShapeDtypeStruct((ns, s, 2 * ATT_WIDTH), F32)),
        grid=(ns, nt),
        in_specs=[
            pl.BlockSpec((1, tm, d), lambda n, t: (n, t, 0)),
            pl.BlockSpec((1, tm, d), lambda n, t: (n, jnp.minimum(t + 1, nt - 1), 0)),
            pl.BlockSpec((1, N_MOD, r, d), lambda n, t: (n, 0, 0, 0)),
            pl.BlockSpec((1, 2, r, d), lambda n, t: (n, 0, 0, 0)),
            _const_spec((1, d)),
            _const_spec((1, d)),
            _const_spec((d, ATT_WIDTH)),
            _const_spec((d, 2 * ATT_WIDTH)),
        ],
        out_specs=(
            pl.BlockSpec((1, tm, ATT_WIDTH), lambda n, t: (n, t, 0)),
            pl.BlockSpec((1, tm, 2 * ATT_WIDTH), lambda n, t: (n, t, 0)),
        ),
        scratch_shapes=[pltpu.VMEM((2, tm, d), BF), pltpu.VMEM((2, tm, d), BF)],
        compiler_params=_params(("arbitrary", "arbitrary")),
        name="qkv_proj",
    )(x, x, mods, modkv, nq, nkv, wq, wkv)


def _rel_bucket(dist):
    max_exact = N_BUCKETS // 2
    dd = jnp.maximum(dist, 1).astype(F32)
    large = max_exact + (jnp.log(dd / max_exact) / math.log(MAX_DISTANCE / max_exact)
                         * (N_BUCKETS - max_exact)).astype(jnp.int32)
    large = jnp.minimum(large, N_BUCKETS - 1)
    return jnp.where(dist < max_exact, dist, large)


def _prompt_bucket_table(dil):
    qi = jnp.arange(SPAN, dtype=jnp.int32)[:, None]
    ki = jnp.arange(2 * SPAN, dtype=jnp.int32)[None, :]
    m = qi + SPAN - ki
    valid = (m >= 0) & (m <= SPAN)
    return jnp.where(valid, _rel_bucket(jnp.clip(m, 0, SPAN) * dil), -1).astype(jnp.int32)


def _sample_bucket_tables(dil, width, t_new):
    row = jnp.arange(SAMPLE_QROWS, dtype=jnp.int32)[:, None]
    t = (row % SUBLANES) % t_new
    tn = jnp.arange(SUBLANES, dtype=jnp.int32)[None, :]

    def table(dist, ok):
        ok = ok & (dist >= 0) & (dist % dil == 0) & (dist // dil <= SPAN)
        return jnp.where(ok, _rel_bucket(jnp.clip(dist, 0, SPAN * dil)), -1).astype(jnp.int32)

    old = table(width + t - jnp.arange(width, dtype=jnp.int32)[None, :], True)
    new = table(t - tn, tn < t_new)
    return old, new


def _bias_from_buckets(bkt, relb_ref, head):
    acc = jnp.full(bkt.shape, NEG_INF, F32)
    for b in range(N_BUCKETS):
        acc = jnp.where(bkt == b, relb_ref[b, head], acc)
    return acc


def _pattn_kernel(relb_ref, bkt_ref, q0_ref, q1_ref, k0_ref, k1_ref, v0_ref, v1_ref,
                  o0_ref, o1_ref, l0_ref, l1_ref, bias_ref, carry_ref, *, g, dil, nbk):
    b = pl.program_id(0)
    i = pl.program_id(1)
    hpg = HEADS_PER_GROUP
    span_rows = SPAN * dil
    kv_refs = (k0_ref, k1_ref, v0_ref, v1_ref)

    @pl.when((b == 0) & (i == 0))
    def _():
        bkt = bkt_ref[...]
        for h in range(hpg):
            bias_ref[h] = _bias_from_buckets(bkt, relb_ref, g * hpg + h)

    @pl.when(i == 0)
    def _():
        carry_ref[...] = jnp.zeros(carry_ref.shape, F32)

    lanehead = lax.broadcasted_iota(jnp.int32, (SPAN, GROUP_WIDTH), 1) >> HEAD_SHIFT
    col = lax.broadcasted_iota(jnp.int32, (SPAN, 2 * SPAN), 1)
    keep = col >= jnp.where(i > 0, 0, SPAN)

    def block(j, r):
        rows = pl.ds(j * span_rows + r, SPAN, stride=dil)
        cur = lambda r0, r1: jnp.concatenate([r0[0, rows, :], r1[0, rows, :]], axis=1)
        if j == 0:
            crow = pl.ds(r, SPAN, stride=dil)
            prev = lambda a: jnp.concatenate([carry_ref[a, crow, :], carry_ref[a + 1, crow, :]], axis=1)
            kprev, vprev = prev(0), prev(2)
        else:
            prow = pl.ds((j - 1) * span_rows + r, SPAN, stride=dil)
            prev = lambda r0, r1: jnp.concatenate([r0[0, prow, :], r1[0, prow, :]], axis=1)
            kprev, vprev = prev(k0_ref, k1_ref), prev(v0_ref, v1_ref)
        q = cur(q0_ref, q1_ref) * (HEAD_DIM ** -0.5)
        lhs = jnp.concatenate([jnp.where(lanehead == h, q, 0.0).astype(BF) for h in range(hpg)], axis=0)
        kcat = jnp.concatenate([kprev, cur(k0_ref, k1_ref)], axis=0).astype(BF)
        s = _dot_nt(lhs, kcat)

        ps, ms, sums = [], [], []
        for h in range(hpg):
            logit = s[h * SPAN:(h + 1) * SPAN] + bias_ref[h]
            if j == 0:
                logit = jnp.where(keep, logit, NEG_INF)
            m = jnp.max(logit, axis=-1, keepdims=True)
            e = jnp.exp(logit - m)
            ps.append(e.astype(BF))
            ms.append(m)
            sums.append(jnp.sum(e, axis=-1, keepdims=True))
        vcat = jnp.concatenate([vprev, cur(v0_ref, v1_ref)], axis=0).astype(BF)
        pv = _dot(jnp.concatenate(ps, axis=0), vcat)

        o = jnp.zeros((SPAN, GROUP_WIDTH), F32)
        l = jnp.zeros((SPAN, GROUP_WIDTH), F32)
        for h in range(hpg):
            o = jnp.where(lanehead == h, pv[h * SPAN:(h + 1) * SPAN] * (1.0 / sums[h]), o)
            l = jnp.where(lanehead == h, ms[h] + jnp.log(sums[h]), l)
        o0_ref[0, rows, :] = o[:, :LANES]
        o1_ref[0, rows, :] = o[:, LANES:]
        l0_ref[0, rows, :] = l[:, :LANES]
        l1_ref[0, rows, :] = l[:, LANES:]

    for j in range(nbk):
        if dil == 1:
            block(j, 0)
        else:
            def body(r, c, j=j):
                block(j, r)
                return c

            lax.fori_loop(0, dil, body, 0, unroll=2)

    for a, ref in enumerate(kv_refs):
        carry_ref[a] = ref[0, (nbk - 1) * span_rows:nbk * span_rows, :]


def _pattn_call(q, kv, rel_bias, g, dil):
    bsz, s, _ = q.shape
    nbk = max(1, PATTN_ROWS // (dil * SPAN))
    rows = nbk * dil * SPAN
    assert s % rows == 0 and GROUP_WIDTH == 2 * LANES
    blk = (1, rows, LANES)
    kcol = 2 * g
    vcol = ATT_WIDTH // LANES + 2 * g
    out = jax.ShapeDtypeStruct((bsz, s, LANES), F32)
    ospec = pl.BlockSpec(blk, lambda b, i: (b, i, 0))
    res = pl.pallas_call(
        functools.partial(_pattn_kernel, g=g, dil=dil, nbk=nbk),
        out_shape=(out,) * 4,
        grid=(bsz, s // rows),
        in_specs=[
            pl.BlockSpec(memory_space=pltpu.SMEM),
            _const_spec((SPAN, 2 * SPAN)),
            pl.BlockSpec(blk, lambda b, i: (b, i, kcol)),
            pl.BlockSpec(blk, lambda b, i: (b, i, kcol + 1)),
            pl.BlockSpec(blk, lambda b, i: (b, i, kcol)),
            pl.BlockSpec(blk, lambda b, i: (b, i, kcol + 1)),
            pl.BlockSpec(blk, lambda b, i: (b, i, vcol)),
            pl.BlockSpec(blk, lambda b, i: (b, i, vcol + 1)),
        ],
        out_specs=(ospec,) * 4,
        scratch_shapes=[pltpu.VMEM((HEADS_PER_GROUP, SPAN, 2 * SPAN), F32),
                        pltpu.VMEM((4, dil * SPAN, LANES), F32)],
        compiler_params=_params(("arbitrary", "arbitrary")),
        name=f"prompt_attn_g{g}",
    )(rel_bias, _prompt_bucket_table(dil), q, q, kv, kv, kv, kv)
    return list(res[:2]), list(res[2:])


def _sattn_kernel(relb_ref, bo0_ref, bo1_ref, bo2_ref, bn_ref, q_ref, kvn_ref, kvt_ref, c0_ref, c1_ref, c2_ref,
                  o0_ref, o1_ref, o2_ref, l0_ref, l1_ref, l2_ref, n0_ref, n1_ref, n2_ref,
                  bias0_ref, bias1_ref, bias2_ref, biasn_ref, tail_ref, *, t_new):
    hpg = HEADS_PER_GROUP
    gw = GROUP_WIDTH
    rows = SAMPLE_QROWS
    bo_refs = (bo0_ref, bo1_ref, bo2_ref)
    bias_refs = (bias0_ref, bias1_ref, bias2_ref)
    caches = (c0_ref, c1_ref, c2_ref)
    outs = ((o0_ref, l0_ref, n0_ref), (o1_ref, l1_ref, n1_ref), (o2_ref, l2_ref, n2_ref))

    @pl.when(pl.program_id(0) == 0)
    def _():
        tail_ref[...] = jnp.zeros(tail_ref.shape, F32)
        for g in range(N_GROUPS):
            for h in range(hpg):
                sl = slice(h * SUBLANES, (h + 1) * SUBLANES)
                bias_refs[g][sl, :] = _bias_from_buckets(bo_refs[g][sl, :], relb_ref, g * hpg + h)
                biasn_ref[g, sl, :] = _bias_from_buckets(bn_ref[g, sl, :], relb_ref, g * hpg + h)

    lanehead = lax.broadcasted_iota(jnp.int32, (rows, gw), 1) >> HEAD_SHIFT
    rowhead = lax.broadcasted_iota(jnp.int32, (rows, gw), 0) >> SUBLANE_SHIFT
    own = lanehead == rowhead
    tail_lane = lax.broadcasted_iota(jnp.int32, (gw, LANES), 1) >= LANES - t_new

    for g in range(N_GROUPS):
        cref = caches[g]
        o_ref, l_ref, n_ref = outs[g]
        width = cref.shape[-1]
        qm = jnp.where(own, q_ref[0, :, g * gw:(g + 1) * gw] * (HEAD_DIM ** -0.5), 0.0).astype(BF)
        knew = kvn_ref[0, :, g * gw:(g + 1) * gw].astype(BF)
        vnew = kvn_ref[0, :, ATT_WIDTH + g * gw:ATT_WIDTH + (g + 1) * gw].astype(BF)
        lo = _dot(qm, cref[0, 0].astype(BF)) + bias_refs[g][...]
        ln = _dot_nt(qm, knew) + biasn_ref[g]
        m = jnp.maximum(jnp.max(lo, axis=-1, keepdims=True), jnp.max(ln, axis=-1, keepdims=True))
        eo = jnp.exp(lo - m)
        en = jnp.exp(ln - m)
        ssum = jnp.sum(eo, axis=-1, keepdims=True) + jnp.sum(en, axis=-1, keepdims=True)
        pv = _dot_nt(eo.astype(BF), cref[0, 1].astype(BF)) + _dot(en.astype(BF), vnew)
        om = jnp.where(own, pv * (1.0 / ssum), 0.0)
        lm = jnp.where(own, m + jnp.log(ssum), 0.0)
        o_acc = om[0:SUBLANES]
        l_acc = lm[0:SUBLANES]
        for h in range(1, hpg):
            o_acc = o_acc + om[h * SUBLANES:(h + 1) * SUBLANES]
            l_acc = l_acc + lm[h * SUBLANES:(h + 1) * SUBLANES]
        o_ref[0] = o_acc
        l_ref[0] = l_acc

        for kv in range(2):
            rolled = pltpu.roll(cref[0, kv], width - t_new, 1)
            tail_ref[:, 0:SUBLANES] = kvt_ref[0, kv * ATT_WIDTH + g * gw:kv * ATT_WIDTH + (g + 1) * gw, :]
            tail = pltpu.roll(tail_ref[...], LANES - t_new, 1)
            if width > LANES:
                n_ref[0, kv, :, 0:width - LANES] = rolled[:, 0:width - LANES]
            n_ref[0, kv, :, width - LANES:width] = jnp.where(tail_lane, tail, rolled[:, width - LANES:width])


def _sattn_call(q, kvn, caches, rel_bias):
    n, t_new, _ = q.shape
    gw = GROUP_WIDTH
    hpg = HEADS_PER_GROUP
    assert t_new <= SUBLANES
    views, bkt_old, bkt_new = [], [], []
    for g, (window, dil) in enumerate(DIL_GROUPS):
        width = caches[g].shape[1]
        assert width == window and width % LANES == 0
        views.append(jnp.transpose(caches[g], (0, 2, 3, 4, 1)).reshape(n, 2, gw, width))
        old, new = _sample_bucket_tables(dil, width, t_new)
        bkt_old.append(old)
        bkt_new.append(new)
    pad = SUBLANES - t_new
    qp = jnp.tile(jnp.pad(q, ((0, 0), (0, pad), (0, 0))), (1, hpg, 1))
    kvp = jnp.pad(kvn, ((0, 0), (0, pad), (0, 0)))
    kvt = jnp.swapaxes(kvp, 1, 2)
    out = jax.ShapeDtypeStruct((n, SUBLANES, gw), F32)
    oblk = pl.BlockSpec((1, SUBLANES, gw), lambda i: (i, 0, 0))
    cspecs = [pl.BlockSpec((1, 2, gw, v.shape[-1]), lambda i: (i, 0, 0, 0)) for v in views]
    res = pl.pallas_call(
        functools.partial(_sattn_kernel, t_new=t_new),
        out_shape=(out,) * 6 + tuple(jax.ShapeDtypeStruct(v.shape, F32) for v in views),
        grid=(n,),
        in_specs=[pl.BlockSpec(memory_space=pltpu.SMEM)]
        + [_const_spec(b.shape) for b in bkt_old]
        + [
            _const_spec((N_GROUPS, SAMPLE_QROWS, SUBLANES)),
            pl.BlockSpec((1, SAMPLE_QROWS, ATT_WIDTH), lambda i: (i, 0, 0)),
            pl.BlockSpec((1, SUBLANES, 2 * ATT_WIDTH), lambda i: (i, 0, 0)),
            pl.BlockSpec((1, 2 * ATT_WIDTH, SUBLANES), lambda i: (i, 0, 0)),
        ]
        + cspecs,
        out_specs=(oblk,) * 6 + tuple(cspecs),
        scratch_shapes=[pltpu.VMEM(b.shape, F32) for b in bkt_old]
        + [pltpu.VMEM((N_GROUPS, SAMPLE_QROWS, SUBLANES), F32), pltpu.VMEM((gw, LANES), F32)],
        compiler_params=_params(("arbitrary",)),
        name="sample_attn",
    )(rel_bias, *bkt_old, jnp.stack(bkt_new), qp, kvp, kvt, *views)
    new_caches = [jnp.transpose(c.reshape(n, 2, hpg, HEAD_DIM, c.shape[-1]), (0, 4, 1, 2, 3)) for c in res[6:]]
    return [a[:, :t_new] for a in res[:3]], [a[:, :t_new] for a in res[3:6]], new_caches


def _oproj_kernel(x_ref, mod_ref, *refs, tm):
    nblk = ATT_WIDTH // LANES
    halves = nblk // N_GROUPS
    o_refs, l_refs = refs[:nblk], refs[nblk:2 * nblk]
    wo_ref, out_ref, att_ref = refs[2 * nblk:]
    rc = BF16_ROWS

    def body(i, c):
        r0 = pl.multiple_of(i * rc, rc)
        for hf in range(halves):
            blks = [g * halves + hf for g in range(N_GROUPS)]
            ls = [l_refs[k][0, pl.ds(r0, rc), :] for k in blks]
            mx = functools.reduce(jnp.maximum, ls)
            es = [jnp.exp(v - mx) for v in ls]
            inv = 1.0 / functools.reduce(lambda a, b: a + b, es)
            for k, e in zip(blks, es):
                att_ref[pl.ds(r0, rc), k * LANES:(k + 1) * LANES] = (o_refs[k][0, pl.ds(r0, rc), :] * (e * inv)).astype(BF)
        return c

    lax.fori_loop(0, tm // rc, body, 0, unroll=2)
    out_ref[0] = x_ref[0] + mod_ref[0, 2] * _dot(att_ref[...], wo_ref[...])


def _oproj_call(x, mods, os_, ls_, wo, *, tm):
    ns, s, d = x.shape
    r = mods.shape[2]
    kern = functools.partial(_oproj_kernel, tm=tm)
    gspec = pl.BlockSpec((1, tm, LANES), lambda n, t: (n, t, 0))
    return pl.pallas_call(
        kern,
        out_shape=jax.ShapeDtypeStruct((ns, s, d), F32),
        grid=(ns, s // tm),
        in_specs=[
            pl.BlockSpec((1, tm, d), lambda n, t: (n, t, 0)),
            pl.BlockSpec((1, N_MOD, r, d), lambda n, t: (n, 0, 0, 0)),
        ] + [gspec] * (2 * ATT_WIDTH // LANES) + [_const_spec((ATT_WIDTH, d))],
        out_specs=pl.BlockSpec((1, tm, d), lambda n, t: (n, t, 0)),
        scratch_shapes=[pltpu.VMEM((tm, ATT_WIDTH), BF)],
        compiler_params=_params(("arbitrary", "arbitrary")),
        name="attn_out_proj",
    )(x, mods, *os_, *ls_, wo)


def _trunk(x, mods, modkv, hist_a, hist_f, w, attn_fn, *, tm, rd):
    row = lambda v: v.reshape(1, -1)
    x, st_a = _conformer_call(x, mods[0], hist_a, row(w["norm_mix"][0]), w["a_w1"], row(w["a_b1"][0]),
                              w["a_dw"][0], row(w["a_dwb"][0]), row(w["a_ln_g"][0]), row(w["a_ln_b"][0]),
                              w["a_w2"], row(w["a_b2"][0]), tm=tm, rd=rd)
    x, st_f0 = _ffn_call(x, mods[0], hist_f[0], row(w["norm_ffn"][0]), w["f_wup"][0], w["f_cw"][0],
                         row(w["f_cb"][0]), w["f_wdown"][0], row(w["norm_f"]), tm=tm, rd=rd, final_norm=False)
    q, kv = _qkv_call(x, mods[1], modkv, row(w["norm_mix"][1]), row(w["norm_kv"]), w["w_q"], w["w_kv"], tm=tm)
    os_, ls_, attn_extra = attn_fn(q, kv)
    x = _oproj_call(x, mods[1], os_, ls_, w["w_o"], tm=tm)
    y, st_f1 = _ffn_call(x, mods[1], hist_f[1], row(w["norm_ffn"][1]), w["f_wup"][1], w["f_cw"][1],
                         row(w["f_cb"][1]), w["f_wdown"][1], row(w["norm_f"]), tm=tm, rd=rd, final_norm=True)
    return y, kv, st_a, [st_f0, st_f1], attn_extra


def kernel(x_prompt, x_sample, cache_kv_w128, cache_kv_w512, cache_kv_w2048, state_conv_a, state_conv_ffn, c_prompt, c_sample, w_mod, b_mod, norm_mix, norm_ffn, a_w1, a_b1, a_dw, a_dwb, a_ln_g, a_ln_b, a_w2, a_b2, w_mod_kv, b_mod_kv, norm_kv, w_kv, w_q, w_o, rel_bias, f_wup, f_cw, f_cb, f_wdown, norm_f):
    d = D_MODEL
    bsz, seq, _ = x_prompt.shape
    nseq, t_new, _ = x_sample.shape
    caches = (cache_kv_w128, cache_kv_w512, cache_kv_w2048)
    hpg = HEADS_PER_GROUP

    w = dict(norm_mix=norm_mix, norm_ffn=norm_ffn, a_w1=a_w1[0].astype(BF), a_b1=a_b1, a_dw=a_dw, a_dwb=a_dwb,
             a_ln_g=a_ln_g, a_ln_b=a_ln_b, a_w2=a_w2[0].astype(BF), a_b2=a_b2, norm_kv=norm_kv,
             w_kv=w_kv.astype(BF), w_q=w_q[0].astype(BF), w_o=w_o[0].astype(BF),
             f_wup=[f_wup[l].astype(BF) for l in range(DEPTH)], f_cw=f_cw, f_cb=f_cb,
             f_wdown=[f_wdown[l].astype(BF) for l in range(DEPTH)], norm_f=norm_f)

    n_c = bsz + nseq
    mp = _round_up(n_c, SUBLANES)
    c_all = jnp.pad(jnp.concatenate([c_prompt, c_sample], axis=0), ((0, mp - n_c), (0, 0)))
    mod = _mod_call(c_all, w_mod, b_mod.reshape(DEPTH, 1, N_MOD * d), tn=1536)
    modkv = _mod_call(c_all, w_mod_kv[None], b_mod_kv.reshape(1, 1, 2 * d), tn=1024)[0]

    mods_p = [mod[l, :bsz].reshape(bsz, N_MOD, 1, d) for l in range(DEPTH)]
    modkv_p = modkv[:bsz].reshape(bsz, 2, 1, d)
    hist_a_p = jnp.zeros((bsz, CONV_A_WIDTH - 1, d), F32)
    hist_f_p = [jnp.zeros((bsz, CONV_F_WIDTH - 1, 2 * D_FF), F32)] * DEPTH

    def prompt_attn(q, kv):
        res = [_pattn_call(q, kv, rel_bias, g, dil) for g, (_, dil) in enumerate(DIL_GROUPS)]
        return [a for r in res for a in r[0]], [a for r in res for a in r[1]], None

    y_p, kv_p, st_a_p, st_f_p, _ = _trunk(x_prompt, mods_p, modkv_p, hist_a_p, hist_f_p, w, prompt_attn,
                                          tm=TM_PROMPT, rd=1)
    wmax = max(c.shape[1] for c in caches)
    assert seq >= wmax
    kv_tail = kv_p[:, seq - wmax:].reshape(bsz, wmax, 2, N_HEADS, HEAD_DIM)
    kv_bufs_p = [kv_tail[:, wmax - c.shape[1]:, :, g * hpg:(g + 1) * hpg] for g, c in enumerate(caches)]
    conv_a_p = st_a_p[None]
    conv_f_p = jnp.stack(st_f_p)

    rows = t_new * nseq
    tmaj = lambda v: jnp.swapaxes(v, 0, 1).reshape(1, -1, v.shape[-1])
    smaj = lambda v, c: jnp.swapaxes(v.reshape(-1, nseq, c), 0, 1)

    def per_row(m, k):
        return jnp.tile(jnp.swapaxes(m.reshape(nseq, k, d), 0, 1), (1, t_new, 1))[None]

    mods_s = [per_row(mod[l, bsz:n_c], N_MOD) for l in range(DEPTH)]
    modkv_s = per_row(modkv[bsz:n_c], 2)
    hist_a_s = tmaj(state_conv_a[0])
    hist_f_s = [tmaj(state_conv_ffn[l]) for l in range(DEPTH)]

    def sample_attn(q, kv):
        os_, ls_, new_caches = _sattn_call(smaj(q, ATT_WIDTH), smaj(kv, 2 * ATT_WIDTH), caches, rel_bias)
        halves = lambda vs: [tmaj(a[..., c0:c0 + LANES]) for a in vs for c0 in range(0, GROUP_WIDTH, LANES)]
        return halves(os_), halves(ls_), new_caches

    y_s, _, st_a_s, st_f_s, kv_bufs_s = _trunk(tmaj(x_sample), mods_s, modkv_s, hist_a_s, hist_f_s, w,
                                               sample_attn, tm=rows, rd=nseq)
    y_s = smaj(y_s, d)
    conv_a_s = smaj(st_a_s, d)[None]
    conv_f_s = jnp.stack([smaj(s, 2 * D_FF) for s in st_f_s])

    return (y_p, y_s, kv_bufs_p[0], kv_bufs_p[1], kv_bufs_p[2], conv_a_p, conv_f_p,
            kv_bufs_s[0], kv_bufs_s[1], kv_bufs_s[2], conv_a_s, conv_f_s)
```

```python
import functools
import math

import jax
import jax.numpy as jnp
from jax import lax
from jax.experimental import pallas as pl
from jax.experimental.pallas import tpu as pltpu

D_MODEL = 1024
DEPTH = 2
HEAD_DIM = 64
HEADS_PER_GROUP = 4
DIL_GROUPS = ((128, 1), (512, 4), (2048, 16))
N_GROUPS = len(DIL_GROUPS)
N_HEADS = HEADS_PER_GROUP * N_GROUPS
ATT_WIDTH = N_HEADS * HEAD_DIM
GROUP_WIDTH = HEADS_PER_GROUP * HEAD_DIM
SPAN = 128
N_BUCKETS = 32
MAX_DISTANCE = 2048
CONV_A_WIDTH = 31
D_FF = 2816
CONV_F_WIDTH = 3
N_MOD = 6
EPS = 1e-6
LN_EPS = 1e-5
NEG_INF = -1e30

BF = jnp.bfloat16
F32 = jnp.float32

HEAD_SHIFT = HEAD_DIM.bit_length() - 1
SUBLANES = 8
SUBLANE_SHIFT = SUBLANES.bit_length() - 1
LANES = 128
BF16_ROWS = 16
ROW_UNROLL = 4
VMEM_LIMIT = 56 * 1024 * 1024

TM_PROMPT = 512
PATTN_ROWS = 512
CONF_ROWS = 128
FF_SUB = 256
SAMPLE_QROWS = HEADS_PER_GROUP * SUBLANES


def _round_up(a, b):
    return -(-a // b) * b


def _params(sem):
    return pltpu.CompilerParams(dimension_semantics=sem, vmem_limit_bytes=VMEM_LIMIT)


def _const_spec(shape):
    nd = len(shape)
    return pl.BlockSpec(shape, lambda *_: (0,) * nd, pipeline_mode=pl.Buffered(1))


def _dot(a, b):
    return jnp.dot(a, b, preferred_element_type=F32)


def _dot_nt(a, b):
    return lax.dot_general(a, b, (((1,), (1,)), ((), ())), preferred_element_type=F32)


def _sigmoid(v):
    return 1.0 / (1.0 + jnp.exp(-v))


def _mod_rows(mod_ref, idx, r0, rows, per_row):
    if per_row:
        return mod_ref[0, idx, pl.ds(r0, rows), :]
    return mod_ref[0, idx]


def _rms_mod_rows(x_ref, rows, targets, per_row, *, slot=None, straight=False):
    rc = BF16_ROWS
    lo, hi = rows

    def chunk(r0):
        x = x_ref[0, pl.ds(r0, rc), :]
        xn = x * lax.rsqrt(jnp.mean(x * x, axis=-1, keepdims=True) + EPS)
        for gain_ref, mod_ref, i_sh, i_sc, dst_ref in targets:
            sh = _mod_rows(mod_ref, i_sh, r0, rc, per_row)
            sc = _mod_rows(mod_ref, i_sc, r0, rc, per_row)
            idx = (pl.ds(r0, rc), slice(None))
            dst_ref[idx if slot is None else (slot,) + idx] = ((xn * gain_ref[...]) * (1.0 + sc) + sh).astype(BF)

    if straight:
        for r0 in range(lo, hi, rc):
            chunk(r0)
        return

    def body(i, c):
        chunk(pl.multiple_of(lo + i * rc, rc))
        return c

    lax.fori_loop(0, (hi - lo) // rc, body, 0, unroll=ROW_UNROLL)


def _tap_conv(src_ref, blk, w_ref, offsets, r0, rows):
    cols = slice(blk * LANES, (blk + 1) * LANES)
    acc = None
    for k, off in enumerate(offsets):
        term = w_ref[k:k + 1, cols] * src_ref[blk, r0 + off:r0 + off + rows, :]
        acc = term if acc is None else acc + term
    return acc


def _mod_kernel(c_ref, w_ref, b_ref, o_ref):
    c = c_ref[...]
    a = (c * _sigmoid(c)).astype(BF)
    o_ref[0] = _dot(a, w_ref[0].astype(BF)) + b_ref[0]


def _mod_call(c_all, w, b, tn):
    nl, d, n = w.shape
    mp = c_all.shape[0]
    return pl.pallas_call(
        _mod_kernel,
        out_shape=jax.ShapeDtypeStruct((nl, mp, n), F32),
        grid=(nl, n // tn),
        in_specs=[
            pl.BlockSpec((mp, d), lambda l, j: (0, 0)),
            pl.BlockSpec((1, d, tn), lambda l, j: (l, 0, j)),
            pl.BlockSpec((1, 1, tn), lambda l, j: (l, 0, j)),
        ],
        out_specs=pl.BlockSpec((1, mp, tn), lambda l, j: (l, 0, j)),
        compiler_params=_params(("arbitrary", "arbitrary")),
        name="adaln_mod",
    )(c_all, w, b)


def _conformer_kernel(x_ref, mod_ref, hist_ref, nrm_ref, w1_ref, b1_ref, dw_ref, dwb_ref, lng_ref, lnb_ref,
                      w2_ref, b2_ref, o_ref, st_ref, h_ref, h2_ref, u_ref, full_ref, y_ref, *, tm, rd, per_row):
    d = D_MODEL
    hh = (CONV_A_WIDTH - 1) * rd
    hp = _round_up(hh, SUBLANES)
    t = pl.program_id(1)

    nblk = d // LANES
    lanes = lambda blk: slice(blk * LANES, (blk + 1) * LANES)

    @pl.when(t == 0)
    def _():
        full_ref[:, 0:hp, :] = jnp.zeros((nblk, hp, LANES), F32)
        for blk in range(nblk):
            full_ref[blk, hp - hh:hp, :] = hist_ref[0, :, lanes(blk)]

    rb = min(tm, CONF_ROWS)
    rg, cg = 16, 512
    rcv = 64
    offsets = [hp - hh + k * rd for k in range(CONV_A_WIDTH)]
    for rs in range(0, tm, rb):
        _rms_mod_rows(x_ref, (rs, rs + rb), [(nrm_ref, mod_ref, 0, 1, h_ref)], per_row, straight=True)
        u_ref[rs:rs + rb, :] = _dot(h_ref[rs:rs + rb, :], w1_ref[...])

        for r0 in range(rs, rs + rb, rg):
            for c0 in range(0, d, cg):
                a = u_ref[r0:r0 + rg, c0:c0 + cg] + b1_ref[:, c0:c0 + cg]
                g = u_ref[r0:r0 + rg, d + c0:d + c0 + cg] + b1_ref[:, d + c0:d + c0 + cg]
                glu = a * _sigmoid(g)
                for j in range(cg // LANES):
                    full_ref[c0 // LANES + j, hp + r0:hp + r0 + rg, :] = glu[:, lanes(j)]

        for r0 in range(rs, rs + rb, rcv):
            for blk in range(nblk):
                y_ref[r0:r0 + rcv, lanes(blk)] = (_tap_conv(full_ref, blk, dw_ref, offsets, r0, rcv)
                                                  + dwb_ref[:, lanes(blk)])

        for r0 in range(rs, rs + rb, BF16_ROWS):
            y = y_ref[r0:r0 + BF16_ROWS, :]
            mu = jnp.mean(y, axis=-1, keepdims=True)
            dv = y - mu
            var = jnp.mean(dv * dv, axis=-1, keepdims=True)
            yn = dv * lax.rsqrt(var + LN_EPS) * lng_ref[...] + lnb_ref[...]
            h2_ref[r0:r0 + BF16_ROWS, :] = (yn * _sigmoid(yn)).astype(BF)

        out = _dot(h2_ref[rs:rs + rb, :], w2_ref[...]) + b2_ref[...]
        gate = mod_ref[0, 2, rs:rs + rb, :] if per_row else mod_ref[0, 2]
        o_ref[0, rs:rs + rb, :] = x_ref[0, rs:rs + rb, :] + gate * out

    for blk in range(nblk):
        new_hist = full_ref[blk, hp + tm - hh:hp + tm, :]
        st_ref[0, :, lanes(blk)] = new_hist
        full_ref[blk, hp - hh:hp, :] = new_hist


def _conformer_call(x, mods, hist, nrm, w1, b1, dw, dwb, lng, lnb, w2, b2, *, tm, rd):
    ns, s, d = x.shape
    r = mods.shape[2]
    hh = hist.shape[1]
    hp = _round_up(hh, SUBLANES)
    per_row = r > 1
    assert s % tm == 0 and (not per_row or (r == tm and s == tm))
    kern = functools.partial(_conformer_kernel, tm=tm, rd=rd, per_row=per_row)
    return pl.pallas_call(
        kern,
        out_shape=(jax.ShapeDtypeStruct((ns, s, d), F32), jax.ShapeDtypeStruct((ns, hh, d), F32)),
        grid=(ns, s // tm),
        in_specs=[
            pl.BlockSpec((1, tm, d), lambda n, t: (n, t, 0)),
            pl.BlockSpec((1, N_MOD, r, d), lambda n, t: (n, 0, 0, 0)),
            pl.BlockSpec((1, hh, d), lambda n, t: (n, 0, 0)),
            _const_spec((1, d)),
            _const_spec((d, 2 * d)),
            _const_spec((1, 2 * d)),
            _const_spec((CONV_A_WIDTH, d)),
            _const_spec((1, d)),
            _const_spec((1, d)),
            _const_spec((1, d)),
            _const_spec((d, d)),
            _const_spec((1, d)),
        ],
        out_specs=(
            pl.BlockSpec((1, tm, d), lambda n, t: (n, t, 0)),
            pl.BlockSpec((1, hh, d), lambda n, t: (n, 0, 0)),
        ),
        scratch_shapes=[
            pltpu.VMEM((tm, d), BF),
            pltpu.VMEM((tm, d), BF),
            pltpu.VMEM((tm, 2 * d), F32),
            pltpu.VMEM((d // LANES, hp + tm, LANES), F32),
            pltpu.VMEM((tm, d), F32),
        ],
        compiler_params=_params(("arbitrary", "arbitrary")),
        name="conformer_mixer",
    )(x, mods, hist, nrm, w1, b1, dw, dwb, lng, lnb, w2, b2)


def _ffn_kernel(x_ref, xnext_ref, mod_ref, hist_ref, nrm_ref, wu_ref, cw_ref, cb_ref, wd_ref, nf_ref, o_ref, st_ref,
                h_ref, ubuf_ref, carry_ref, act_ref, part_ref, *, tm, rd, per_row, final_norm):
    f = D_FF
    hh = (CONV_F_WIDTH - 1) * rd
    hp = _round_up(hh, SUBLANES)
    t = pl.program_id(1)

    cur = lax.rem(t, 2)
    norm_targets = [(nrm_ref, mod_ref, 3, 4, h_ref)]

    @pl.when(t == 0)
    def _():
        _rms_mod_rows(x_ref, (0, tm), norm_targets, per_row, slot=0)

    nblk = 2 * f // LANES
    lanes = lambda blk: slice(blk * LANES, (blk + 1) * LANES)

    @pl.when(t == 0)
    def _():
        carry_ref[...] = jnp.zeros((nblk, hp, LANES), F32)
        for blk in range(nblk):
            carry_ref[blk, hp - hh:hp, :] = hist_ref[0, :, lanes(blk)]

    ubuf_ref[:, 0:hp, :] = carry_ref[...]

    rc = 32
    offsets = [hp - hh + k * rd for k in range(CONV_F_WIDTH)]
    for c0 in range(0, f, FF_SUB):
        for half in range(2):
            col = half * f + c0
            u = _dot(h_ref[cur], wu_ref[:, col:col + FF_SUB])
            for j in range(FF_SUB // LANES):
                ubuf_ref[col // LANES + j, hp:hp + tm, :] = u[:, lanes(j)]
        for r0 in range(0, tm, rc):
            for cc0 in range(c0, c0 + FF_SUB, LANES):
                ys = []
                for half in range(2):
                    blk = (half * f + cc0) // LANES
                    ys.append(_tap_conv(ubuf_ref, blk, cw_ref, offsets, r0, rc) + cb_ref[:, lanes(blk)])
                yg, yv = ys
                act_ref[r0:r0 + rc, cc0:cc0 + LANES] = (yg * _sigmoid(yg) * yv).astype(BF)

    for blk in range(nblk):
        st_ref[0, :, lanes(blk)] = ubuf_ref[blk, hp + tm - hh:hp + tm, :]
    carry_ref[...] = ubuf_ref[:, tm:tm + hp, :]

    _rms_mod_rows(xnext_ref, (0, tm), norm_targets, per_row, slot=1 - cur, straight=True)
    xo = x_ref[0] + mod_ref[0, 5] * _dot(act_ref[...], wd_ref[...])
    if not final_norm:
        o_ref[0] = xo
        return
    part_ref[...] = xo
    rows = BF16_ROWS

    def norm_body(i, c):
        r0 = pl.multiple_of(i * rows, rows)
        v = part_ref[pl.ds(r0, rows), :]
        o_ref[0, pl.ds(r0, rows), :] = v * lax.rsqrt(jnp.mean(v * v, axis=-1, keepdims=True) + EPS) * nf_ref[...]
        return c

    lax.fori_loop(0, tm // rows, norm_body, 0, unroll=ROW_UNROLL)


def _ffn_call(x, mods, hist, nrm, wup, cw, cb, wdown, nf, *, tm, rd, final_norm):
    ns, s, d = x.shape
    r = mods.shape[2]
    hh = hist.shape[1]
    hp = _round_up(hh, SUBLANES)
    f = wdown.shape[0]
    per_row = r > 1
    nt = s // tm
    assert f == D_FF and f % FF_SUB == 0 and s % tm == 0 and (not per_row or (r == tm and s == tm))
    kern = functools.partial(_ffn_kernel, tm=tm, rd=rd, per_row=per_row, final_norm=final_norm)
    return pl.pallas_call(
        kern,
        out_shape=(jax.ShapeDtypeStruct((ns, s, d), F32), jax.ShapeDtypeStruct((ns, hh, 2 * f), F32)),
        grid=(ns, nt),
        in_specs=[
            pl.BlockSpec((1, tm, d), lambda n, t: (n, t, 0)),
            pl.BlockSpec((1, tm, d), lambda n, t: (n, jnp.minimum(t + 1, nt - 1), 0)),
            pl.BlockSpec((1, N_MOD, r, d), lambda n, t: (n, 0, 0, 0)),
            pl.BlockSpec((1, hh, 2 * f), lambda n, t: (n, 0, 0)),
            _const_spec((1, d)),
            _const_spec((d, 2 * f)),
            _const_spec((CONV_F_WIDTH, 2 * f)),
            _const_spec((1, 2 * f)),
            _const_spec((f, d)),
            _const_spec((1, d)),
        ],
        out_specs=(
            pl.BlockSpec((1, tm, d), lambda n, t: (n, t, 0)),
            pl.BlockSpec((1, hh, 2 * f), lambda n, t: (n, 0, 0)),
        ),
        scratch_shapes=[
            pltpu.VMEM((2, tm, d), BF),
            pltpu.VMEM((2 * f // LANES, hp + tm, LANES), F32),
            pltpu.VMEM((2 * f // LANES, hp, LANES), F32),
            pltpu.VMEM((tm, f), BF),
            pltpu.VMEM((tm, d), F32),
        ],
        compiler_params=_params(("arbitrary", "arbitrary")),
        name="conv_ffn",
    )(x, x, mods, hist, nrm, wup, cw, cb, wdown, nf)


def _qkv_kernel(x_ref, xnext_ref, mod_ref, modkv_ref, nq_ref, nkv_ref, wq_ref, wkv_ref, q_ref, kv_ref,
                hq_ref, hkv_ref, *, tm, per_row):
    t = pl.program_id(1)
    cur = lax.rem(t, 2)
    norm_targets = [(nq_ref, mod_ref, 0, 1, hq_ref), (nkv_ref, modkv_ref, 0, 1, hkv_ref)]

    @pl.when(t == 0)
    def _():
        _rms_mod_rows(x_ref, (0, tm), norm_targets, per_row, slot=0)

    q_ref[0] = _dot(hq_ref[cur], wq_ref[...])
    _rms_mod_rows(xnext_ref, (0, tm), norm_targets, per_row, slot=1 - cur, straight=True)
    kv_ref[0] = _dot(hkv_ref[cur], wkv_ref[...])


def _qkv_call(x, mods, modkv, nq, nkv, wq, wkv, *, tm):
    ns, s, d = x.shape
    r = mods.shape[2]
    per_row = r > 1
    nt = s // tm
    kern = functools.partial(_qkv_kernel, tm=tm, per_row=per_row)
    return pl.pallas_call(
        kern,
        out_shape=(jax.ShapeDtypeStruct((ns, s, ATT_WIDTH), F32), jax.ShapeDtypeStruct((ns, s, 2 * ATT_WIDTH), F32)),
        grid=(ns, nt),
        in_specs=[
            pl.BlockSpec((1, tm, d), lambda n, t: (n, t, 0)),
            pl.BlockSpec((1, tm, d), lambda n, t: (n, jnp.minimum(t + 1, nt - 1), 0)),
            pl.BlockSpec((1, N_MOD, r, d), lambda n, t: (n, 0, 0, 0)),
            pl.BlockSpec((1, 2, r, d), lambda n, t: (n, 0, 0, 0)),
            _const_spec((1, d)),
            _const_spec((1, d)),
            _const_spec((d, ATT_WIDTH)),
            _const_spec((d, 2 * ATT_WIDTH)),
        ],
        out_specs=(
            pl.BlockSpec((1, tm, ATT_WIDTH), lambda n, t: (n, t, 0)),
            pl.BlockSpec((1, tm, 2 * ATT_WIDTH), lambda n, t: (n, t, 0)),
        ),
        scratch_shapes=[pltpu.VMEM((2, tm, d), BF), pltpu.VMEM((2, tm, d), BF)],
        compiler_params=_params(("arbitrary", "arbitrary")),
        name="qkv_proj",
    )(x, x, mods, modkv, nq, nkv, wq, wkv)


def _rel_bucket(dist):
    max_exact = N_BUCKETS // 2
    dd = jnp.maximum(dist, 1).astype(F32)
    large = max_exact + (jnp.log(dd / max_exact) / math.log(MAX_DISTANCE / max_exact)
                         * (N_BUCKETS - max_exact)).astype(jnp.int32)
    large = jnp.minimum(large, N_BUCKETS - 1)
    return jnp.where(dist < max_exact, dist, large)


def _prompt_bucket_table(dil):
    qi = jnp.arange(SPAN, dtype=jnp.int32)[:, None]
    ki = jnp.arange(2 * SPAN, dtype=jnp.int32)[None, :]
    m = qi + SPAN - ki
    valid = (m >= 0) & (m <= SPAN)
    return jnp.where(valid, _rel_bucket(jnp.clip(m, 0, SPAN) * dil), -1).astype(jnp.int32)


def _sample_bucket_tables(dil, width, t_new):
    row = jnp.arange(SAMPLE_QROWS, dtype=jnp.int32)[:, None]
    t = (row % SUBLANES) % t_new
    tn = jnp.arange(SUBLANES, dtype=jnp.int32)[None, :]

    def table(dist, ok):
        ok = ok & (dist >= 0) & (dist % dil == 0) & (dist // dil <= SPAN)
        return jnp.where(ok, _rel_bucket(jnp.clip(dist, 0, SPAN * dil)), -1).astype(jnp.int32)

    old = table(width + t - jnp.arange(width, dtype=jnp.int32)[None, :], True)
    new = table(t - tn, tn < t_new)
    return old, new


def _bias_from_buckets(bkt, relb_ref, head):
    acc = jnp.full(bkt.shape, NEG_INF, F32)
    for b in range(N_BUCKETS):
        acc = jnp.where(bkt == b, relb_ref[b, head], acc)
    return acc


def _pattn_kernel(relb_ref, bkt_ref, q0_ref, q1_ref, k0_ref, k1_ref, v0_ref, v1_ref,
                  o0_ref, o1_ref, l0_ref, l1_ref, bias_ref, carry_ref, *, g, dil, nbk):
    b = pl.program_id(0)
    i = pl.program_id(1)
    hpg = HEADS_PER_GROUP
    span_rows = SPAN * dil

    @pl.when((b == 0) & (i == 0))
    def _():
        bkt = bkt_ref[...]
        no_prev = lax.broadcasted_iota(jnp.int32, (SPAN, 2 * SPAN), 1) < SPAN
        for h in range(hpg):
            bias = _bias_from_buckets(bkt, relb_ref, g * hpg + h)
            bias_ref[0, h] = bias
            bias_ref[1, h] = jnp.where(no_prev, NEG_INF, bias)

    @pl.when(i == 0)
    def _():
        carry_ref[...] = jnp.zeros(carry_ref.shape, BF)

    lanehead = lax.broadcasted_iota(jnp.int32, (SPAN, GROUP_WIDTH), 1) >> HEAD_SHIFT
    first = jnp.where(i == 0, 1, 0)
    rd_slot = lax.rem(i, 2)
    last_k, last_v = {}, {}

    def block(j, r):
        rows = pl.ds(j * span_rows + r, SPAN, stride=dil)
        cur = lambda r0, r1: jnp.concatenate([r0[0, rows, :], r1[0, rows, :]], axis=1)
        if j == 0:
            kprev, vprev = carry_ref[rd_slot, 0, r], carry_ref[rd_slot, 1, r]
        else:
            kprev, vprev = last_k[r], last_v[r]
        kcur = cur(k0_ref, k1_ref).astype(BF)
        vcur = cur(v0_ref, v1_ref).astype(BF)
        if j == nbk - 1:
            carry_ref[1 - rd_slot, 0, r] = kcur
            carry_ref[1 - rd_slot, 1, r] = vcur
        else:
            last_k[r], last_v[r] = kcur, vcur
        q = cur(q0_ref, q1_ref) * (HEAD_DIM ** -0.5)
        lhs = jnp.concatenate([jnp.where(lanehead == h, q, 0.0).astype(BF) for h in range(hpg)], axis=0)
        s = _dot_nt(lhs, jnp.concatenate([kprev, kcur], axis=0))

        ps, ms, sums = [], [], []
        for h in range(hpg):
            logit = s[h * SPAN:(h + 1) * SPAN] + (bias_ref[first, h] if j == 0 else bias_ref[0, h])
            m = jnp.max(logit, axis=-1, keepdims=True)
            e = jnp.exp(logit - m)
            ps.append(e.astype(BF))
            ms.append(m)
            sums.append(jnp.sum(e, axis=-1, keepdims=True))
        pv = _dot(jnp.concatenate(ps, axis=0), jnp.concatenate([vprev, vcur], axis=0))

        o = jnp.zeros((SPAN, GROUP_WIDTH), F32)
        l = jnp.zeros((SPAN, GROUP_WIDTH), F32)
        for h in range(hpg):
            o = jnp.where(lanehead == h, pv[h * SPAN:(h + 1) * SPAN] * (1.0 / sums[h]), o)
            l = jnp.where(lanehead == h, ms[h] + jnp.log(sums[h]), l)
        o0_ref[0, rows, :] = o[:, :LANES]
        o1_ref[0, rows, :] = o[:, LANES:]
        l0_ref[0, rows, :] = l[:, :LANES]
        l1_ref[0, rows, :] = l[:, LANES:]

    if dil == 1:
        for j in range(nbk):
            block(j, 0)
    else:
        assert nbk == 1

        def body(r, c):
            block(0, r)
            return c

        lax.fori_loop(0, dil, body, 0, unroll=2)


def _pattn_call(q, kv, rel_bias, g, dil):
    bsz, s, _ = q.shape
    nbk = max(1, PATTN_ROWS // (dil * SPAN))
    rows = nbk * dil * SPAN
    assert s % rows == 0 and GROUP_WIDTH == 2 * LANES
    blk = (1, rows, LANES)
    kcol = 2 * g
    vcol = ATT_WIDTH // LANES + 2 * g
    out = jax.ShapeDtypeStruct((bsz, s, LANES), F32)
    ospec = pl.BlockSpec(blk, lambda b, i: (b, i, 0))
    res = pl.pallas_call(
        functools.partial(_pattn_kernel, g=g, dil=dil, nbk=nbk),
        out_shape=(out,) * 4,
        grid=(bsz, s // rows),
        in_specs=[
            pl.BlockSpec(memory_space=pltpu.SMEM),
            _const_spec((SPAN, 2 * SPAN)),
            pl.BlockSpec(blk, lambda b, i: (b, i, kcol)),
            pl.BlockSpec(blk, lambda b, i: (b, i, kcol + 1)),
            pl.BlockSpec(blk, lambda b, i: (b, i, kcol)),
            pl.BlockSpec(blk, lambda b, i: (b, i, kcol + 1)),
            pl.BlockSpec(blk, lambda b, i: (b, i, vcol)),
            pl.BlockSpec(blk, lambda b, i: (b, i, vcol + 1)),
        ],
        out_specs=(ospec,) * 4,
        scratch_shapes=[pltpu.VMEM((2, HEADS_PER_GROUP, SPAN, 2 * SPAN), F32),
                        pltpu.VMEM((2, 2, dil, SPAN, GROUP_WIDTH), BF)],
        compiler_params=_params(("arbitrary", "arbitrary")),
        name=f"prompt_attn_g{g}",
    )(rel_bias, _prompt_bucket_table(dil), q, q, kv, kv, kv, kv)
    return list(res[:2]), list(res[2:])


def _sattn_kernel(relb_ref, bo0_ref, bo1_ref, bo2_ref, bn_ref, q_ref, kvn_ref, kvt_ref, c0_ref, c1_ref, c2_ref,
                  o0_ref, o1_ref, o2_ref, l0_ref, l1_ref, l2_ref, n0_ref, n1_ref, n2_ref,
                  bias0_ref, bias1_ref, bias2_ref, biasn_ref, tail_ref, *, t_new):
    hpg = HEADS_PER_GROUP
    gw = GROUP_WIDTH
    rows = SAMPLE_QROWS
    bo_refs = (bo0_ref, bo1_ref, bo2_ref)
    bias_refs = (bias0_ref, bias1_ref, bias2_ref)
    caches = (c0_ref, c1_ref, c2_ref)
    outs = ((o0_ref, l0_ref, n0_ref), (o1_ref, l1_ref, n1_ref), (o2_ref, l2_ref, n2_ref))

    @pl.when(pl.program_id(0) == 0)
    def _():
        tail_ref[...] = jnp.zeros(tail_ref.shape, F32)
        for g in range(N_GROUPS):
            for h in range(hpg):
                sl = slice(h * SUBLANES, (h + 1) * SUBLANES)
                bias_refs[g][sl, :] = _bias_from_buckets(bo_refs[g][sl, :], relb_ref, g * hpg + h)
                biasn_ref[g, sl, :] = _bias_from_buckets(bn_ref[g, sl, :], relb_ref, g * hpg + h)

    lanehead = lax.broadcasted_iota(jnp.int32, (rows, gw), 1) >> HEAD_SHIFT
    rowhead = lax.broadcasted_iota(jnp.int32, (rows, gw), 0) >> SUBLANE_SHIFT
    own = lanehead == rowhead
    tail_lane = lax.broadcasted_iota(jnp.int32, (gw, LANES), 1) >= LANES - t_new

    for g in range(N_GROUPS):
        cref = caches[g]
        o_ref, l_ref, n_ref = outs[g]
        width = cref.shape[-1]
        qm = jnp.where(own, q_ref[0, :, g * gw:(g + 1) * gw] * (HEAD_DIM ** -0.5), 0.0).astype(BF)
        knew = kvn_ref[0, :, g * gw:(g + 1) * gw].astype(BF)
        vnew = kvn_ref[0, :, ATT_WIDTH + g * gw:ATT_WIDTH + (g + 1) * gw].astype(BF)
        lo = _dot(qm, cref[0, 0].astype(BF)) + bias_refs[g][...]
        ln = _dot_nt(qm, knew) + biasn_ref[g]
        m = jnp.maximum(jnp.max(lo, axis=-1, keepdims=True), jnp.max(ln, axis=-1, keepdims=True))
        eo = jnp.exp(lo - m)
        en = jnp.exp(ln - m)
        ssum = jnp.sum(eo, axis=-1, keepdims=True) + jnp.sum(en, axis=-1, keepdims=True)
        pv = _dot_nt(eo.astype(BF), cref[0, 1].astype(BF)) + _dot(en.astype(BF), vnew)
        om = jnp.where(own, pv * (1.0 / ssum), 0.0)
        lm = jnp.where(own, m + jnp.log(ssum), 0.0)
        o_acc = om[0:SUBLANES]
        l_acc = lm[0:SUBLANES]
        for h in range(1, hpg):
            o_acc = o_acc + om[h * SUBLANES:(h + 1) * SUBLANES]
            l_acc = l_acc + lm[h * SUBLANES:(h + 1) * SUBLANES]
        o_ref[0] = o_acc
        l_ref[0] = l_acc

        for kv in range(2):
            rolled = pltpu.roll(cref[0, kv], width - t_new, 1)
            tail_ref[:, 0:SUBLANES] = kvt_ref[0, kv * ATT_WIDTH + g * gw:kv * ATT_WIDTH + (g + 1) * gw, :]
            tail = pltpu.roll(tail_ref[...], LANES - t_new, 1)
            if width > LANES:
                n_ref[0, kv, :, 0:width - LANES] = rolled[:, 0:width - LANES]
            n_ref[0, kv, :, width - LANES:width] = jnp.where(tail_lane, tail, rolled[:, width - LANES:width])


def _sattn_call(q, kvn, caches, rel_bias):
    n, t_new, _ = q.shape
    gw = GROUP_WIDTH
    hpg = HEADS_PER_GROUP
    assert t_new <= SUBLANES
    views, bkt_old, bkt_new = [], [], []
    for g, (window, dil) in enumerate(DIL_GROUPS):
        width = caches[g].shape[1]
        assert width == window and width % LANES == 0
        views.append(jnp.transpose(caches[g], (0, 2, 3, 4, 1)).reshape(n, 2, gw, width))
        old, new = _sample_bucket_tables(dil, width, t_new)
        bkt_old.append(old)
        bkt_new.append(new)
    pad = SUBLANES - t_new
    qp = jnp.tile(jnp.pad(q, ((0, 0), (0, pad), (0, 0))), (1, hpg, 1))
    kvp = jnp.pad(kvn, ((0, 0), (0, pad), (0, 0)))
    kvt = jnp.swapaxes(kvp, 1, 2)
    out = jax.ShapeDtypeStruct((n, SUBLANES, gw), F32)
    oblk = pl.BlockSpec((1, SUBLANES, gw), lambda i: (i, 0, 0))
    cspecs = [pl.BlockSpec((1, 2, gw, v.shape[-1]), lambda i: (i, 0, 0, 0)) for v in views]
    res = pl.pallas_call(
        functools.partial(_sattn_kernel, t_new=t_new),
        out_shape=(out,) * 6 + tuple(jax.ShapeDtypeStruct(v.shape, F32) for v in views),
        grid=(n,),
        in_specs=[pl.BlockSpec(memory_space=pltpu.SMEM)]
        + [_const_spec(b.shape) for b in bkt_old]
        + [
            _const_spec((N_GROUPS, SAMPLE_QROWS, SUBLANES)),
            pl.BlockSpec((1, SAMPLE_QROWS, ATT_WIDTH), lambda i: (i, 0, 0)),
            pl.BlockSpec((1, SUBLANES, 2 * ATT_WIDTH), lambda i: (i, 0, 0)),
            pl.BlockSpec((1, 2 * ATT_WIDTH, SUBLANES), lambda i: (i, 0, 0)),
        ]
        + cspecs,
        out_specs=(oblk,) * 6 + tuple(cspecs),
        scratch_shapes=[pltpu.VMEM(b.shape, F32) for b in bkt_old]
        + [pltpu.VMEM((N_GROUPS, SAMPLE_QROWS, SUBLANES), F32), pltpu.VMEM((gw, LANES), F32)],
        compiler_params=_params(("arbitrary",)),
        name="sample_attn",
    )(rel_bias, *bkt_old, jnp.stack(bkt_new), qp, kvp, kvt, *views)
    new_caches = [jnp.transpose(c.reshape(n, 2, hpg, HEAD_DIM, c.shape[-1]), (0, 4, 1, 2, 3)) for c in res[6:]]
    return [a[:, :t_new] for a in res[:3]], [a[:, :t_new] for a in res[3:6]], new_caches


def _oproj_kernel(x_ref, mod_ref, *refs, tm):
    nblk = ATT_WIDTH // LANES
    halves = nblk // N_GROUPS
    o_refs, l_refs = refs[:nblk], refs[nblk:2 * nblk]
    wo_ref, out_ref, att_ref = refs[2 * nblk:]
    rc = BF16_ROWS

    def body(i, c):
        r0 = pl.multiple_of(i * rc, rc)
        for hf in range(halves):
            blks = [g * halves + hf for g in range(N_GROUPS)]
            ls = [l_refs[k][0, pl.ds(r0, rc), :] for k in blks]
            mx = functools.reduce(jnp.maximum, ls)
            es = [jnp.exp(v - mx) for v in ls]
            inv = 1.0 / functools.reduce(lambda a, b: a + b, es)
            for k, e in zip(blks, es):
                att_ref[pl.ds(r0, rc), k * LANES:(k + 1) * LANES] = (o_refs[k][0, pl.ds(r0, rc), :] * (e * inv)).astype(BF)
        return c

    lax.fori_loop(0, tm // rc, body, 0, unroll=2)
    out_ref[0] = x_ref[0] + mod_ref[0, 2] * _dot(att_ref[...], wo_ref[...])


def _oproj_call(x, mods, os_, ls_, wo, *, tm):
    ns, s, d = x.shape
    r = mods.shape[2]
    kern = functools.partial(_oproj_kernel, tm=tm)
    gspec = pl.BlockSpec((1, tm, LANES), lambda n, t: (n, t, 0))
    return pl.pallas_call(
        kern,
        out_shape=jax.ShapeDtypeStruct((ns, s, d), F32),
        grid=(ns, s // tm),
        in_specs=[
            pl.BlockSpec((1, tm, d), lambda n, t: (n, t, 0)),
            pl.BlockSpec((1, N_MOD, r, d), lambda n, t: (n, 0, 0, 0)),
        ] + [gspec] * (2 * ATT_WIDTH // LANES) + [_const_spec((ATT_WIDTH, d))],
        out_specs=pl.BlockSpec((1, tm, d), lambda n, t: (n, t, 0)),
        scratch_shapes=[pltpu.VMEM((tm, ATT_WIDTH), BF)],
        compiler_params=_params(("arbitrary", "arbitrary")),
        name="attn_out_proj",
    )(x, mods, *os_, *ls_, wo)


def _trunk(x, mods, modkv, hist_a, hist_f, w, attn_fn, *, tm, rd):
    row = lambda v: v.reshape(1, -1)
    x, st_a = _conformer_call(x, mods[0], hist_a, row(w["norm_mix"][0]), w["a_w1"], row(w["a_b1"][0]),
                              w["a_dw"][0], row(w["a_dwb"][0]), row(w["a_ln_g"][0]), row(w["a_ln_b"][0]),
                              w["a_w2"], row(w["a_b2"][0]), tm=tm, rd=rd)
    x, st_f0 = _ffn_call(x, mods[0], hist_f[0], row(w["norm_ffn"][0]), w["f_wup"][0], w["f_cw"][0],
                         row(w["f_cb"][0]), w["f_wdown"][0], row(w["norm_f"]), tm=tm, rd=rd, final_norm=False)
    q, kv = _qkv_call(x, mods[1], modkv, row(w["norm_mix"][1]), row(w["norm_kv"]), w["w_q"], w["w_kv"], tm=tm)
    os_, ls_, attn_extra = attn_fn(q, kv)
    x = _oproj_call(x, mods[1], os_, ls_, w["w_o"], tm=tm)
    y, st_f1 = _ffn_call(x, mods[1], hist_f[1], row(w["norm_ffn"][1]), w["f_wup"][1], w["f_cw"][1],
                         row(w["f_cb"][1]), w["f_wdown"][1], row(w["norm_f"]), tm=tm, rd=rd, final_norm=True)
    return y, kv, st_a, [st_f0, st_f1], attn_extra


def kernel(x_prompt, x_sample, cache_kv_w128, cache_kv_w512, cache_kv_w2048, state_conv_a, state_conv_ffn, c_prompt, c_sample, w_mod, b_mod, norm_mix, norm_ffn, a_w1, a_b1, a_dw, a_dwb, a_ln_g, a_ln_b, a_w2, a_b2, w_mod_kv, b_mod_kv, norm_kv, w_kv, w_q, w_o, rel_bias, f_wup, f_cw, f_cb, f_wdown, norm_f):
    d = D_MODEL
    bsz, seq, _ = x_prompt.shape
    nseq, t_new, _ = x_sample.shape
    caches = (cache_kv_w128, cache_kv_w512, cache_kv_w2048)
    hpg = HEADS_PER_GROUP

    w = dict(norm_mix=norm_mix, norm_ffn=norm_ffn, a_w1=a_w1[0].astype(BF), a_b1=a_b1, a_dw=a_dw, a_dwb=a_dwb,
             a_ln_g=a_ln_g, a_ln_b=a_ln_b, a_w2=a_w2[0].astype(BF), a_b2=a_b2, norm_kv=norm_kv,
             w_kv=w_kv.astype(BF), w_q=w_q[0].astype(BF), w_o=w_o[0].astype(BF),
             f_wup=[f_wup[l].astype(BF) for l in range(DEPTH)], f_cw=f_cw, f_cb=f_cb,
             f_wdown=[f_wdown[l].astype(BF) for l in range(DEPTH)], norm_f=norm_f)

    n_c = bsz + nseq
    mp = _round_up(n_c, SUBLANES)
    c_all = jnp.pad(jnp.concatenate([c_prompt, c_sample], axis=0), ((0, mp - n_c), (0, 0)))
    mod = _mod_call(c_all, w_mod, b_mod.reshape(DEPTH, 1, N_MOD * d), tn=1536)
    modkv = _mod_call(c_all, w_mod_kv[None], b_mod_kv.reshape(1, 1, 2 * d), tn=1024)[0]

    mods_p = [mod[l, :bsz].reshape(bsz, N_MOD, 1, d) for l in range(DEPTH)]
    modkv_p = modkv[:bsz].reshape(bsz, 2, 1, d)
    hist_a_p = jnp.zeros((bsz, CONV_A_WIDTH - 1, d), F32)
    hist_f_p = [jnp.zeros((bsz, CONV_F_WIDTH - 1, 2 * D_FF), F32)] * DEPTH

    def prompt_attn(q, kv):
        res = [_pattn_call(q, kv, rel_bias, g, dil) for g, (_, dil) in enumerate(DIL_GROUPS)]
        return [a for r in res for a in r[0]], [a for r in res for a in r[1]], None

    y_p, kv_p, st_a_p, st_f_p, _ = _trunk(x_prompt, mods_p, modkv_p, hist_a_p, hist_f_p, w, prompt_attn,
                                          tm=TM_PROMPT, rd=1)
    wmax = max(c.shape[1] for c in caches)
    assert seq >= wmax
    kv_tail = kv_p[:, seq - wmax:].reshape(bsz, wmax, 2, N_HEADS, HEAD_DIM)
    kv_bufs_p = [kv_tail[:, wmax - c.shape[1]:, :, g * hpg:(g + 1) * hpg] for g, c in enumerate(caches)]
    conv_a_p = st_a_p[None]
    conv_f_p = jnp.stack(st_f_p)

    rows = t_new * nseq
    tmaj = lambda v: jnp.swapaxes(v, 0, 1).reshape(1, -1, v.shape[-1])
    smaj = lambda v, c: jnp.swapaxes(v.reshape(-1, nseq, c), 0, 1)

    def per_row(m, k):
        return jnp.tile(jnp.swapaxes(m.reshape(nseq, k, d), 0, 1), (1, t_new, 1))[None]

    mods_s = [per_row(mod[l, bsz:n_c], N_MOD) for l in range(DEPTH)]
    modkv_s = per_row(modkv[bsz:n_c], 2)
    hist_a_s = tmaj(state_conv_a[0])
    hist_f_s = [tmaj(state_conv_ffn[l]) for l in range(DEPTH)]

    def sample_attn(q, kv):
        os_, ls_, new_caches = _sattn_call(smaj(q, ATT_WIDTH), smaj(kv, 2 * ATT_WIDTH), caches, rel_bias)
        halves = lambda vs: [tmaj(a[..., c0:c0 + LANES]) for a in vs for c0 in range(0, GROUP_WIDTH, LANES)]
        return halves(os_), halves(ls_), new_caches

    y_s, _, st_a_s, st_f_s, kv_bufs_s = _trunk(tmaj(x_sample), mods_s, modkv_s, hist_a_s, hist_f_s, w,
                                               sample_attn, tm=rows, rd=nseq)
    y_s = smaj(y_s, d)
    conv_a_s = smaj(st_a_s, d)[None]
    conv_f_s = jnp.stack([smaj(s, 2 * D_FF) for s in st_f_s])

    return (y_p, y_s, kv_bufs_p[0], kv_bufs_p[1], kv_bufs_p[2], conv_a_p, conv_f_p,
            kv_bufs_s[0], kv_bufs_s[1], kv_bufs_s[2], conv_a_s, conv_f_s)
```

```python
import functools
import math

import jax
import jax.numpy as jnp
from jax import lax
from jax.experimental import pallas as pl
from jax.experimental.pallas import tpu as pltpu

D_MODEL = 1024
DEPTH = 2
HEAD_DIM = 64
HEADS_PER_GROUP = 4
DIL_GROUPS = ((128, 1), (512, 4), (2048, 16))
N_GROUPS = len(DIL_GROUPS)
N_HEADS = HEADS_PER_GROUP * N_GROUPS
ATT_WIDTH = N_HEADS * HEAD_DIM
GROUP_WIDTH = HEADS_PER_GROUP * HEAD_DIM
SPAN = 128
N_BUCKETS = 32
MAX_DISTANCE = 2048
CONV_A_WIDTH = 31
D_FF = 2816
CONV_F_WIDTH = 3
N_MOD = 6
EPS = 1e-6
LN_EPS = 1e-5
NEG_INF = -1e30

BF = jnp.bfloat16
F32 = jnp.float32

HEAD_SHIFT = HEAD_DIM.bit_length() - 1
SUBLANES = 8
SUBLANE_SHIFT = SUBLANES.bit_length() - 1
LANES = 128
BF16_ROWS = 16
ROW_UNROLL = 4
VMEM_LIMIT = 56 * 1024 * 1024

TM_PROMPT = 512
PATTN_ROWS = 512
CONF_ROWS = 512
FF_SUB = 256
SAMPLE_QROWS = HEADS_PER_GROUP * SUBLANES


def _round_up(a, b):
    return -(-a // b) * b


def _params(sem):
    return pltpu.CompilerParams(dimension_semantics=sem, vmem_limit_bytes=VMEM_LIMIT)


def _const_spec(shape):
    nd = len(shape)
    return pl.BlockSpec(shape, lambda *_: (0,) * nd, pipeline_mode=pl.Buffered(1))


def _dot(a, b):
    return jnp.dot(a, b, preferred_element_type=F32)


def _dot_nt(a, b):
    return lax.dot_general(a, b, (((1,), (1,)), ((), ())), preferred_element_type=F32)


def _sigmoid(v):
    return 1.0 / (1.0 + jnp.exp(-v))


def _mod_rows(mod_ref, idx, r0, rows, per_row):
    if per_row:
        return mod_ref[0, idx, pl.ds(r0, rows), :]
    return mod_ref[0, idx]


def _rms_mod_rows(x_ref, rows, targets, per_row, *, slot=None, straight=False):
    rc = BF16_ROWS
    lo, hi = rows

    def chunk(r0):
        x = x_ref[0, pl.ds(r0, rc), :]
        xn = x * lax.rsqrt(jnp.mean(x * x, axis=-1, keepdims=True) + EPS)
        for gain_ref, mod_ref, i_sh, i_sc, dst_ref in targets:
            sh = _mod_rows(mod_ref, i_sh, r0, rc, per_row)
            sc = _mod_rows(mod_ref, i_sc, r0, rc, per_row)
            idx = (pl.ds(r0, rc), slice(None))
            dst_ref[idx if slot is None else (slot,) + idx] = ((xn * gain_ref[...]) * (1.0 + sc) + sh).astype(BF)

    if straight:
        for r0 in range(lo, hi, rc):
            chunk(r0)
        return

    def body(i, c):
        chunk(pl.multiple_of(lo + i * rc, rc))
        return c

    lax.fori_loop(0, (hi - lo) // rc, body, 0, unroll=ROW_UNROLL)


def _tap_conv(src_ref, blk, w_ref, offsets, r0, rows):
    cols = slice(blk * LANES, (blk + 1) * LANES)
    acc = None
    for k, off in enumerate(offsets):
        term = w_ref[k:k + 1, cols] * src_ref[blk, r0 + off:r0 + off + rows, :]
        acc = term if acc is None else acc + term
    return acc


def _mod_kernel(c_ref, w_ref, b_ref, o_ref):
    c = c_ref[...]
    a = (c * _sigmoid(c)).astype(BF)
    o_ref[0] = _dot(a, w_ref[0].astype(BF)) + b_ref[0]


def _mod_call(c_all, w, b, tn):
    nl, d, n = w.shape
    mp = c_all.shape[0]
    return pl.pallas_call(
        _mod_kernel,
        out_shape=jax.ShapeDtypeStruct((nl, mp, n), F32),
        grid=(nl, n // tn),
        in_specs=[
            pl.BlockSpec((mp, d), lambda l, j: (0, 0)),
            pl.BlockSpec((1, d, tn), lambda l, j: (l, 0, j)),
            pl.BlockSpec((1, 1, tn), lambda l, j: (l, 0, j)),
        ],
        out_specs=pl.BlockSpec((1, mp, tn), lambda l, j: (l, 0, j)),
        compiler_params=_params(("arbitrary", "arbitrary")),
        name="adaln_mod",
    )(c_all, w, b)


def _conformer_kernel(x_ref, mod_ref, hist_ref, nrm_ref, w1_ref, b1_ref, dw_ref, dwb_ref, lng_ref, lnb_ref,
                      w2_ref, b2_ref, o_ref, st_ref, h_ref, h2_ref, u_ref, full_ref, y_ref, *, tm, rd, per_row):
    d = D_MODEL
    hh = (CONV_A_WIDTH - 1) * rd
    hp = _round_up(hh, SUBLANES)
    t = pl.program_id(1)

    nblk = d // LANES
    lanes = lambda blk: slice(blk * LANES, (blk + 1) * LANES)

    @pl.when(t == 0)
    def _():
        full_ref[:, 0:hp, :] = jnp.zeros((nblk, hp, LANES), F32)
        for blk in range(nblk):
            full_ref[blk, hp - hh:hp, :] = hist_ref[0, :, lanes(blk)]

    rb = min(tm, CONF_ROWS)
    rg, cg = 16, 512
    rcv = 64
    offsets = [hp - hh + k * rd for k in range(CONV_A_WIDTH)]
    for rs in range(0, tm, rb):
        _rms_mod_rows(x_ref, (rs, rs + rb), [(nrm_ref, mod_ref, 0, 1, h_ref)], per_row, straight=True)
        u_ref[rs:rs + rb, :] = _dot(h_ref[rs:rs + rb, :], w1_ref[...])

        for r0 in range(rs, rs + rb, rg):
            for c0 in range(0, d, cg):
                a = u_ref[r0:r0 + rg, c0:c0 + cg] + b1_ref[:, c0:c0 + cg]
                g = u_ref[r0:r0 + rg, d + c0:d + c0 + cg] + b1_ref[:, d + c0:d + c0 + cg]
                glu = a * _sigmoid(g)
                for j in range(cg // LANES):
                    full_ref[c0 // LANES + j, hp + r0:hp + r0 + rg, :] = glu[:, lanes(j)]

        for r0 in range(rs, rs + rb, rcv):
            for blk in range(nblk):
                y_ref[r0:r0 + rcv, lanes(blk)] = (_tap_conv(full_ref, blk, dw_ref, offsets, r0, rcv)
                                                  + dwb_ref[:, lanes(blk)])

        for r0 in range(rs, rs + rb, BF16_ROWS):
            y = y_ref[r0:r0 + BF16_ROWS, :]
            mu = jnp.mean(y, axis=-1, keepdims=True)
            dv = y - mu
            var = jnp.mean(dv * dv, axis=-1, keepdims=True)
            yn = dv * lax.rsqrt(var + LN_EPS) * lng_ref[...] + lnb_ref[...]
            h2_ref[r0:r0 + BF16_ROWS, :] = (yn * _sigmoid(yn)).astype(BF)

        out = _dot(h2_ref[rs:rs + rb, :], w2_ref[...]) + b2_ref[...]
        gate = mod_ref[0, 2, rs:rs + rb, :] if per_row else mod_ref[0, 2]
        o_ref[0, rs:rs + rb, :] = x_ref[0, rs:rs + rb, :] + gate * out

    for blk in range(nblk):
        new_hist = full_ref[blk, hp + tm - hh:hp + tm, :]
        st_ref[0, :, lanes(blk)] = new_hist
        full_ref[blk, hp - hh:hp, :] = new_hist


def _conformer_call(x, mods, hist, nrm, w1, b1, dw, dwb, lng, lnb, w2, b2, *, tm, rd):
    ns, s, d = x.shape
    r = mods.shape[2]
    hh = hist.shape[1]
    hp = _round_up(hh, SUBLANES)
    per_row = r > 1
    assert s % tm == 0 and (not per_row or (r == tm and s == tm))
    kern = functools.partial(_conformer_kernel, tm=tm, rd=rd, per_row=per_row)
    return pl.pallas_call(
        kern,
        out_shape=(jax.ShapeDtypeStruct((ns, s, d), F32), jax.ShapeDtypeStruct((ns, hh, d), F32)),
        grid=(ns, s // tm),
        in_specs=[
            pl.BlockSpec((1, tm, d), lambda n, t: (n, t, 0)),
            pl.BlockSpec((1, N_MOD, r, d), lambda n, t: (n, 0, 0, 0)),
            pl.BlockSpec((1, hh, d), lambda n, t: (n, 0, 0)),
            _const_spec((1, d)),
            _const_spec((d, 2 * d)),
            _const_spec((1, 2 * d)),
            _const_spec((CONV_A_WIDTH, d)),
            _const_spec((1, d)),
            _const_spec((1, d)),
            _const_spec((1, d)),
            _const_spec((d, d)),
            _const_spec((1, d)),
        ],
        out_specs=(
            pl.BlockSpec((1, tm, d), lambda n, t: (n, t, 0)),
            pl.BlockSpec((1, hh, d), lambda n, t: (n, 0, 0)),
        ),
        scratch_shapes=[
            pltpu.VMEM((tm, d), BF),
            pltpu.VMEM((tm, d), BF),
            pltpu.VMEM((tm, 2 * d), F32),
            pltpu.VMEM((d // LANES, hp + tm, LANES), F32),
            pltpu.VMEM((tm, d), F32),
        ],
        compiler_params=_params(("arbitrary", "arbitrary")),
        name="conformer_mixer",
    )(x, mods, hist, nrm, w1, b1, dw, dwb, lng, lnb, w2, b2)


def _ffn_kernel(x_ref, xnext_ref, mod_ref, hist_ref, nrm_ref, wu_ref, cw_ref, cb_ref, wd_ref, nf_ref, o_ref, st_ref,
                h_ref, ubuf_ref, carry_ref, act_ref, part_ref, *, tm, rd, per_row, final_norm):
    f = D_FF
    hh = (CONV_F_WIDTH - 1) * rd
    hp = _round_up(hh, SUBLANES)
    t = pl.program_id(1)

    cur = lax.rem(t, 2)
    norm_targets = [(nrm_ref, mod_ref, 3, 4, h_ref)]

    @pl.when(t == 0)
    def _():
        _rms_mod_rows(x_ref, (0, tm), norm_targets, per_row, slot=0)

    nblk = 2 * f // LANES
    lanes = lambda blk: slice(blk * LANES, (blk + 1) * LANES)

    @pl.when(t == 0)
    def _():
        carry_ref[...] = jnp.zeros((nblk, hp, LANES), F32)
        for blk in range(nblk):
            carry_ref[blk, hp - hh:hp, :] = hist_ref[0, :, lanes(blk)]

    ubuf_ref[:, 0:hp, :] = carry_ref[...]

    rc = 32
    offsets = [hp - hh + k * rd for k in range(CONV_F_WIDTH)]
    for c0 in range(0, f, FF_SUB):
        for half in range(2):
            col = half * f + c0
            u = _dot(h_ref[cur], wu_ref[:, col:col + FF_SUB])
            for j in range(FF_SUB // LANES):
                ubuf_ref[col // LANES + j, hp:hp + tm, :] = u[:, lanes(j)]
        for r0 in range(0, tm, rc):
            for cc0 in range(c0, c0 + FF_SUB, LANES):
                ys = []
                for half in range(2):
                    blk = (half * f + cc0) // LANES
                    ys.append(_tap_conv(ubuf_ref, blk, cw_ref, offsets, r0, rc) + cb_ref[:, lanes(blk)])
                yg, yv = ys
                act_ref[r0:r0 + rc, cc0:cc0 + LANES] = (yg * _sigmoid(yg) * yv).astype(BF)

    for blk in range(nblk):
        st_ref[0, :, lanes(blk)] = ubuf_ref[blk, hp + tm - hh:hp + tm, :]
    carry_ref[...] = ubuf_ref[:, tm:tm + hp, :]

    _rms_mod_rows(xnext_ref, (0, tm), norm_targets, per_row, slot=1 - cur, straight=True)
    xo = x_ref[0] + mod_ref[0, 5] * _dot(act_ref[...], wd_ref[...])
    if not final_norm:
        o_ref[0] = xo
        return
    part_ref[...] = xo
    rows = BF16_ROWS

    def norm_body(i, c):
        r0 = pl.multiple_of(i * rows, rows)
        v = part_ref[pl.ds(r0, rows), :]
        o_ref[0, pl.ds(r0, rows), :] = v * lax.rsqrt(jnp.mean(v * v, axis=-1, keepdims=True) + EPS) * nf_ref[...]
        return c

    lax.fori_loop(0, tm // rows, norm_body, 0, unroll=ROW_UNROLL)


def _ffn_call(x, mods, hist, nrm, wup, cw, cb, wdown, nf, *, tm, rd, final_norm):
    ns, s, d = x.shape
    r = mods.shape[2]
    hh = hist.shape[1]
    hp = _round_up(hh, SUBLANES)
    f = wdown.shape[0]
    per_row = r > 1
    nt = s // tm
    assert f == D_FF and f % FF_SUB == 0 and s % tm == 0 and (not per_row or (r == tm and s == tm))
    kern = functools.partial(_ffn_kernel, tm=tm, rd=rd, per_row=per_row, final_norm=final_norm)
    return pl.pallas_call(
        kern,
        out_shape=(jax.ShapeDtypeStruct((ns, s, d), F32), jax.ShapeDtypeStruct((ns, hh, 2 * f), F32)),
        grid=(ns, nt),
        in_specs=[
            pl.BlockSpec((1, tm, d), lambda n, t: (n, t, 0)),
            pl.BlockSpec((1, tm, d), lambda n, t: (n, jnp.minimum(t + 1, nt - 1), 0)),
            pl.BlockSpec((1, N_MOD, r, d), lambda n, t: (n, 0, 0, 0)),
            pl.BlockSpec((1, hh, 2 * f), lambda n, t: (n, 0, 0)),
            _const_spec((1, d)),
            _const_spec((d, 2 * f)),
            _const_spec((CONV_F_WIDTH, 2 * f)),
            _const_spec((1, 2 * f)),
            _const_spec((f, d)),
            _const_spec((1, d)),
        ],
        out_specs=(
            pl.BlockSpec((1, tm, d), lambda n, t: (n, t, 0)),
            pl.BlockSpec((1, hh, 2 * f), lambda n, t: (n, 0, 0)),
        ),
        scratch_shapes=[
            pltpu.VMEM((2, tm, d), BF),
            pltpu.VMEM((2 * f // LANES, hp + tm, LANES), F32),
            pltpu.VMEM((2 * f // LANES, hp, LANES), F32),
            pltpu.VMEM((tm, f), BF),
            pltpu.VMEM((tm, d), F32),
        ],
        compiler_params=_params(("arbitrary", "arbitrary")),
        name="conv_ffn",
    )(x, x, mods, hist, nrm, wup, cw, cb, wdown, nf)


def _qkv_kernel(x_ref, xnext_ref, mod_ref, modkv_ref, nq_ref, nkv_ref, wq_ref, wkv_ref, q_ref, kv_ref, kvtail_ref,
                hq_ref, hkv_ref, *, tm, per_row):
    t = pl.program_id(1)
    cur = lax.rem(t, 2)
    norm_targets = [(nq_ref, mod_ref, 0, 1, hq_ref), (nkv_ref, modkv_ref, 0, 1, hkv_ref)]

    @pl.when(t == 0)
    def _():
        _rms_mod_rows(x_ref, (0, tm), norm_targets, per_row, slot=0)

    q_ref[0] = _dot(hq_ref[cur], wq_ref[...])
    _rms_mod_rows(xnext_ref, (0, tm), norm_targets, per_row, slot=1 - cur, straight=True)
    kv = _dot(hkv_ref[cur], wkv_ref[...])
    kv_ref[0] = kv
    kvtail_ref[0] = kv


def _qkv_call(x, mods, modkv, nq, nkv, wq, wkv, *, tm, tail_rows):
    ns, s, d = x.shape
    r = mods.shape[2]
    per_row = r > 1
    nt = s // tm
    tail_tiles = -(-tail_rows // tm)
    assert tail_tiles <= nt and tail_rows % tm == 0
    kern = functools.partial(_qkv_kernel, tm=tm, per_row=per_row)
    return pl.pallas_call(
        kern,
        out_shape=(jax.ShapeDtypeStruct((ns, s, ATT_WIDTH), F32), jax.ShapeDtypeStruct((ns, s, 2 * ATT_WIDTH), F32),
                   jax.ShapeDtypeStruct((ns, tail_rows, 2 * ATT_WIDTH), F32)),
        grid=(ns, nt),
        in_specs=[
            pl.BlockSpec((1, tm, d), lambda n, t: (n, t, 0)),
            pl.BlockSpec((1, tm, d), lambda n, t: (n, jnp.minimum(t + 1, nt - 1), 0)),
            pl.BlockSpec((1, N_MOD, r, d), lambda n, t: (n, 0, 0, 0)),
            pl.BlockSpec((1, 2, r, d), lambda n, t: (n, 0, 0, 0)),
            _const_spec((1, d)),
            _const_spec((1, d)),
            _const_spec((d, ATT_WIDTH)),
            _const_spec((d, 2 * ATT_WIDTH)),
        ],
        out_specs=(
            pl.BlockSpec((1, tm, ATT_WIDTH), lambda n, t: (n, t, 0)),
            pl.BlockSpec((1, tm, 2 * ATT_WIDTH), lambda n, t: (n, t, 0)),
            pl.BlockSpec((1, tm, 2 * ATT_WIDTH), lambda n, t: (n, jnp.maximum(t - (nt - tail_tiles), 0), 0)),
        ),
        scratch_shapes=[pltpu.VMEM((2, tm, d), BF), pltpu.VMEM((2, tm, d), BF)],
        compiler_params=_params(("arbitrary", "arbitrary")),
        name="qkv_proj",
    )(x, x, mods, modkv, nq, nkv, wq, wkv)


def _rel_bucket(dist):
    max_exact = N_BUCKETS // 2
    dd = jnp.maximum(dist, 1).astype(F32)
    large = max_exact + (jnp.log(dd / max_exact) / math.log(MAX_DISTANCE / max_exact)
                         * (N_BUCKETS - max_exact)).astype(jnp.int32)
    large = jnp.minimum(large, N_BUCKETS - 1)
    return jnp.where(dist < max_exact, dist, large)


def _prompt_bucket_table(dil):
    qi = jnp.arange(SPAN, dtype=jnp.int32)[:, None]
    ki = jnp.arange(2 * SPAN, dtype=jnp.int32)[None, :]
    m = qi + SPAN - ki
    valid = (m >= 0) & (m <= SPAN)
    return jnp.where(valid, _rel_bucket(jnp.clip(m, 0, SPAN) * dil), -1).astype(jnp.int32)


def _sample_bucket_tables(dil, width, t_new):
    row = jnp.arange(SAMPLE_QROWS, dtype=jnp.int32)[:, None]
    t = (row % SUBLANES) % t_new
    tn = jnp.arange(SUBLANES, dtype=jnp.int32)[None, :]

    def table(dist, ok):
        ok = ok & (dist >= 0) & (dist % dil == 0) & (dist // dil <= SPAN)
        return jnp.where(ok, _rel_bucket(jnp.clip(dist, 0, SPAN * dil)), -1).astype(jnp.int32)

    old = table(width + t - jnp.arange(width, dtype=jnp.int32)[None, :], True)
    new = table(t - tn, tn < t_new)
    return old, new


def _bias_from_buckets(bkt, relb_ref, head):
    acc = jnp.full(bkt.shape, NEG_INF, F32)
    for b in range(N_BUCKETS):
        acc = jnp.where(bkt == b, relb_ref[b, head], acc)
    return acc


def _pattn_kernel(relb_ref, bkt_ref, q0_ref, q1_ref, k0_ref, k1_ref, v0_ref, v1_ref,
                  o0_ref, o1_ref, l0_ref, l1_ref, bias_ref, carry_ref, *, g, dil, nbk):
    b = pl.program_id(0)
    i = pl.program_id(1)
    hpg = HEADS_PER_GROUP
    span_rows = SPAN * dil

    @pl.when((b == 0) & (i == 0))
    def _():
        bkt = bkt_ref[...]
        no_prev = lax.broadcasted_iota(jnp.int32, (SPAN, 2 * SPAN), 1) < SPAN
        for h in range(hpg):
            bias = _bias_from_buckets(bkt, relb_ref, g * hpg + h)
            bias_ref[0, h] = bias
            bias_ref[1, h] = jnp.where(no_prev, NEG_INF, bias)

    @pl.when(i == 0)
    def _():
        carry_ref[...] = jnp.zeros(carry_ref.shape, BF)

    lanehead = lax.broadcasted_iota(jnp.int32, (SPAN, GROUP_WIDTH), 1) >> HEAD_SHIFT
    first = jnp.where(i == 0, 1, 0)
    rd_slot = lax.rem(i, 2)
    last_k, last_v = {}, {}

    def block(j, r):
        rows = pl.ds(j * span_rows + r, SPAN, stride=dil)
        cur = lambda r0, r1: jnp.concatenate([r0[0, rows, :], r1[0, rows, :]], axis=1)
        if j == 0:
            kprev, vprev = carry_ref[rd_slot, 0, r], carry_ref[rd_slot, 1, r]
        else:
            kprev, vprev = last_k[r], last_v[r]
        kcur = cur(k0_ref, k1_ref).astype(BF)
        vcur = cur(v0_ref, v1_ref).astype(BF)
        if j == nbk - 1:
            carry_ref[1 - rd_slot, 0, r] = kcur
            carry_ref[1 - rd_slot, 1, r] = vcur
        else:
            last_k[r], last_v[r] = kcur, vcur
        q = cur(q0_ref, q1_ref) * (HEAD_DIM ** -0.5)
        lhs = jnp.concatenate([jnp.where(lanehead == h, q, 0.0).astype(BF) for h in range(hpg)], axis=0)
        s = _dot_nt(lhs, jnp.concatenate([kprev, kcur], axis=0))

        ps, ms, sums = [], [], []
        for h in range(hpg):
            logit = s[h * SPAN:(h + 1) * SPAN] + (bias_ref[first, h] if j == 0 else bias_ref[0, h])
            m = jnp.max(logit, axis=-1, keepdims=True)
            e = jnp.exp(logit - m)
            ps.append(e.astype(BF))
            ms.append(m)
            sums.append(jnp.sum(e, axis=-1, keepdims=True))
        pv = _dot(jnp.concatenate(ps, axis=0), jnp.concatenate([vprev, vcur], axis=0))

        o = jnp.zeros((SPAN, GROUP_WIDTH), F32)
        l = jnp.zeros((SPAN, GROUP_WIDTH), F32)
        for h in range(hpg):
            o = jnp.where(lanehead == h, pv[h * SPAN:(h + 1) * SPAN] * (1.0 / sums[h]), o)
            l = jnp.where(lanehead == h, ms[h] + jnp.log(sums[h]), l)
        o0_ref[0, rows, :] = o[:, :LANES]
        o1_ref[0, rows, :] = o[:, LANES:]
        l0_ref[0, rows, :] = l[:, :LANES]
        l1_ref[0, rows, :] = l[:, LANES:]

    for j in range(nbk):
        for r in range(dil):
            block(j, r)


def _pattn_call(q, kv, rel_bias, g, dil):
    bsz, s, _ = q.shape
    nbk = max(1, PATTN_ROWS // (dil * SPAN))
    rows = nbk * dil * SPAN
    assert s % rows == 0 and GROUP_WIDTH == 2 * LANES
    blk = (1, rows, LANES)
    kcol = 2 * g
    vcol = ATT_WIDTH // LANES + 2 * g
    out = jax.ShapeDtypeStruct((bsz, s, LANES), F32)
    ospec = pl.BlockSpec(blk, lambda b, i: (b, i, 0))
    res = pl.pallas_call(
        functools.partial(_pattn_kernel, g=g, dil=dil, nbk=nbk),
        out_shape=(out,) * 4,
        grid=(bsz, s // rows),
        in_specs=[
            pl.BlockSpec(memory_space=pltpu.SMEM),
            _const_spec((SPAN, 2 * SPAN)),
            pl.BlockSpec(blk, lambda b, i: (b, i, kcol)),
            pl.BlockSpec(blk, lambda b, i: (b, i, kcol + 1)),
            pl.BlockSpec(blk, lambda b, i: (b, i, kcol)),
            pl.BlockSpec(blk, lambda b, i: (b, i, kcol + 1)),
            pl.BlockSpec(blk, lambda b, i: (b, i, vcol)),
            pl.BlockSpec(blk, lambda b, i: (b, i, vcol + 1)),
        ],
        out_specs=(ospec,) * 4,
        scratch_shapes=[pltpu.VMEM((2, HEADS_PER_GROUP, SPAN, 2 * SPAN), F32),
                        pltpu.VMEM((2, 2, dil, SPAN, GROUP_WIDTH), BF)],
        compiler_params=_params(("arbitrary", "arbitrary")),
        name=f"prompt_attn_g{g}",
    )(rel_bias, _prompt_bucket_table(dil), q, q, kv, kv, kv, kv)
    return list(res[:2]), list(res[2:])


def _sattn_kernel(relb_ref, bo0_ref, bo1_ref, bo2_ref, bn_ref, q_ref, kvn_ref, kvt_ref, c0_ref, c1_ref, c2_ref,
                  o0_ref, o1_ref, o2_ref, l0_ref, l1_ref, l2_ref, n0_ref, n1_ref, n2_ref,
                  bias0_ref, bias1_ref, bias2_ref, biasn_ref, tail_ref, *, t_new):
    hpg = HEADS_PER_GROUP
    gw = GROUP_WIDTH
    rows = SAMPLE_QROWS
    bo_refs = (bo0_ref, bo1_ref, bo2_ref)
    bias_refs = (bias0_ref, bias1_ref, bias2_ref)
    caches = (c0_ref, c1_ref, c2_ref)
    outs = ((o0_ref, l0_ref, n0_ref), (o1_ref, l1_ref, n1_ref), (o2_ref, l2_ref, n2_ref))

    @pl.when(pl.program_id(0) == 0)
    def _():
        tail_ref[...] = jnp.zeros(tail_ref.shape, F32)
        for g in range(N_GROUPS):
            for h in range(hpg):
                sl = slice(h * SUBLANES, (h + 1) * SUBLANES)
                bias_refs[g][sl, :] = _bias_from_buckets(bo_refs[g][sl, :], relb_ref, g * hpg + h)
                biasn_ref[g, sl, :] = _bias_from_buckets(bn_ref[g, sl, :], relb_ref, g * hpg + h)

    lanehead = lax.broadcasted_iota(jnp.int32, (rows, gw), 1) >> HEAD_SHIFT
    rowhead = lax.broadcasted_iota(jnp.int32, (rows, gw), 0) >> SUBLANE_SHIFT
    own = lanehead == rowhead
    tail_lane = lax.broadcasted_iota(jnp.int32, (gw, LANES), 1) >= LANES - t_new

    for g in range(N_GROUPS):
        cref = caches[g]
        o_ref, l_ref, n_ref = outs[g]
        width = cref.shape[-1]
        qm = jnp.where(own, q_ref[0, :, g * gw:(g + 1) * gw] * (HEAD_DIM ** -0.5), 0.0).astype(BF)
        knew = kvn_ref[0, :, g * gw:(g + 1) * gw].astype(BF)
        vnew = kvn_ref[0, :, ATT_WIDTH + g * gw:ATT_WIDTH + (g + 1) * gw].astype(BF)
        lo = _dot(qm, cref[0, 0].astype(BF)) + bias_refs[g][...]
        ln = _dot_nt(qm, knew) + biasn_ref[g]
        m = jnp.maximum(jnp.max(lo, axis=-1, keepdims=True), jnp.max(ln, axis=-1, keepdims=True))
        eo = jnp.exp(lo - m)
        en = jnp.exp(ln - m)
        ssum = jnp.sum(eo, axis=-1, keepdims=True) + jnp.sum(en, axis=-1, keepdims=True)
        pv = _dot_nt(eo.astype(BF), cref[0, 1].astype(BF)) + _dot(en.astype(BF), vnew)
        om = jnp.where(own, pv * (1.0 / ssum), 0.0)
        lm = jnp.where(own, m + jnp.log(ssum), 0.0)
        o_acc = om[0:SUBLANES]
        l_acc = lm[0:SUBLANES]
        for h in range(1, hpg):
            o_acc = o_acc + om[h * SUBLANES:(h + 1) * SUBLANES]
            l_acc = l_acc + lm[h * SUBLANES:(h + 1) * SUBLANES]
        o_ref[0] = o_acc
        l_ref[0] = l_acc

        for kv in range(2):
            rolled = pltpu.roll(cref[0, kv], width - t_new, 1)
            tail_ref[:, 0:SUBLANES] = kvt_ref[0, kv * ATT_WIDTH + g * gw:kv * ATT_WIDTH + (g + 1) * gw, :]
            tail = pltpu.roll(tail_ref[...], LANES - t_new, 1)
            if width > LANES:
                n_ref[0, kv, :, 0:width - LANES] = rolled[:, 0:width - LANES]
            n_ref[0, kv, :, width - LANES:width] = jnp.where(tail_lane, tail, rolled[:, width - LANES:width])


def _sattn_call(q, kvn, caches, rel_bias):
    n, t_new, _ = q.shape
    gw = GROUP_WIDTH
    hpg = HEADS_PER_GROUP
    assert t_new <= SUBLANES
    views, bkt_old, bkt_new = [], [], []
    for g, (window, dil) in enumerate(DIL_GROUPS):
        width = caches[g].shape[1]
        assert width == window and width % LANES == 0
        views.append(jnp.transpose(caches[g], (0, 2, 3, 4, 1)).reshape(n, 2, gw, width))
        old, new = _sample_bucket_tables(dil, width, t_new)
        bkt_old.append(old)
        bkt_new.append(new)
    pad = SUBLANES - t_new
    qp = jnp.tile(jnp.pad(q, ((0, 0), (0, pad), (0, 0))), (1, hpg, 1))
    kvp = jnp.pad(kvn, ((0, 0), (0, pad), (0, 0)))
    kvt = jnp.swapaxes(kvp, 1, 2)
    out = jax.ShapeDtypeStruct((n, SUBLANES, gw), F32)
    oblk = pl.BlockSpec((1, SUBLANES, gw), lambda i: (i, 0, 0))
    cspecs = [pl.BlockSpec((1, 2, gw, v.shape[-1]), lambda i: (i, 0, 0, 0)) for v in views]
    res = pl.pallas_call(
        functools.partial(_sattn_kernel, t_new=t_new),
        out_shape=(out,) * 6 + tuple(jax.ShapeDtypeStruct(v.shape, F32) for v in views),
        grid=(n,),
        in_specs=[pl.BlockSpec(memory_space=pltpu.SMEM)]
        + [_const_spec(b.shape) for b in bkt_old]
        + [
            _const_spec((N_GROUPS, SAMPLE_QROWS, SUBLANES)),
            pl.BlockSpec((1, SAMPLE_QROWS, ATT_WIDTH), lambda i: (i, 0, 0)),
            pl.BlockSpec((1, SUBLANES, 2 * ATT_WIDTH), lambda i: (i, 0, 0)),
            pl.BlockSpec((1, 2 * ATT_WIDTH, SUBLANES), lambda i: (i, 0, 0)),
        ]
        + cspecs,
        out_specs=(oblk,) * 6 + tuple(cspecs),
        scratch_shapes=[pltpu.VMEM(b.shape, F32) for b in bkt_old]
        + [pltpu.VMEM((N_GROUPS, SAMPLE_QROWS, SUBLANES), F32), pltpu.VMEM((gw, LANES), F32)],
        compiler_params=_params(("arbitrary",)),
        name="sample_attn",
    )(rel_bias, *bkt_old, jnp.stack(bkt_new), qp, kvp, kvt, *views)
    new_caches = [jnp.transpose(c.reshape(n, 2, hpg, HEAD_DIM, c.shape[-1]), (0, 4, 1, 2, 3)) for c in res[6:]]
    return [a[:, :t_new] for a in res[:3]], [a[:, :t_new] for a in res[3:6]], new_caches


def _oproj_kernel(x_ref, mod_ref, *refs, tm):
    nblk = ATT_WIDTH // LANES
    halves = nblk // N_GROUPS
    o_refs, l_refs = refs[:nblk], refs[nblk:2 * nblk]
    wo_ref, out_ref, att_ref = refs[2 * nblk:]
    rc = BF16_ROWS

    def body(i, c):
        r0 = pl.multiple_of(i * rc, rc)
        for hf in range(halves):
            blks = [g * halves + hf for g in range(N_GROUPS)]
            ls = [l_refs[k][0, pl.ds(r0, rc), :] for k in blks]
            mx = functools.reduce(jnp.maximum, ls)
            es = [jnp.exp(v - mx) for v in ls]
            inv = 1.0 / functools.reduce(lambda a, b: a + b, es)
            for k, e in zip(blks, es):
                att_ref[pl.ds(r0, rc), k * LANES:(k + 1) * LANES] = (o_refs[k][0, pl.ds(r0, rc), :] * (e * inv)).astype(BF)
        return c

    lax.fori_loop(0, tm // rc, body, 0, unroll=2)
    out_ref[0] = x_ref[0] + mod_ref[0, 2] * _dot(att_ref[...], wo_ref[...])


def _oproj_call(x, mods, os_, ls_, wo, *, tm):
    ns, s, d = x.shape
    r = mods.shape[2]
    kern = functools.partial(_oproj_kernel, tm=tm)
    gspec = pl.BlockSpec((1, tm, LANES), lambda n, t: (n, t, 0))
    return pl.pallas_call(
        kern,
        out_shape=jax.ShapeDtypeStruct((ns, s, d), F32),
        grid=(ns, s // tm),
        in_specs=[
            pl.BlockSpec((1, tm, d), lambda n, t: (n, t, 0)),
            pl.BlockSpec((1, N_MOD, r, d), lambda n, t: (n, 0, 0, 0)),
        ] + [gspec] * (2 * ATT_WIDTH // LANES) + [_const_spec((ATT_WIDTH, d))],
        out_specs=pl.BlockSpec((1, tm, d), lambda n, t: (n, t, 0)),
        scratch_shapes=[pltpu.VMEM((tm, ATT_WIDTH), BF)],
        compiler_params=_params(("arbitrary", "arbitrary")),
        name="attn_out_proj",
    )(x, mods, *os_, *ls_, wo)


def _trunk(x, mods, modkv, hist_a, hist_f, w, attn_fn, *, tm, rd, kv_tail_rows):
    row = lambda v: v.reshape(1, -1)
    x, st_a = _conformer_call(x, mods[0], hist_a, row(w["norm_mix"][0]), w["a_w1"], row(w["a_b1"][0]),
                              w["a_dw"][0], row(w["a_dwb"][0]), row(w["a_ln_g"][0]), row(w["a_ln_b"][0]),
                              w["a_w2"], row(w["a_b2"][0]), tm=tm, rd=rd)
    x, st_f0 = _ffn_call(x, mods[0], hist_f[0], row(w["norm_ffn"][0]), w["f_wup"][0], w["f_cw"][0],
                         row(w["f_cb"][0]), w["f_wdown"][0], row(w["norm_f"]), tm=tm, rd=rd, final_norm=False)
    q, kv, kv_tail = _qkv_call(x, mods[1], modkv, row(w["norm_mix"][1]), row(w["norm_kv"]), w["w_q"], w["w_kv"],
                               tm=tm, tail_rows=kv_tail_rows)
    os_, ls_, attn_extra = attn_fn(q, kv)
    x = _oproj_call(x, mods[1], os_, ls_, w["w_o"], tm=tm)
    y, st_f1 = _ffn_call(x, mods[1], hist_f[1], row(w["norm_ffn"][1]), w["f_wup"][1], w["f_cw"][1],
                         row(w["f_cb"][1]), w["f_wdown"][1], row(w["norm_f"]), tm=tm, rd=rd, final_norm=True)
    return y, kv_tail, st_a, [st_f0, st_f1], attn_extra


def kernel(x_prompt, x_sample, cache_kv_w128, cache_kv_w512, cache_kv_w2048, state_conv_a, state_conv_ffn, c_prompt, c_sample, w_mod, b_mod, norm_mix, norm_ffn, a_w1, a_b1, a_dw, a_dwb, a_ln_g, a_ln_b, a_w2, a_b2, w_mod_kv, b_mod_kv, norm_kv, w_kv, w_q, w_o, rel_bias, f_wup, f_cw, f_cb, f_wdown, norm_f):
    d = D_MODEL
    bsz, seq, _ = x_prompt.shape
    nseq, t_new, _ = x_sample.shape
    caches = (cache_kv_w128, cache_kv_w512, cache_kv_w2048)
    hpg = HEADS_PER_GROUP

    w = dict(norm_mix=norm_mix, norm_ffn=norm_ffn, a_w1=a_w1[0].astype(BF), a_b1=a_b1, a_dw=a_dw, a_dwb=a_dwb,
             a_ln_g=a_ln_g, a_ln_b=a_ln_b, a_w2=a_w2[0].astype(BF), a_b2=a_b2, norm_kv=norm_kv,
             w_kv=w_kv.astype(BF), w_q=w_q[0].astype(BF), w_o=w_o[0].astype(BF),
             f_wup=[f_wup[l].astype(BF) for l in range(DEPTH)], f_cw=f_cw, f_cb=f_cb,
             f_wdown=[f_wdown[l].astype(BF) for l in range(DEPTH)], norm_f=norm_f)

    n_c = bsz + nseq
    mp = _round_up(n_c, SUBLANES)
    c_all = jnp.pad(jnp.concatenate([c_prompt, c_sample], axis=0), ((0, mp - n_c), (0, 0)))
    mod = _mod_call(c_all, w_mod, b_mod.reshape(DEPTH, 1, N_MOD * d), tn=1536)
    modkv = _mod_call(c_all, w_mod_kv[None], b_mod_kv.reshape(1, 1, 2 * d), tn=1024)[0]

    mods_p = [mod[l, :bsz].reshape(bsz, N_MOD, 1, d) for l in range(DEPTH)]
    modkv_p = modkv[:bsz].reshape(bsz, 2, 1, d)
    hist_a_p = jnp.zeros((bsz, CONV_A_WIDTH - 1, d), F32)
    hist_f_p = [jnp.zeros((bsz, CONV_F_WIDTH - 1, 2 * D_FF), F32)] * DEPTH

    def prompt_attn(q, kv):
        res = [_pattn_call(q, kv, rel_bias, g, dil) for g, (_, dil) in enumerate(DIL_GROUPS)]
        return [a for r in res for a in r[0]], [a for r in res for a in r[1]], None

    wmax = max(c.shape[1] for c in caches)
    assert seq >= wmax
    y_p, kv_tail, st_a_p, st_f_p, _ = _trunk(x_prompt, mods_p, modkv_p, hist_a_p, hist_f_p, w, prompt_attn,
                                             tm=TM_PROMPT, rd=1, kv_tail_rows=wmax)
    kv_tail = kv_tail.reshape(bsz, wmax, 2, N_HEADS, HEAD_DIM)
    kv_bufs_p = [kv_tail[:, wmax - c.shape[1]:, :, g * hpg:(g + 1) * hpg] for g, c in enumerate(caches)]
    conv_a_p = st_a_p[None]
    conv_f_p = jnp.stack(st_f_p)

    rows = t_new * nseq
    tmaj = lambda v: jnp.swapaxes(v, 0, 1).reshape(1, -1, v.shape[-1])
    smaj = lambda v, c: jnp.swapaxes(v.reshape(-1, nseq, c), 0, 1)

    def per_row(m, k):
        return jnp.tile(jnp.swapaxes(m.reshape(nseq, k, d), 0, 1), (1, t_new, 1))[None]

    mods_s = [per_row(mod[l, bsz:n_c], N_MOD) for l in range(DEPTH)]
    modkv_s = per_row(modkv[bsz:n_c], 2)
    hist_a_s = tmaj(state_conv_a[0])
    hist_f_s = [tmaj(state_conv_ffn[l]) for l in range(DEPTH)]

    def sample_attn(q, kv):
        os_, ls_, new_caches = _sattn_call(smaj(q, ATT_WIDTH), smaj(kv, 2 * ATT_WIDTH), caches, rel_bias)
        halves = lambda vs: [tmaj(a[..., c0:c0 + LANES]) for a in vs for c0 in range(0, GROUP_WIDTH, LANES)]
        return halves(os_), halves(ls_), new_caches

    y_s, _, st_a_s, st_f_s, kv_bufs_s = _trunk(tmaj(x_sample), mods_s, modkv_s, hist_a_s, hist_f_s, w,
                                               sample_attn, tm=rows, rd=nseq, kv_tail_rows=rows)
    y_s = smaj(y_s, d)
    conv_a_s = smaj(st_a_s, d)[None]
    conv_f_s = jnp.stack([smaj(s, 2 * D_FF) for s in st_f_s])

    return (y_p, y_s, kv_bufs_p[0], kv_bufs_p[1], kv_bufs_p[2], conv_a_p, conv_f_p,
            kv_bufs_s[0], kv_bufs_s[1], kv_bufs_s[2], conv_a_s, conv_f_s)
```

```python
import functools
import math

import jax
import jax.numpy as jnp
from jax import lax
from jax.experimental import pallas as pl
from jax.experimental.pallas import tpu as pltpu

D_MODEL = 1024
DEPTH = 2
HEAD_DIM = 64
HEADS_PER_GROUP = 4
DIL_GROUPS = ((128, 1), (512, 4), (2048, 16))
N_GROUPS = len(DIL_GROUPS)
N_HEADS = HEADS_PER_GROUP * N_GROUPS
ATT_WIDTH = N_HEADS * HEAD_DIM
GROUP_WIDTH = HEADS_PER_GROUP * HEAD_DIM
SPAN = 128
N_BUCKETS = 32
MAX_DISTANCE = 2048
CONV_A_WIDTH = 31
D_FF = 2816
CONV_F_WIDTH = 3
N_MOD = 6
EPS = 1e-6
LN_EPS = 1e-5
NEG_INF = -1e30

BF = jnp.bfloat16
F32 = jnp.float32

HEAD_SHIFT = HEAD_DIM.bit_length() - 1
SUBLANES = 8
SUBLANE_SHIFT = SUBLANES.bit_length() - 1
LANES = 128
BF16_ROWS = 16
ROW_UNROLL = 4
VMEM_LIMIT = 56 * 1024 * 1024

TM_PROMPT = 512
PATTN_ROWS = 512
CONF_ROWS = 512
FF_ROWS = 32
FF_SUB = 256
SAMPLE_QROWS = HEADS_PER_GROUP * SUBLANES


def _round_up(a, b):
    return -(-a // b) * b


def _params(sem):
    return pltpu.CompilerParams(dimension_semantics=sem, vmem_limit_bytes=VMEM_LIMIT)


def _const_spec(shape):
    nd = len(shape)
    return pl.BlockSpec(shape, lambda *_: (0,) * nd, pipeline_mode=pl.Buffered(1))


def _dot(a, b):
    return jnp.dot(a, b, preferred_element_type=F32)


def _dot_nt(a, b):
    return lax.dot_general(a, b, (((1,), (1,)), ((), ())), preferred_element_type=F32)


def _sigmoid(v):
    return 1.0 / (1.0 + jnp.exp(-v))


def _mod_rows(mod_ref, idx, r0, rows, per_row):
    if per_row:
        return mod_ref[0, idx, pl.ds(r0, rows), :]
    return mod_ref[0, idx]


def _rms_mod_rows(x_ref, rows, targets, per_row, *, slot=None, straight=False):
    rc = BF16_ROWS
    lo, hi = rows

    def chunk(r0):
        x = x_ref[0, pl.ds(r0, rc), :]
        xn = x * lax.rsqrt(jnp.mean(x * x, axis=-1, keepdims=True) + EPS)
        for gain_ref, mod_ref, i_sh, i_sc, dst_ref in targets:
            sh = _mod_rows(mod_ref, i_sh, r0, rc, per_row)
            sc = _mod_rows(mod_ref, i_sc, r0, rc, per_row)
            idx = (pl.ds(r0, rc), slice(None))
            dst_ref[idx if slot is None else (slot,) + idx] = ((xn * gain_ref[...]) * (1.0 + sc) + sh).astype(BF)

    if straight:
        for r0 in range(lo, hi, rc):
            chunk(r0)
        return

    def body(i, c):
        chunk(pl.multiple_of(lo + i * rc, rc))
        return c

    lax.fori_loop(0, (hi - lo) // rc, body, 0, unroll=ROW_UNROLL)


def _tap_conv(src_ref, blk, w_ref, offsets, r0, rows):
    cols = slice(blk * LANES, (blk + 1) * LANES)
    acc = None
    for k, off in enumerate(offsets):
        term = w_ref[k:k + 1, cols] * src_ref[blk, r0 + off:r0 + off + rows, :]
        acc = term if acc is None else acc + term
    return acc


def _mod_kernel(c_ref, w_ref, b_ref, o_ref):
    c = c_ref[...]
    a = (c * _sigmoid(c)).astype(BF)
    o_ref[0] = _dot(a, w_ref[0].astype(BF)) + b_ref[0]


def _mod_call(c_all, w, b, tn):
    nl, d, n = w.shape
    mp = c_all.shape[0]
    return pl.pallas_call(
        _mod_kernel,
        out_shape=jax.ShapeDtypeStruct((nl, mp, n), F32),
        grid=(nl, n // tn),
        in_specs=[
            pl.BlockSpec((mp, d), lambda l, j: (0, 0)),
            pl.BlockSpec((1, d, tn), lambda l, j: (l, 0, j)),
            pl.BlockSpec((1, 1, tn), lambda l, j: (l, 0, j)),
        ],
        out_specs=pl.BlockSpec((1, mp, tn), lambda l, j: (l, 0, j)),
        compiler_params=_params(("arbitrary", "arbitrary")),
        name="adaln_mod",
    )(c_all, w, b)


def _conformer_kernel(x_ref, mod_ref, hist_ref, nrm_ref, w1_ref, b1_ref, dw_ref, dwb_ref, lng_ref, lnb_ref,
                      w2_ref, b2_ref, o_ref, st_ref, h_ref, h2_ref, u_ref, full_ref, y_ref, *, tm, rd, per_row):
    d = D_MODEL
    hh = (CONV_A_WIDTH - 1) * rd
    hp = _round_up(hh, SUBLANES)
    t = pl.program_id(1)

    nblk = d // LANES
    lanes = lambda blk: slice(blk * LANES, (blk + 1) * LANES)

    @pl.when(t == 0)
    def _():
        full_ref[:, 0:hp, :] = jnp.zeros((nblk, hp, LANES), F32)
        for blk in range(nblk):
            full_ref[blk, hp - hh:hp, :] = hist_ref[0, :, lanes(blk)]

    rb = min(tm, CONF_ROWS)
    rg, cg = 16, 512
    rcv = 64
    offsets = [hp - hh + k * rd for k in range(CONV_A_WIDTH)]
    for rs in range(0, tm, rb):
        _rms_mod_rows(x_ref, (rs, rs + rb), [(nrm_ref, mod_ref, 0, 1, h_ref)], per_row, straight=True)
        u_ref[rs:rs + rb, :] = _dot(h_ref[rs:rs + rb, :], w1_ref[...])

        for r0 in range(rs, rs + rb, rg):
            for c0 in range(0, d, cg):
                a = u_ref[r0:r0 + rg, c0:c0 + cg] + b1_ref[:, c0:c0 + cg]
                g = u_ref[r0:r0 + rg, d + c0:d + c0 + cg] + b1_ref[:, d + c0:d + c0 + cg]
                glu = a * _sigmoid(g)
                for j in range(cg // LANES):
                    full_ref[c0 // LANES + j, hp + r0:hp + r0 + rg, :] = glu[:, lanes(j)]

        for r0 in range(rs, rs + rb, rcv):
            for blk in range(nblk):
                y_ref[r0:r0 + rcv, lanes(blk)] = (_tap_conv(full_ref, blk, dw_ref, offsets, r0, rcv)
                                                  + dwb_ref[:, lanes(blk)])

        for r0 in range(rs, rs + rb, BF16_ROWS):
            y = y_ref[r0:r0 + BF16_ROWS, :]
            mu = jnp.mean(y, axis=-1, keepdims=True)
            dv = y - mu
            var = jnp.mean(dv * dv, axis=-1, keepdims=True)
            yn = dv * lax.rsqrt(var + LN_EPS) * lng_ref[...] + lnb_ref[...]
            h2_ref[r0:r0 + BF16_ROWS, :] = (yn * _sigmoid(yn)).astype(BF)

        out = _dot(h2_ref[rs:rs + rb, :], w2_ref[...]) + b2_ref[...]
        gate = mod_ref[0, 2, rs:rs + rb, :] if per_row else mod_ref[0, 2]
        o_ref[0, rs:rs + rb, :] = x_ref[0, rs:rs + rb, :] + gate * out

    for blk in range(nblk):
        new_hist = full_ref[blk, hp + tm - hh:hp + tm, :]
        st_ref[0, :, lanes(blk)] = new_hist
        full_ref[blk, hp - hh:hp, :] = new_hist


def _conformer_call(x, mods, hist, nrm, w1, b1, dw, dwb, lng, lnb, w2, b2, *, tm, rd):
    ns, s, d = x.shape
    r = mods.shape[2]
    hh = hist.shape[1]
    hp = _round_up(hh, SUBLANES)
    per_row = r > 1
    assert s % tm == 0 and (not per_row or (r == tm and s == tm))
    kern = functools.partial(_conformer_kernel, tm=tm, rd=rd, per_row=per_row)
    return pl.pallas_call(
        kern,
        out_shape=(jax.ShapeDtypeStruct((ns, s, d), F32), jax.ShapeDtypeStruct((ns, hh, d), F32)),
        grid=(ns, s // tm),
        in_specs=[
            pl.BlockSpec((1, tm, d), lambda n, t: (n, t, 0)),
            pl.BlockSpec((1, N_MOD, r, d), lambda n, t: (n, 0, 0, 0)),
            pl.BlockSpec((1, hh, d), lambda n, t: (n, 0, 0)),
            _const_spec((1, d)),
            _const_spec((d, 2 * d)),
            _const_spec((1, 2 * d)),
            _const_spec((CONV_A_WIDTH, d)),
            _const_spec((1, d)),
            _const_spec((1, d)),
            _const_spec((1, d)),
            _const_spec((d, d)),
            _const_spec((1, d)),
        ],
        out_specs=(
            pl.BlockSpec((1, tm, d), lambda n, t: (n, t, 0)),
            pl.BlockSpec((1, hh, d), lambda n, t: (n, 0, 0)),
        ),
        scratch_shapes=[
            pltpu.VMEM((tm, d), BF),
            pltpu.VMEM((tm, d), BF),
            pltpu.VMEM((tm, 2 * d), F32),
            pltpu.VMEM((d // LANES, hp + tm, LANES), F32),
            pltpu.VMEM((tm, d), F32),
        ],
        compiler_params=_params(("arbitrary", "arbitrary")),
        name="conformer_mixer",
    )(x, mods, hist, nrm, w1, b1, dw, dwb, lng, lnb, w2, b2)


def _ffn_kernel(x_ref, xnext_ref, mod_ref, hist_ref, nrm_ref, wu_ref, cw_ref, cb_ref, wd_ref, nf_ref, o_ref, st_ref,
                h_ref, ubuf_ref, carry_ref, act_ref, part_ref, *, tm, rd, per_row, final_norm):
    f = D_FF
    hh = (CONV_F_WIDTH - 1) * rd
    hp = _round_up(hh, SUBLANES)
    t = pl.program_id(1)

    cur = lax.rem(t, 2)
    norm_targets = [(nrm_ref, mod_ref, 3, 4, h_ref)]

    @pl.when(t == 0)
    def _():
        _rms_mod_rows(x_ref, (0, tm), norm_targets, per_row, slot=0)

    nblk = 2 * f // LANES
    lanes = lambda blk: slice(blk * LANES, (blk + 1) * LANES)

    @pl.when(t == 0)
    def _():
        carry_ref[...] = jnp.zeros((nblk, hp, LANES), F32)
        for blk in range(nblk):
            carry_ref[blk, hp - hh:hp, :] = hist_ref[0, :, lanes(blk)]

    ubuf_ref[:, 0:hp, :] = carry_ref[...]

    rc = FF_ROWS
    offsets = [hp - hh + k * rd for k in range(CONV_F_WIDTH)]
    for c0 in range(0, f, FF_SUB):
        for half in range(2):
            col = half * f + c0
            u = _dot(h_ref[cur], wu_ref[:, col:col + FF_SUB])
            for j in range(FF_SUB // LANES):
                ubuf_ref[col // LANES + j, hp:hp + tm, :] = u[:, lanes(j)]
        for r0 in range(0, tm, rc):
            for cc0 in range(c0, c0 + FF_SUB, LANES):
                ys = []
                for half in range(2):
                    blk = (half * f + cc0) // LANES
                    ys.append(_tap_conv(ubuf_ref, blk, cw_ref, offsets, r0, rc) + cb_ref[:, lanes(blk)])
                yg, yv = ys
                act_ref[r0:r0 + rc, cc0:cc0 + LANES] = (yg * _sigmoid(yg) * yv).astype(BF)

    for blk in range(nblk):
        st_ref[0, :, lanes(blk)] = ubuf_ref[blk, hp + tm - hh:hp + tm, :]
    carry_ref[...] = ubuf_ref[:, tm:tm + hp, :]

    _rms_mod_rows(xnext_ref, (0, tm), norm_targets, per_row, slot=1 - cur, straight=True)
    xo = x_ref[0] + mod_ref[0, 5] * _dot(act_ref[...], wd_ref[...])
    if not final_norm:
        o_ref[0] = xo
        return
    part_ref[...] = xo
    rows = BF16_ROWS

    def norm_body(i, c):
        r0 = pl.multiple_of(i * rows, rows)
        v = part_ref[pl.ds(r0, rows), :]
        o_ref[0, pl.ds(r0, rows), :] = v * lax.rsqrt(jnp.mean(v * v, axis=-1, keepdims=True) + EPS) * nf_ref[...]
        return c

    lax.fori_loop(0, tm // rows, norm_body, 0, unroll=ROW_UNROLL)


def _ffn_call(x, mods, hist, nrm, wup, cw, cb, wdown, nf, *, tm, rd, final_norm):
    ns, s, d = x.shape
    r = mods.shape[2]
    hh = hist.shape[1]
    hp = _round_up(hh, SUBLANES)
    f = wdown.shape[0]
    per_row = r > 1
    nt = s // tm
    assert f == D_FF and f % FF_SUB == 0 and s % tm == 0 and (not per_row or (r == tm and s == tm))
    kern = functools.partial(_ffn_kernel, tm=tm, rd=rd, per_row=per_row, final_norm=final_norm)
    return pl.pallas_call(
        kern,
        out_shape=(jax.ShapeDtypeStruct((ns, s, d), F32), jax.ShapeDtypeStruct((ns, hh, 2 * f), F32)),
        grid=(ns, nt),
        in_specs=[
            pl.BlockSpec((1, tm, d), lambda n, t: (n, t, 0)),
            pl.BlockSpec((1, tm, d), lambda n, t: (n, jnp.minimum(t + 1, nt - 1), 0)),
            pl.BlockSpec((1, N_MOD, r, d), lambda n, t: (n, 0, 0, 0)),
            pl.BlockSpec((1, hh, 2 * f), lambda n, t: (n, 0, 0)),
            _const_spec((1, d)),
            _const_spec((d, 2 * f)),
            _const_spec((CONV_F_WIDTH, 2 * f)),
            _const_spec((1, 2 * f)),
            _const_spec((f, d)),
            _const_spec((1, d)),
        ],
        out_specs=(
            pl.BlockSpec((1, tm, d), lambda n, t: (n, t, 0)),
            pl.BlockSpec((1, hh, 2 * f), lambda n, t: (n, 0, 0)),
        ),
        scratch_shapes=[
            pltpu.VMEM((2, tm, d), BF),
            pltpu.VMEM((2 * f // LANES, hp + tm, LANES), F32),
            pltpu.VMEM((2 * f // LANES, hp, LANES), F32),
            pltpu.VMEM((tm, f), BF),
            pltpu.VMEM((tm, d), F32),
        ],
        compiler_params=_params(("arbitrary", "arbitrary")),
        name="conv_ffn",
    )(x, x, mods, hist, nrm, wup, cw, cb, wdown, nf)


def _qkv_kernel(x_ref, xnext_ref, mod_ref, modkv_ref, nq_ref, nkv_ref, wq_ref, wkv_ref, q_ref, kv_ref, kvtail_ref,
                hq_ref, hkv_ref, *, tm, per_row):
    t = pl.program_id(1)
    cur = lax.rem(t, 2)
    norm_targets = [(nq_ref, mod_ref, 0, 1, hq_ref), (nkv_ref, modkv_ref, 0, 1, hkv_ref)]

    @pl.when(t == 0)
    def _():
        _rms_mod_rows(x_ref, (0, tm), norm_targets, per_row, slot=0)

    q_ref[0] = _dot(hq_ref[cur], wq_ref[...])
    _rms_mod_rows(xnext_ref, (0, tm), norm_targets, per_row, slot=1 - cur, straight=True)
    kv = _dot(hkv_ref[cur], wkv_ref[...])
    kv_ref[0] = kv
    kvtail_ref[0] = kv


def _qkv_call(x, mods, modkv, nq, nkv, wq, wkv, *, tm, tail_rows):
    ns, s, d = x.shape
    r = mods.shape[2]
    per_row = r > 1
    nt = s // tm
    tail_tiles = -(-tail_rows // tm)
    assert tail_tiles <= nt and tail_rows % tm == 0
    kern = functools.partial(_qkv_kernel, tm=tm, per_row=per_row)
    return pl.pallas_call(
        kern,
        out_shape=(jax.ShapeDtypeStruct((ns, s, ATT_WIDTH), F32), jax.ShapeDtypeStruct((ns, s, 2 * ATT_WIDTH), F32),
                   jax.ShapeDtypeStruct((ns, tail_rows, 2 * ATT_WIDTH), F32)),
        grid=(ns, nt),
        in_specs=[
            pl.BlockSpec((1, tm, d), lambda n, t: (n, t, 0)),
            pl.BlockSpec((1, tm, d), lambda n, t: (n, jnp.minimum(t + 1, nt - 1), 0)),
            pl.BlockSpec((1, N_MOD, r, d), lambda n, t: (n, 0, 0, 0)),
            pl.BlockSpec((1, 2, r, d), lambda n, t: (n, 0, 0, 0)),
            _const_spec((1, d)),
            _const_spec((1, d)),
            _const_spec((d, ATT_WIDTH)),
            _const_spec((d, 2 * ATT_WIDTH)),
        ],
        out_specs=(
            pl.BlockSpec((1, tm, ATT_WIDTH), lambda n, t: (n, t, 0)),
            pl.BlockSpec((1, tm, 2 * ATT_WIDTH), lambda n, t: (n, t, 0)),
            pl.BlockSpec((1, tm, 2 * ATT_WIDTH), lambda n, t: (n, jnp.maximum(t - (nt - tail_tiles), 0), 0)),
        ),
        scratch_shapes=[pltpu.VMEM((2, tm, d), BF), pltpu.VMEM((2, tm, d), BF)],
        compiler_params=_params(("arbitrary", "arbitrary")),
        name="qkv_proj",
    )(x, x, mods, modkv, nq, nkv, wq, wkv)


def _rel_bucket(dist):
    max_exact = N_BUCKETS // 2
    dd = jnp.maximum(dist, 1).astype(F32)
    large = max_exact + (jnp.log(dd / max_exact) / math.log(MAX_DISTANCE / max_exact)
                         * (N_BUCKETS - max_exact)).astype(jnp.int32)
    large = jnp.minimum(large, N_BUCKETS - 1)
    return jnp.where(dist < max_exact, dist, large)


def _prompt_bucket_table(dil):
    qi = jnp.arange(SPAN, dtype=jnp.int32)[:, None]
    ki = jnp.arange(2 * SPAN, dtype=jnp.int32)[None, :]
    m = qi + SPAN - ki
    valid = (m >= 0) & (m <= SPAN)
    return jnp.where(valid, _rel_bucket(jnp.clip(m, 0, SPAN) * dil), -1).astype(jnp.int32)


def _sample_bucket_tables(dil, width, t_new):
    row = jnp.arange(SAMPLE_QROWS, dtype=jnp.int32)[:, None]
    t = (row % SUBLANES) % t_new
    tn = jnp.arange(SUBLANES, dtype=jnp.int32)[None, :]

    def table(dist, ok):
        ok = ok & (dist >= 0) & (dist % dil == 0) & (dist // dil <= SPAN)
        return jnp.where(ok, _rel_bucket(jnp.clip(dist, 0, SPAN * dil)), -1).astype(jnp.int32)

    old = table(width + t - jnp.arange(width, dtype=jnp.int32)[None, :], True)
    new = table(t - tn, tn < t_new)
    return old, new


def _bias_from_buckets(bkt, relb_ref, head):
    acc = jnp.full(bkt.shape, NEG_INF, F32)
    for b in range(N_BUCKETS):
        acc = jnp.where(bkt == b, relb_ref[b, head], acc)
    return acc


def _pattn_kernel(relb_ref, bkt_ref, q0_ref, q1_ref, k0_ref, k1_ref, v0_ref, v1_ref,
                  o0_ref, o1_ref, l0_ref, l1_ref, bias_ref, carry_ref, *, g, dil, nbk):
    outs = (o0_ref, o1_ref, l0_ref, l1_ref)

    def put(k, rows, val):
        outs[k][0, rows, :] = val

    _pattn_blocks(relb_ref, bkt_ref, q0_ref, q1_ref, k0_ref, k1_ref, v0_ref, v1_ref, put, bias_ref, carry_ref,
                  g=g, dil=dil, nbk=nbk)


def _pattn_oproj_kernel(relb_ref, bkt_ref, q0_ref, q1_ref, k0_ref, k1_ref, v0_ref, v1_ref, x_ref, mod_ref, *refs,
                        g, dil, nbk):
    nblk = ATT_WIDTH // LANES
    halves = nblk // N_GROUPS
    nother = nblk - halves
    o_other, l_other = refs[:nother], refs[nother:2 * nother]
    wo_ref, out_ref, bias_ref, carry_ref, stage_ref, att_ref = refs[2 * nother:]
    rows_step = nbk * dil * SPAN

    def put(k, rows, val):
        stage_ref[k, rows, :] = val

    _pattn_blocks(relb_ref, bkt_ref, q0_ref, q1_ref, k0_ref, k1_ref, v0_ref, v1_ref, put, bias_ref, carry_ref,
                  g=g, dil=dil, nbk=nbk)

    others = [gi for gi in range(N_GROUPS) if gi != g]
    rc = BF16_ROWS

    def body(i, c):
        rows = pl.ds(pl.multiple_of(i * rc, rc), rc)
        for hf in range(halves):
            os_ = {g: stage_ref[hf, rows, :]}
            ls_ = {g: stage_ref[halves + hf, rows, :]}
            for n, gi in enumerate(others):
                os_[gi] = o_other[n * halves + hf][0, rows, :]
                ls_[gi] = l_other[n * halves + hf][0, rows, :]
            mx = functools.reduce(jnp.maximum, ls_.values())
            es = {gi: jnp.exp(v - mx) for gi, v in ls_.items()}
            inv = 1.0 / functools.reduce(lambda a, b: a + b, [es[gi] for gi in range(N_GROUPS)])
            for gi in range(N_GROUPS):
                k = gi * halves + hf
                att_ref[rows, k * LANES:(k + 1) * LANES] = (os_[gi] * (es[gi] * inv)).astype(BF)
        return c

    lax.fori_loop(0, rows_step // rc, body, 0, unroll=2)
    out_ref[0] = x_ref[0] + mod_ref[0, 2] * _dot(att_ref[...], wo_ref[...])


def _pattn_blocks(relb_ref, bkt_ref, q0_ref, q1_ref, k0_ref, k1_ref, v0_ref, v1_ref, put, bias_ref, carry_ref,
                  *, g, dil, nbk):
    b = pl.program_id(0)
    i = pl.program_id(1)
    hpg = HEADS_PER_GROUP
    span_rows = SPAN * dil

    @pl.when((b == 0) & (i == 0))
    def _():
        bkt = bkt_ref[...]
        no_prev = lax.broadcasted_iota(jnp.int32, (SPAN, 2 * SPAN), 1) < SPAN
        for h in range(hpg):
            bias = _bias_from_buckets(bkt, relb_ref, g * hpg + h)
            bias_ref[0, h] = bias
            bias_ref[1, h] = jnp.where(no_prev, NEG_INF, bias)

    @pl.when(i == 0)
    def _():
        carry_ref[...] = jnp.zeros(carry_ref.shape, BF)

    lanehead = lax.broadcasted_iota(jnp.int32, (SPAN, GROUP_WIDTH), 1) >> HEAD_SHIFT
    first = jnp.where(i == 0, 1, 0)
    rd_slot = lax.rem(i, 2)
    last_k, last_v = {}, {}

    def block(j, r):
        rows = pl.ds(j * span_rows + r, SPAN, stride=dil)
        cur = lambda r0, r1: jnp.concatenate([r0[0, rows, :], r1[0, rows, :]], axis=1)
        if j == 0:
            kprev, vprev = carry_ref[rd_slot, 0, r], carry_ref[rd_slot, 1, r]
        else:
            kprev, vprev = last_k[r], last_v[r]
        kcur = cur(k0_ref, k1_ref).astype(BF)
        vcur = cur(v0_ref, v1_ref).astype(BF)
        if j == nbk - 1:
            carry_ref[1 - rd_slot, 0, r] = kcur
            carry_ref[1 - rd_slot, 1, r] = vcur
        else:
            last_k[r], last_v[r] = kcur, vcur
        q = cur(q0_ref, q1_ref) * (HEAD_DIM ** -0.5)
        lhs = jnp.concatenate([jnp.where(lanehead == h, q, 0.0).astype(BF) for h in range(hpg)], axis=0)
        s = _dot_nt(lhs, jnp.concatenate([kprev, kcur], axis=0))

        ps, ms, sums = [], [], []
        for h in range(hpg):
            logit = s[h * SPAN:(h + 1) * SPAN] + (bias_ref[first, h] if j == 0 else bias_ref[0, h])
            m = jnp.max(logit, axis=-1, keepdims=True)
            e = jnp.exp(logit - m)
            ps.append(e.astype(BF))
            ms.append(m)
            sums.append(jnp.sum(e, axis=-1, keepdims=True))
        pv = _dot(jnp.concatenate(ps, axis=0), jnp.concatenate([vprev, vcur], axis=0))

        o = jnp.zeros((SPAN, GROUP_WIDTH), F32)
        l = jnp.zeros((SPAN, GROUP_WIDTH), F32)
        for h in range(hpg):
            o = jnp.where(lanehead == h, pv[h * SPAN:(h + 1) * SPAN] * (1.0 / sums[h]), o)
            l = jnp.where(lanehead == h, ms[h] + jnp.log(sums[h]), l)
        put(0, rows, o[:, :LANES])
        put(1, rows, o[:, LANES:])
        put(2, rows, l[:, :LANES])
        put(3, rows, l[:, LANES:])

    for j in range(nbk):
        for r in range(dil):
            block(j, r)


def _pattn_call(q, kv, rel_bias, g, dil, finish=None):
    bsz, s, _ = q.shape
    nbk = max(1, PATTN_ROWS // (dil * SPAN))
    rows = nbk * dil * SPAN
    assert s % rows == 0 and GROUP_WIDTH == 2 * LANES
    blk = (1, rows, LANES)
    kcol = 2 * g
    vcol = ATT_WIDTH // LANES + 2 * g
    ospec = pl.BlockSpec(blk, lambda b, i: (b, i, 0))
    in_specs = [
        pl.BlockSpec(memory_space=pltpu.SMEM),
        _const_spec((SPAN, 2 * SPAN)),
        pl.BlockSpec(blk, lambda b, i: (b, i, kcol)),
        pl.BlockSpec(blk, lambda b, i: (b, i, kcol + 1)),
        pl.BlockSpec(blk, lambda b, i: (b, i, kcol)),
        pl.BlockSpec(blk, lambda b, i: (b, i, kcol + 1)),
        pl.BlockSpec(blk, lambda b, i: (b, i, vcol)),
        pl.BlockSpec(blk, lambda b, i: (b, i, vcol + 1)),
    ]
    args = [rel_bias, _prompt_bucket_table(dil), q, q, kv, kv, kv, kv]
    scratch = [pltpu.VMEM((2, HEADS_PER_GROUP, SPAN, 2 * SPAN), F32),
               pltpu.VMEM((2, 2, dil, SPAN, GROUP_WIDTH), BF)]
    if finish is None:
        out = jax.ShapeDtypeStruct((bsz, s, LANES), F32)
        res = pl.pallas_call(
            functools.partial(_pattn_kernel, g=g, dil=dil, nbk=nbk),
            out_shape=(out,) * 4,
            grid=(bsz, s // rows),
            in_specs=in_specs,
            out_specs=(ospec,) * 4,
            scratch_shapes=scratch,
            compiler_params=_params(("arbitrary", "arbitrary")),
            name=f"prompt_attn_g{g}",
        )(*args)
        return list(res[:2]), list(res[2:])

    x, mods, o_other, l_other, wo = finish
    d = x.shape[-1]
    r = mods.shape[2]
    assert r == 1
    xspec = pl.BlockSpec((1, rows, d), lambda b, i: (b, i, 0))
    return pl.pallas_call(
        functools.partial(_pattn_oproj_kernel, g=g, dil=dil, nbk=nbk),
        out_shape=jax.ShapeDtypeStruct((bsz, s, d), F32),
        grid=(bsz, s // rows),
        in_specs=in_specs + [xspec, pl.BlockSpec((1, N_MOD, r, d), lambda b, i: (b, 0, 0, 0))]
        + [ospec] * (len(o_other) + len(l_other)) + [_const_spec((ATT_WIDTH, d))],
        out_specs=xspec,
        scratch_shapes=scratch + [pltpu.VMEM((4, rows, LANES), F32), pltpu.VMEM((rows, ATT_WIDTH), BF)],
        compiler_params=_params(("arbitrary", "arbitrary")),
        name=f"prompt_attn_g{g}_out_proj",
    )(*args, x, mods, *o_other, *l_other, wo)


def _sattn_kernel(relb_ref, bo0_ref, bo1_ref, bo2_ref, bn_ref, q_ref, kvn_ref, kvt_ref, c0_ref, c1_ref, c2_ref,
                  o0_ref, o1_ref, o2_ref, l0_ref, l1_ref, l2_ref, n0_ref, n1_ref, n2_ref,
                  bias0_ref, bias1_ref, bias2_ref, biasn_ref, tail_ref, *, t_new):
    hpg = HEADS_PER_GROUP
    gw = GROUP_WIDTH
    rows = SAMPLE_QROWS
    bo_refs = (bo0_ref, bo1_ref, bo2_ref)
    bias_refs = (bias0_ref, bias1_ref, bias2_ref)
    caches = (c0_ref, c1_ref, c2_ref)
    outs = ((o0_ref, l0_ref, n0_ref), (o1_ref, l1_ref, n1_ref), (o2_ref, l2_ref, n2_ref))

    @pl.when(pl.program_id(0) == 0)
    def _():
        tail_ref[...] = jnp.zeros(tail_ref.shape, F32)
        for g in range(N_GROUPS):
            for h in range(hpg):
                sl = slice(h * SUBLANES, (h + 1) * SUBLANES)
                bias_refs[g][sl, :] = _bias_from_buckets(bo_refs[g][sl, :], relb_ref, g * hpg + h)
                biasn_ref[g, sl, :] = _bias_from_buckets(bn_ref[g, sl, :], relb_ref, g * hpg + h)

    lanehead = lax.broadcasted_iota(jnp.int32, (rows, gw), 1) >> HEAD_SHIFT
    rowhead = lax.broadcasted_iota(jnp.int32, (rows, gw), 0) >> SUBLANE_SHIFT
    own = lanehead == rowhead
    tail_lane = lax.broadcasted_iota(jnp.int32, (gw, LANES), 1) >= LANES - t_new

    for g in range(N_GROUPS):
        cref = caches[g]
        o_ref, l_ref, n_ref = outs[g]
        width = cref.shape[-1]
        qm = jnp.where(own, q_ref[0, :, g * gw:(g + 1) * gw] * (HEAD_DIM ** -0.5), 0.0).astype(BF)
        knew = kvn_ref[0, :, g * gw:(g + 1) * gw].astype(BF)
        vnew = kvn_ref[0, :, ATT_WIDTH + g * gw:ATT_WIDTH + (g + 1) * gw].astype(BF)
        lo = _dot(qm, cref[0, 0].astype(BF)) + bias_refs[g][...]
        ln = _dot_nt(qm, knew) + biasn_ref[g]
        m = jnp.maximum(jnp.max(lo, axis=-1, keepdims=True), jnp.max(ln, axis=-1, keepdims=True))
        eo = jnp.exp(lo - m)
        en = jnp.exp(ln - m)
        ssum = jnp.sum(eo, axis=-1, keepdims=True) + jnp.sum(en, axis=-1, keepdims=True)
        pv = _dot_nt(eo.astype(BF), cref[0, 1].astype(BF)) + _dot(en.astype(BF), vnew)
        om = jnp.where(own, pv * (1.0 / ssum), 0.0)
        lm = jnp.where(own, m + jnp.log(ssum), 0.0)
        o_acc = om[0:SUBLANES]
        l_acc = lm[0:SUBLANES]
        for h in range(1, hpg):
            o_acc = o_acc + om[h * SUBLANES:(h + 1) * SUBLANES]
            l_acc = l_acc + lm[h * SUBLANES:(h + 1) * SUBLANES]
        o_ref[0] = o_acc
        l_ref[0] = l_acc

        for kv in range(2):
            rolled = pltpu.roll(cref[0, kv], width - t_new, 1)
            tail_ref[:, 0:SUBLANES] = kvt_ref[0, kv * ATT_WIDTH + g * gw:kv * ATT_WIDTH + (g + 1) * gw, :]
            tail = pltpu.roll(tail_ref[...], LANES - t_new, 1)
            if width > LANES:
                n_ref[0, kv, :, 0:width - LANES] = rolled[:, 0:width - LANES]
            n_ref[0, kv, :, width - LANES:width] = jnp.where(tail_lane, tail, rolled[:, width - LANES:width])


def _sattn_call(q, kvn, caches, rel_bias):
    n, t_new, _ = q.shape
    gw = GROUP_WIDTH
    hpg = HEADS_PER_GROUP
    assert t_new <= SUBLANES
    views, bkt_old, bkt_new = [], [], []
    for g, (window, dil) in enumerate(DIL_GROUPS):
        width = caches[g].shape[1]
        assert width == window and width % LANES == 0
        views.append(jnp.transpose(caches[g], (0, 2, 3, 4, 1)).reshape(n, 2, gw, width))
        old, new = _sample_bucket_tables(dil, width, t_new)
        bkt_old.append(old)
        bkt_new.append(new)
    pad = SUBLANES - t_new
    qp = jnp.tile(jnp.pad(q, ((0, 0), (0, pad), (0, 0))), (1, hpg, 1))
    kvp = jnp.pad(kvn, ((0, 0), (0, pad), (0, 0)))
    kvt = jnp.swapaxes(kvp, 1, 2)
    out = jax.ShapeDtypeStruct((n, SUBLANES, gw), F32)
    oblk = pl.BlockSpec((1, SUBLANES, gw), lambda i: (i, 0, 0))
    cspecs = [pl.BlockSpec((1, 2, gw, v.shape[-1]), lambda i: (i, 0, 0, 0)) for v in views]
    res = pl.pallas_call(
        functools.partial(_sattn_kernel, t_new=t_new),
        out_shape=(out,) * 6 + tuple(jax.ShapeDtypeStruct(v.shape, F32) for v in views),
        grid=(n,),
        in_specs=[pl.BlockSpec(memory_space=pltpu.SMEM)]
        + [_const_spec(b.shape) for b in bkt_old]
        + [
            _const_spec((N_GROUPS, SAMPLE_QROWS, SUBLANES)),
            pl.BlockSpec((1, SAMPLE_QROWS, ATT_WIDTH), lambda i: (i, 0, 0)),
            pl.BlockSpec((1, SUBLANES, 2 * ATT_WIDTH), lambda i: (i, 0, 0)),
            pl.BlockSpec((1, 2 * ATT_WIDTH, SUBLANES), lambda i: (i, 0, 0)),
        ]
        + cspecs,
        out_specs=(oblk,) * 6 + tuple(cspecs),
        scratch_shapes=[pltpu.VMEM(b.shape, F32) for b in bkt_old]
        + [pltpu.VMEM((N_GROUPS, SAMPLE_QROWS, SUBLANES), F32), pltpu.VMEM((gw, LANES), F32)],
        compiler_params=_params(("arbitrary",)),
        name="sample_attn",
    )(rel_bias, *bkt_old, jnp.stack(bkt_new), qp, kvp, kvt, *views)
    new_caches = [jnp.transpose(c.reshape(n, 2, hpg, HEAD_DIM, c.shape[-1]), (0, 4, 1, 2, 3)) for c in res[6:]]
    return [a[:, :t_new] for a in res[:3]], [a[:, :t_new] for a in res[3:6]], new_caches


def _oproj_kernel(x_ref, mod_ref, *refs, tm):
    nblk = ATT_WIDTH // LANES
    halves = nblk // N_GROUPS
    o_refs, l_refs = refs[:nblk], refs[nblk:2 * nblk]
    wo_ref, out_ref, att_ref = refs[2 * nblk:]
    rc = BF16_ROWS

    def body(i, c):
        r0 = pl.multiple_of(i * rc, rc)
        for hf in range(halves):
            blks = [g * halves + hf for g in range(N_GROUPS)]
            ls = [l_refs[k][0, pl.ds(r0, rc), :] for k in blks]
            mx = functools.reduce(jnp.maximum, ls)
            es = [jnp.exp(v - mx) for v in ls]
            inv = 1.0 / functools.reduce(lambda a, b: a + b, es)
            for k, e in zip(blks, es):
                att_ref[pl.ds(r0, rc), k * LANES:(k + 1) * LANES] = (o_refs[k][0, pl.ds(r0, rc), :] * (e * inv)).astype(BF)
        return c

    lax.fori_loop(0, tm // rc, body, 0, unroll=2)
    out_ref[0] = x_ref[0] + mod_ref[0, 2] * _dot(att_ref[...], wo_ref[...])


def _oproj_call(x, mods, os_, ls_, wo, *, tm):
    ns, s, d = x.shape
    r = mods.shape[2]
    kern = functools.partial(_oproj_kernel, tm=tm)
    gspec = pl.BlockSpec((1, tm, LANES), lambda n, t: (n, t, 0))
    return pl.pallas_call(
        kern,
        out_shape=jax.ShapeDtypeStruct((ns, s, d), F32),
        grid=(ns, s // tm),
        in_specs=[
            pl.BlockSpec((1, tm, d), lambda n, t: (n, t, 0)),
            pl.BlockSpec((1, N_MOD, r, d), lambda n, t: (n, 0, 0, 0)),
        ] + [gspec] * (2 * ATT_WIDTH // LANES) + [_const_spec((ATT_WIDTH, d))],
        out_specs=pl.BlockSpec((1, tm, d), lambda n, t: (n, t, 0)),
        scratch_shapes=[pltpu.VMEM((tm, ATT_WIDTH), BF)],
        compiler_params=_params(("arbitrary", "arbitrary")),
        name="attn_out_proj",
    )(x, mods, *os_, *ls_, wo)


def _trunk(x, mods, modkv, hist_a, hist_f, w, attn_fn, *, tm, rd, kv_tail_rows):
    row = lambda v: v.reshape(1, -1)
    x, st_a = _conformer_call(x, mods[0], hist_a, row(w["norm_mix"][0]), w["a_w1"], row(w["a_b1"][0]),
                              w["a_dw"][0], row(w["a_dwb"][0]), row(w["a_ln_g"][0]), row(w["a_ln_b"][0]),
                              w["a_w2"], row(w["a_b2"][0]), tm=tm, rd=rd)
    x, st_f0 = _ffn_call(x, mods[0], hist_f[0], row(w["norm_ffn"][0]), w["f_wup"][0], w["f_cw"][0],
                         row(w["f_cb"][0]), w["f_wdown"][0], row(w["norm_f"]), tm=tm, rd=rd, final_norm=False)
    q, kv, kv_tail = _qkv_call(x, mods[1], modkv, row(w["norm_mix"][1]), row(w["norm_kv"]), w["w_q"], w["w_kv"],
                               tm=tm, tail_rows=kv_tail_rows)
    x, attn_extra = attn_fn(q, kv, x, mods[1], w["w_o"])
    y, st_f1 = _ffn_call(x, mods[1], hist_f[1], row(w["norm_ffn"][1]), w["f_wup"][1], w["f_cw"][1],
                         row(w["f_cb"][1]), w["f_wdown"][1], row(w["norm_f"]), tm=tm, rd=rd, final_norm=True)
    return y, kv_tail, st_a, [st_f0, st_f1], attn_extra


def kernel(x_prompt, x_sample, cache_kv_w128, cache_kv_w512, cache_kv_w2048, state_conv_a, state_conv_ffn, c_prompt, c_sample, w_mod, b_mod, norm_mix, norm_ffn, a_w1, a_b1, a_dw, a_dwb, a_ln_g, a_ln_b, a_w2, a_b2, w_mod_kv, b_mod_kv, norm_kv, w_kv, w_q, w_o, rel_bias, f_wup, f_cw, f_cb, f_wdown, norm_f):
    d = D_MODEL
    bsz, seq, _ = x_prompt.shape
    nseq, t_new, _ = x_sample.shape
    caches = (cache_kv_w128, cache_kv_w512, cache_kv_w2048)
    hpg = HEADS_PER_GROUP

    w = dict(norm_mix=norm_mix, norm_ffn=norm_ffn, a_w1=a_w1[0].astype(BF), a_b1=a_b1, a_dw=a_dw, a_dwb=a_dwb,
             a_ln_g=a_ln_g, a_ln_b=a_ln_b, a_w2=a_w2[0].astype(BF), a_b2=a_b2, norm_kv=norm_kv,
             w_kv=w_kv.astype(BF), w_q=w_q[0].astype(BF), w_o=w_o[0].astype(BF),
             f_wup=[f_wup[l].astype(BF) for l in range(DEPTH)], f_cw=f_cw, f_cb=f_cb,
             f_wdown=[f_wdown[l].astype(BF) for l in range(DEPTH)], norm_f=norm_f)

    n_c = bsz + nseq
    mp = _round_up(n_c, SUBLANES)
    c_all = jnp.pad(jnp.concatenate([c_prompt, c_sample], axis=0), ((0, mp - n_c), (0, 0)))
    mod = _mod_call(c_all, w_mod, b_mod.reshape(DEPTH, 1, N_MOD * d), tn=1536)
    modkv = _mod_call(c_all, w_mod_kv[None], b_mod_kv.reshape(1, 1, 2 * d), tn=1024)[0]

    mods_p = [mod[l, :bsz].reshape(bsz, N_MOD, 1, d) for l in range(DEPTH)]
    modkv_p = modkv[:bsz].reshape(bsz, 2, 1, d)
    hist_a_p = jnp.zeros((bsz, CONV_A_WIDTH - 1, d), F32)
    hist_f_p = [jnp.zeros((bsz, CONV_F_WIDTH - 1, 2 * D_FF), F32)] * DEPTH

    def prompt_attn(q, kv, x, mods1, wo):
        res = [_pattn_call(q, kv, rel_bias, g, dil) for g, (_, dil) in enumerate(DIL_GROUPS) if g > 0]
        finish = (x, mods1, [a for r in res for a in r[0]], [a for r in res for a in r[1]], wo)
        return _pattn_call(q, kv, rel_bias, 0, DIL_GROUPS[0][1], finish=finish), None

    wmax = max(c.shape[1] for c in caches)
    assert seq >= wmax
    y_p, kv_tail, st_a_p, st_f_p, _ = _trunk(x_prompt, mods_p, modkv_p, hist_a_p, hist_f_p, w, prompt_attn,
                                             tm=TM_PROMPT, rd=1, kv_tail_rows=wmax)
    kv_tail = kv_tail.reshape(bsz, wmax, 2, N_HEADS, HEAD_DIM)
    kv_bufs_p = [kv_tail[:, wmax - c.shape[1]:, :, g * hpg:(g + 1) * hpg] for g, c in enumerate(caches)]
    conv_a_p = st_a_p[None]
    conv_f_p = jnp.stack(st_f_p)

    rows = t_new * nseq
    tmaj = lambda v: jnp.swapaxes(v, 0, 1).reshape(1, -1, v.shape[-1])
    smaj = lambda v, c: jnp.swapaxes(v.reshape(-1, nseq, c), 0, 1)

    def per_row(m, k):
        return jnp.tile(jnp.swapaxes(m.reshape(nseq, k, d), 0, 1), (1, t_new, 1))[None]

    mods_s = [per_row(mod[l, bsz:n_c], N_MOD) for l in range(DEPTH)]
    modkv_s = per_row(modkv[bsz:n_c], 2)
    hist_a_s = tmaj(state_conv_a[0])
    hist_f_s = [tmaj(state_conv_ffn[l]) for l in range(DEPTH)]

    def sample_attn(q, kv, x, mods1, wo):
        os_, ls_, new_caches = _sattn_call(smaj(q, ATT_WIDTH), smaj(kv, 2 * ATT_WIDTH), caches, rel_bias)
        halves = lambda vs: [tmaj(a[..., c0:c0 + LANES]) for a in vs for c0 in range(0, GROUP_WIDTH, LANES)]
        return _oproj_call(x, mods1, halves(os_), halves(ls_), wo, tm=rows), new_caches

    y_s, _, st_a_s, st_f_s, kv_bufs_s = _trunk(tmaj(x_sample), mods_s, modkv_s, hist_a_s, hist_f_s, w,
                                               sample_attn, tm=rows, rd=nseq, kv_tail_rows=rows)
    y_s = smaj(y_s, d)
    conv_a_s = smaj(st_a_s, d)[None]
    conv_f_s = jnp.stack([smaj(s, 2 * D_FF) for s in st_f_s])

    return (y_p, y_s, kv_bufs_p[0], kv_bufs_p[1], kv_bufs_p[2], conv_a_p, conv_f_p,
            kv_bufs_s[0], kv_bufs_s[1], kv_bufs_s[2], conv_a_s, conv_f_s)
```

```python
import functools
import math

import jax
import jax.numpy as jnp
from jax import lax
from jax.experimental import pallas as pl
from jax.experimental.pallas import tpu as pltpu

D_MODEL = 1024
DEPTH = 2
HEAD_DIM = 64
HEADS_PER_GROUP = 4
DIL_GROUPS = ((128, 1), (512, 4), (2048, 16))
N_GROUPS = len(DIL_GROUPS)
N_HEADS = HEADS_PER_GROUP * N_GROUPS
ATT_WIDTH = N_HEADS * HEAD_DIM
GROUP_WIDTH = HEADS_PER_GROUP * HEAD_DIM
SPAN = 128
N_BUCKETS = 32
MAX_DISTANCE = 2048
CONV_A_WIDTH = 31
D_FF = 2816
CONV_F_WIDTH = 3
N_MOD = 6
EPS = 1e-6
LN_EPS = 1e-5
NEG_INF = -1e30

BF = jnp.bfloat16
F32 = jnp.float32

HEAD_SHIFT = HEAD_DIM.bit_length() - 1
SUBLANES = 8
SUBLANE_SHIFT = SUBLANES.bit_length() - 1
LANES = 128
BF16_ROWS = 16
ROW_UNROLL = 4
VMEM_LIMIT = 56 * 1024 * 1024

TM_PROMPT = 512
PATTN_ROWS = 1024
CONF_ROWS = 512
FF_ROWS = 32
FF_SUB = 256
SAMPLE_QROWS = HEADS_PER_GROUP * SUBLANES


def _round_up(a, b):
    return -(-a // b) * b


def _params(sem):
    return pltpu.CompilerParams(dimension_semantics=sem, vmem_limit_bytes=VMEM_LIMIT)


def _const_spec(shape):
    nd = len(shape)
    return pl.BlockSpec(shape, lambda *_: (0,) * nd, pipeline_mode=pl.Buffered(1))


def _dot(a, b):
    return jnp.dot(a, b, preferred_element_type=F32)


def _dot_nt(a, b):
    return lax.dot_general(a, b, (((1,), (1,)), ((), ())), preferred_element_type=F32)


def _sigmoid(v):
    return 1.0 / (1.0 + jnp.exp(-v))


def _mod_rows(mod_ref, idx, r0, rows, per_row):
    if per_row:
        return mod_ref[0, idx, pl.ds(r0, rows), :]
    return mod_ref[0, idx]


def _rms_mod_rows(x_ref, rows, targets, per_row, *, slot=None, straight=False):
    rc = BF16_ROWS
    lo, hi = rows

    def chunk(r0):
        x = x_ref[0, pl.ds(r0, rc), :]
        xn = x * lax.rsqrt(jnp.mean(x * x, axis=-1, keepdims=True) + EPS)
        for gain_ref, mod_ref, i_sh, i_sc, dst_ref in targets:
            sh = _mod_rows(mod_ref, i_sh, r0, rc, per_row)
            sc = _mod_rows(mod_ref, i_sc, r0, rc, per_row)
            idx = (pl.ds(r0, rc), slice(None))
            dst_ref[idx if slot is None else (slot,) + idx] = ((xn * gain_ref[...]) * (1.0 + sc) + sh).astype(BF)

    if straight:
        for r0 in range(lo, hi, rc):
            chunk(r0)
        return

    def body(i, c):
        chunk(pl.multiple_of(lo + i * rc, rc))
        return c

    lax.fori_loop(0, (hi - lo) // rc, body, 0, unroll=ROW_UNROLL)


def _tap_conv(src_ref, blk, w_ref, offsets, r0, rows):
    cols = slice(blk * LANES, (blk + 1) * LANES)
    acc = None
    for k, off in enumerate(offsets):
        term = w_ref[k:k + 1, cols] * src_ref[blk, r0 + off:r0 + off + rows, :]
        acc = term if acc is None else acc + term
    return acc


def _mod_kernel(c_ref, w_ref, b_ref, o_ref):
    c = c_ref[...]
    a = (c * _sigmoid(c)).astype(BF)
    o_ref[0] = _dot(a, w_ref[0].astype(BF)) + b_ref[0]


def _mod_call(c_all, w, b, tn):
    nl, d, n = w.shape
    mp = c_all.shape[0]
    return pl.pallas_call(
        _mod_kernel,
        out_shape=jax.ShapeDtypeStruct((nl, mp, n), F32),
        grid=(nl, n // tn),
        in_specs=[
            pl.BlockSpec((mp, d), lambda l, j: (0, 0)),
            pl.BlockSpec((1, d, tn), lambda l, j: (l, 0, j)),
            pl.BlockSpec((1, 1, tn), lambda l, j: (l, 0, j)),
        ],
        out_specs=pl.BlockSpec((1, mp, tn), lambda l, j: (l, 0, j)),
        compiler_params=_params(("arbitrary", "arbitrary")),
        name="adaln_mod",
    )(c_all, w, b)


def _conformer_kernel(x_ref, mod_ref, hist_ref, nrm_ref, w1_ref, b1_ref, dw_ref, dwb_ref, lng_ref, lnb_ref,
                      w2_ref, b2_ref, o_ref, st_ref, h_ref, h2_ref, u_ref, full_ref, y_ref, *, tm, rd, per_row):
    d = D_MODEL
    hh = (CONV_A_WIDTH - 1) * rd
    hp = _round_up(hh, SUBLANES)
    t = pl.program_id(1)

    nblk = d // LANES
    lanes = lambda blk: slice(blk * LANES, (blk + 1) * LANES)

    @pl.when(t == 0)
    def _():
        full_ref[:, 0:hp, :] = jnp.zeros((nblk, hp, LANES), F32)
        for blk in range(nblk):
            full_ref[blk, hp - hh:hp, :] = hist_ref[0, :, lanes(blk)]

    rb = min(tm, CONF_ROWS)
    rg, cg = 16, 512
    rcv = 64
    offsets = [hp - hh + k * rd for k in range(CONV_A_WIDTH)]
    for rs in range(0, tm, rb):
        _rms_mod_rows(x_ref, (rs, rs + rb), [(nrm_ref, mod_ref, 0, 1, h_ref)], per_row, straight=True)
        u_ref[rs:rs + rb, :] = _dot(h_ref[rs:rs + rb, :], w1_ref[...])

        for r0 in range(rs, rs + rb, rg):
            for c0 in range(0, d, cg):
                a = u_ref[r0:r0 + rg, c0:c0 + cg] + b1_ref[:, c0:c0 + cg]
                g = u_ref[r0:r0 + rg, d + c0:d + c0 + cg] + b1_ref[:, d + c0:d + c0 + cg]
                glu = a * _sigmoid(g)
                for j in range(cg // LANES):
                    full_ref[c0 // LANES + j, hp + r0:hp + r0 + rg, :] = glu[:, lanes(j)]

        for r0 in range(rs, rs + rb, rcv):
            for blk in range(nblk):
                y_ref[r0:r0 + rcv, lanes(blk)] = (_tap_conv(full_ref, blk, dw_ref, offsets, r0, rcv)
                                                  + dwb_ref[:, lanes(blk)])

        for r0 in range(rs, rs + rb, BF16_ROWS):
            y = y_ref[r0:r0 + BF16_ROWS, :]
            mu = jnp.mean(y, axis=-1, keepdims=True)
            dv = y - mu
            var = jnp.mean(dv * dv, axis=-1, keepdims=True)
            yn = dv * lax.rsqrt(var + LN_EPS) * lng_ref[...] + lnb_ref[...]
            h2_ref[r0:r0 + BF16_ROWS, :] = (yn * _sigmoid(yn)).astype(BF)

        out = _dot(h2_ref[rs:rs + rb, :], w2_ref[...]) + b2_ref[...]
        gate = mod_ref[0, 2, rs:rs + rb, :] if per_row else mod_ref[0, 2]
        o_ref[0, rs:rs + rb, :] = x_ref[0, rs:rs + rb, :] + gate * out

    for blk in range(nblk):
        new_hist = full_ref[blk, hp + tm - hh:hp + tm, :]
        st_ref[0, :, lanes(blk)] = new_hist
        full_ref[blk, hp - hh:hp, :] = new_hist


def _conformer_call(x, mods, hist, nrm, w1, b1, dw, dwb, lng, lnb, w2, b2, *, tm, rd):
    ns, s, d = x.shape
    r = mods.shape[2]
    hh = hist.shape[1]
    hp = _round_up(hh, SUBLANES)
    per_row = r > 1
    assert s % tm == 0 and (not per_row or (r == tm and s == tm))
    kern = functools.partial(_conformer_kernel, tm=tm, rd=rd, per_row=per_row)
    return pl.pallas_call(
        kern,
        out_shape=(jax.ShapeDtypeStruct((ns, s, d), F32), jax.ShapeDtypeStruct((ns, hh, d), F32)),
        grid=(ns, s // tm),
        in_specs=[
            pl.BlockSpec((1, tm, d), lambda n, t: (n, t, 0)),
            pl.BlockSpec((1, N_MOD, r, d), lambda n, t: (n, 0, 0, 0)),
            pl.BlockSpec((1, hh, d), lambda n, t: (n, 0, 0)),
            _const_spec((1, d)),
            _const_spec((d, 2 * d)),
            _const_spec((1, 2 * d)),
            _const_spec((CONV_A_WIDTH, d)),
            _const_spec((1, d)),
            _const_spec((1, d)),
            _const_spec((1, d)),
            _const_spec((d, d)),
            _const_spec((1, d)),
        ],
        out_specs=(
            pl.BlockSpec((1, tm, d), lambda n, t: (n, t, 0)),
            pl.BlockSpec((1, hh, d), lambda n, t: (n, 0, 0)),
        ),
        scratch_shapes=[
            pltpu.VMEM((tm, d), BF),
            pltpu.VMEM((tm, d), BF),
            pltpu.VMEM((tm, 2 * d), F32),
            pltpu.VMEM((d // LANES, hp + tm, LANES), F32),
            pltpu.VMEM((tm, d), F32),
        ],
        compiler_params=_params(("arbitrary", "arbitrary")),
        name="conformer_mixer",
    )(x, mods, hist, nrm, w1, b1, dw, dwb, lng, lnb, w2, b2)


def _ffn_kernel(x_ref, xnext_ref, mod_ref, hist_ref, nrm_ref, wu_ref, cw_ref, cb_ref, wd_ref, nf_ref, o_ref, st_ref,
                h_ref, ubuf_ref, carry_ref, act_ref, part_ref, *, tm, rd, per_row, final_norm):
    f = D_FF
    hh = (CONV_F_WIDTH - 1) * rd
    hp = _round_up(hh, SUBLANES)
    t = pl.program_id(1)

    cur = lax.rem(t, 2)
    norm_targets = [(nrm_ref, mod_ref, 3, 4, h_ref)]

    @pl.when(t == 0)
    def _():
        _rms_mod_rows(x_ref, (0, tm), norm_targets, per_row, slot=0)

    nblk = 2 * f // LANES
    lanes = lambda blk: slice(blk * LANES, (blk + 1) * LANES)

    @pl.when(t == 0)
    def _():
        carry_ref[...] = jnp.zeros((nblk, hp, LANES), F32)
        for blk in range(nblk):
            carry_ref[blk, hp - hh:hp, :] = hist_ref[0, :, lanes(blk)]

    ubuf_ref[:, 0:hp, :] = carry_ref[...]

    rc = FF_ROWS
    offsets = [hp - hh + k * rd for k in range(CONV_F_WIDTH)]
    for c0 in range(0, f, FF_SUB):
        for half in range(2):
            col = half * f + c0
            u = _dot(h_ref[cur], wu_ref[:, col:col + FF_SUB])
            for j in range(FF_SUB // LANES):
                ubuf_ref[col // LANES + j, hp:hp + tm, :] = u[:, lanes(j)]
        for r0 in range(0, tm, rc):
            for cc0 in range(c0, c0 + FF_SUB, LANES):
                ys = []
                for half in range(2):
                    blk = (half * f + cc0) // LANES
                    ys.append(_tap_conv(ubuf_ref, blk, cw_ref, offsets, r0, rc) + cb_ref[:, lanes(blk)])
                yg, yv = ys
                act_ref[r0:r0 + rc, cc0:cc0 + LANES] = (yg * _sigmoid(yg) * yv).astype(BF)

    for blk in range(nblk):
        st_ref[0, :, lanes(blk)] = ubuf_ref[blk, hp + tm - hh:hp + tm, :]
    carry_ref[...] = ubuf_ref[:, tm:tm + hp, :]

    _rms_mod_rows(xnext_ref, (0, tm), norm_targets, per_row, slot=1 - cur, straight=True)
    xo = x_ref[0] + mod_ref[0, 5] * _dot(act_ref[...], wd_ref[...])
    if not final_norm:
        o_ref[0] = xo
        return
    part_ref[...] = xo
    rows = BF16_ROWS

    def norm_body(i, c):
        r0 = pl.multiple_of(i * rows, rows)
        v = part_ref[pl.ds(r0, rows), :]
        o_ref[0, pl.ds(r0, rows), :] = v * lax.rsqrt(jnp.mean(v * v, axis=-1, keepdims=True) + EPS) * nf_ref[...]
        return c

    lax.fori_loop(0, tm // rows, norm_body, 0, unroll=ROW_UNROLL)


def _ffn_call(x, mods, hist, nrm, wup, cw, cb, wdown, nf, *, tm, rd, final_norm):
    ns, s, d = x.shape
    r = mods.shape[2]
    hh = hist.shape[1]
    hp = _round_up(hh, SUBLANES)
    f = wdown.shape[0]
    per_row = r > 1
    nt = s // tm
    assert f == D_FF and f % FF_SUB == 0 and s % tm == 0 and (not per_row or (r == tm and s == tm))
    kern = functools.partial(_ffn_kernel, tm=tm, rd=rd, per_row=per_row, final_norm=final_norm)
    return pl.pallas_call(
        kern,
        out_shape=(jax.ShapeDtypeStruct((ns, s, d), F32), jax.ShapeDtypeStruct((ns, hh, 2 * f), F32)),
        grid=(ns, nt),
        in_specs=[
            pl.BlockSpec((1, tm, d), lambda n, t: (n, t, 0)),
            pl.BlockSpec((1, tm, d), lambda n, t: (n, jnp.minimum(t + 1, nt - 1), 0)),
            pl.BlockSpec((1, N_MOD, r, d), lambda n, t: (n, 0, 0, 0)),
            pl.BlockSpec((1, hh, 2 * f), lambda n, t: (n, 0, 0)),
            _const_spec((1, d)),
            _const_spec((d, 2 * f)),
            _const_spec((CONV_F_WIDTH, 2 * f)),
            _const_spec((1, 2 * f)),
            _const_spec((f, d)),
            _const_spec((1, d)),
        ],
        out_specs=(
            pl.BlockSpec((1, tm, d), lambda n, t: (n, t, 0)),
            pl.BlockSpec((1, hh, 2 * f), lambda n, t: (n, 0, 0)),
        ),
        scratch_shapes=[
            pltpu.VMEM((2, tm, d), BF),
            pltpu.VMEM((2 * f // LANES, hp + tm, LANES), F32),
            pltpu.VMEM((2 * f // LANES, hp, LANES), F32),
            pltpu.VMEM((tm, f), BF),
            pltpu.VMEM((tm, d), F32),
        ],
        compiler_params=_params(("arbitrary", "arbitrary")),
        name="conv_ffn",
    )(x, x, mods, hist, nrm, wup, cw, cb, wdown, nf)


def _qkv_kernel(x_ref, xnext_ref, mod_ref, modkv_ref, nq_ref, nkv_ref, wq_ref, wkv_ref, q_ref, kv_ref, kvtail_ref,
                hq_ref, hkv_ref, *, tm, per_row):
    t = pl.program_id(1)
    cur = lax.rem(t, 2)
    norm_targets = [(nq_ref, mod_ref, 0, 1, hq_ref), (nkv_ref, modkv_ref, 0, 1, hkv_ref)]

    @pl.when(t == 0)
    def _():
        _rms_mod_rows(x_ref, (0, tm), norm_targets, per_row, slot=0)

    q_ref[0] = _dot(hq_ref[cur], wq_ref[...])
    _rms_mod_rows(xnext_ref, (0, tm), norm_targets, per_row, slot=1 - cur, straight=True)
    kv = _dot(hkv_ref[cur], wkv_ref[...])
    kv_ref[0] = kv
    kvtail_ref[0] = kv


def _qkv_call(x, mods, modkv, nq, nkv, wq, wkv, *, tm, tail_rows):
    ns, s, d = x.shape
    r = mods.shape[2]
    per_row = r > 1
    nt = s // tm
    tail_tiles = -(-tail_rows // tm)
    assert tail_tiles <= nt and tail_rows % tm == 0
    kern = functools.partial(_qkv_kernel, tm=tm, per_row=per_row)
    return pl.pallas_call(
        kern,
        out_shape=(jax.ShapeDtypeStruct((ns, s, ATT_WIDTH), F32), jax.ShapeDtypeStruct((ns, s, 2 * ATT_WIDTH), F32),
                   jax.ShapeDtypeStruct((ns, tail_rows, 2 * ATT_WIDTH), F32)),
        grid=(ns, nt),
        in_specs=[
            pl.BlockSpec((1, tm, d), lambda n, t: (n, t, 0)),
            pl.BlockSpec((1, tm, d), lambda n, t: (n, jnp.minimum(t + 1, nt - 1), 0)),
            pl.BlockSpec((1, N_MOD, r, d), lambda n, t: (n, 0, 0, 0)),
            pl.BlockSpec((1, 2, r, d), lambda n, t: (n, 0, 0, 0)),
            _const_spec((1, d)),
            _const_spec((1, d)),
            _const_spec((d, ATT_WIDTH)),
            _const_spec((d, 2 * ATT_WIDTH)),
        ],
        out_specs=(
            pl.BlockSpec((1, tm, ATT_WIDTH), lambda n, t: (n, t, 0)),
            pl.BlockSpec((1, tm, 2 * ATT_WIDTH), lambda n, t: (n, t, 0)),
            pl.BlockSpec((1, tm, 2 * ATT_WIDTH), lambda n, t: (n, jnp.maximum(t - (nt - tail_tiles), 0), 0)),
        ),
        scratch_shapes=[pltpu.VMEM((2, tm, d), BF), pltpu.VMEM((2, tm, d), BF)],
        compiler_params=_params(("arbitrary", "arbitrary")),
        name="qkv_proj",
    )(x, x, mods, modkv, nq, nkv, wq, wkv)


def _rel_bucket(dist):
    max_exact = N_BUCKETS // 2
    dd = jnp.maximum(dist, 1).astype(F32)
    large = max_exact + (jnp.log(dd / max_exact) / math.log(MAX_DISTANCE / max_exact)
                         * (N_BUCKETS - max_exact)).astype(jnp.int32)
    large = jnp.minimum(large, N_BUCKETS - 1)
    return jnp.where(dist < max_exact, dist, large)


def _prompt_bucket_table(dil):
    qi = jnp.arange(SPAN, dtype=jnp.int32)[:, None]
    ki = jnp.arange(2 * SPAN, dtype=jnp.int32)[None, :]
    m = qi + SPAN - ki
    valid = (m >= 0) & (m <= SPAN)
    return jnp.where(valid, _rel_bucket(jnp.clip(m, 0, SPAN) * dil), -1).astype(jnp.int32)


def _sample_bucket_tables(dil, width, t_new):
    row = jnp.arange(SAMPLE_QROWS, dtype=jnp.int32)[:, None]
    t = (row % SUBLANES) % t_new
    tn = jnp.arange(SUBLANES, dtype=jnp.int32)[None, :]

    def table(dist, ok):
        ok = ok & (dist >= 0) & (dist % dil == 0) & (dist // dil <= SPAN)
        return jnp.where(ok, _rel_bucket(jnp.clip(dist, 0, SPAN * dil)), -1).astype(jnp.int32)

    old = table(width + t - jnp.arange(width, dtype=jnp.int32)[None, :], True)
    new = table(t - tn, tn < t_new)
    return old, new


def _bias_from_buckets(bkt, relb_ref, head):
    acc = jnp.full(bkt.shape, NEG_INF, F32)
    for b in range(N_BUCKETS):
        acc = jnp.where(bkt == b, relb_ref[b, head], acc)
    return acc


def _pattn_kernel(relb_ref, bkt_ref, q0_ref, q1_ref, k0_ref, k1_ref, v0_ref, v1_ref,
                  o0_ref, o1_ref, l0_ref, l1_ref, bias_ref, carry_ref, *, g, dil, nbk):
    outs = (o0_ref, o1_ref, l0_ref, l1_ref)

    def put(k, rows, val):
        outs[k][0, rows, :] = val

    _pattn_blocks(relb_ref, bkt_ref, q0_ref, q1_ref, k0_ref, k1_ref, v0_ref, v1_ref, put, bias_ref, carry_ref,
                  g=g, dil=dil, nbk=nbk)


def _pattn_oproj_kernel(relb_ref, bkt_ref, q0_ref, q1_ref, k0_ref, k1_ref, v0_ref, v1_ref, x_ref, mod_ref, *refs,
                        g, dil, nbk):
    nblk = ATT_WIDTH // LANES
    halves = nblk // N_GROUPS
    nother = nblk - halves
    o_other, l_other = refs[:nother], refs[nother:2 * nother]
    wo_ref, out_ref, bias_ref, carry_ref, stage_ref, att_ref = refs[2 * nother:]
    rows_step = nbk * dil * SPAN

    def put(k, rows, val):
        stage_ref[k, rows, :] = val

    _pattn_blocks(relb_ref, bkt_ref, q0_ref, q1_ref, k0_ref, k1_ref, v0_ref, v1_ref, put, bias_ref, carry_ref,
                  g=g, dil=dil, nbk=nbk)

    others = [gi for gi in range(N_GROUPS) if gi != g]
    rc = BF16_ROWS

    def body(i, c):
        rows = pl.ds(pl.multiple_of(i * rc, rc), rc)
        for hf in range(halves):
            os_ = {g: stage_ref[hf, rows, :]}
            ls_ = {g: stage_ref[halves + hf, rows, :]}
            for n, gi in enumerate(others):
                os_[gi] = o_other[n * halves + hf][0, rows, :]
                ls_[gi] = l_other[n * halves + hf][0, rows, :]
            mx = functools.reduce(jnp.maximum, ls_.values())
            es = {gi: jnp.exp(v - mx) for gi, v in ls_.items()}
            inv = 1.0 / functools.reduce(lambda a, b: a + b, [es[gi] for gi in range(N_GROUPS)])
            for gi in range(N_GROUPS):
                k = gi * halves + hf
                att_ref[rows, k * LANES:(k + 1) * LANES] = (os_[gi] * (es[gi] * inv)).astype(BF)
        return c

    lax.fori_loop(0, rows_step // rc, body, 0, unroll=2)
    out_ref[0] = x_ref[0] + mod_ref[0, 2] * _dot(att_ref[...], wo_ref[...])


def _pattn_blocks(relb_ref, bkt_ref, q0_ref, q1_ref, k0_ref, k1_ref, v0_ref, v1_ref, put, bias_ref, carry_ref,
                  *, g, dil, nbk):
    b = pl.program_id(0)
    i = pl.program_id(1)
    hpg = HEADS_PER_GROUP
    span_rows = SPAN * dil

    @pl.when((b == 0) & (i == 0))
    def _():
        bkt = bkt_ref[...]
        no_prev = lax.broadcasted_iota(jnp.int32, (SPAN, 2 * SPAN), 1) < SPAN
        for h in range(hpg):
            bias = _bias_from_buckets(bkt, relb_ref, g * hpg + h)
            bias_ref[0, h] = bias
            bias_ref[1, h] = jnp.where(no_prev, NEG_INF, bias)

    @pl.when(i == 0)
    def _():
        carry_ref[...] = jnp.zeros(carry_ref.shape, BF)

    lanehead = lax.broadcasted_iota(jnp.int32, (SPAN, GROUP_WIDTH), 1) >> HEAD_SHIFT
    first = jnp.where(i == 0, 1, 0)
    rd_slot = lax.rem(i, 2)
    last_k, last_v = {}, {}

    def block(j, r):
        rows = pl.ds(j * span_rows + r, SPAN, stride=dil)
        cur = lambda r0, r1: jnp.concatenate([r0[0, rows, :], r1[0, rows, :]], axis=1)
        if j == 0:
            kprev, vprev = carry_ref[rd_slot, 0, r], carry_ref[rd_slot, 1, r]
        else:
            kprev, vprev = last_k[r], last_v[r]
        kcur = cur(k0_ref, k1_ref).astype(BF)
        vcur = cur(v0_ref, v1_ref).astype(BF)
        if j == nbk - 1:
            carry_ref[1 - rd_slot, 0, r] = kcur
            carry_ref[1 - rd_slot, 1, r] = vcur
        else:
            last_k[r], last_v[r] = kcur, vcur
        q = cur(q0_ref, q1_ref) * (HEAD_DIM ** -0.5)
        lhs = jnp.concatenate([jnp.where(lanehead == h, q, 0.0).astype(BF) for h in range(hpg)], axis=0)
        s = _dot_nt(lhs, jnp.concatenate([kprev, kcur], axis=0))

        ps, ms, sums = [], [], []
        for h in range(hpg):
            logit = s[h * SPAN:(h + 1) * SPAN] + (bias_ref[first, h] if j == 0 else bias_ref[0, h])
            m = jnp.max(logit, axis=-1, keepdims=True)
            e = jnp.exp(logit - m)
            ps.append(e.astype(BF))
            ms.append(m)
            sums.append(jnp.sum(e, axis=-1, keepdims=True))
        pv = _dot(jnp.concatenate(ps, axis=0), jnp.concatenate([vprev, vcur], axis=0))

        o = jnp.zeros((SPAN, GROUP_WIDTH), F32)
        l = jnp.zeros((SPAN, GROUP_WIDTH), F32)
        for h in range(hpg):
            o = jnp.where(lanehead == h, pv[h * SPAN:(h + 1) * SPAN] * (1.0 / sums[h]), o)
            l = jnp.where(lanehead == h, ms[h] + jnp.log(sums[h]), l)
        put(0, rows, o[:, :LANES])
        put(1, rows, o[:, LANES:])
        put(2, rows, l[:, :LANES])
        put(3, rows, l[:, LANES:])

    for j in range(nbk):
        for r in range(dil):
            block(j, r)


def _pattn_call(q, kv, rel_bias, g, dil, finish=None):
    bsz, s, _ = q.shape
    nbk = max(1, PATTN_ROWS // (dil * SPAN))
    rows = nbk * dil * SPAN
    assert s % rows == 0 and GROUP_WIDTH == 2 * LANES
    blk = (1, rows, LANES)
    kcol = 2 * g
    vcol = ATT_WIDTH // LANES + 2 * g
    ospec = pl.BlockSpec(blk, lambda b, i: (b, i, 0))
    in_specs = [
        pl.BlockSpec(memory_space=pltpu.SMEM),
        _const_spec((SPAN, 2 * SPAN)),
        pl.BlockSpec(blk, lambda b, i: (b, i, kcol)),
        pl.BlockSpec(blk, lambda b, i: (b, i, kcol + 1)),
        pl.BlockSpec(blk, lambda b, i: (b, i, kcol)),
        pl.BlockSpec(blk, lambda b, i: (b, i, kcol + 1)),
        pl.BlockSpec(blk, lambda b, i: (b, i, vcol)),
        pl.BlockSpec(blk, lambda b, i: (b, i, vcol + 1)),
    ]
    args = [rel_bias, _prompt_bucket_table(dil), q, q, kv, kv, kv, kv]
    scratch = [pltpu.VMEM((2, HEADS_PER_GROUP, SPAN, 2 * SPAN), F32),
               pltpu.VMEM((2, 2, dil, SPAN, GROUP_WIDTH), BF)]
    if finish is None:
        out = jax.ShapeDtypeStruct((bsz, s, LANES), F32)
        res = pl.pallas_call(
            functools.partial(_pattn_kernel, g=g, dil=dil, nbk=nbk),
            out_shape=(out,) * 4,
            grid=(bsz, s // rows),
            in_specs=in_specs,
            out_specs=(ospec,) * 4,
            scratch_shapes=scratch,
            compiler_params=_params(("arbitrary", "arbitrary")),
            name=f"prompt_attn_g{g}",
        )(*args)
        return list(res[:2]), list(res[2:])

    x, mods, o_other, l_other, wo = finish
    d = x.shape[-1]
    r = mods.shape[2]
    assert r == 1
    xspec = pl.BlockSpec((1, rows, d), lambda b, i: (b, i, 0))
    return pl.pallas_call(
        functools.partial(_pattn_oproj_kernel, g=g, dil=dil, nbk=nbk),
        out_shape=jax.ShapeDtypeStruct((bsz, s, d), F32),
        grid=(bsz, s // rows),
        in_specs=in_specs + [xspec, pl.BlockSpec((1, N_MOD, r, d), lambda b, i: (b, 0, 0, 0))]
        + [ospec] * (len(o_other) + len(l_other)) + [_const_spec((ATT_WIDTH, d))],
        out_specs=xspec,
        scratch_shapes=scratch + [pltpu.VMEM((4, rows, LANES), F32), pltpu.VMEM((rows, ATT_WIDTH), BF)],
        compiler_params=_params(("arbitrary", "arbitrary")),
        name=f"prompt_attn_g{g}_out_proj",
    )(*args, x, mods, *o_other, *l_other, wo)


def _sattn_kernel(relb_ref, bo0_ref, bo1_ref, bo2_ref, bn_ref, q_ref, kvn_ref, kvt_ref, c0_ref, c1_ref, c2_ref,
                  o0_ref, o1_ref, o2_ref, l0_ref, l1_ref, l2_ref, n0_ref, n1_ref, n2_ref,
                  bias0_ref, bias1_ref, bias2_ref, biasn_ref, tail_ref, *, t_new):
    hpg = HEADS_PER_GROUP
    gw = GROUP_WIDTH
    rows = SAMPLE_QROWS
    bo_refs = (bo0_ref, bo1_ref, bo2_ref)
    bias_refs = (bias0_ref, bias1_ref, bias2_ref)
    caches = (c0_ref, c1_ref, c2_ref)
    outs = ((o0_ref, l0_ref, n0_ref), (o1_ref, l1_ref, n1_ref), (o2_ref, l2_ref, n2_ref))

    @pl.when(pl.program_id(0) == 0)
    def _():
        tail_ref[...] = jnp.zeros(tail_ref.shape, F32)
        for g in range(N_GROUPS):
            for h in range(hpg):
                sl = slice(h * SUBLANES, (h + 1) * SUBLANES)
                bias_refs[g][sl, :] = _bias_from_buckets(bo_refs[g][sl, :], relb_ref, g * hpg + h)
                biasn_ref[g, sl, :] = _bias_from_buckets(bn_ref[g, sl, :], relb_ref, g * hpg + h)

    lanehead = lax.broadcasted_iota(jnp.int32, (rows, gw), 1) >> HEAD_SHIFT
    rowhead = lax.broadcasted_iota(jnp.int32, (rows, gw), 0) >> SUBLANE_SHIFT
    own = lanehead == rowhead
    tail_lane = lax.broadcasted_iota(jnp.int32, (gw, LANES), 1) >= LANES - t_new

    for g in range(N_GROUPS):
        cref = caches[g]
        o_ref, l_ref, n_ref = outs[g]
        width = cref.shape[-1]
        qm = jnp.where(own, q_ref[0, :, g * gw:(g + 1) * gw] * (HEAD_DIM ** -0.5), 0.0).astype(BF)
        knew = kvn_ref[0, :, g * gw:(g + 1) * gw].astype(BF)
        vnew = kvn_ref[0, :, ATT_WIDTH + g * gw:ATT_WIDTH + (g + 1) * gw].astype(BF)
        lo = _dot(qm, cref[0, 0].astype(BF)) + bias_refs[g][...]
        ln = _dot_nt(qm, knew) + biasn_ref[g]
        m = jnp.maximum(jnp.max(lo, axis=-1, keepdims=True), jnp.max(ln, axis=-1, keepdims=True))
        eo = jnp.exp(lo - m)
        en = jnp.exp(ln - m)
        ssum = jnp.sum(eo, axis=-1, keepdims=True) + jnp.sum(en, axis=-1, keepdims=True)
        pv = _dot_nt(eo.astype(BF), cref[0, 1].astype(BF)) + _dot(en.astype(BF), vnew)
        om = jnp.where(own, pv * (1.0 / ssum), 0.0)
        lm = jnp.where(own, m + jnp.log(ssum), 0.0)
        o_acc = om[0:SUBLANES]
        l_acc = lm[0:SUBLANES]
        for h in range(1, hpg):
            o_acc = o_acc + om[h * SUBLANES:(h + 1) * SUBLANES]
            l_acc = l_acc + lm[h * SUBLANES:(h + 1) * SUBLANES]
        o_ref[0] = o_acc
        l_ref[0] = l_acc

        for kv in range(2):
            rolled = pltpu.roll(cref[0, kv], width - t_new, 1)
            tail_ref[:, 0:SUBLANES] = kvt_ref[0, kv * ATT_WIDTH + g * gw:kv * ATT_WIDTH + (g + 1) * gw, :]
            tail = pltpu.roll(tail_ref[...], LANES - t_new, 1)
            if width > LANES:
                n_ref[0, kv, :, 0:width - LANES] = rolled[:, 0:width - LANES]
            n_ref[0, kv, :, width - LANES:width] = jnp.where(tail_lane, tail, rolled[:, width - LANES:width])


def _sattn_call(q, kvn, caches, rel_bias):
    n, t_new, _ = q.shape
    gw = GROUP_WIDTH
    hpg = HEADS_PER_GROUP
    assert t_new <= SUBLANES
    views, bkt_old, bkt_new = [], [], []
    for g, (window, dil) in enumerate(DIL_GROUPS):
        width = caches[g].shape[1]
        assert width == window and width % LANES == 0
        views.append(jnp.transpose(caches[g], (0, 2, 3, 4, 1)).reshape(n, 2, gw, width))
        old, new = _sample_bucket_tables(dil, width, t_new)
        bkt_old.append(old)
        bkt_new.append(new)
    pad = SUBLANES - t_new
    qp = jnp.tile(jnp.pad(q, ((0, 0), (0, pad), (0, 0))), (1, hpg, 1))
    kvp = jnp.pad(kvn, ((0, 0), (0, pad), (0, 0)))
    kvt = jnp.swapaxes(kvp, 1, 2)
    out = jax.ShapeDtypeStruct((n, SUBLANES, gw), F32)
    oblk = pl.BlockSpec((1, SUBLANES, gw), lambda i: (i, 0, 0))
    cspecs = [pl.BlockSpec((1, 2, gw, v.shape[-1]), lambda i: (i, 0, 0, 0)) for v in views]
    res = pl.pallas_call(
        functools.partial(_sattn_kernel, t_new=t_new),
        out_shape=(out,) * 6 + tuple(jax.ShapeDtypeStruct(v.shape, F32) for v in views),
        grid=(n,),
        in_specs=[pl.BlockSpec(memory_space=pltpu.SMEM)]
        + [_const_spec(b.shape) for b in bkt_old]
        + [
            _const_spec((N_GROUPS, SAMPLE_QROWS, SUBLANES)),
            pl.BlockSpec((1, SAMPLE_QROWS, ATT_WIDTH), lambda i: (i, 0, 0)),
            pl.BlockSpec((1, SUBLANES, 2 * ATT_WIDTH), lambda i: (i, 0, 0)),
            pl.BlockSpec((1, 2 * ATT_WIDTH, SUBLANES), lambda i: (i, 0, 0)),
        ]
        + cspecs,
        out_specs=(oblk,) * 6 + tuple(cspecs),
        scratch_shapes=[pltpu.VMEM(b.shape, F32) for b in bkt_old]
        + [pltpu.VMEM((N_GROUPS, SAMPLE_QROWS, SUBLANES), F32), pltpu.VMEM((gw, LANES), F32)],
        compiler_params=_params(("arbitrary",)),
        name="sample_attn",
    )(rel_bias, *bkt_old, jnp.stack(bkt_new), qp, kvp, kvt, *views)
    new_caches = [jnp.transpose(c.reshape(n, 2, hpg, HEAD_DIM, c.shape[-1]), (0, 4, 1, 2, 3)) for c in res[6:]]
    return [a[:, :t_new] for a in res[:3]], [a[:, :t_new] for a in res[3:6]], new_caches


def _oproj_kernel(x_ref, mod_ref, *refs, tm):
    nblk = ATT_WIDTH // LANES
    halves = nblk // N_GROUPS
    o_refs, l_refs = refs[:nblk], refs[nblk:2 * nblk]
    wo_ref, out_ref, att_ref = refs[2 * nblk:]
    rc = BF16_ROWS

    def body(i, c):
        r0 = pl.multiple_of(i * rc, rc)
        for hf in range(halves):
            blks = [g * halves + hf for g in range(N_GROUPS)]
            ls = [l_refs[k][0, pl.ds(r0, rc), :] for k in blks]
            mx = functools.reduce(jnp.maximum, ls)
            es = [jnp.exp(v - mx) for v in ls]
            inv = 1.0 / functools.reduce(lambda a, b: a + b, es)
            for k, e in zip(blks, es):
                att_ref[pl.ds(r0, rc), k * LANES:(k + 1) * LANES] = (o_refs[k][0, pl.ds(r0, rc), :] * (e * inv)).astype(BF)
        return c

    lax.fori_loop(0, tm // rc, body, 0, unroll=2)
    out_ref[0] = x_ref[0] + mod_ref[0, 2] * _dot(att_ref[...], wo_ref[...])


def _oproj_call(x, mods, os_, ls_, wo, *, tm):
    ns, s, d = x.shape
    r = mods.shape[2]
    kern = functools.partial(_oproj_kernel, tm=tm)
    gspec = pl.BlockSpec((1, tm, LANES), lambda n, t: (n, t, 0))
    return pl.pallas_call(
        kern,
        out_shape=jax.ShapeDtypeStruct((ns, s, d), F32),
        grid=(ns, s // tm),
        in_specs=[
            pl.BlockSpec((1, tm, d), lambda n, t: (n, t, 0)),
            pl.BlockSpec((1, N_MOD, r, d), lambda n, t: (n, 0, 0, 0)),
        ] + [gspec] * (2 * ATT_WIDTH // LANES) + [_const_spec((ATT_WIDTH, d))],
        out_specs=pl.BlockSpec((1, tm, d), lambda n, t: (n, t, 0)),
        scratch_shapes=[pltpu.VMEM((tm, ATT_WIDTH), BF)],
        compiler_params=_params(("arbitrary", "arbitrary")),
        name="attn_out_proj",
    )(x, mods, *os_, *ls_, wo)


def _trunk(x, mods, modkv, hist_a, hist_f, w, attn_fn, *, tm, rd, kv_tail_rows):
    row = lambda v: v.reshape(1, -1)
    x, st_a = _conformer_call(x, mods[0], hist_a, row(w["norm_mix"][0]), w["a_w1"], row(w["a_b1"][0]),
                              w["a_dw"][0], row(w["a_dwb"][0]), row(w["a_ln_g"][0]), row(w["a_ln_b"][0]),
                              w["a_w2"], row(w["a_b2"][0]), tm=tm, rd=rd)
    x, st_f0 = _ffn_call(x, mods[0], hist_f[0], row(w["norm_ffn"][0]), w["f_wup"][0], w["f_cw"][0],
                         row(w["f_cb"][0]), w["f_wdown"][0], row(w["norm_f"]), tm=tm, rd=rd, final_norm=False)
    q, kv, kv_tail = _qkv_call(x, mods[1], modkv, row(w["norm_mix"][1]), row(w["norm_kv"]), w["w_q"], w["w_kv"],
                               tm=tm, tail_rows=kv_tail_rows)
    x, attn_extra = attn_fn(q, kv, x, mods[1], w["w_o"])
    y, st_f1 = _ffn_call(x, mods[1], hist_f[1], row(w["norm_ffn"][1]), w["f_wup"][1], w["f_cw"][1],
                         row(w["f_cb"][1]), w["f_wdown"][1], row(w["norm_f"]), tm=tm, rd=rd, final_norm=True)
    return y, kv_tail, st_a, [st_f0, st_f1], attn_extra


def kernel(x_prompt, x_sample, cache_kv_w128, cache_kv_w512, cache_kv_w2048, state_conv_a, state_conv_ffn, c_prompt, c_sample, w_mod, b_mod, norm_mix, norm_ffn, a_w1, a_b1, a_dw, a_dwb, a_ln_g, a_ln_b, a_w2, a_b2, w_mod_kv, b_mod_kv, norm_kv, w_kv, w_q, w_o, rel_bias, f_wup, f_cw, f_cb, f_wdown, norm_f):
    d = D_MODEL
    bsz, seq, _ = x_prompt.shape
    nseq, t_new, _ = x_sample.shape
    caches = (cache_kv_w128, cache_kv_w512, cache_kv_w2048)
    hpg = HEADS_PER_GROUP

    w = dict(norm_mix=norm_mix, norm_ffn=norm_ffn, a_w1=a_w1[0].astype(BF), a_b1=a_b1, a_dw=a_dw, a_dwb=a_dwb,
             a_ln_g=a_ln_g, a_ln_b=a_ln_b, a_w2=a_w2[0].astype(BF), a_b2=a_b2, norm_kv=norm_kv,
             w_kv=w_kv.astype(BF), w_q=w_q[0].astype(BF), w_o=w_o[0].astype(BF),
             f_wup=[f_wup[l].astype(BF) for l in range(DEPTH)], f_cw=f_cw, f_cb=f_cb,
             f_wdown=[f_wdown[l].astype(BF) for l in range(DEPTH)], norm_f=norm_f)

    n_c = bsz + nseq
    mp = _round_up(n_c, SUBLANES)
    c_all = jnp.pad(jnp.concatenate([c_prompt, c_sample], axis=0), ((0, mp - n_c), (0, 0)))
    mod = _mod_call(c_all, w_mod, b_mod.reshape(DEPTH, 1, N_MOD * d), tn=1536)
    modkv = _mod_call(c_all, w_mod_kv[None], b_mod_kv.reshape(1, 1, 2 * d), tn=1024)[0]

    mods_p = [mod[l, :bsz].reshape(bsz, N_MOD, 1, d) for l in range(DEPTH)]
    modkv_p = modkv[:bsz].reshape(bsz, 2, 1, d)
    hist_a_p = jnp.zeros((bsz, CONV_A_WIDTH - 1, d), F32)
    hist_f_p = [jnp.zeros((bsz, CONV_F_WIDTH - 1, 2 * D_FF), F32)] * DEPTH

    def prompt_attn(q, kv, x, mods1, wo):
        res = [_pattn_call(q, kv, rel_bias, g, dil) for g, (_, dil) in enumerate(DIL_GROUPS) if g > 0]
        finish = (x, mods1, [a for r in res for a in r[0]], [a for r in res for a in r[1]], wo)
        return _pattn_call(q, kv, rel_bias, 0, DIL_GROUPS[0][1], finish=finish), None

    wmax = max(c.shape[1] for c in caches)
    assert seq >= wmax
    y_p, kv_tail, st_a_p, st_f_p, _ = _trunk(x_prompt, mods_p, modkv_p, hist_a_p, hist_f_p, w, prompt_attn,
                                             tm=TM_PROMPT, rd=1, kv_tail_rows=wmax)
    kv_tail = kv_tail.reshape(bsz, wmax, 2, N_HEADS, HEAD_DIM)
    kv_bufs_p = [kv_tail[:, wmax - c.shape[1]:, :, g * hpg:(g + 1) * hpg] for g, c in enumerate(caches)]
    conv_a_p = st_a_p[None]
    conv_f_p = jnp.stack(st_f_p)

    rows = t_new * nseq
    tmaj = lambda v: jnp.swapaxes(v, 0, 1).reshape(1, -1, v.shape[-1])
    smaj = lambda v, c: jnp.swapaxes(v.reshape(-1, nseq, c), 0, 1)

    def per_row(m, k):
        return jnp.tile(jnp.swapaxes(m.reshape(nseq, k, d), 0, 1), (1, t_new, 1))[None]

    mods_s = [per_row(mod[l, bsz:n_c], N_MOD) for l in range(DEPTH)]
    modkv_s = per_row(modkv[bsz:n_c], 2)
    hist_a_s = tmaj(state_conv_a[0])
    hist_f_s = [tmaj(state_conv_ffn[l]) for l in range(DEPTH)]

    def sample_attn(q, kv, x, mods1, wo):
        os_, ls_, new_caches = _sattn_call(smaj(q, ATT_WIDTH), smaj(kv, 2 * ATT_WIDTH), caches, rel_bias)
        halves = lambda vs: [tmaj(a[..., c0:c0 + LANES]) for a in vs for c0 in range(0, GROUP_WIDTH, LANES)]
        return _oproj_call(x, mods1, halves(os_), halves(ls_), wo, tm=rows), new_caches

    y_s, _, st_a_s, st_f_s, kv_bufs_s = _trunk(tmaj(x_sample), mods_s, modkv_s, hist_a_s, hist_f_s, w,
                                               sample_attn, tm=rows, rd=nseq, kv_tail_rows=rows)
    y_s = smaj(y_s, d)
    conv_a_s = smaj(st_a_s, d)[None]
    conv_f_s = jnp.stack([smaj(s, 2 * D_FF) for s in st_f_s])

    return (y_p, y_s, kv_bufs_p[0], kv_bufs_p[1], kv_bufs_p[2], conv_a_p, conv_f_p,
            kv_bufs_s[0], kv_bufs_s[1], kv_bufs_s[2], conv_a_s, conv_f_s)
```

```python
import functools
import math

import jax
import jax.numpy as jnp
from jax import lax
from jax.experimental import pallas as pl
from jax.experimental.pallas import tpu as pltpu

D_MODEL = 1024
DEPTH = 2
HEAD_DIM = 64
HEADS_PER_GROUP = 4
DIL_GROUPS = ((128, 1), (512, 4), (2048, 16))
N_GROUPS = len(DIL_GROUPS)
N_HEADS = HEADS_PER_GROUP * N_GROUPS
ATT_WIDTH = N_HEADS * HEAD_DIM
GROUP_WIDTH = HEADS_PER_GROUP * HEAD_DIM
SPAN = 128
N_BUCKETS = 32
MAX_DISTANCE = 2048
CONV_A_WIDTH = 31
D_FF = 2816
CONV_F_WIDTH = 3
N_MOD = 6
EPS = 1e-6
LN_EPS = 1e-5
NEG_INF = -1e30

BF = jnp.bfloat16
F32 = jnp.float32

HEAD_SHIFT = HEAD_DIM.bit_length() - 1
SUBLANES = 8
SUBLANE_SHIFT = SUBLANES.bit_length() - 1
LANES = 128
BF16_ROWS = 16
ROW_UNROLL = 4
VMEM_LIMIT = 56 * 1024 * 1024

TM_PROMPT = 512
PATTN_ROWS = 1024
CONF_ROWS = 512
FF_ROWS = 32
FF_SUB = 256
SAMPLE_QROWS = HEADS_PER_GROUP * SUBLANES


def _round_up(a, b):
    return -(-a // b) * b


def _params(sem):
    return pltpu.CompilerParams(dimension_semantics=sem, vmem_limit_bytes=VMEM_LIMIT)


def _const_spec(shape):
    nd = len(shape)
    return pl.BlockSpec(shape, lambda *_: (0,) * nd, pipeline_mode=pl.Buffered(1))


def _dot(a, b):
    return jnp.dot(a, b, preferred_element_type=F32)


def _dot_nt(a, b):
    return lax.dot_general(a, b, (((1,), (1,)), ((), ())), preferred_element_type=F32)


def _sigmoid(v):
    return 1.0 / (1.0 + jnp.exp(-v))


def _mod_rows(mod_ref, idx, r0, rows, per_row):
    if per_row:
        return mod_ref[0, idx, pl.ds(r0, rows), :]
    return mod_ref[0, idx]


def _rms_mod_rows(x_ref, rows, targets, per_row, *, slot=None, straight=False):
    rc = BF16_ROWS
    lo, hi = rows

    def chunk(r0):
        x = x_ref[0, pl.ds(r0, rc), :]
        xn = x * lax.rsqrt(jnp.mean(x * x, axis=-1, keepdims=True) + EPS)
        for gain_ref, mod_ref, i_sh, i_sc, dst_ref in targets:
            sh = _mod_rows(mod_ref, i_sh, r0, rc, per_row)
            sc = _mod_rows(mod_ref, i_sc, r0, rc, per_row)
            idx = (pl.ds(r0, rc), slice(None))
            dst_ref[idx if slot is None else (slot,) + idx] = ((xn * gain_ref[...]) * (1.0 + sc) + sh).astype(BF)

    if straight:
        for r0 in range(lo, hi, rc):
            chunk(r0)
        return

    def body(i, c):
        chunk(pl.multiple_of(lo + i * rc, rc))
        return c

    lax.fori_loop(0, (hi - lo) // rc, body, 0, unroll=ROW_UNROLL)


def _tap_conv(src_ref, blk, w_ref, offsets, r0, rows):
    cols = slice(blk * LANES, (blk + 1) * LANES)
    acc = None
    for k, off in enumerate(offsets):
        term = w_ref[k:k + 1, cols] * src_ref[blk, r0 + off:r0 + off + rows, :]
        acc = term if acc is None else acc + term
    return acc


def _mod_kernel(c_ref, w_ref, b_ref, o_ref):
    c = c_ref[...]
    a = (c * _sigmoid(c)).astype(BF)
    o_ref[0] = _dot(a, w_ref[0].astype(BF)) + b_ref[0]


def _mod_call(c_all, w, b, tn):
    nl, d, n = w.shape
    mp = c_all.shape[0]
    return pl.pallas_call(
        _mod_kernel,
        out_shape=jax.ShapeDtypeStruct((nl, mp, n), F32),
        grid=(nl, n // tn),
        in_specs=[
            pl.BlockSpec((mp, d), lambda l, j: (0, 0)),
            pl.BlockSpec((1, d, tn), lambda l, j: (l, 0, j)),
            pl.BlockSpec((1, 1, tn), lambda l, j: (l, 0, j)),
        ],
        out_specs=pl.BlockSpec((1, mp, tn), lambda l, j: (l, 0, j)),
        compiler_params=_params(("arbitrary", "arbitrary")),
        name="adaln_mod",
    )(c_all, w, b)


def _conformer_kernel(x_ref, mod_ref, hist_ref, nrm_ref, w1_ref, b1_ref, dw_ref, dwb_ref, lng_ref, lnb_ref,
                      w2_ref, b2_ref, o_ref, st_ref, h_ref, h2_ref, u_ref, full_ref, y_ref, *, tm, rd, per_row):
    d = D_MODEL
    hh = (CONV_A_WIDTH - 1) * rd
    hp = _round_up(hh, SUBLANES)
    t = pl.program_id(1)

    nblk = d // LANES
    lanes = lambda blk: slice(blk * LANES, (blk + 1) * LANES)

    @pl.when(t == 0)
    def _():
        full_ref[:, 0:hp, :] = jnp.zeros((nblk, hp, LANES), F32)
        for blk in range(nblk):
            full_ref[blk, hp - hh:hp, :] = hist_ref[0, :, lanes(blk)]

    rb = min(tm, CONF_ROWS)
    rg, cg = 16, 512
    rcv = 64
    offsets = [hp - hh + k * rd for k in range(CONV_A_WIDTH)]
    for rs in range(0, tm, rb):
        _rms_mod_rows(x_ref, (rs, rs + rb), [(nrm_ref, mod_ref, 0, 1, h_ref)], per_row, straight=True)
        u_ref[rs:rs + rb, :] = _dot(h_ref[rs:rs + rb, :], w1_ref[...])

        for r0 in range(rs, rs + rb, rg):
            for c0 in range(0, d, cg):
                a = u_ref[r0:r0 + rg, c0:c0 + cg] + b1_ref[:, c0:c0 + cg]
                g = u_ref[r0:r0 + rg, d + c0:d + c0 + cg] + b1_ref[:, d + c0:d + c0 + cg]
                glu = a * _sigmoid(g)
                for j in range(cg // LANES):
                    full_ref[c0 // LANES + j, hp + r0:hp + r0 + rg, :] = glu[:, lanes(j)]

        for r0 in range(rs, rs + rb, rcv):
            for blk in range(nblk):
                y_ref[r0:r0 + rcv, lanes(blk)] = (_tap_conv(full_ref, blk, dw_ref, offsets, r0, rcv)
                                                  + dwb_ref[:, lanes(blk)])

        for r0 in range(rs, rs + rb, BF16_ROWS):
            y = y_ref[r0:r0 + BF16_ROWS, :]
            mu = jnp.mean(y, axis=-1, keepdims=True)
            dv = y - mu
            var = jnp.mean(dv * dv, axis=-1, keepdims=True)
            yn = dv * lax.rsqrt(var + LN_EPS) * lng_ref[...] + lnb_ref[...]
            h2_ref[r0:r0 + BF16_ROWS, :] = (yn * _sigmoid(yn)).astype(BF)

        out = _dot(h2_ref[rs:rs + rb, :], w2_ref[...]) + b2_ref[...]
        gate = mod_ref[0, 2, rs:rs + rb, :] if per_row else mod_ref[0, 2]
        o_ref[0, rs:rs + rb, :] = x_ref[0, rs:rs + rb, :] + gate * out

    for blk in range(nblk):
        new_hist = full_ref[blk, hp + tm - hh:hp + tm, :]
        st_ref[0, :, lanes(blk)] = new_hist
        full_ref[blk, hp - hh:hp, :] = new_hist


def _conformer_call(x, mods, hist, nrm, w1, b1, dw, dwb, lng, lnb, w2, b2, *, tm, rd):
    ns, s, d = x.shape
    r = mods.shape[2]
    hh = hist.shape[1]
    hp = _round_up(hh, SUBLANES)
    per_row = r > 1
    assert s % tm == 0 and (not per_row or (r == tm and s == tm))
    kern = functools.partial(_conformer_kernel, tm=tm, rd=rd, per_row=per_row)
    return pl.pallas_call(
        kern,
        out_shape=(jax.ShapeDtypeStruct((ns, s, d), F32), jax.ShapeDtypeStruct((ns, hh, d), F32)),
        grid=(ns, s // tm),
        in_specs=[
            pl.BlockSpec((1, tm, d), lambda n, t: (n, t, 0)),
            pl.BlockSpec((1, N_MOD, r, d), lambda n, t: (n, 0, 0, 0)),
            pl.BlockSpec((1, hh, d), lambda n, t: (n, 0, 0)),
            _const_spec((1, d)),
            _const_spec((d, 2 * d)),
            _const_spec((1, 2 * d)),
            _const_spec((CONV_A_WIDTH, d)),
            _const_spec((1, d)),
            _const_spec((1, d)),
            _const_spec((1, d)),
            _const_spec((d, d)),
            _const_spec((1, d)),
        ],
        out_specs=(
            pl.BlockSpec((1, tm, d), lambda n, t: (n, t, 0)),
            pl.BlockSpec((1, hh, d), lambda n, t: (n, 0, 0)),
        ),
        scratch_shapes=[
            pltpu.VMEM((tm, d), BF),
            pltpu.VMEM((tm, d), BF),
            pltpu.VMEM((tm, 2 * d), F32),
            pltpu.VMEM((d // LANES, hp + tm, LANES), F32),
            pltpu.VMEM((tm, d), F32),
        ],
        compiler_params=_params(("arbitrary", "arbitrary")),
        name="conformer_mixer",
    )(x, mods, hist, nrm, w1, b1, dw, dwb, lng, lnb, w2, b2)


def _ffn_kernel(x_ref, xnext_ref, mod_ref, hist_ref, nrm_ref, wu_ref, cw_ref, cb_ref, wd_ref, nf_ref, o_ref, st_ref,
                h_ref, ubuf_ref, carry_ref, act_ref, part_ref, *, tm, rd, per_row, final_norm):
    f = D_FF
    hh = (CONV_F_WIDTH - 1) * rd
    hp = _round_up(hh, SUBLANES)
    t = pl.program_id(1)

    cur = lax.rem(t, 2)
    norm_targets = [(nrm_ref, mod_ref, 3, 4, h_ref)]

    @pl.when(t == 0)
    def _():
        _rms_mod_rows(x_ref, (0, tm), norm_targets, per_row, slot=0)

    nblk = 2 * f // LANES
    lanes = lambda blk: slice(blk * LANES, (blk + 1) * LANES)

    @pl.when(t == 0)
    def _():
        carry_ref[...] = jnp.zeros((nblk, hp, LANES), F32)
        for blk in range(nblk):
            carry_ref[blk, hp - hh:hp, :] = hist_ref[0, :, lanes(blk)]

    ubuf_ref[:, 0:hp, :] = carry_ref[...]

    rc = FF_ROWS
    offsets = [hp - hh + k * rd for k in range(CONV_F_WIDTH)]
    for c0 in range(0, f, FF_SUB):
        for half in range(2):
            col = half * f + c0
            u = _dot(h_ref[cur], wu_ref[:, col:col + FF_SUB])
            for j in range(FF_SUB // LANES):
                ubuf_ref[col // LANES + j, hp:hp + tm, :] = u[:, lanes(j)]
        for r0 in range(0, tm, rc):
            for cc0 in range(c0, c0 + FF_SUB, LANES):
                ys = []
                for half in range(2):
                    blk = (half * f + cc0) // LANES
                    ys.append(_tap_conv(ubuf_ref, blk, cw_ref, offsets, r0, rc) + cb_ref[:, lanes(blk)])
                yg, yv = ys
                act_ref[r0:r0 + rc, cc0:cc0 + LANES] = (yg * _sigmoid(yg) * yv).astype(BF)

    for blk in range(nblk):
        st_ref[0, :, lanes(blk)] = ubuf_ref[blk, hp + tm - hh:hp + tm, :]
    carry_ref[...] = ubuf_ref[:, tm:tm + hp, :]

    _rms_mod_rows(xnext_ref, (0, tm), norm_targets, per_row, slot=1 - cur, straight=True)
    xo = x_ref[0] + mod_ref[0, 5] * _dot(act_ref[...], wd_ref[...])
    if not final_norm:
        o_ref[0] = xo
        return
    part_ref[...] = xo
    rows = BF16_ROWS

    for r0 in range(0, tm, rows):
        v = part_ref[r0:r0 + rows, :]
        o_ref[0, r0:r0 + rows, :] = v * lax.rsqrt(jnp.mean(v * v, axis=-1, keepdims=True) + EPS) * nf_ref[...]


def _ffn_call(x, mods, hist, nrm, wup, cw, cb, wdown, nf, *, tm, rd, final_norm):
    ns, s, d = x.shape
    r = mods.shape[2]
    hh = hist.shape[1]
    hp = _round_up(hh, SUBLANES)
    f = wdown.shape[0]
    per_row = r > 1
    nt = s // tm
    assert f == D_FF and f % FF_SUB == 0 and s % tm == 0 and (not per_row or (r == tm and s == tm))
    kern = functools.partial(_ffn_kernel, tm=tm, rd=rd, per_row=per_row, final_norm=final_norm)
    return pl.pallas_call(
        kern,
        out_shape=(jax.ShapeDtypeStruct((ns, s, d), F32), jax.ShapeDtypeStruct((ns, hh, 2 * f), F32)),
        grid=(ns, nt),
        in_specs=[
            pl.BlockSpec((1, tm, d), lambda n, t: (n, t, 0)),
            pl.BlockSpec((1, tm, d), lambda n, t: (n, jnp.minimum(t + 1, nt - 1), 0)),
            pl.BlockSpec((1, N_MOD, r, d), lambda n, t: (n, 0, 0, 0)),
            pl.BlockSpec((1, hh, 2 * f), lambda n, t: (n, 0, 0)),
            _const_spec((1, d)),
            _const_spec((d, 2 * f)),
            _const_spec((CONV_F_WIDTH, 2 * f)),
            _const_spec((1, 2 * f)),
            _const_spec((f, d)),
            _const_spec((1, d)),
        ],
        out_specs=(
            pl.BlockSpec((1, tm, d), lambda n, t: (n, t, 0)),
            pl.BlockSpec((1, hh, 2 * f), lambda n, t: (n, 0, 0)),
        ),
        scratch_shapes=[
            pltpu.VMEM((2, tm, d), BF),
            pltpu.VMEM((2 * f // LANES, hp + tm, LANES), F32),
            pltpu.VMEM((2 * f // LANES, hp, LANES), F32),
            pltpu.VMEM((tm, f), BF),
            pltpu.VMEM((tm, d), F32),
        ],
        compiler_params=_params(("arbitrary", "arbitrary")),
        name="conv_ffn",
    )(x, x, mods, hist, nrm, wup, cw, cb, wdown, nf)


def _qkv_kernel(x_ref, xnext_ref, mod_ref, modkv_ref, nq_ref, nkv_ref, wq_ref, wkv_ref, q_ref, kv_ref, kvtail_ref,
                hq_ref, hkv_ref, *, tm, per_row):
    t = pl.program_id(1)
    cur = lax.rem(t, 2)
    norm_targets = [(nq_ref, mod_ref, 0, 1, hq_ref), (nkv_ref, modkv_ref, 0, 1, hkv_ref)]

    @pl.when(t == 0)
    def _():
        _rms_mod_rows(x_ref, (0, tm), norm_targets, per_row, slot=0)

    q_ref[0] = _dot(hq_ref[cur], wq_ref[...])
    _rms_mod_rows(xnext_ref, (0, tm), norm_targets, per_row, slot=1 - cur, straight=True)
    kv = _dot(hkv_ref[cur], wkv_ref[...])
    kv_ref[0] = kv
    kvtail_ref[0] = kv


def _qkv_call(x, mods, modkv, nq, nkv, wq, wkv, *, tm, tail_rows):
    ns, s, d = x.shape
    r = mods.shape[2]
    per_row = r > 1
    nt = s // tm
    tail_tiles = -(-tail_rows // tm)
    assert tail_tiles <= nt and tail_rows % tm == 0
    kern = functools.partial(_qkv_kernel, tm=tm, per_row=per_row)
    return pl.pallas_call(
        kern,
        out_shape=(jax.ShapeDtypeStruct((ns, s, ATT_WIDTH), F32), jax.ShapeDtypeStruct((ns, s, 2 * ATT_WIDTH), F32),
                   jax.ShapeDtypeStruct((ns, tail_rows, 2 * ATT_WIDTH), F32)),
        grid=(ns, nt),
        in_specs=[
            pl.BlockSpec((1, tm, d), lambda n, t: (n, t, 0)),
            pl.BlockSpec((1, tm, d), lambda n, t: (n, jnp.minimum(t + 1, nt - 1), 0)),
            pl.BlockSpec((1, N_MOD, r, d), lambda n, t: (n, 0, 0, 0)),
            pl.BlockSpec((1, 2, r, d), lambda n, t: (n, 0, 0, 0)),
            _const_spec((1, d)),
            _const_spec((1, d)),
            _const_spec((d, ATT_WIDTH)),
            _const_spec((d, 2 * ATT_WIDTH)),
        ],
        out_specs=(
            pl.BlockSpec((1, tm, ATT_WIDTH), lambda n, t: (n, t, 0)),
            pl.BlockSpec((1, tm, 2 * ATT_WIDTH), lambda n, t: (n, t, 0)),
            pl.BlockSpec((1, tm, 2 * ATT_WIDTH), lambda n, t: (n, jnp.maximum(t - (nt - tail_tiles), 0), 0)),
        ),
        scratch_shapes=[pltpu.VMEM((2, tm, d), BF), pltpu.VMEM((2, tm, d), BF)],
        compiler_params=_params(("arbitrary", "arbitrary")),
        name="qkv_proj",
    )(x, x, mods, modkv, nq, nkv, wq, wkv)


def _rel_bucket(dist):
    max_exact = N_BUCKETS // 2
    dd = jnp.maximum(dist, 1).astype(F32)
    large = max_exact + (jnp.log(dd / max_exact) / math.log(MAX_DISTANCE / max_exact)
                         * (N_BUCKETS - max_exact)).astype(jnp.int32)
    large = jnp.minimum(large, N_BUCKETS - 1)
    return jnp.where(dist < max_exact, dist, large)


def _prompt_bucket_table(dil):
    qi = jnp.arange(SPAN, dtype=jnp.int32)[:, None]
    ki = jnp.arange(2 * SPAN, dtype=jnp.int32)[None, :]
    m = qi + SPAN - ki
    valid = (m >= 0) & (m <= SPAN)
    return jnp.where(valid, _rel_bucket(jnp.clip(m, 0, SPAN) * dil), -1).astype(jnp.int32)


def _sample_bucket_tables(dil, width, t_new):
    row = jnp.arange(SAMPLE_QROWS, dtype=jnp.int32)[:, None]
    t = (row % SUBLANES) % t_new
    tn = jnp.arange(SUBLANES, dtype=jnp.int32)[None, :]

    def table(dist, ok):
        ok = ok & (dist >= 0) & (dist % dil == 0) & (dist // dil <= SPAN)
        return jnp.where(ok, _rel_bucket(jnp.clip(dist, 0, SPAN * dil)), -1).astype(jnp.int32)

    old = table(width + t - jnp.arange(width, dtype=jnp.int32)[None, :], True)
    new = table(t - tn, tn < t_new)
    return old, new


def _bias_from_buckets(bkt, relb_ref, head):
    acc = jnp.full(bkt.shape, NEG_INF, F32)
    for b in range(N_BUCKETS):
        acc = jnp.where(bkt == b, relb_ref[b, head], acc)
    return acc


def _pattn_kernel(relb_ref, bkt_ref, q0_ref, q1_ref, k0_ref, k1_ref, v0_ref, v1_ref,
                  o0_ref, o1_ref, l0_ref, l1_ref, bias_ref, carry_ref, *, g, dil, nbk):
    outs = (o0_ref, o1_ref, l0_ref, l1_ref)

    def put(k, rows, val):
        outs[k][0, rows, :] = val

    _pattn_blocks(relb_ref, bkt_ref, q0_ref, q1_ref, k0_ref, k1_ref, v0_ref, v1_ref, put, bias_ref, carry_ref,
                  g=g, dil=dil, nbk=nbk)


def _pattn_oproj_kernel(relb_ref, bkt_ref, q0_ref, q1_ref, k0_ref, k1_ref, v0_ref, v1_ref, x_ref, mod_ref, *refs,
                        g, dil, nbk):
    nblk = ATT_WIDTH // LANES
    halves = nblk // N_GROUPS
    nother = nblk - halves
    o_other, l_other = refs[:nother], refs[nother:2 * nother]
    wo_ref, out_ref, bias_ref, carry_ref, stage_ref, att_ref = refs[2 * nother:]
    rows_step = nbk * dil * SPAN

    def put(k, rows, val):
        stage_ref[k, rows, :] = val

    _pattn_blocks(relb_ref, bkt_ref, q0_ref, q1_ref, k0_ref, k1_ref, v0_ref, v1_ref, put, bias_ref, carry_ref,
                  g=g, dil=dil, nbk=nbk)

    others = [gi for gi in range(N_GROUPS) if gi != g]
    rc = BF16_ROWS

    def body(i, c):
        rows = pl.ds(pl.multiple_of(i * rc, rc), rc)
        for hf in range(halves):
            os_ = {g: stage_ref[hf, rows, :]}
            ls_ = {g: stage_ref[halves + hf, rows, :]}
            for n, gi in enumerate(others):
                os_[gi] = o_other[n * halves + hf][0, rows, :]
                ls_[gi] = l_other[n * halves + hf][0, rows, :]
            mx = functools.reduce(jnp.maximum, ls_.values())
            es = {gi: jnp.exp(v - mx) for gi, v in ls_.items()}
            inv = 1.0 / functools.reduce(lambda a, b: a + b, [es[gi] for gi in range(N_GROUPS)])
            for gi in range(N_GROUPS):
                k = gi * halves + hf
                att_ref[rows, k * LANES:(k + 1) * LANES] = (os_[gi] * (es[gi] * inv)).astype(BF)
        return c

    lax.fori_loop(0, rows_step // rc, body, 0, unroll=2)
    out_ref[0] = x_ref[0] + mod_ref[0, 2] * _dot(att_ref[...], wo_ref[...])


def _pattn_blocks(relb_ref, bkt_ref, q0_ref, q1_ref, k0_ref, k1_ref, v0_ref, v1_ref, put, bias_ref, carry_ref,
                  *, g, dil, nbk):
    b = pl.program_id(0)
    i = pl.program_id(1)
    hpg = HEADS_PER_GROUP
    span_rows = SPAN * dil

    @pl.when((b == 0) & (i == 0))
    def _():
        bkt = bkt_ref[...]
        no_prev = lax.broadcasted_iota(jnp.int32, (SPAN, 2 * SPAN), 1) < SPAN
        for h in range(hpg):
            bias = _bias_from_buckets(bkt, relb_ref, g * hpg + h)
            bias_ref[0, h] = bias
            bias_ref[1, h] = jnp.where(no_prev, NEG_INF, bias)

    @pl.when(i == 0)
    def _():
        carry_ref[...] = jnp.zeros(carry_ref.shape, BF)

    lanehead = lax.broadcasted_iota(jnp.int32, (SPAN, GROUP_WIDTH), 1) >> HEAD_SHIFT
    first = jnp.where(i == 0, 1, 0)
    rd_slot = lax.rem(i, 2)
    last_k, last_v = {}, {}

    def block(j, r):
        rows = pl.ds(j * span_rows + r, SPAN, stride=dil)
        cur = lambda r0, r1: jnp.concatenate([r0[0, rows, :], r1[0, rows, :]], axis=1)
        if j == 0:
            kprev, vprev = carry_ref[rd_slot, 0, r], carry_ref[rd_slot, 1, r]
        else:
            kprev, vprev = last_k[r], last_v[r]
        kcur = cur(k0_ref, k1_ref).astype(BF)
        vcur = cur(v0_ref, v1_ref).astype(BF)
        if j == nbk - 1:
            carry_ref[1 - rd_slot, 0, r] = kcur
            carry_ref[1 - rd_slot, 1, r] = vcur
        else:
            last_k[r], last_v[r] = kcur, vcur
        q = cur(q0_ref, q1_ref) * (HEAD_DIM ** -0.5)
        lhs = jnp.concatenate([jnp.where(lanehead == h, q, 0.0).astype(BF) for h in range(hpg)], axis=0)
        s = _dot_nt(lhs, jnp.concatenate([kprev, kcur], axis=0))

        ps, ms, sums = [], [], []
        for h in range(hpg):
            logit = s[h * SPAN:(h + 1) * SPAN] + (bias_ref[first, h] if j == 0 else bias_ref[0, h])
            m = jnp.max(logit, axis=-1, keepdims=True)
            e = jnp.exp(logit - m)
            ps.append(e.astype(BF))
            ms.append(m)
            sums.append(jnp.sum(e, axis=-1, keepdims=True))
        pv = _dot(jnp.concatenate(ps, axis=0), jnp.concatenate([vprev, vcur], axis=0))

        o = jnp.zeros((SPAN, GROUP_WIDTH), F32)
        l = jnp.zeros((SPAN, GROUP_WIDTH), F32)
        for h in range(hpg):
            o = jnp.where(lanehead == h, pv[h * SPAN:(h + 1) * SPAN] * (1.0 / sums[h]), o)
            l = jnp.where(lanehead == h, ms[h] + jnp.log(sums[h]), l)
        put(0, rows, o[:, :LANES])
        put(1, rows, o[:, LANES:])
        put(2, rows, l[:, :LANES])
        put(3, rows, l[:, LANES:])

    for j in range(nbk):
        for r in range(dil):
            block(j, r)


def _pattn_call(q, kv, rel_bias, g, dil, finish=None):
    bsz, s, _ = q.shape
    nbk = max(1, PATTN_ROWS // (dil * SPAN))
    rows = nbk * dil * SPAN
    assert s % rows == 0 and GROUP_WIDTH == 2 * LANES
    blk = (1, rows, LANES)
    kcol = 2 * g
    vcol = ATT_WIDTH // LANES + 2 * g
    ospec = pl.BlockSpec(blk, lambda b, i: (b, i, 0))
    in_specs = [
        pl.BlockSpec(memory_space=pltpu.SMEM),
        _const_spec((SPAN, 2 * SPAN)),
        pl.BlockSpec(blk, lambda b, i: (b, i, kcol)),
        pl.BlockSpec(blk, lambda b, i: (b, i, kcol + 1)),
        pl.BlockSpec(blk, lambda b, i: (b, i, kcol)),
        pl.BlockSpec(blk, lambda b, i: (b, i, kcol + 1)),
        pl.BlockSpec(blk, lambda b, i: (b, i, vcol)),
        pl.BlockSpec(blk, lambda b, i: (b, i, vcol + 1)),
    ]
    args = [rel_bias, _prompt_bucket_table(dil), q, q, kv, kv, kv, kv]
    scratch = [pltpu.VMEM((2, HEADS_PER_GROUP, SPAN, 2 * SPAN), F32),
               pltpu.VMEM((2, 2, dil, SPAN, GROUP_WIDTH), BF)]
    if finish is None:
        out = jax.ShapeDtypeStruct((bsz, s, LANES), F32)
        res = pl.pallas_call(
            functools.partial(_pattn_kernel, g=g, dil=dil, nbk=nbk),
            out_shape=(out,) * 4,
            grid=(bsz, s // rows),
            in_specs=in_specs,
            out_specs=(ospec,) * 4,
            scratch_shapes=scratch,
            compiler_params=_params(("arbitrary", "arbitrary")),
            name=f"prompt_attn_g{g}",
        )(*args)
        return list(res[:2]), list(res[2:])

    x, mods, o_other, l_other, wo = finish
    d = x.shape[-1]
    r = mods.shape[2]
    assert r == 1
    xspec = pl.BlockSpec((1, rows, d), lambda b, i: (b, i, 0))
    return pl.pallas_call(
        functools.partial(_pattn_oproj_kernel, g=g, dil=dil, nbk=nbk),
        out_shape=jax.ShapeDtypeStruct((bsz, s, d), F32),
        grid=(bsz, s // rows),
        in_specs=in_specs + [xspec, pl.BlockSpec((1, N_MOD, r, d), lambda b, i: (b, 0, 0, 0))]
        + [ospec] * (len(o_other) + len(l_other)) + [_const_spec((ATT_WIDTH, d))],
        out_specs=xspec,
        scratch_shapes=scratch + [pltpu.VMEM((4, rows, LANES), F32), pltpu.VMEM((rows, ATT_WIDTH), BF)],
        compiler_params=_params(("arbitrary", "arbitrary")),
        name=f"prompt_attn_g{g}_out_proj",
    )(*args, x, mods, *o_other, *l_other, wo)


def _sattn_kernel(relb_ref, bo0_ref, bo1_ref, bo2_ref, bn_ref, q_ref, kvn_ref, kvt_ref, c0_ref, c1_ref, c2_ref,
                  o0_ref, o1_ref, o2_ref, l0_ref, l1_ref, l2_ref, n0_ref, n1_ref, n2_ref,
                  bias0_ref, bias1_ref, bias2_ref, biasn_ref, tail_ref, *, t_new):
    hpg = HEADS_PER_GROUP
    gw = GROUP_WIDTH
    rows = SAMPLE_QROWS
    bo_refs = (bo0_ref, bo1_ref, bo2_ref)
    bias_refs = (bias0_ref, bias1_ref, bias2_ref)
    caches = (c0_ref, c1_ref, c2_ref)
    outs = ((o0_ref, l0_ref, n0_ref), (o1_ref, l1_ref, n1_ref), (o2_ref, l2_ref, n2_ref))

    @pl.when(pl.program_id(0) == 0)
    def _():
        tail_ref[...] = jnp.zeros(tail_ref.shape, F32)
        for g in range(N_GROUPS):
            for h in range(hpg):
                sl = slice(h * SUBLANES, (h + 1) * SUBLANES)
                bias_refs[g][sl, :] = _bias_from_buckets(bo_refs[g][sl, :], relb_ref, g * hpg + h)
                biasn_ref[g, sl, :] = _bias_from_buckets(bn_ref[g, sl, :], relb_ref, g * hpg + h)

    lanehead = lax.broadcasted_iota(jnp.int32, (rows, gw), 1) >> HEAD_SHIFT
    rowhead = lax.broadcasted_iota(jnp.int32, (rows, gw), 0) >> SUBLANE_SHIFT
    own = lanehead == rowhead
    tail_lane = lax.broadcasted_iota(jnp.int32, (gw, LANES), 1) >= LANES - t_new

    for g in range(N_GROUPS):
        cref = caches[g]
        o_ref, l_ref, n_ref = outs[g]
        width = cref.shape[-1]
        qm = jnp.where(own, q_ref[0, :, g * gw:(g + 1) * gw] * (HEAD_DIM ** -0.5), 0.0).astype(BF)
        knew = kvn_ref[0, :, g * gw:(g + 1) * gw].astype(BF)
        vnew = kvn_ref[0, :, ATT_WIDTH + g * gw:ATT_WIDTH + (g + 1) * gw].astype(BF)
        lo = _dot(qm, cref[0, 0].astype(BF)) + bias_refs[g][...]
        ln = _dot_nt(qm, knew) + biasn_ref[g]
        m = jnp.maximum(jnp.max(lo, axis=-1, keepdims=True), jnp.max(ln, axis=-1, keepdims=True))
        eo = jnp.exp(lo - m)
        en = jnp.exp(ln - m)
        ssum = jnp.sum(eo, axis=-1, keepdims=True) + jnp.sum(en, axis=-1, keepdims=True)
        pv = _dot_nt(eo.astype(BF), cref[0, 1].astype(BF)) + _dot(en.astype(BF), vnew)
        om = jnp.where(own, pv * (1.0 / ssum), 0.0)
        lm = jnp.where(own, m + jnp.log(ssum), 0.0)
        o_acc = om[0:SUBLANES]
        l_acc = lm[0:SUBLANES]
        for h in range(1, hpg):
            o_acc = o_acc + om[h * SUBLANES:(h + 1) * SUBLANES]
            l_acc = l_acc + lm[h * SUBLANES:(h + 1) * SUBLANES]
        o_ref[0] = o_acc
        l_ref[0] = l_acc

        for kv in range(2):
            rolled = pltpu.roll(cref[0, kv], width - t_new, 1)
            tail_ref[:, 0:SUBLANES] = kvt_ref[0, kv * ATT_WIDTH + g * gw:kv * ATT_WIDTH + (g + 1) * gw, :]
            tail = pltpu.roll(tail_ref[...], LANES - t_new, 1)
            if width > LANES:
                n_ref[0, kv, :, 0:width - LANES] = rolled[:, 0:width - LANES]
            n_ref[0, kv, :, width - LANES:width] = jnp.where(tail_lane, tail, rolled[:, width - LANES:width])


def _sattn_call(q, kvn, caches, rel_bias):
    n, t_new, _ = q.shape
    gw = GROUP_WIDTH
    hpg = HEADS_PER_GROUP
    assert t_new <= SUBLANES
    views, bkt_old, bkt_new = [], [], []
    for g, (window, dil) in enumerate(DIL_GROUPS):
        width = caches[g].shape[1]
        assert width == window and width % LANES == 0
        views.append(jnp.transpose(caches[g], (0, 2, 3, 4, 1)).reshape(n, 2, gw, width))
        old, new = _sample_bucket_tables(dil, width, t_new)
        bkt_old.append(old)
        bkt_new.append(new)
    pad = SUBLANES - t_new
    qp = jnp.tile(jnp.pad(q, ((0, 0), (0, pad), (0, 0))), (1, hpg, 1))
    kvp = jnp.pad(kvn, ((0, 0), (0, pad), (0, 0)))
    kvt = jnp.swapaxes(kvp, 1, 2)
    out = jax.ShapeDtypeStruct((n, SUBLANES, gw), F32)
    oblk = pl.BlockSpec((1, SUBLANES, gw), lambda i: (i, 0, 0))
    cspecs = [pl.BlockSpec((1, 2, gw, v.shape[-1]), lambda i: (i, 0, 0, 0)) for v in views]
    res = pl.pallas_call(
        functools.partial(_sattn_kernel, t_new=t_new),
        out_shape=(out,) * 6 + tuple(jax.ShapeDtypeStruct(v.shape, F32) for v in views),
        grid=(n,),
        in_specs=[pl.BlockSpec(memory_space=pltpu.SMEM)]
        + [_const_spec(b.shape) for b in bkt_old]
        + [
            _const_spec((N_GROUPS, SAMPLE_QROWS, SUBLANES)),
            pl.BlockSpec((1, SAMPLE_QROWS, ATT_WIDTH), lambda i: (i, 0, 0)),
            pl.BlockSpec((1, SUBLANES, 2 * ATT_WIDTH), lambda i: (i, 0, 0)),
            pl.BlockSpec((1, 2 * ATT_WIDTH, SUBLANES), lambda i: (i, 0, 0)),
        ]
        + cspecs,
        out_specs=(oblk,) * 6 + tuple(cspecs),
        scratch_shapes=[pltpu.VMEM(b.shape, F32) for b in bkt_old]
        + [pltpu.VMEM((N_GROUPS, SAMPLE_QROWS, SUBLANES), F32), pltpu.VMEM((gw, LANES), F32)],
        compiler_params=_params(("arbitrary",)),
        name="sample_attn",
    )(rel_bias, *bkt_old, jnp.stack(bkt_new), qp, kvp, kvt, *views)
    new_caches = [jnp.transpose(c.reshape(n, 2, hpg, HEAD_DIM, c.shape[-1]), (0, 4, 1, 2, 3)) for c in res[6:]]
    return [a[:, :t_new] for a in res[:3]], [a[:, :t_new] for a in res[3:6]], new_caches


def _oproj_kernel(x_ref, mod_ref, *refs, tm):
    nblk = ATT_WIDTH // LANES
    halves = nblk // N_GROUPS
    o_refs, l_refs = refs[:nblk], refs[nblk:2 * nblk]
    wo_ref, out_ref, att_ref = refs[2 * nblk:]
    rc = BF16_ROWS

    def body(i, c):
        r0 = pl.multiple_of(i * rc, rc)
        for hf in range(halves):
            blks = [g * halves + hf for g in range(N_GROUPS)]
            ls = [l_refs[k][0, pl.ds(r0, rc), :] for k in blks]
            mx = functools.reduce(jnp.maximum, ls)
            es = [jnp.exp(v - mx) for v in ls]
            inv = 1.0 / functools.reduce(lambda a, b: a + b, es)
            for k, e in zip(blks, es):
                att_ref[pl.ds(r0, rc), k * LANES:(k + 1) * LANES] = (o_refs[k][0, pl.ds(r0, rc), :] * (e * inv)).astype(BF)
        return c

    lax.fori_loop(0, tm // rc, body, 0, unroll=2)
    out_ref[0] = x_ref[0] + mod_ref[0, 2] * _dot(att_ref[...], wo_ref[...])


def _oproj_call(x, mods, os_, ls_, wo, *, tm):
    ns, s, d = x.shape
    r = mods.shape[2]
    kern = functools.partial(_oproj_kernel, tm=tm)
    gspec = pl.BlockSpec((1, tm, LANES), lambda n, t: (n, t, 0))
    return pl.pallas_call(
        kern,
        out_shape=jax.ShapeDtypeStruct((ns, s, d), F32),
        grid=(ns, s // tm),
        in_specs=[
            pl.BlockSpec((1, tm, d), lambda n, t: (n, t, 0)),
            pl.BlockSpec((1, N_MOD, r, d), lambda n, t: (n, 0, 0, 0)),
        ] + [gspec] * (2 * ATT_WIDTH // LANES) + [_const_spec((ATT_WIDTH, d))],
        out_specs=pl.BlockSpec((1, tm, d), lambda n, t: (n, t, 0)),
        scratch_shapes=[pltpu.VMEM((tm, ATT_WIDTH), BF)],
        compiler_params=_params(("arbitrary", "arbitrary")),
        name="attn_out_proj",
    )(x, mods, *os_, *ls_, wo)


def _trunk(x, mods, modkv, hist_a, hist_f, w, attn_fn, *, tm, rd, kv_tail_rows):
    row = lambda v: v.reshape(1, -1)
    x, st_a = _conformer_call(x, mods[0], hist_a, row(w["norm_mix"][0]), w["a_w1"], row(w["a_b1"][0]),
                              w["a_dw"][0], row(w["a_dwb"][0]), row(w["a_ln_g"][0]), row(w["a_ln_b"][0]),
                              w["a_w2"], row(w["a_b2"][0]), tm=tm, rd=rd)
    x, st_f0 = _ffn_call(x, mods[0], hist_f[0], row(w["norm_ffn"][0]), w["f_wup"][0], w["f_cw"][0],
                         row(w["f_cb"][0]), w["f_wdown"][0], row(w["norm_f"]), tm=tm, rd=rd, final_norm=False)
    q, kv, kv_tail = _qkv_call(x, mods[1], modkv, row(w["norm_mix"][1]), row(w["norm_kv"]), w["w_q"], w["w_kv"],
                               tm=tm, tail_rows=kv_tail_rows)
    x, attn_extra = attn_fn(q, kv, x, mods[1], w["w_o"])
    y, st_f1 = _ffn_call(x, mods[1], hist_f[1], row(w["norm_ffn"][1]), w["f_wup"][1], w["f_cw"][1],
                         row(w["f_cb"][1]), w["f_wdown"][1], row(w["norm_f"]), tm=tm, rd=rd, final_norm=True)
    return y, kv_tail, st_a, [st_f0, st_f1], attn_extra


def kernel(x_prompt, x_sample, cache_kv_w128, cache_kv_w512, cache_kv_w2048, state_conv_a, state_conv_ffn, c_prompt, c_sample, w_mod, b_mod, norm_mix, norm_ffn, a_w1, a_b1, a_dw, a_dwb, a_ln_g, a_ln_b, a_w2, a_b2, w_mod_kv, b_mod_kv, norm_kv, w_kv, w_q, w_o, rel_bias, f_wup, f_cw, f_cb, f_wdown, norm_f):
    d = D_MODEL
    bsz, seq, _ = x_prompt.shape
    nseq, t_new, _ = x_sample.shape
    caches = (cache_kv_w128, cache_kv_w512, cache_kv_w2048)
    hpg = HEADS_PER_GROUP

    w = dict(norm_mix=norm_mix, norm_ffn=norm_ffn, a_w1=a_w1[0].astype(BF), a_b1=a_b1, a_dw=a_dw, a_dwb=a_dwb,
             a_ln_g=a_ln_g, a_ln_b=a_ln_b, a_w2=a_w2[0].astype(BF), a_b2=a_b2, norm_kv=norm_kv,
             w_kv=w_kv.astype(BF), w_q=w_q[0].astype(BF), w_o=w_o[0].astype(BF),
             f_wup=[f_wup[l].astype(BF) for l in range(DEPTH)], f_cw=f_cw, f_cb=f_cb,
             f_wdown=[f_wdown[l].astype(BF) for l in range(DEPTH)], norm_f=norm_f)

    n_c = bsz + nseq
    mp = _round_up(n_c, SUBLANES)
    c_all = jnp.pad(jnp.concatenate([c_prompt, c_sample], axis=0), ((0, mp - n_c), (0, 0)))
    mod = _mod_call(c_all, w_mod, b_mod.reshape(DEPTH, 1, N_MOD * d), tn=1536)
    modkv = _mod_call(c_all, w_mod_kv[None], b_mod_kv.reshape(1, 1, 2 * d), tn=1024)[0]

    mods_p = [mod[l, :bsz].reshape(bsz, N_MOD, 1, d) for l in range(DEPTH)]
    modkv_p = modkv[:bsz].reshape(bsz, 2, 1, d)
    hist_a_p = jnp.zeros((bsz, CONV_A_WIDTH - 1, d), F32)
    hist_f_p = [jnp.zeros((bsz, CONV_F_WIDTH - 1, 2 * D_FF), F32)] * DEPTH

    def prompt_attn(q, kv, x, mods1, wo):
        res = [_pattn_call(q, kv, rel_bias, g, dil) for g, (_, dil) in enumerate(DIL_GROUPS) if g > 0]
        finish = (x, mods1, [a for r in res for a in r[0]], [a for r in res for a in r[1]], wo)
        return _pattn_call(q, kv, rel_bias, 0, DIL_GROUPS[0][1], finish=finish), None

    wmax = max(c.shape[1] for c in caches)
    assert seq >= wmax
    y_p, kv_tail, st_a_p, st_f_p, _ = _trunk(x_prompt, mods_p, modkv_p, hist_a_p, hist_f_p, w, prompt_attn,
                                             tm=TM_PROMPT, rd=1, kv_tail_rows=wmax)
    kv_tail = kv_tail.reshape(bsz, wmax, 2, N_HEADS, HEAD_DIM)
    kv_bufs_p = [kv_tail[:, wmax - c.shape[1]:, :, g * hpg:(g + 1) * hpg] for g, c in enumerate(caches)]
    conv_a_p = st_a_p[None]
    conv_f_p = jnp.stack(st_f_p)

    rows = t_new * nseq
    tmaj = lambda v: jnp.swapaxes(v, 0, 1).reshape(1, -1, v.shape[-1])
    smaj = lambda v, c: jnp.swapaxes(v.reshape(-1, nseq, c), 0, 1)

    def per_row(m, k):
        return jnp.tile(jnp.swapaxes(m.reshape(nseq, k, d), 0, 1), (1, t_new, 1))[None]

    mods_s = [per_row(mod[l, bsz:n_c], N_MOD) for l in range(DEPTH)]
    modkv_s = per_row(modkv[bsz:n_c], 2)
    hist_a_s = tmaj(state_conv_a[0])
    hist_f_s = [tmaj(state_conv_ffn[l]) for l in range(DEPTH)]

    def sample_attn(q, kv, x, mods1, wo):
        os_, ls_, new_caches = _sattn_call(smaj(q, ATT_WIDTH), smaj(kv, 2 * ATT_WIDTH), caches, rel_bias)
        halves = lambda vs: [tmaj(a[..., c0:c0 + LANES]) for a in vs for c0 in range(0, GROUP_WIDTH, LANES)]
        return _oproj_call(x, mods1, halves(os_), halves(ls_), wo, tm=rows), new_caches

    y_s, _, st_a_s, st_f_s, kv_bufs_s = _trunk(tmaj(x_sample), mods_s, modkv_s, hist_a_s, hist_f_s, w,
                                               sample_attn, tm=rows, rd=nseq, kv_tail_rows=rows)
    y_s = smaj(y_s, d)
    conv_a_s = smaj(st_a_s, d)[None]
    conv_f_s = jnp.stack([smaj(s, 2 * D_FF) for s in st_f_s])

    return (y_p, y_s, kv_bufs_p[0], kv_bufs_p[1], kv_bufs_p[2], conv_a_p, conv_f_p,
            kv_bufs_s[0], kv_bufs_s[1], kv_bufs_s[2], conv_a_s, conv_f_s)
```

```python
import functools
import math

import jax
import jax.numpy as jnp
from jax import lax
from jax.experimental import pallas as pl
from jax.experimental.pallas import tpu as pltpu

D_MODEL = 1024
DEPTH = 2
HEAD_DIM = 64
HEADS_PER_GROUP = 4
DIL_GROUPS = ((128, 1), (512, 4), (2048, 16))
N_GROUPS = len(DIL_GROUPS)
N_HEADS = HEADS_PER_GROUP * N_GROUPS
ATT_WIDTH = N_HEADS * HEAD_DIM
GROUP_WIDTH = HEADS_PER_GROUP * HEAD_DIM
SPAN = 128
N_BUCKETS = 32
MAX_DISTANCE = 2048
CONV_A_WIDTH = 31
D_FF = 2816
CONV_F_WIDTH = 3
N_MOD = 6
EPS = 1e-6
LN_EPS = 1e-5
NEG_INF = -1e30

BF = jnp.bfloat16
F32 = jnp.float32

HEAD_SHIFT = HEAD_DIM.bit_length() - 1
SUBLANES = 8
SUBLANE_SHIFT = SUBLANES.bit_length() - 1
LANES = 128
BF16_ROWS = 16
ROW_UNROLL = 4
VMEM_LIMIT = 56 * 1024 * 1024

TM_PROMPT = 512
PATTN_ROWS = 1024
CONF_ROWS = 512
FF_ROWS = 32
FF_SUB = 256
SAMPLE_QROWS = HEADS_PER_GROUP * SUBLANES


def _round_up(a, b):
    return -(-a // b) * b


def _params(sem):
    return pltpu.CompilerParams(dimension_semantics=sem, vmem_limit_bytes=VMEM_LIMIT)


def _const_spec(shape):
    nd = len(shape)
    return pl.BlockSpec(shape, lambda *_: (0,) * nd, pipeline_mode=pl.Buffered(1))


def _dot(a, b):
    return jnp.dot(a, b, preferred_element_type=F32)


def _dot_nt(a, b):
    return lax.dot_general(a, b, (((1,), (1,)), ((), ())), preferred_element_type=F32)


def _sigmoid(v):
    return 1.0 / (1.0 + jnp.exp(-v))


def _mod_rows(mod_ref, idx, r0, rows, per_row):
    if per_row:
        return mod_ref[0, idx, pl.ds(r0, rows), :]
    return mod_ref[0, idx]


def _rms_mod_rows(x_ref, rows, targets, per_row, *, slot=None, straight=False):
    rc = BF16_ROWS
    lo, hi = rows

    def chunk(r0):
        x = x_ref[0, pl.ds(r0, rc), :]
        xn = x * lax.rsqrt(jnp.mean(x * x, axis=-1, keepdims=True) + EPS)
        for gain_ref, mod_ref, i_sh, i_sc, dst_ref in targets:
            sh = _mod_rows(mod_ref, i_sh, r0, rc, per_row)
            sc = _mod_rows(mod_ref, i_sc, r0, rc, per_row)
            idx = (pl.ds(r0, rc), slice(None))
            dst_ref[idx if slot is None else (slot,) + idx] = ((xn * gain_ref[...]) * (1.0 + sc) + sh).astype(BF)

    if straight:
        for r0 in range(lo, hi, rc):
            chunk(r0)
        return

    def body(i, c):
        chunk(pl.multiple_of(lo + i * rc, rc))
        return c

    lax.fori_loop(0, (hi - lo) // rc, body, 0, unroll=ROW_UNROLL)


def _tap_conv(src_ref, blk, w_ref, offsets, r0, rows):
    cols = slice(blk * LANES, (blk + 1) * LANES)
    acc = None
    for k, off in enumerate(offsets):
        term = w_ref[k:k + 1, cols] * src_ref[blk, r0 + off:r0 + off + rows, :]
        acc = term if acc is None else acc + term
    return acc


def _mod_kernel(c_ref, w_ref, b_ref, o_ref):
    c = c_ref[...]
    a = (c * _sigmoid(c)).astype(BF)
    o_ref[0] = _dot(a, w_ref[0].astype(BF)) + b_ref[0]


def _mod_call(c_all, w, b, tn):
    nl, d, n = w.shape
    mp = c_all.shape[0]
    return pl.pallas_call(
        _mod_kernel,
        out_shape=jax.ShapeDtypeStruct((nl, mp, n), F32),
        grid=(nl, n // tn),
        in_specs=[
            pl.BlockSpec((mp, d), lambda l, j: (0, 0)),
            pl.BlockSpec((1, d, tn), lambda l, j: (l, 0, j)),
            pl.BlockSpec((1, 1, tn), lambda l, j: (l, 0, j)),
        ],
        out_specs=pl.BlockSpec((1, mp, tn), lambda l, j: (l, 0, j)),
        compiler_params=_params(("arbitrary", "arbitrary")),
        name="adaln_mod",
    )(c_all, w, b)


def _conformer_kernel(x_ref, mod_ref, hist_ref, nrm_ref, w1_ref, b1_ref, dw_ref, dwb_ref, lng_ref, lnb_ref,
                      w2_ref, b2_ref, o_ref, st_ref, h_ref, h2_ref, u_ref, full_ref, y_ref, *, tm, rd, per_row):
    d = D_MODEL
    hh = (CONV_A_WIDTH - 1) * rd
    hp = _round_up(hh, SUBLANES)
    t = pl.program_id(1)

    nblk = d // LANES
    lanes = lambda blk: slice(blk * LANES, (blk + 1) * LANES)

    @pl.when(t == 0)
    def _():
        full_ref[:, 0:hp, :] = jnp.zeros((nblk, hp, LANES), F32)
        for blk in range(nblk):
            full_ref[blk, hp - hh:hp, :] = hist_ref[0, :, lanes(blk)]

    rb = min(tm, CONF_ROWS)
    rg, cg = 16, 512
    rcv = 64
    offsets = [hp - hh + k * rd for k in range(CONV_A_WIDTH)]
    for rs in range(0, tm, rb):
        _rms_mod_rows(x_ref, (rs, rs + rb), [(nrm_ref, mod_ref, 0, 1, h_ref)], per_row, straight=True)
        u_ref[rs:rs + rb, :] = _dot(h_ref[rs:rs + rb, :], w1_ref[...])

        for r0 in range(rs, rs + rb, rg):
            for c0 in range(0, d, cg):
                a = u_ref[r0:r0 + rg, c0:c0 + cg] + b1_ref[:, c0:c0 + cg]
                g = u_ref[r0:r0 + rg, d + c0:d + c0 + cg] + b1_ref[:, d + c0:d + c0 + cg]
                glu = a * _sigmoid(g)
                for j in range(cg // LANES):
                    full_ref[c0 // LANES + j, hp + r0:hp + r0 + rg, :] = glu[:, lanes(j)]

        for r0 in range(rs, rs + rb, rcv):
            for blk in range(nblk):
                y_ref[r0:r0 + rcv, lanes(blk)] = (_tap_conv(full_ref, blk, dw_ref, offsets, r0, rcv)
                                                  + dwb_ref[:, lanes(blk)])

        for r0 in range(rs, rs + rb, BF16_ROWS):
            y = y_ref[r0:r0 + BF16_ROWS, :]
            mu = jnp.mean(y, axis=-1, keepdims=True)
            dv = y - mu
            var = jnp.mean(dv * dv, axis=-1, keepdims=True)
            yn = dv * lax.rsqrt(var + LN_EPS) * lng_ref[...] + lnb_ref[...]
            h2_ref[r0:r0 + BF16_ROWS, :] = (yn * _sigmoid(yn)).astype(BF)

        out = _dot(h2_ref[rs:rs + rb, :], w2_ref[...]) + b2_ref[...]
        gate = mod_ref[0, 2, rs:rs + rb, :] if per_row else mod_ref[0, 2]
        o_ref[0, rs:rs + rb, :] = x_ref[0, rs:rs + rb, :] + gate * out

    for blk in range(nblk):
        new_hist = full_ref[blk, hp + tm - hh:hp + tm, :]
        st_ref[0, :, lanes(blk)] = new_hist
        full_ref[blk, hp - hh:hp, :] = new_hist


def _conformer_call(x, mods, hist, nrm, w1, b1, dw, dwb, lng, lnb, w2, b2, *, tm, rd):
    ns, s, d = x.shape
    r = mods.shape[2]
    hh = hist.shape[1]
    hp = _round_up(hh, SUBLANES)
    per_row = r > 1
    assert s % tm == 0 and (not per_row or (r == tm and s == tm))
    kern = functools.partial(_conformer_kernel, tm=tm, rd=rd, per_row=per_row)
    return pl.pallas_call(
        kern,
        out_shape=(jax.ShapeDtypeStruct((ns, s, d), F32), jax.ShapeDtypeStruct((ns, hh, d), F32)),
        grid=(ns, s // tm),
        in_specs=[
            pl.BlockSpec((1, tm, d), lambda n, t: (n, t, 0)),
            pl.BlockSpec((1, N_MOD, r, d), lambda n, t: (n, 0, 0, 0)),
            pl.BlockSpec((1, hh, d), lambda n, t: (n, 0, 0)),
            _const_spec((1, d)),
            _const_spec((d, 2 * d)),
            _const_spec((1, 2 * d)),
            _const_spec((CONV_A_WIDTH, d)),
            _const_spec((1, d)),
            _const_spec((1, d)),
            _const_spec((1, d)),
            _const_spec((d, d)),
            _const_spec((1, d)),
        ],
        out_specs=(
            pl.BlockSpec((1, tm, d), lambda n, t: (n, t, 0)),
            pl.BlockSpec((1, hh, d), lambda n, t: (n, 0, 0)),
        ),
        scratch_shapes=[
            pltpu.VMEM((tm, d), BF),
            pltpu.VMEM((tm, d), BF),
            pltpu.VMEM((tm, 2 * d), F32),
            pltpu.VMEM((d // LANES, hp + tm, LANES), F32),
            pltpu.VMEM((tm, d), F32),
        ],
        compiler_params=_params(("arbitrary", "arbitrary")),
        name="conformer_mixer",
    )(x, mods, hist, nrm, w1, b1, dw, dwb, lng, lnb, w2, b2)


def _ffn_kernel(x_ref, xnext_ref, mod_ref, hist_ref, nrm_ref, wu_ref, cw_ref, cb_ref, wd_ref, nf_ref, o_ref, st_ref,
                h_ref, ubuf_ref, carry_ref, act_ref, part_ref, *, tm, rd, per_row, final_norm):
    f = D_FF
    hh = (CONV_F_WIDTH - 1) * rd
    hp = _round_up(hh, SUBLANES)
    t = pl.program_id(1)

    cur = lax.rem(t, 2)
    norm_targets = [(nrm_ref, mod_ref, 3, 4, h_ref)]

    @pl.when(t == 0)
    def _():
        _rms_mod_rows(x_ref, (0, tm), norm_targets, per_row, slot=0)

    nblk = 2 * f // LANES
    lanes = lambda blk: slice(blk * LANES, (blk + 1) * LANES)

    @pl.when(t == 0)
    def _():
        carry_ref[...] = jnp.zeros((nblk, hp, LANES), F32)
        for blk in range(nblk):
            carry_ref[blk, hp - hh:hp, :] = hist_ref[0, :, lanes(blk)]

    ubuf_ref[:, 0:hp, :] = carry_ref[...]

    rc = FF_ROWS
    offsets = [hp - hh + k * rd for k in range(CONV_F_WIDTH)]
    for c0 in range(0, f, FF_SUB):
        for half in range(2):
            col = half * f + c0
            u = _dot(h_ref[cur], wu_ref[:, col:col + FF_SUB])
            for j in range(FF_SUB // LANES):
                ubuf_ref[col // LANES + j, hp:hp + tm, :] = u[:, lanes(j)]
        for r0 in range(0, tm, rc):
            for cc0 in range(c0, c0 + FF_SUB, LANES):
                ys = []
                for half in range(2):
                    blk = (half * f + cc0) // LANES
                    ys.append(_tap_conv(ubuf_ref, blk, cw_ref, offsets, r0, rc) + cb_ref[:, lanes(blk)])
                yg, yv = ys
                act_ref[r0:r0 + rc, cc0:cc0 + LANES] = (yg * _sigmoid(yg) * yv).astype(BF)

    for blk in range(nblk):
        st_ref[0, :, lanes(blk)] = ubuf_ref[blk, hp + tm - hh:hp + tm, :]
    carry_ref[...] = ubuf_ref[:, tm:tm + hp, :]

    _rms_mod_rows(xnext_ref, (0, tm), norm_targets, per_row, slot=1 - cur, straight=True)
    xo = x_ref[0] + mod_ref[0, 5] * _dot(act_ref[...], wd_ref[...])
    if not final_norm:
        o_ref[0] = xo
        return
    part_ref[...] = xo
    rows = BF16_ROWS

    for r0 in range(0, tm, rows):
        v = part_ref[r0:r0 + rows, :]
        o_ref[0, r0:r0 + rows, :] = v * lax.rsqrt(jnp.mean(v * v, axis=-1, keepdims=True) + EPS) * nf_ref[...]


def _ffn_call(x, mods, hist, nrm, wup, cw, cb, wdown, nf, *, tm, rd, final_norm):
    ns, s, d = x.shape
    r = mods.shape[2]
    hh = hist.shape[1]
    hp = _round_up(hh, SUBLANES)
    f = wdown.shape[0]
    per_row = r > 1
    nt = s // tm
    assert f == D_FF and f % FF_SUB == 0 and s % tm == 0 and (not per_row or (r == tm and s == tm))
    kern = functools.partial(_ffn_kernel, tm=tm, rd=rd, per_row=per_row, final_norm=final_norm)
    return pl.pallas_call(
        kern,
        out_shape=(jax.ShapeDtypeStruct((ns, s, d), F32), jax.ShapeDtypeStruct((ns, hh, 2 * f), F32)),
        grid=(ns, nt),
        in_specs=[
            pl.BlockSpec((1, tm, d), lambda n, t: (n, t, 0)),
            pl.BlockSpec((1, tm, d), lambda n, t: (n, jnp.minimum(t + 1, nt - 1), 0)),
            pl.BlockSpec((1, N_MOD, r, d), lambda n, t: (n, 0, 0, 0)),
            pl.BlockSpec((1, hh, 2 * f), lambda n, t: (n, 0, 0)),
            _const_spec((1, d)),
            _const_spec((d, 2 * f)),
            _const_spec((CONV_F_WIDTH, 2 * f)),
            _const_spec((1, 2 * f)),
            _const_spec((f, d)),
            _const_spec((1, d)),
        ],
        out_specs=(
            pl.BlockSpec((1, tm, d), lambda n, t: (n, t, 0)),
            pl.BlockSpec((1, hh, 2 * f), lambda n, t: (n, 0, 0)),
        ),
        scratch_shapes=[
            pltpu.VMEM((2, tm, d), BF),
            pltpu.VMEM((2 * f // LANES, hp + tm, LANES), F32),
            pltpu.VMEM((2 * f // LANES, hp, LANES), F32),
            pltpu.VMEM((tm, f), BF),
            pltpu.VMEM((tm, d), F32),
        ],
        compiler_params=_params(("arbitrary", "arbitrary")),
        name="conv_ffn",
    )(x, x, mods, hist, nrm, wup, cw, cb, wdown, nf)


def _qkv_kernel(x_ref, xnext_ref, mod_ref, modkv_ref, nq_ref, nkv_ref, wq_ref, wkv_ref, q_ref, kv_ref, kvtail_ref,
                hq_ref, hkv_ref, *, tm, per_row):
    t = pl.program_id(1)
    cur = lax.rem(t, 2)
    norm_targets = [(nq_ref, mod_ref, 0, 1, hq_ref), (nkv_ref, modkv_ref, 0, 1, hkv_ref)]

    @pl.when(t == 0)
    def _():
        _rms_mod_rows(x_ref, (0, tm), norm_targets, per_row, slot=0)

    q_ref[0] = _dot(hq_ref[cur], wq_ref[...])
    _rms_mod_rows(xnext_ref, (0, tm), norm_targets, per_row, slot=1 - cur, straight=True)
    kv = _dot(hkv_ref[cur], wkv_ref[...])
    kv_ref[0] = kv
    kvtail_ref[0] = kv


def _qkv_call(x, mods, modkv, nq, nkv, wq, wkv, *, tm, tail_rows):
    ns, s, d = x.shape
    r = mods.shape[2]
    per_row = r > 1
    nt = s // tm
    tail_tiles = -(-tail_rows // tm)
    assert tail_tiles <= nt and tail_rows % tm == 0
    kern = functools.partial(_qkv_kernel, tm=tm, per_row=per_row)
    return pl.pallas_call(
        kern,
        out_shape=(jax.ShapeDtypeStruct((ns, s, ATT_WIDTH), F32), jax.ShapeDtypeStruct((ns, s, 2 * ATT_WIDTH), F32),
                   jax.ShapeDtypeStruct((ns, tail_rows, 2 * ATT_WIDTH), F32)),
        grid=(ns, nt),
        in_specs=[
            pl.BlockSpec((1, tm, d), lambda n, t: (n, t, 0)),
            pl.BlockSpec((1, tm, d), lambda n, t: (n, jnp.minimum(t + 1, nt - 1), 0)),
            pl.BlockSpec((1, N_MOD, r, d), lambda n, t: (n, 0, 0, 0)),
            pl.BlockSpec((1, 2, r, d), lambda n, t: (n, 0, 0, 0)),
            _const_spec((1, d)),
            _const_spec((1, d)),
            _const_spec((d, ATT_WIDTH)),
            _const_spec((d, 2 * ATT_WIDTH)),
        ],
        out_specs=(
            pl.BlockSpec((1, tm, ATT_WIDTH), lambda n, t: (n, t, 0)),
            pl.BlockSpec((1, tm, 2 * ATT_WIDTH), lambda n, t: (n, t, 0)),
            pl.BlockSpec((1, tm, 2 * ATT_WIDTH), lambda n, t: (n, jnp.maximum(t - (nt - tail_tiles), 0), 0)),
        ),
        scratch_shapes=[pltpu.VMEM((2, tm, d), BF), pltpu.VMEM((2, tm, d), BF)],
        compiler_params=_params(("arbitrary", "arbitrary")),
        name="qkv_proj",
    )(x, x, mods, modkv, nq, nkv, wq, wkv)


def _rel_bucket(dist):
    max_exact = N_BUCKETS // 2
    dd = jnp.maximum(dist, 1).astype(F32)
    large = max_exact + (jnp.log(dd / max_exact) / math.log(MAX_DISTANCE / max_exact)
                         * (N_BUCKETS - max_exact)).astype(jnp.int32)
    large = jnp.minimum(large, N_BUCKETS - 1)
    return jnp.where(dist < max_exact, dist, large)


def _prompt_bucket_table(dil):
    qi = jnp.arange(SPAN, dtype=jnp.int32)[:, None]
    ki = jnp.arange(2 * SPAN, dtype=jnp.int32)[None, :]
    m = qi + SPAN - ki
    valid = (m >= 0) & (m <= SPAN)
    return jnp.where(valid, _rel_bucket(jnp.clip(m, 0, SPAN) * dil), -1).astype(jnp.int32)


def _sample_bucket_tables(dil, width, t_new):
    row = jnp.arange(SAMPLE_QROWS, dtype=jnp.int32)[:, None]
    t = (row % SUBLANES) % t_new
    tn = jnp.arange(SUBLANES, dtype=jnp.int32)[None, :]

    def table(dist, ok):
        ok = ok & (dist >= 0) & (dist % dil == 0) & (dist // dil <= SPAN)
        return jnp.where(ok, _rel_bucket(jnp.clip(dist, 0, SPAN * dil)), -1).astype(jnp.int32)

    old = table(width + t - jnp.arange(width, dtype=jnp.int32)[None, :], True)
    new = table(t - tn, tn < t_new)
    return old, new


def _bias_from_buckets(bkt, relb_ref, head):
    acc = jnp.full(bkt.shape, NEG_INF, F32)
    for b in range(N_BUCKETS):
        acc = jnp.where(bkt == b, relb_ref[b, head], acc)
    return acc


def _pattn_kernel(relb_ref, bkt_ref, q0_ref, q1_ref, k0_ref, k1_ref, v0_ref, v1_ref,
                  o0_ref, o1_ref, l0_ref, l1_ref, bias_ref, carry_ref, *, g, dil, nbk):
    outs = (o0_ref, o1_ref, l0_ref, l1_ref)

    def put(k, rows, val):
        outs[k][0, rows, :] = val

    _pattn_blocks(relb_ref, bkt_ref, q0_ref, q1_ref, k0_ref, k1_ref, v0_ref, v1_ref, put, bias_ref, carry_ref,
                  g=g, dil=dil, nbk=nbk)


def _pattn_oproj_kernel(relb_ref, bkt_ref, q0_ref, q1_ref, k0_ref, k1_ref, v0_ref, v1_ref, x_ref, mod_ref, *refs,
                        g, dil, nbk):
    nblk = ATT_WIDTH // LANES
    halves = nblk // N_GROUPS
    nother = nblk - halves
    o_other, l_other = refs[:nother], refs[nother:2 * nother]
    wo_ref, out_ref, bias_ref, carry_ref, stage_ref, att_ref = refs[2 * nother:]
    rows_step = nbk * dil * SPAN

    def put(k, rows, val):
        stage_ref[k, rows, :] = val

    _pattn_blocks(relb_ref, bkt_ref, q0_ref, q1_ref, k0_ref, k1_ref, v0_ref, v1_ref, put, bias_ref, carry_ref,
                  g=g, dil=dil, nbk=nbk)

    others = [gi for gi in range(N_GROUPS) if gi != g]
    rc = BF16_ROWS

    for r0 in range(0, rows_step, rc):
        rows = pl.ds(r0, rc)
        for hf in range(halves):
            os_ = {g: stage_ref[hf, rows, :]}
            ls_ = {g: stage_ref[halves + hf, rows, :]}
            for n, gi in enumerate(others):
                os_[gi] = o_other[n * halves + hf][0, rows, :]
                ls_[gi] = l_other[n * halves + hf][0, rows, :]
            mx = functools.reduce(jnp.maximum, ls_.values())
            es = {gi: jnp.exp(v - mx) for gi, v in ls_.items()}
            inv = 1.0 / functools.reduce(lambda a, b: a + b, [es[gi] for gi in range(N_GROUPS)])
            for gi in range(N_GROUPS):
                k = gi * halves + hf
                att_ref[rows, k * LANES:(k + 1) * LANES] = (os_[gi] * (es[gi] * inv)).astype(BF)

    out_ref[0] = x_ref[0] + mod_ref[0, 2] * _dot(att_ref[...], wo_ref[...])


def _pattn_blocks(relb_ref, bkt_ref, q0_ref, q1_ref, k0_ref, k1_ref, v0_ref, v1_ref, put, bias_ref, carry_ref,
                  *, g, dil, nbk):
    b = pl.program_id(0)
    i = pl.program_id(1)
    hpg = HEADS_PER_GROUP
    span_rows = SPAN * dil

    @pl.when((b == 0) & (i == 0))
    def _():
        bkt = bkt_ref[...]
        no_prev = lax.broadcasted_iota(jnp.int32, (SPAN, 2 * SPAN), 1) < SPAN
        for h in range(hpg):
            bias = _bias_from_buckets(bkt, relb_ref, g * hpg + h)
            bias_ref[0, h] = bias
            bias_ref[1, h] = jnp.where(no_prev, NEG_INF, bias)

    @pl.when(i == 0)
    def _():
        carry_ref[...] = jnp.zeros(carry_ref.shape, BF)

    lanehead = lax.broadcasted_iota(jnp.int32, (SPAN, GROUP_WIDTH), 1) >> HEAD_SHIFT
    first = jnp.where(i == 0, 1, 0)
    rd_slot = lax.rem(i, 2)
    last_k, last_v = {}, {}

    def block(j, r):
        rows = pl.ds(j * span_rows + r, SPAN, stride=dil)
        cur = lambda r0, r1: jnp.concatenate([r0[0, rows, :], r1[0, rows, :]], axis=1)
        if j == 0:
            kprev, vprev = carry_ref[rd_slot, 0, r], carry_ref[rd_slot, 1, r]
        else:
            kprev, vprev = last_k[r], last_v[r]
        kcur = cur(k0_ref, k1_ref).astype(BF)
        vcur = cur(v0_ref, v1_ref).astype(BF)
        if j == nbk - 1:
            carry_ref[1 - rd_slot, 0, r] = kcur
            carry_ref[1 - rd_slot, 1, r] = vcur
        else:
            last_k[r], last_v[r] = kcur, vcur
        q = cur(q0_ref, q1_ref) * (HEAD_DIM ** -0.5)
        lhs = jnp.concatenate([jnp.where(lanehead == h, q, 0.0).astype(BF) for h in range(hpg)], axis=0)
        s = _dot_nt(lhs, jnp.concatenate([kprev, kcur], axis=0))

        ps, ms, sums = [], [], []
        for h in range(hpg):
            logit = s[h * SPAN:(h + 1) * SPAN] + (bias_ref[first, h] if j == 0 else bias_ref[0, h])
            m = jnp.max(logit, axis=-1, keepdims=True)
            e = jnp.exp(logit - m)
            ps.append(e.astype(BF))
            ms.append(m)
            sums.append(jnp.sum(e, axis=-1, keepdims=True))
        pv = _dot(jnp.concatenate(ps, axis=0), jnp.concatenate([vprev, vcur], axis=0))

        o = jnp.zeros((SPAN, GROUP_WIDTH), F32)
        l = jnp.zeros((SPAN, GROUP_WIDTH), F32)
        for h in range(hpg):
            o = jnp.where(lanehead == h, pv[h * SPAN:(h + 1) * SPAN] * (1.0 / sums[h]), o)
            l = jnp.where(lanehead == h, ms[h] + jnp.log(sums[h]), l)
        put(0, rows, o[:, :LANES])
        put(1, rows, o[:, LANES:])
        put(2, rows, l[:, :LANES])
        put(3, rows, l[:, LANES:])

    for j in range(nbk):
        for r in range(dil):
            block(j, r)


def _pattn_call(q, kv, rel_bias, g, dil, finish=None):
    bsz, s, _ = q.shape
    nbk = max(1, PATTN_ROWS // (dil * SPAN))
    rows = nbk * dil * SPAN
    assert s % rows == 0 and GROUP_WIDTH == 2 * LANES
    blk = (1, rows, LANES)
    kcol = 2 * g
    vcol = ATT_WIDTH // LANES + 2 * g
    ospec = pl.BlockSpec(blk, lambda b, i: (b, i, 0))
    in_specs = [
        pl.BlockSpec(memory_space=pltpu.SMEM),
        _const_spec((SPAN, 2 * SPAN)),
        pl.BlockSpec(blk, lambda b, i: (b, i, kcol)),
        pl.BlockSpec(blk, lambda b, i: (b, i, kcol + 1)),
        pl.BlockSpec(blk, lambda b, i: (b, i, kcol)),
        pl.BlockSpec(blk, lambda b, i: (b, i, kcol + 1)),
        pl.BlockSpec(blk, lambda b, i: (b, i, vcol)),
        pl.BlockSpec(blk, lambda b, i: (b, i, vcol + 1)),
    ]
    args = [rel_bias, _prompt_bucket_table(dil), q, q, kv, kv, kv, kv]
    scratch = [pltpu.VMEM((2, HEADS_PER_GROUP, SPAN, 2 * SPAN), F32),
               pltpu.VMEM((2, 2, dil, SPAN, GROUP_WIDTH), BF)]
    if finish is None:
        out = jax.ShapeDtypeStruct((bsz, s, LANES), F32)
        res = pl.pallas_call(
            functools.partial(_pattn_kernel, g=g, dil=dil, nbk=nbk),
            out_shape=(out,) * 4,
            grid=(bsz, s // rows),
            in_specs=in_specs,
            out_specs=(ospec,) * 4,
            scratch_shapes=scratch,
            compiler_params=_params(("arbitrary", "arbitrary")),
            name=f"prompt_attn_g{g}",
        )(*args)
        return list(res[:2]), list(res[2:])

    x, mods, o_other, l_other, wo = finish
    d = x.shape[-1]
    r = mods.shape[2]
    assert r == 1
    xspec = pl.BlockSpec((1, rows, d), lambda b, i: (b, i, 0))
    return pl.pallas_call(
        functools.partial(_pattn_oproj_kernel, g=g, dil=dil, nbk=nbk),
        out_shape=jax.ShapeDtypeStruct((bsz, s, d), F32),
        grid=(bsz, s // rows),
        in_specs=in_specs + [xspec, pl.BlockSpec((1, N_MOD, r, d), lambda b, i: (b, 0, 0, 0))]
        + [ospec] * (len(o_other) + len(l_other)) + [_const_spec((ATT_WIDTH, d))],
        out_specs=xspec,
        scratch_shapes=scratch + [pltpu.VMEM((4, rows, LANES), F32), pltpu.VMEM((rows, ATT_WIDTH), BF)],
        compiler_params=_params(("arbitrary", "arbitrary")),
        name=f"prompt_attn_g{g}_out_proj",
    )(*args, x, mods, *o_other, *l_other, wo)


def _sattn_kernel(relb_ref, bo0_ref, bo1_ref, bo2_ref, bn_ref, q_ref, kvn_ref, kvt_ref, c0_ref, c1_ref, c2_ref,
                  o0_ref, o1_ref, o2_ref, l0_ref, l1_ref, l2_ref, n0_ref, n1_ref, n2_ref,
                  bias0_ref, bias1_ref, bias2_ref, biasn_ref, tail_ref, *, t_new):
    hpg = HEADS_PER_GROUP
    gw = GROUP_WIDTH
    rows = SAMPLE_QROWS
    bo_refs = (bo0_ref, bo1_ref, bo2_ref)
    bias_refs = (bias0_ref, bias1_ref, bias2_ref)
    caches = (c0_ref, c1_ref, c2_ref)
    outs = ((o0_ref, l0_ref, n0_ref), (o1_ref, l1_ref, n1_ref), (o2_ref, l2_ref, n2_ref))

    @pl.when(pl.program_id(0) == 0)
    def _():
        tail_ref[...] = jnp.zeros(tail_ref.shape, F32)
        for g in range(N_GROUPS):
            for h in range(hpg):
                sl = slice(h * SUBLANES, (h + 1) * SUBLANES)
                bias_refs[g][sl, :] = _bias_from_buckets(bo_refs[g][sl, :], relb_ref, g * hpg + h)
                biasn_ref[g, sl, :] = _bias_from_buckets(bn_ref[g, sl, :], relb_ref, g * hpg + h)

    lanehead = lax.broadcasted_iota(jnp.int32, (rows, gw), 1) >> HEAD_SHIFT
    rowhead = lax.broadcasted_iota(jnp.int32, (rows, gw), 0) >> SUBLANE_SHIFT
    own = lanehead == rowhead
    tail_lane = lax.broadcasted_iota(jnp.int32, (gw, LANES), 1) >= LANES - t_new

    for g in range(N_GROUPS):
        cref = caches[g]
        o_ref, l_ref, n_ref = outs[g]
        width = cref.shape[-1]
        qm = jnp.where(own, q_ref[0, :, g * gw:(g + 1) * gw] * (HEAD_DIM ** -0.5), 0.0).astype(BF)
        knew = kvn_ref[0, :, g * gw:(g + 1) * gw].astype(BF)
        vnew = kvn_ref[0, :, ATT_WIDTH + g * gw:ATT_WIDTH + (g + 1) * gw].astype(BF)
        lo = _dot(qm, cref[0, 0].astype(BF)) + bias_refs[g][...]
        ln = _dot_nt(qm, knew) + biasn_ref[g]
        m = jnp.maximum(jnp.max(lo, axis=-1, keepdims=True), jnp.max(ln, axis=-1, keepdims=True))
        eo = jnp.exp(lo - m)
        en = jnp.exp(ln - m)
        ssum = jnp.sum(eo, axis=-1, keepdims=True) + jnp.sum(en, axis=-1, keepdims=True)
        pv = _dot_nt(eo.astype(BF), cref[0, 1].astype(BF)) + _dot(en.astype(BF), vnew)
        om = jnp.where(own, pv * (1.0 / ssum), 0.0)
        lm = jnp.where(own, m + jnp.log(ssum), 0.0)
        o_acc = om[0:SUBLANES]
        l_acc = lm[0:SUBLANES]
        for h in range(1, hpg):
            o_acc = o_acc + om[h * SUBLANES:(h + 1) * SUBLANES]
            l_acc = l_acc + lm[h * SUBLANES:(h + 1) * SUBLANES]
        o_ref[0] = o_acc
        l_ref[0] = l_acc

        for kv in range(2):
            rolled = pltpu.roll(cref[0, kv], width - t_new, 1)
            tail_ref[:, 0:SUBLANES] = kvt_ref[0, kv * ATT_WIDTH + g * gw:kv * ATT_WIDTH + (g + 1) * gw, :]
            tail = pltpu.roll(tail_ref[...], LANES - t_new, 1)
            if width > LANES:
                n_ref[0, kv, :, 0:width - LANES] = rolled[:, 0:width - LANES]
            n_ref[0, kv, :, width - LANES:width] = jnp.where(tail_lane, tail, rolled[:, width - LANES:width])


def _sattn_call(q, kvn, caches, rel_bias):
    n, t_new, _ = q.shape
    gw = GROUP_WIDTH
    hpg = HEADS_PER_GROUP
    assert t_new <= SUBLANES
    views, bkt_old, bkt_new = [], [], []
    for g, (window, dil) in enumerate(DIL_GROUPS):
        width = caches[g].shape[1]
        assert width == window and width % LANES == 0
        views.append(jnp.transpose(caches[g], (0, 2, 3, 4, 1)).reshape(n, 2, gw, width))
        old, new = _sample_bucket_tables(dil, width, t_new)
        bkt_old.append(old)
        bkt_new.append(new)
    pad = SUBLANES - t_new
    qp = jnp.tile(jnp.pad(q, ((0, 0), (0, pad), (0, 0))), (1, hpg, 1))
    kvp = jnp.pad(kvn, ((0, 0), (0, pad), (0, 0)))
    kvt = jnp.swapaxes(kvp, 1, 2)
    out = jax.ShapeDtypeStruct((n, SUBLANES, gw), F32)
    oblk = pl.BlockSpec((1, SUBLANES, gw), lambda i: (i, 0, 0))
    cspecs = [pl.BlockSpec((1, 2, gw, v.shape[-1]), lambda i: (i, 0, 0, 0)) for v in views]
    res = pl.pallas_call(
        functools.partial(_sattn_kernel, t_new=t_new),
        out_shape=(out,) * 6 + tuple(jax.ShapeDtypeStruct(v.shape, F32) for v in views),
        grid=(n,),
        in_specs=[pl.BlockSpec(memory_space=pltpu.SMEM)]
        + [_const_spec(b.shape) for b in bkt_old]
        + [
            _const_spec((N_GROUPS, SAMPLE_QROWS, SUBLANES)),
            pl.BlockSpec((1, SAMPLE_QROWS, ATT_WIDTH), lambda i: (i, 0, 0)),
            pl.BlockSpec((1, SUBLANES, 2 * ATT_WIDTH), lambda i: (i, 0, 0)),
            pl.BlockSpec((1, 2 * ATT_WIDTH, SUBLANES), lambda i: (i, 0, 0)),
        ]
        + cspecs,
        out_specs=(oblk,) * 6 + tuple(cspecs),
        scratch_shapes=[pltpu.VMEM(b.shape, F32) for b in bkt_old]
        + [pltpu.VMEM((N_GROUPS, SAMPLE_QROWS, SUBLANES), F32), pltpu.VMEM((gw, LANES), F32)],
        compiler_params=_params(("arbitrary",)),
        name="sample_attn",
    )(rel_bias, *bkt_old, jnp.stack(bkt_new), qp, kvp, kvt, *views)
    new_caches = [jnp.transpose(c.reshape(n, 2, hpg, HEAD_DIM, c.shape[-1]), (0, 4, 1, 2, 3)) for c in res[6:]]
    return [a[:, :t_new] for a in res[:3]], [a[:, :t_new] for a in res[3:6]], new_caches


def _oproj_kernel(x_ref, mod_ref, *refs, tm):
    nblk = ATT_WIDTH // LANES
    halves = nblk // N_GROUPS
    o_refs, l_refs = refs[:nblk], refs[nblk:2 * nblk]
    wo_ref, out_ref, att_ref = refs[2 * nblk:]
    rc = BF16_ROWS

    def body(i, c):
        r0 = pl.multiple_of(i * rc, rc)
        for hf in range(halves):
            blks = [g * halves + hf for g in range(N_GROUPS)]
            ls = [l_refs[k][0, pl.ds(r0, rc), :] for k in blks]
            mx = functools.reduce(jnp.maximum, ls)
            es = [jnp.exp(v - mx) for v in ls]
            inv = 1.0 / functools.reduce(lambda a, b: a + b, es)
            for k, e in zip(blks, es):
                att_ref[pl.ds(r0, rc), k * LANES:(k + 1) * LANES] = (o_refs[k][0, pl.ds(r0, rc), :] * (e * inv)).astype(BF)
        return c

    lax.fori_loop(0, tm // rc, body, 0, unroll=2)
    out_ref[0] = x_ref[0] + mod_ref[0, 2] * _dot(att_ref[...], wo_ref[...])


def _oproj_call(x, mods, os_, ls_, wo, *, tm):
    ns, s, d = x.shape
    r = mods.shape[2]
    kern = functools.partial(_oproj_kernel, tm=tm)
    gspec = pl.BlockSpec((1, tm, LANES), lambda n, t: (n, t, 0))
    return pl.pallas_call(
        kern,
        out_shape=jax.ShapeDtypeStruct((ns, s, d), F32),
        grid=(ns, s // tm),
        in_specs=[
            pl.BlockSpec((1, tm, d), lambda n, t: (n, t, 0)),
            pl.BlockSpec((1, N_MOD, r, d), lambda n, t: (n, 0, 0, 0)),
        ] + [gspec] * (2 * ATT_WIDTH // LANES) + [_const_spec((ATT_WIDTH, d))],
        out_specs=pl.BlockSpec((1, tm, d), lambda n, t: (n, t, 0)),
        scratch_shapes=[pltpu.VMEM((tm, ATT_WIDTH), BF)],
        compiler_params=_params(("arbitrary", "arbitrary")),
        name="attn_out_proj",
    )(x, mods, *os_, *ls_, wo)


def _trunk(x, mods, modkv, hist_a, hist_f, w, attn_fn, *, tm, rd, kv_tail_rows):
    row = lambda v: v.reshape(1, -1)
    x, st_a = _conformer_call(x, mods[0], hist_a, row(w["norm_mix"][0]), w["a_w1"], row(w["a_b1"][0]),
                              w["a_dw"][0], row(w["a_dwb"][0]), row(w["a_ln_g"][0]), row(w["a_ln_b"][0]),
                              w["a_w2"], row(w["a_b2"][0]), tm=tm, rd=rd)
    x, st_f0 = _ffn_call(x, mods[0], hist_f[0], row(w["norm_ffn"][0]), w["f_wup"][0], w["f_cw"][0],
                         row(w["f_cb"][0]), w["f_wdown"][0], row(w["norm_f"]), tm=tm, rd=rd, final_norm=False)
    q, kv, kv_tail = _qkv_call(x, mods[1], modkv, row(w["norm_mix"][1]), row(w["norm_kv"]), w["w_q"], w["w_kv"],
                               tm=tm, tail_rows=kv_tail_rows)
    x, attn_extra = attn_fn(q, kv, x, mods[1], w["w_o"])
    y, st_f1 = _ffn_call(x, mods[1], hist_f[1], row(w["norm_ffn"][1]), w["f_wup"][1], w["f_cw"][1],
                         row(w["f_cb"][1]), w["f_wdown"][1], row(w["norm_f"]), tm=tm, rd=rd, final_norm=True)
    return y, kv_tail, st_a, [st_f0, st_f1], attn_extra


def kernel(x_prompt, x_sample, cache_kv_w128, cache_kv_w512, cache_kv_w2048, state_conv_a, state_conv_ffn, c_prompt, c_sample, w_mod, b_mod, norm_mix, norm_ffn, a_w1, a_b1, a_dw, a_dwb, a_ln_g, a_ln_b, a_w2, a_b2, w_mod_kv, b_mod_kv, norm_kv, w_kv, w_q, w_o, rel_bias, f_wup, f_cw, f_cb, f_wdown, norm_f):
    d = D_MODEL
    bsz, seq, _ = x_prompt.shape
    nseq, t_new, _ = x_sample.shape
    caches = (cache_kv_w128, cache_kv_w512, cache_kv_w2048)
    hpg = HEADS_PER_GROUP

    w = dict(norm_mix=norm_mix, norm_ffn=norm_ffn, a_w1=a_w1[0].astype(BF), a_b1=a_b1, a_dw=a_dw, a_dwb=a_dwb,
             a_ln_g=a_ln_g, a_ln_b=a_ln_b, a_w2=a_w2[0].astype(BF), a_b2=a_b2, norm_kv=norm_kv,
             w_kv=w_kv.astype(BF), w_q=w_q[0].astype(BF), w_o=w_o[0].astype(BF),
             f_wup=[f_wup[l].astype(BF) for l in range(DEPTH)], f_cw=f_cw, f_cb=f_cb,
             f_wdown=[f_wdown[l].astype(BF) for l in range(DEPTH)], norm_f=norm_f)

    n_c = bsz + nseq
    mp = _round_up(n_c, SUBLANES)
    c_all = jnp.pad(jnp.concatenate([c_prompt, c_sample], axis=0), ((0, mp - n_c), (0, 0)))
    mod = _mod_call(c_all, w_mod, b_mod.reshape(DEPTH, 1, N_MOD * d), tn=1536)
    modkv = _mod_call(c_all, w_mod_kv[None], b_mod_kv.reshape(1, 1, 2 * d), tn=1024)[0]

    mods_p = [mod[l, :bsz].reshape(bsz, N_MOD, 1, d) for l in range(DEPTH)]
    modkv_p = modkv[:bsz].reshape(bsz, 2, 1, d)
    hist_a_p = jnp.zeros((bsz, CONV_A_WIDTH - 1, d), F32)
    hist_f_p = [jnp.zeros((bsz, CONV_F_WIDTH - 1, 2 * D_FF), F32)] * DEPTH

    def prompt_attn(q, kv, x, mods1, wo):
        res = [_pattn_call(q, kv, rel_bias, g, dil) for g, (_, dil) in enumerate(DIL_GROUPS) if g > 0]
        finish = (x, mods1, [a for r in res for a in r[0]], [a for r in res for a in r[1]], wo)
        return _pattn_call(q, kv, rel_bias, 0, DIL_GROUPS[0][1], finish=finish), None

    wmax = max(c.shape[1] for c in caches)
    assert seq >= wmax
    y_p, kv_tail, st_a_p, st_f_p, _ = _trunk(x_prompt, mods_p, modkv_p, hist_a_p, hist_f_p, w, prompt_attn,
                                             tm=TM_PROMPT, rd=1, kv_tail_rows=wmax)
    kv_tail = kv_tail.reshape(bsz, wmax, 2, N_HEADS, HEAD_DIM)
    kv_bufs_p = [kv_tail[:, wmax - c.shape[1]:, :, g * hpg:(g + 1) * hpg] for g, c in enumerate(caches)]
    conv_a_p = st_a_p[None]
    conv_f_p = jnp.stack(st_f_p)

    rows = t_new * nseq
    tmaj = lambda v: jnp.swapaxes(v, 0, 1).reshape(1, -1, v.shape[-1])
    smaj = lambda v, c: jnp.swapaxes(v.reshape(-1, nseq, c), 0, 1)

    def per_row(m, k):
        return jnp.tile(jnp.swapaxes(m.reshape(nseq, k, d), 0, 1), (1, t_new, 1))[None]

    mods_s = [per_row(mod[l, bsz:n_c], N_MOD) for l in range(DEPTH)]
    modkv_s = per_row(modkv[bsz:n_c], 2)
    hist_a_s = tmaj(state_conv_a[0])
    hist_f_s = [tmaj(state_conv_ffn[l]) for l in range(DEPTH)]

    def sample_attn(q, kv, x, mods1, wo):
        os_, ls_, new_caches = _sattn_call(smaj(q, ATT_WIDTH), smaj(kv, 2 * ATT_WIDTH), caches, rel_bias)
        halves = lambda vs: [tmaj(a[..., c0:c0 + LANES]) for a in vs for c0 in range(0, GROUP_WIDTH, LANES)]
        return _oproj_call(x, mods1, halves(os_), halves(ls_), wo, tm=rows), new_caches

    y_s, _, st_a_s, st_f_s, kv_bufs_s = _trunk(tmaj(x_sample), mods_s, modkv_s, hist_a_s, hist_f_s, w,
                                               sample_attn, tm=rows, rd=nseq, kv_tail_rows=rows)
    y_s = smaj(y_s, d)
    conv_a_s = smaj(st_a_s, d)[None]
    conv_f_s = jnp.stack([smaj(s, 2 * D_FF) for s in st_f_s])

    return (y_p, y_s, kv_bufs_p[0], kv_bufs_p[1], kv_bufs_p[2], conv_a_p, conv_f_p,
            kv_bufs_s[0], kv_bufs_s[1], kv_bufs_s[2], conv_a_s, conv_f_s)
```

```python
import functools
import math

import jax
import jax.numpy as jnp
from jax import lax
from jax.experimental import pallas as pl
from jax.experimental.pallas import tpu as pltpu

D_MODEL = 1024
DEPTH = 2
HEAD_DIM = 64
HEADS_PER_GROUP = 4
DIL_GROUPS = ((128, 1), (512, 4), (2048, 16))
N_GROUPS = len(DIL_GROUPS)
N_HEADS = HEADS_PER_GROUP * N_GROUPS
ATT_WIDTH = N_HEADS * HEAD_DIM
GROUP_WIDTH = HEADS_PER_GROUP * HEAD_DIM
SPAN = 128
N_BUCKETS = 32
MAX_DISTANCE = 2048
CONV_A_WIDTH = 31
D_FF = 2816
CONV_F_WIDTH = 3
N_MOD = 6
EPS = 1e-6
LN_EPS = 1e-5
NEG_INF = -1e30

BF = jnp.bfloat16
F32 = jnp.float32

HEAD_SHIFT = HEAD_DIM.bit_length() - 1
SUBLANES = 8
SUBLANE_SHIFT = SUBLANES.bit_length() - 1
LANES = 128
BF16_ROWS = 16
ROW_UNROLL = 4
VMEM_LIMIT = 56 * 1024 * 1024

TM_PROMPT = 512
PATTN_ROWS = 1024
CONF_ROWS = 512
FF_ROWS = 32
FF_SUB = 256
SAMPLE_QROWS = HEADS_PER_GROUP * SUBLANES


def _round_up(a, b):
    return -(-a // b) * b


def _params(sem):
    return pltpu.CompilerParams(dimension_semantics=sem, vmem_limit_bytes=VMEM_LIMIT)


def _const_spec(shape):
    nd = len(shape)
    return pl.BlockSpec(shape, lambda *_: (0,) * nd, pipeline_mode=pl.Buffered(1))


def _dot(a, b):
    return jnp.dot(a, b, preferred_element_type=F32)


def _dot_nt(a, b):
    return lax.dot_general(a, b, (((1,), (1,)), ((), ())), preferred_element_type=F32)


def _sigmoid(v):
    return 1.0 / (1.0 + jnp.exp(-v))


def _mod_rows(mod_ref, idx, r0, rows, per_row):
    if per_row:
        return mod_ref[0, idx, pl.ds(r0, rows), :]
    return mod_ref[0, idx]


def _rms_mod_rows(x_ref, rows, targets, per_row, *, slot=None, straight=False):
    rc = BF16_ROWS
    lo, hi = rows

    def chunk(r0):
        x = x_ref[0, pl.ds(r0, rc), :]
        xn = x * lax.rsqrt(jnp.mean(x * x, axis=-1, keepdims=True) + EPS)
        for gain_ref, mod_ref, i_sh, i_sc, dst_ref in targets:
            sh = _mod_rows(mod_ref, i_sh, r0, rc, per_row)
            sc = _mod_rows(mod_ref, i_sc, r0, rc, per_row)
            idx = (pl.ds(r0, rc), slice(None))
            dst_ref[idx if slot is None else (slot,) + idx] = ((xn * gain_ref[...]) * (1.0 + sc) + sh).astype(BF)

    if straight:
        for r0 in range(lo, hi, rc):
            chunk(r0)
        return

    def body(i, c):
        chunk(pl.multiple_of(lo + i * rc, rc))
        return c

    lax.fori_loop(0, (hi - lo) // rc, body, 0, unroll=ROW_UNROLL)


def _tap_conv(src_ref, blk, w_ref, offsets, r0, rows):
    cols = slice(blk * LANES, (blk + 1) * LANES)
    acc = None
    for k, off in enumerate(offsets):
        term = w_ref[k:k + 1, cols] * src_ref[blk, r0 + off:r0 + off + rows, :]
        acc = term if acc is None else acc + term
    return acc


def _mod_kernel(c_ref, w_ref, b_ref, o_ref):
    c = c_ref[...]
    a = (c * _sigmoid(c)).astype(BF)
    o_ref[0] = _dot(a, w_ref[0].astype(BF)) + b_ref[0]


def _mod_call(c_all, w, b, tn):
    nl, d, n = w.shape
    mp = c_all.shape[0]
    return pl.pallas_call(
        _mod_kernel,
        out_shape=jax.ShapeDtypeStruct((nl, mp, n), F32),
        grid=(nl, n // tn),
        in_specs=[
            pl.BlockSpec((mp, d), lambda l, j: (0, 0)),
            pl.BlockSpec((1, d, tn), lambda l, j: (l, 0, j)),
            pl.BlockSpec((1, 1, tn), lambda l, j: (l, 0, j)),
        ],
        out_specs=pl.BlockSpec((1, mp, tn), lambda l, j: (l, 0, j)),
        compiler_params=_params(("arbitrary", "arbitrary")),
        name="adaln_mod",
    )(c_all, w, b)


def _conformer_kernel(x_ref, mod_ref, hist_ref, nrm_ref, w1_ref, b1_ref, dw_ref, dwb_ref, lng_ref, lnb_ref,
                      w2_ref, b2_ref, o_ref, st_ref, h_ref, h2_ref, u_ref, full_ref, y_ref, *, tm, rd, per_row):
    d = D_MODEL
    hh = (CONV_A_WIDTH - 1) * rd
    hp = _round_up(hh, SUBLANES)
    t = pl.program_id(1)

    nblk = d // LANES
    lanes = lambda blk: slice(blk * LANES, (blk + 1) * LANES)

    @pl.when(t == 0)
    def _():
        full_ref[:, 0:hp, :] = jnp.zeros((nblk, hp, LANES), F32)
        for blk in range(nblk):
            full_ref[blk, hp - hh:hp, :] = hist_ref[0, :, lanes(blk)]

    rb = min(tm, CONF_ROWS)
    rg, cg = 16, 512
    rcv = 64
    offsets = [hp - hh + k * rd for k in range(CONV_A_WIDTH)]
    for rs in range(0, tm, rb):
        _rms_mod_rows(x_ref, (rs, rs + rb), [(nrm_ref, mod_ref, 0, 1, h_ref)], per_row, straight=True)
        u_ref[rs:rs + rb, :] = _dot(h_ref[rs:rs + rb, :], w1_ref[...])

        for r0 in range(rs, rs + rb, rg):
            for c0 in range(0, d, cg):
                a = u_ref[r0:r0 + rg, c0:c0 + cg] + b1_ref[:, c0:c0 + cg]
                g = u_ref[r0:r0 + rg, d + c0:d + c0 + cg] + b1_ref[:, d + c0:d + c0 + cg]
                glu = a * _sigmoid(g)
                for j in range(cg // LANES):
                    full_ref[c0 // LANES + j, hp + r0:hp + r0 + rg, :] = glu[:, lanes(j)]

        for r0 in range(rs, rs + rb, rcv):
            for blk in range(nblk):
                y_ref[r0:r0 + rcv, lanes(blk)] = (_tap_conv(full_ref, blk, dw_ref, offsets, r0, rcv)
                                                  + dwb_ref[:, lanes(blk)])

        for r0 in range(rs, rs + rb, BF16_ROWS):
            y = y_ref[r0:r0 + BF16_ROWS, :]
            mu = jnp.mean(y, axis=-1, keepdims=True)
            dv = y - mu
            var = jnp.mean(dv * dv, axis=-1, keepdims=True)
            yn = dv * lax.rsqrt(var + LN_EPS) * lng_ref[...] + lnb_ref[...]
            h2_ref[r0:r0 + BF16_ROWS, :] = (yn * _sigmoid(yn)).astype(BF)

        out = _dot(h2_ref[rs:rs + rb, :], w2_ref[...]) + b2_ref[...]
        gate = mod_ref[0, 2, rs:rs + rb, :] if per_row else mod_ref[0, 2]
        o_ref[0, rs:rs + rb, :] = x_ref[0, rs:rs + rb, :] + gate * out

    for blk in range(nblk):
        new_hist = full_ref[blk, hp + tm - hh:hp + tm, :]
        st_ref[0, :, lanes(blk)] = new_hist
        full_ref[blk, hp - hh:hp, :] = new_hist


def _conformer_call(x, mods, hist, nrm, w1, b1, dw, dwb, lng, lnb, w2, b2, *, tm, rd):
    ns, s, d = x.shape
    r = mods.shape[2]
    hh = hist.shape[1]
    hp = _round_up(hh, SUBLANES)
    per_row = r > 1
    assert s % tm == 0 and (not per_row or (r == tm and s == tm))
    kern = functools.partial(_conformer_kernel, tm=tm, rd=rd, per_row=per_row)
    return pl.pallas_call(
        kern,
        out_shape=(jax.ShapeDtypeStruct((ns, s, d), F32), jax.ShapeDtypeStruct((ns, hh, d), F32)),
        grid=(ns, s // tm),
        in_specs=[
            pl.BlockSpec((1, tm, d), lambda n, t: (n, t, 0)),
            pl.BlockSpec((1, N_MOD, r, d), lambda n, t: (n, 0, 0, 0)),
            pl.BlockSpec((1, hh, d), lambda n, t: (n, 0, 0)),
            _const_spec((1, d)),
            _const_spec((d, 2 * d)),
            _const_spec((1, 2 * d)),
            _const_spec((CONV_A_WIDTH, d)),
            _const_spec((1, d)),
            _const_spec((1, d)),
            _const_spec((1, d)),
            _const_spec((d, d)),
            _const_spec((1, d)),
        ],
        out_specs=(
            pl.BlockSpec((1, tm, d), lambda n, t: (n, t, 0)),
            pl.BlockSpec((1, hh, d), lambda n, t: (n, 0, 0)),
        ),
        scratch_shapes=[
            pltpu.VMEM((tm, d), BF),
            pltpu.VMEM((tm, d), BF),
            pltpu.VMEM((tm, 2 * d), F32),
            pltpu.VMEM((d // LANES, hp + tm, LANES), F32),
            pltpu.VMEM((tm, d), F32),
        ],
        compiler_params=_params(("arbitrary", "arbitrary")),
        name="conformer_mixer",
    )(x, mods, hist, nrm, w1, b1, dw, dwb, lng, lnb, w2, b2)


def _ffn_kernel(x_ref, xnext_ref, mod_ref, hist_ref, nrm_ref, wu_ref, cw_ref, cb_ref, wd_ref, nf_ref, o_ref, st_ref,
                h_ref, ubuf_ref, carry_ref, act_ref, part_ref, *, tm, rd, per_row, final_norm):
    f = D_FF
    hh = (CONV_F_WIDTH - 1) * rd
    hp = _round_up(hh, SUBLANES)
    t = pl.program_id(1)

    cur = lax.rem(t, 2)
    norm_targets = [(nrm_ref, mod_ref, 3, 4, h_ref)]

    @pl.when(t == 0)
    def _():
        _rms_mod_rows(x_ref, (0, tm), norm_targets, per_row, slot=0)

    nblk = 2 * f // LANES
    lanes = lambda blk: slice(blk * LANES, (blk + 1) * LANES)

    @pl.when(t == 0)
    def _():
        carry_ref[...] = jnp.zeros((nblk, hp, LANES), F32)
        for blk in range(nblk):
            carry_ref[blk, hp - hh:hp, :] = hist_ref[0, :, lanes(blk)]

    ubuf_ref[:, 0:hp, :] = carry_ref[...]

    rc = FF_ROWS
    offsets = [hp - hh + k * rd for k in range(CONV_F_WIDTH)]
    for c0 in range(0, f, FF_SUB):
        for half in range(2):
            col = half * f + c0
            u = _dot(h_ref[cur], wu_ref[:, col:col + FF_SUB])
            for j in range(FF_SUB // LANES):
                ubuf_ref[col // LANES + j, hp:hp + tm, :] = u[:, lanes(j)]
        for r0 in range(0, tm, rc):
            for cc0 in range(c0, c0 + FF_SUB, LANES):
                ys = []
                for half in range(2):
                    blk = (half * f + cc0) // LANES
                    ys.append(_tap_conv(ubuf_ref, blk, cw_ref, offsets, r0, rc) + cb_ref[:, lanes(blk)])
                yg, yv = ys
                act_ref[r0:r0 + rc, cc0:cc0 + LANES] = (yg * _sigmoid(yg) * yv).astype(BF)

    for blk in range(nblk):
        st_ref[0, :, lanes(blk)] = ubuf_ref[blk, hp + tm - hh:hp + tm, :]
    carry_ref[...] = ubuf_ref[:, tm:tm + hp, :]

    _rms_mod_rows(xnext_ref, (0, tm), norm_targets, per_row, slot=1 - cur, straight=True)
    xo = x_ref[0] + mod_ref[0, 5] * _dot(act_ref[...], wd_ref[...])
    if not final_norm:
        o_ref[0] = xo
        return
    part_ref[...] = xo
    rows = BF16_ROWS

    for r0 in range(0, tm, rows):
        v = part_ref[r0:r0 + rows, :]
        o_ref[0, r0:r0 + rows, :] = v * lax.rsqrt(jnp.mean(v * v, axis=-1, keepdims=True) + EPS) * nf_ref[...]


def _ffn_call(x, mods, hist, nrm, wup, cw, cb, wdown, nf, *, tm, rd, final_norm):
    ns, s, d = x.shape
    r = mods.shape[2]
    hh = hist.shape[1]
    hp = _round_up(hh, SUBLANES)
    f = wdown.shape[0]
    per_row = r > 1
    nt = s // tm
    assert f == D_FF and f % FF_SUB == 0 and s % tm == 0 and (not per_row or (r == tm and s == tm))
    kern = functools.partial(_ffn_kernel, tm=tm, rd=rd, per_row=per_row, final_norm=final_norm)
    return pl.pallas_call(
        kern,
        out_shape=(jax.ShapeDtypeStruct((ns, s, d), F32), jax.ShapeDtypeStruct((ns, hh, 2 * f), F32)),
        grid=(ns, nt),
        in_specs=[
            pl.BlockSpec((1, tm, d), lambda n, t: (n, t, 0)),
            pl.BlockSpec((1, tm, d), lambda n, t: (n, jnp.minimum(t + 1, nt - 1), 0)),
            pl.BlockSpec((1, N_MOD, r, d), lambda n, t: (n, 0, 0, 0)),
            pl.BlockSpec((1, hh, 2 * f), lambda n, t: (n, 0, 0)),
            _const_spec((1, d)),
            _const_spec((d, 2 * f)),
            _const_spec((CONV_F_WIDTH, 2 * f)),
            _const_spec((1, 2 * f)),
            _const_spec((f, d)),
            _const_spec((1, d)),
        ],
        out_specs=(
            pl.BlockSpec((1, tm, d), lambda n, t: (n, t, 0)),
            pl.BlockSpec((1, hh, 2 * f), lambda n, t: (n, 0, 0)),
        ),
        scratch_shapes=[
            pltpu.VMEM((2, tm, d), BF),
            pltpu.VMEM((2 * f // LANES, hp + tm, LANES), F32),
            pltpu.VMEM((2 * f // LANES, hp, LANES), F32),
            pltpu.VMEM((tm, f), BF),
            pltpu.VMEM((tm, d), F32),
        ],
        compiler_params=_params(("arbitrary", "arbitrary")),
        name="conv_ffn",
    )(x, x, mods, hist, nrm, wup, cw, cb, wdown, nf)


def _qkv_kernel(x_ref, xnext_ref, mod_ref, modkv_ref, nq_ref, nkv_ref, wq_ref, wkv_ref, q_ref, kv_ref, kvtail_ref,
                hq_ref, hkv_ref, *, tm, per_row):
    t = pl.program_id(1)
    cur = lax.rem(t, 2)
    norm_targets = [(nq_ref, mod_ref, 0, 1, hq_ref), (nkv_ref, modkv_ref, 0, 1, hkv_ref)]

    @pl.when(t == 0)
    def _():
        _rms_mod_rows(x_ref, (0, tm), norm_targets, per_row, slot=0)

    q_ref[0] = _dot(hq_ref[cur], wq_ref[...])
    _rms_mod_rows(xnext_ref, (0, tm), norm_targets, per_row, slot=1 - cur, straight=True)
    kv = _dot(hkv_ref[cur], wkv_ref[...])
    kv_ref[0] = kv
    kvtail_ref[0] = kv


def _qkv_call(x, mods, modkv, nq, nkv, wq, wkv, *, tm, tail_rows):
    ns, s, d = x.shape
    r = mods.shape[2]
    per_row = r > 1
    nt = s // tm
    tail_tiles = -(-tail_rows // tm)
    assert tail_tiles <= nt and tail_rows % tm == 0
    kern = functools.partial(_qkv_kernel, tm=tm, per_row=per_row)
    return pl.pallas_call(
        kern,
        out_shape=(jax.ShapeDtypeStruct((ns, s, ATT_WIDTH), F32), jax.ShapeDtypeStruct((ns, s, 2 * ATT_WIDTH), F32),
                   jax.ShapeDtypeStruct((ns, tail_rows, 2 * ATT_WIDTH), F32)),
        grid=(ns, nt),
        in_specs=[
            pl.BlockSpec((1, tm, d), lambda n, t: (n, 0, 0)),
            pl.BlockSpec((1, tm, d), lambda n, t: (n, jnp.minimum(t + 1, nt - 1), 0)),
            pl.BlockSpec((1, N_MOD, r, d), lambda n, t: (n, 0, 0, 0)),
            pl.BlockSpec((1, 2, r, d), lambda n, t: (n, 0, 0, 0)),
            _const_spec((1, d)),
            _const_spec((1, d)),
            _const_spec((d, ATT_WIDTH)),
            _const_spec((d, 2 * ATT_WIDTH)),
        ],
        out_specs=(
            pl.BlockSpec((1, tm, ATT_WIDTH), lambda n, t: (n, t, 0)),
            pl.BlockSpec((1, tm, 2 * ATT_WIDTH), lambda n, t: (n, t, 0)),
            pl.BlockSpec((1, tm, 2 * ATT_WIDTH), lambda n, t: (n, jnp.maximum(t - (nt - tail_tiles), 0), 0)),
        ),
        scratch_shapes=[pltpu.VMEM((2, tm, d), BF), pltpu.VMEM((2, tm, d), BF)],
        compiler_params=_params(("arbitrary", "arbitrary")),
        name="qkv_proj",
    )(x, x, mods, modkv, nq, nkv, wq, wkv)


def _rel_bucket(dist):
    max_exact = N_BUCKETS // 2
    dd = jnp.maximum(dist, 1).astype(F32)
    large = max_exact + (jnp.log(dd / max_exact) / math.log(MAX_DISTANCE / max_exact)
                         * (N_BUCKETS - max_exact)).astype(jnp.int32)
    large = jnp.minimum(large, N_BUCKETS - 1)
    return jnp.where(dist < max_exact, dist, large)


def _prompt_bucket_table(dil):
    qi = jnp.arange(SPAN, dtype=jnp.int32)[:, None]
    ki = jnp.arange(2 * SPAN, dtype=jnp.int32)[None, :]
    m = qi + SPAN - ki
    valid = (m >= 0) & (m <= SPAN)
    return jnp.where(valid, _rel_bucket(jnp.clip(m, 0, SPAN) * dil), -1).astype(jnp.int32)


def _sample_bucket_tables(dil, width, t_new):
    row = jnp.arange(SAMPLE_QROWS, dtype=jnp.int32)[:, None]
    t = (row % SUBLANES) % t_new
    tn = jnp.arange(SUBLANES, dtype=jnp.int32)[None, :]

    def table(dist, ok):
        ok = ok & (dist >= 0) & (dist % dil == 0) & (dist // dil <= SPAN)
        return jnp.where(ok, _rel_bucket(jnp.clip(dist, 0, SPAN * dil)), -1).astype(jnp.int32)

    old = table(width + t - jnp.arange(width, dtype=jnp.int32)[None, :], True)
    new = table(t - tn, tn < t_new)
    return old, new


def _bias_from_buckets(bkt, relb_ref, head):
    acc = jnp.full(bkt.shape, NEG_INF, F32)
    for b in range(N_BUCKETS):
        acc = jnp.where(bkt == b, relb_ref[b, head], acc)
    return acc


def _pattn_kernel(relb_ref, bkt_ref, q0_ref, q1_ref, k0_ref, k1_ref, v0_ref, v1_ref,
                  o0_ref, o1_ref, l0_ref, l1_ref, bias_ref, carry_ref, *, g, dil, nbk):
    outs = (o0_ref, o1_ref, l0_ref, l1_ref)

    def put(k, rows, val):
        outs[k][0, rows, :] = val

    _pattn_blocks(relb_ref, bkt_ref, q0_ref, q1_ref, k0_ref, k1_ref, v0_ref, v1_ref, put, bias_ref, carry_ref,
                  g=g, dil=dil, nbk=nbk)


def _pattn_oproj_kernel(relb_ref, bkt_ref, q0_ref, q1_ref, k0_ref, k1_ref, v0_ref, v1_ref, x_ref, mod_ref, *refs,
                        g, dil, nbk):
    nblk = ATT_WIDTH // LANES
    halves = nblk // N_GROUPS
    nother = nblk - halves
    o_other, l_other = refs[:nother], refs[nother:2 * nother]
    wo_ref, out_ref, bias_ref, carry_ref, stage_ref, att_ref = refs[2 * nother:]
    rows_step = nbk * dil * SPAN

    def put(k, rows, val):
        stage_ref[k, rows, :] = val

    _pattn_blocks(relb_ref, bkt_ref, q0_ref, q1_ref, k0_ref, k1_ref, v0_ref, v1_ref, put, bias_ref, carry_ref,
                  g=g, dil=dil, nbk=nbk)

    others = [gi for gi in range(N_GROUPS) if gi != g]
    rc = BF16_ROWS

    for r0 in range(0, rows_step, rc):
        rows = pl.ds(r0, rc)
        for hf in range(halves):
            os_ = {g: stage_ref[hf, rows, :]}
            ls_ = {g: stage_ref[halves + hf, rows, :]}
            for n, gi in enumerate(others):
                os_[gi] = o_other[n * halves + hf][0, rows, :]
                ls_[gi] = l_other[n * halves + hf][0, rows, :]
            mx = functools.reduce(jnp.maximum, ls_.values())
            es = {gi: jnp.exp(v - mx) for gi, v in ls_.items()}
            inv = 1.0 / functools.reduce(lambda a, b: a + b, [es[gi] for gi in range(N_GROUPS)])
            for gi in range(N_GROUPS):
                k = gi * halves + hf
                att_ref[rows, k * LANES:(k + 1) * LANES] = (os_[gi] * (es[gi] * inv)).astype(BF)

    out_ref[0] = x_ref[0] + mod_ref[0, 2] * _dot(att_ref[...], wo_ref[...])


def _pattn_blocks(relb_ref, bkt_ref, q0_ref, q1_ref, k0_ref, k1_ref, v0_ref, v1_ref, put, bias_ref, carry_ref,
                  *, g, dil, nbk):
    b = pl.program_id(0)
    i = pl.program_id(1)
    hpg = HEADS_PER_GROUP
    span_rows = SPAN * dil

    @pl.when((b == 0) & (i == 0))
    def _():
        bkt = bkt_ref[...]
        no_prev = lax.broadcasted_iota(jnp.int32, (SPAN, 2 * SPAN), 1) < SPAN
        for h in range(hpg):
            bias = _bias_from_buckets(bkt, relb_ref, g * hpg + h)
            bias_ref[0, h] = bias
            bias_ref[1, h] = jnp.where(no_prev, NEG_INF, bias)

    @pl.when(i == 0)
    def _():
        carry_ref[...] = jnp.zeros(carry_ref.shape, BF)

    lanehead = lax.broadcasted_iota(jnp.int32, (SPAN, GROUP_WIDTH), 1) >> HEAD_SHIFT
    first = jnp.where(i == 0, 1, 0)
    rd_slot = lax.rem(i, 2)
    last_k, last_v = {}, {}

    def block(j, r):
        rows = pl.ds(j * span_rows + r, SPAN, stride=dil)
        cur = lambda r0, r1: jnp.concatenate([r0[0, rows, :], r1[0, rows, :]], axis=1)
        if j == 0:
            kprev, vprev = carry_ref[rd_slot, 0, r], carry_ref[rd_slot, 1, r]
        else:
            kprev, vprev = last_k[r], last_v[r]
        kcur = cur(k0_ref, k1_ref).astype(BF)
        vcur = cur(v0_ref, v1_ref).astype(BF)
        if j == nbk - 1:
            carry_ref[1 - rd_slot, 0, r] = kcur
            carry_ref[1 - rd_slot, 1, r] = vcur
        else:
            last_k[r], last_v[r] = kcur, vcur
        q = cur(q0_ref, q1_ref) * (HEAD_DIM ** -0.5)
        lhs = jnp.concatenate([jnp.where(lanehead == h, q, 0.0).astype(BF) for h in range(hpg)], axis=0)
        s = _dot_nt(lhs, jnp.concatenate([kprev, kcur], axis=0))

        ps, ms, sums = [], [], []
        for h in range(hpg):
            logit = s[h * SPAN:(h + 1) * SPAN] + (bias_ref[first, h] if j == 0 else bias_ref[0, h])
            m = jnp.max(logit, axis=-1, keepdims=True)
            e = jnp.exp(logit - m)
            ps.append(e.astype(BF))
            ms.append(m)
            sums.append(jnp.sum(e, axis=-1, keepdims=True))
        pv = _dot(jnp.concatenate(ps, axis=0), jnp.concatenate([vprev, vcur], axis=0))

        o = jnp.zeros((SPAN, GROUP_WIDTH), F32)
        l = jnp.zeros((SPAN, GROUP_WIDTH), F32)
        for h in range(hpg):
            o = jnp.where(lanehead == h, pv[h * SPAN:(h + 1) * SPAN] * (1.0 / sums[h]), o)
            l = jnp.where(lanehead == h, ms[h] + jnp.log(sums[h]), l)
        put(0, rows, o[:, :LANES])
        put(1, rows, o[:, LANES:])
        put(2, rows, l[:, :LANES])
        put(3, rows, l[:, LANES:])

    for j in range(nbk):
        for r in range(dil):
            block(j, r)


def _pattn_call(q, kv, rel_bias, g, dil, finish=None):
    bsz, s, _ = q.shape
    nbk = max(1, PATTN_ROWS // (dil * SPAN))
    rows = nbk * dil * SPAN
    assert s % rows == 0 and GROUP_WIDTH == 2 * LANES
    blk = (1, rows, LANES)
    kcol = 2 * g
    vcol = ATT_WIDTH // LANES + 2 * g
    ospec = pl.BlockSpec(blk, lambda b, i: (b, i, 0))
    in_specs = [
        pl.BlockSpec(memory_space=pltpu.SMEM),
        _const_spec((SPAN, 2 * SPAN)),
        pl.BlockSpec(blk, lambda b, i: (b, i, kcol)),
        pl.BlockSpec(blk, lambda b, i: (b, i, kcol + 1)),
        pl.BlockSpec(blk, lambda b, i: (b, i, kcol)),
        pl.BlockSpec(blk, lambda b, i: (b, i, kcol + 1)),
        pl.BlockSpec(blk, lambda b, i: (b, i, vcol)),
        pl.BlockSpec(blk, lambda b, i: (b, i, vcol + 1)),
    ]
    args = [rel_bias, _prompt_bucket_table(dil), q, q, kv, kv, kv, kv]
    scratch = [pltpu.VMEM((2, HEADS_PER_GROUP, SPAN, 2 * SPAN), F32),
               pltpu.VMEM((2, 2, dil, SPAN, GROUP_WIDTH), BF)]
    if finish is None:
        out = jax.ShapeDtypeStruct((bsz, s, LANES), F32)
        res = pl.pallas_call(
            functools.partial(_pattn_kernel, g=g, dil=dil, nbk=nbk),
            out_shape=(out,) * 4,
            grid=(bsz, s // rows),
            in_specs=in_specs,
            out_specs=(ospec,) * 4,
            scratch_shapes=scratch,
            compiler_params=_params(("arbitrary", "arbitrary")),
            name=f"prompt_attn_g{g}",
        )(*args)
        return list(res[:2]), list(res[2:])

    x, mods, o_other, l_other, wo = finish
    d = x.shape[-1]
    r = mods.shape[2]
    assert r == 1
    xspec = pl.BlockSpec((1, rows, d), lambda b, i: (b, i, 0))
    return pl.pallas_call(
        functools.partial(_pattn_oproj_kernel, g=g, dil=dil, nbk=nbk),
        out_shape=jax.ShapeDtypeStruct((bsz, s, d), F32),
        grid=(bsz, s // rows),
        in_specs=in_specs + [xspec, pl.BlockSpec((1, N_MOD, r, d), lambda b, i: (b, 0, 0, 0))]
        + [ospec] * (len(o_other) + len(l_other)) + [_const_spec((ATT_WIDTH, d))],
        out_specs=xspec,
        scratch_shapes=scratch + [pltpu.VMEM((4, rows, LANES), F32), pltpu.VMEM((rows, ATT_WIDTH), BF)],
        compiler_params=_params(("arbitrary", "arbitrary")),
        name=f"prompt_attn_g{g}_out_proj",
    )(*args, x, mods, *o_other, *l_other, wo)


def _sattn_kernel(relb_ref, bo0_ref, bo1_ref, bo2_ref, bn_ref, q_ref, kvn_ref, kvt_ref, c0_ref, c1_ref, c2_ref,
                  o0_ref, o1_ref, o2_ref, l0_ref, l1_ref, l2_ref, n0_ref, n1_ref, n2_ref,
                  bias0_ref, bias1_ref, bias2_ref, biasn_ref, tail_ref, *, t_new):
    hpg = HEADS_PER_GROUP
    gw = GROUP_WIDTH
    rows = SAMPLE_QROWS
    bo_refs = (bo0_ref, bo1_ref, bo2_ref)
    bias_refs = (bias0_ref, bias1_ref, bias2_ref)
    caches = (c0_ref, c1_ref, c2_ref)
    outs = ((o0_ref, l0_ref, n0_ref), (o1_ref, l1_ref, n1_ref), (o2_ref, l2_ref, n2_ref))

    @pl.when(pl.program_id(0) == 0)
    def _():
        tail_ref[...] = jnp.zeros(tail_ref.shape, F32)
        for g in range(N_GROUPS):
            for h in range(hpg):
                sl = slice(h * SUBLANES, (h + 1) * SUBLANES)
                bias_refs[g][sl, :] = _bias_from_buckets(bo_refs[g][sl, :], relb_ref, g * hpg + h)
                biasn_ref[g, sl, :] = _bias_from_buckets(bn_ref[g, sl, :], relb_ref, g * hpg + h)

    lanehead = lax.broadcasted_iota(jnp.int32, (rows, gw), 1) >> HEAD_SHIFT
    rowhead = lax.broadcasted_iota(jnp.int32, (rows, gw), 0) >> SUBLANE_SHIFT
    own = lanehead == rowhead
    tail_lane = lax.broadcasted_iota(jnp.int32, (gw, LANES), 1) >= LANES - t_new

    for g in range(N_GROUPS):
        cref = caches[g]
        o_ref, l_ref, n_ref = outs[g]
        width = cref.shape[-1]
        qm = jnp.where(own, q_ref[0, :, g * gw:(g + 1) * gw] * (HEAD_DIM ** -0.5), 0.0).astype(BF)
        knew = kvn_ref[0, :, g * gw:(g + 1) * gw].astype(BF)
        vnew = kvn_ref[0, :, ATT_WIDTH + g * gw:ATT_WIDTH + (g + 1) * gw].astype(BF)
        lo = _dot(qm, cref[0, 0].astype(BF)) + bias_refs[g][...]
        ln = _dot_nt(qm, knew) + biasn_ref[g]
        m = jnp.maximum(jnp.max(lo, axis=-1, keepdims=True), jnp.max(ln, axis=-1, keepdims=True))
        eo = jnp.exp(lo - m)
        en = jnp.exp(ln - m)
        ssum = jnp.sum(eo, axis=-1, keepdims=True) + jnp.sum(en, axis=-1, keepdims=True)
        pv = _dot_nt(eo.astype(BF), cref[0, 1].astype(BF)) + _dot(en.astype(BF), vnew)
        om = jnp.where(own, pv * (1.0 / ssum), 0.0)
        lm = jnp.where(own, m + jnp.log(ssum), 0.0)
        o_acc = om[0:SUBLANES]
        l_acc = lm[0:SUBLANES]
        for h in range(1, hpg):
            o_acc = o_acc + om[h * SUBLANES:(h + 1) * SUBLANES]
            l_acc = l_acc + lm[h * SUBLANES:(h + 1) * SUBLANES]
        o_ref[0] = o_acc
        l_ref[0] = l_acc

        for kv in range(2):
            rolled = pltpu.roll(cref[0, kv], width - t_new, 1)
            tail_ref[:, 0:SUBLANES] = kvt_ref[0, kv * ATT_WIDTH + g * gw:kv * ATT_WIDTH + (g + 1) * gw, :]
            tail = pltpu.roll(tail_ref[...], LANES - t_new, 1)
            if width > LANES:
                n_ref[0, kv, :, 0:width - LANES] = rolled[:, 0:width - LANES]
            n_ref[0, kv, :, width - LANES:width] = jnp.where(tail_lane, tail, rolled[:, width - LANES:width])


def _sattn_call(q, kvn, caches, rel_bias):
    n, t_new, _ = q.shape
    gw = GROUP_WIDTH
    hpg = HEADS_PER_GROUP
    assert t_new <= SUBLANES
    views, bkt_old, bkt_new = [], [], []
    for g, (window, dil) in enumerate(DIL_GROUPS):
        width = caches[g].shape[1]
        assert width == window and width % LANES == 0
        views.append(jnp.transpose(caches[g], (0, 2, 3, 4, 1)).reshape(n, 2, gw, width))
        old, new = _sample_bucket_tables(dil, width, t_new)
        bkt_old.append(old)
        bkt_new.append(new)
    pad = SUBLANES - t_new
    qp = jnp.tile(jnp.pad(q, ((0, 0), (0, pad), (0, 0))), (1, hpg, 1))
    kvp = jnp.pad(kvn, ((0, 0), (0, pad), (0, 0)))
    kvt = jnp.swapaxes(kvp, 1, 2)
    out = jax.ShapeDtypeStruct((n, SUBLANES, gw), F32)
    oblk = pl.BlockSpec((1, SUBLANES, gw), lambda i: (i, 0, 0))
    cspecs = [pl.BlockSpec((1, 2, gw, v.shape[-1]), lambda i: (i, 0, 0, 0)) for v in views]
    res = pl.pallas_call(
        functools.partial(_sattn_kernel, t_new=t_new),
        out_shape=(out,) * 6 + tuple(jax.ShapeDtypeStruct(v.shape, F32) for v in views),
        grid=(n,),
        in_specs=[pl.BlockSpec(memory_space=pltpu.SMEM)]
        + [_const_spec(b.shape) for b in bkt_old]
        + [
            _const_spec((N_GROUPS, SAMPLE_QROWS, SUBLANES)),
            pl.BlockSpec((1, SAMPLE_QROWS, ATT_WIDTH), lambda i: (i, 0, 0)),
            pl.BlockSpec((1, SUBLANES, 2 * ATT_WIDTH), lambda i: (i, 0, 0)),
            pl.BlockSpec((1, 2 * ATT_WIDTH, SUBLANES), lambda i: (i, 0, 0)),
        ]
        + cspecs,
        out_specs=(oblk,) * 6 + tuple(cspecs),
        scratch_shapes=[pltpu.VMEM(b.shape, F32) for b in bkt_old]
        + [pltpu.VMEM((N_GROUPS, SAMPLE_QROWS, SUBLANES), F32), pltpu.VMEM((gw, LANES), F32)],
        compiler_params=_params(("arbitrary",)),
        name="sample_attn",
    )(rel_bias, *bkt_old, jnp.stack(bkt_new), qp, kvp, kvt, *views)
    new_caches = [jnp.transpose(c.reshape(n, 2, hpg, HEAD_DIM, c.shape[-1]), (0, 4, 1, 2, 3)) for c in res[6:]]
    return [a[:, :t_new] for a in res[:3]], [a[:, :t_new] for a in res[3:6]], new_caches


def _oproj_kernel(x_ref, mod_ref, *refs, tm):
    nblk = ATT_WIDTH // LANES
    halves = nblk // N_GROUPS
    o_refs, l_refs = refs[:nblk], refs[nblk:2 * nblk]
    wo_ref, out_ref, att_ref = refs[2 * nblk:]
    rc = BF16_ROWS

    def body(i, c):
        r0 = pl.multiple_of(i * rc, rc)
        for hf in range(halves):
            blks = [g * halves + hf for g in range(N_GROUPS)]
            ls = [l_refs[k][0, pl.ds(r0, rc), :] for k in blks]
            mx = functools.reduce(jnp.maximum, ls)
            es = [jnp.exp(v - mx) for v in ls]
            inv = 1.0 / functools.reduce(lambda a, b: a + b, es)
            for k, e in zip(blks, es):
                att_ref[pl.ds(r0, rc), k * LANES:(k + 1) * LANES] = (o_refs[k][0, pl.ds(r0, rc), :] * (e * inv)).astype(BF)
        return c

    lax.fori_loop(0, tm // rc, body, 0, unroll=2)
    out_ref[0] = x_ref[0] + mod_ref[0, 2] * _dot(att_ref[...], wo_ref[...])


def _oproj_call(x, mods, os_, ls_, wo, *, tm):
    ns, s, d = x.shape
    r = mods.shape[2]
    kern = functools.partial(_oproj_kernel, tm=tm)
    gspec = pl.BlockSpec((1, tm, LANES), lambda n, t: (n, t, 0))
    return pl.pallas_call(
        kern,
        out_shape=jax.ShapeDtypeStruct((ns, s, d), F32),
        grid=(ns, s // tm),
        in_specs=[
            pl.BlockSpec((1, tm, d), lambda n, t: (n, t, 0)),
            pl.BlockSpec((1, N_MOD, r, d), lambda n, t: (n, 0, 0, 0)),
        ] + [gspec] * (2 * ATT_WIDTH // LANES) + [_const_spec((ATT_WIDTH, d))],
        out_specs=pl.BlockSpec((1, tm, d), lambda n, t: (n, t, 0)),
        scratch_shapes=[pltpu.VMEM((tm, ATT_WIDTH), BF)],
        compiler_params=_params(("arbitrary", "arbitrary")),
        name="attn_out_proj",
    )(x, mods, *os_, *ls_, wo)


def _trunk(x, mods, modkv, hist_a, hist_f, w, attn_fn, *, tm, rd, kv_tail_rows):
    row = lambda v: v.reshape(1, -1)
    x, st_a = _conformer_call(x, mods[0], hist_a, row(w["norm_mix"][0]), w["a_w1"], row(w["a_b1"][0]),
                              w["a_dw"][0], row(w["a_dwb"][0]), row(w["a_ln_g"][0]), row(w["a_ln_b"][0]),
                              w["a_w2"], row(w["a_b2"][0]), tm=tm, rd=rd)
    x, st_f0 = _ffn_call(x, mods[0], hist_f[0], row(w["norm_ffn"][0]), w["f_wup"][0], w["f_cw"][0],
                         row(w["f_cb"][0]), w["f_wdown"][0], row(w["norm_f"]), tm=tm, rd=rd, final_norm=False)
    q, kv, kv_tail = _qkv_call(x, mods[1], modkv, row(w["norm_mix"][1]), row(w["norm_kv"]), w["w_q"], w["w_kv"],
                               tm=tm, tail_rows=kv_tail_rows)
    x, attn_extra = attn_fn(q, kv, x, mods[1], w["w_o"])
    y, st_f1 = _ffn_call(x, mods[1], hist_f[1], row(w["norm_ffn"][1]), w["f_wup"][1], w["f_cw"][1],
                         row(w["f_cb"][1]), w["f_wdown"][1], row(w["norm_f"]), tm=tm, rd=rd, final_norm=True)
    return y, kv_tail, st_a, [st_f0, st_f1], attn_extra


def kernel(x_prompt, x_sample, cache_kv_w128, cache_kv_w512, cache_kv_w2048, state_conv_a, state_conv_ffn, c_prompt, c_sample, w_mod, b_mod, norm_mix, norm_ffn, a_w1, a_b1, a_dw, a_dwb, a_ln_g, a_ln_b, a_w2, a_b2, w_mod_kv, b_mod_kv, norm_kv, w_kv, w_q, w_o, rel_bias, f_wup, f_cw, f_cb, f_wdown, norm_f):
    d = D_MODEL
    bsz, seq, _ = x_prompt.shape
    nseq, t_new, _ = x_sample.shape
    caches = (cache_kv_w128, cache_kv_w512, cache_kv_w2048)
    hpg = HEADS_PER_GROUP

    w = dict(norm_mix=norm_mix, norm_ffn=norm_ffn, a_w1=a_w1[0].astype(BF), a_b1=a_b1, a_dw=a_dw, a_dwb=a_dwb,
             a_ln_g=a_ln_g, a_ln_b=a_ln_b, a_w2=a_w2[0].astype(BF), a_b2=a_b2, norm_kv=norm_kv,
             w_kv=w_kv.astype(BF), w_q=w_q[0].astype(BF), w_o=w_o[0].astype(BF),
             f_wup=[f_wup[l].astype(BF) for l in range(DEPTH)], f_cw=f_cw, f_cb=f_cb,
             f_wdown=[f_wdown[l].astype(BF) for l in range(DEPTH)], norm_f=norm_f)

    n_c = bsz + nseq
    mp = _round_up(n_c, SUBLANES)
    c_all = jnp.pad(jnp.concatenate([c_prompt, c_sample], axis=0), ((0, mp - n_c), (0, 0)))
    mod = _mod_call(c_all, w_mod, b_mod.reshape(DEPTH, 1, N_MOD * d), tn=1536)
    modkv = _mod_call(c_all, w_mod_kv[None], b_mod_kv.reshape(1, 1, 2 * d), tn=1024)[0]

    mods_p = [mod[l, :bsz].reshape(bsz, N_MOD, 1, d) for l in range(DEPTH)]
    modkv_p = modkv[:bsz].reshape(bsz, 2, 1, d)
    hist_a_p = jnp.zeros((bsz, CONV_A_WIDTH - 1, d), F32)
    hist_f_p = [jnp.zeros((bsz, CONV_F_WIDTH - 1, 2 * D_FF), F32)] * DEPTH

    def prompt_attn(q, kv, x, mods1, wo):
        res = [_pattn_call(q, kv, rel_bias, g, dil) for g, (_, dil) in enumerate(DIL_GROUPS) if g > 0]
        finish = (x, mods1, [a for r in res for a in r[0]], [a for r in res for a in r[1]], wo)
        return _pattn_call(q, kv, rel_bias, 0, DIL_GROUPS[0][1], finish=finish), None

    wmax = max(c.shape[1] for c in caches)
    assert seq >= wmax
    y_p, kv_tail, st_a_p, st_f_p, _ = _trunk(x_prompt, mods_p, modkv_p, hist_a_p, hist_f_p, w, prompt_attn,
                                             tm=TM_PROMPT, rd=1, kv_tail_rows=wmax)
    kv_tail = kv_tail.reshape(bsz, wmax, 2, N_HEADS, HEAD_DIM)
    kv_bufs_p = [kv_tail[:, wmax - c.shape[1]:, :, g * hpg:(g + 1) * hpg] for g, c in enumerate(caches)]
    conv_a_p = st_a_p[None]
    conv_f_p = jnp.stack(st_f_p)

    rows = t_new * nseq
    tmaj = lambda v: jnp.swapaxes(v, 0, 1).reshape(1, -1, v.shape[-1])
    smaj = lambda v, c: jnp.swapaxes(v.reshape(-1, nseq, c), 0, 1)

    def per_row(m, k):
        return jnp.tile(jnp.swapaxes(m.reshape(nseq, k, d), 0, 1), (1, t_new, 1))[None]

    mods_s = [per_row(mod[l, bsz:n_c], N_MOD) for l in range(DEPTH)]
    modkv_s = per_row(modkv[bsz:n_c], 2)
    hist_a_s = tmaj(state_conv_a[0])
    hist_f_s = [tmaj(state_conv_ffn[l]) for l in range(DEPTH)]

    def sample_attn(q, kv, x, mods1, wo):
        os_, ls_, new_caches = _sattn_call(smaj(q, ATT_WIDTH), smaj(kv, 2 * ATT_WIDTH), caches, rel_bias)
        halves = lambda vs: [tmaj(a[..., c0:c0 + LANES]) for a in vs for c0 in range(0, GROUP_WIDTH, LANES)]
        return _oproj_call(x, mods1, halves(os_), halves(ls_), wo, tm=rows), new_caches

    y_s, _, st_a_s, st_f_s, kv_bufs_s = _trunk(tmaj(x_sample), mods_s, modkv_s, hist_a_s, hist_f_s, w,
                                               sample_attn, tm=rows, rd=nseq, kv_tail_rows=rows)
    y_s = smaj(y_s, d)
    conv_a_s = smaj(st_a_s, d)[None]
    conv_f_s = jnp.stack([smaj(s, 2 * D_FF) for s in st_f_s])

    return (y_p, y_s, kv_bufs_p[0], kv_bufs_p[1], kv_bufs_p[2], conv_a_p, conv_f_p,
            kv_bufs_s[0], kv_bufs_s[1], kv_bufs_s[2], conv_a_s, conv_f_s)
```
